```python
import jax, jax.numpy as jnp
from jax import lax
import numpy as np

D_MODEL = 1024
BATCH = 4
SEQ = 8192
DEPTH = 1
DEC_BATCH = 128
DEC_SEQ = 8
PAST_LEN = 8192
PAGE_SIZE = 128

HEAD_DIM = 64
HEADS_PER_GROUP = D_MODEL // 128
ATTN_GROUPS = ((128, 1), (512, 4), (2048, 16))
N_GROUPS = 3
ATTN_HEADS = N_GROUPS * HEADS_PER_GROUP
ATTN_QKV_WIDTH = ATTN_HEADS * HEAD_DIM
ATTN_WIDTH = HEADS_PER_GROUP * HEAD_DIM
ROPE_DIM = HEAD_DIM // 4
ROPE_THETA = 500000.0
BAND_BLOCK = 128

GLA_HEADS = 4
GLA_DK = D_MODEL // 16
GLA_DV = D_MODEL // 8
GLA_KEY_WIDTH = GLA_HEADS * GLA_DK
GLA_VAL_WIDTH = GLA_HEADS * GLA_DV
GLA_GATE_RANK = 16
GLA_TAU = 16.0
GLA_CHUNK = 64

NORM_EPS = 1e-6
IN_WIDTH = 3 * ATTN_QKV_WIDTH + ATTN_WIDTH + 2 * GLA_KEY_WIDTH + 2 * GLA_VAL_WIDTH + GLA_GATE_RANK + 2 * D_MODEL

kernel_name = "dilated_window_gla_hybrid_step"


def rms_norm(x, w):
    xf = x.astype(jnp.float32)
    y = xf * lax.rsqrt(jnp.mean(xf * xf, axis=-1, keepdims=True) + NORM_EPS)
    return (y * w.astype(jnp.float32)).astype(x.dtype)


def partial_rope(x, pos):
    half = ROPE_DIM // 2
    inv_freq = 1.0 / (ROPE_THETA ** (jnp.arange(half, dtype=jnp.float32) * (2.0 / ROPE_DIM)))
    ang = pos.astype(jnp.float32)[:, None] * inv_freq[None, :]
    cos = jnp.cos(ang)[:, None, :]
    sin = jnp.sin(ang)[:, None, :]
    xr = x[..., :ROPE_DIM].astype(jnp.float32)
    x1, x2 = xr[..., :half], xr[..., half:]
    rot = jnp.concatenate([x1 * cos - x2 * sin, x2 * cos + x1 * sin], axis=-1)
    return jnp.concatenate([rot.astype(x.dtype), x[..., ROPE_DIM:]], axis=-1)


def project_in(x, pos, ln_w, w_in, w_gla_a2, b_gla_a):
    B, S, _ = x.shape
    h = rms_norm(x, ln_w)
    proj = jnp.einsum('bsd,de->bse', h, w_in)
    widths = (ATTN_QKV_WIDTH, ATTN_QKV_WIDTH, ATTN_QKV_WIDTH, ATTN_WIDTH,
              GLA_KEY_WIDTH, GLA_KEY_WIDTH, GLA_VAL_WIDTH, GLA_VAL_WIDTH, GLA_GATE_RANK,
              D_MODEL, D_MODEL)
    splits = [int(c) for c in np.cumsum(widths)[:-1]]
    aq, ak, av, az, gq, gk, gv, gz, glr, mg_a, mg_b = jnp.split(proj, splits, axis=-1)
    aq = partial_rope(aq.reshape(B, S, ATTN_HEADS, HEAD_DIM), pos) * (HEAD_DIM ** -0.5)
    ak = partial_rope(ak.reshape(B, S, ATTN_HEADS, HEAD_DIM), pos)
    av = av.reshape(B, S, ATTN_HEADS, HEAD_DIM)
    gq = gq.reshape(B, S, GLA_HEADS, GLA_DK).astype(jnp.float32) * (GLA_DK ** -0.5)
    gk = gk.reshape(B, S, GLA_HEADS, GLA_DK).astype(jnp.float32)
    gv = gv.reshape(B, S, GLA_HEADS, GLA_DV).astype(jnp.float32)
    gate_pre = (jnp.einsum('bsr,rk->bsk', glr, w_gla_a2) + b_gla_a).astype(jnp.float32)
    log_a = (jax.nn.log_sigmoid(gate_pre) / GLA_TAU).reshape(B, S, GLA_HEADS, GLA_DK)
    return aq, ak, av, az, gq, gk, gv, log_a, gz, mg_a, mg_b


def dilated_window_attn_prompt(q, k, v, window, dilation):
    B, S, H, dh = q.shape
    L = S // dilation
    reach = window // dilation
    blk = BAND_BLOCK
    nblk = -(-L // blk)
    Lp = nblk * blk
    BD = B * dilation

    def to_residue_blocks(t):
        t = t.reshape(B, L, dilation, H, dh).transpose(0, 2, 1, 3, 4).reshape(BD, L, H, dh)
        t = jnp.pad(t, ((0, 0), (0, Lp - L), (0, 0), (0, 0)))
        return t.reshape(BD, nblk, blk, H, dh)

    def with_prev_block(t):
        prev = jnp.pad(t, ((0, 0), (1, 0), (0, 0), (0, 0), (0, 0)))[:, :-1]
        return jnp.concatenate([prev, t], axis=2)

    qb = to_residue_blocks(q)
    kk = with_prev_block(to_residue_blocks(k))
    vv = with_prev_block(to_residue_blocks(v))
    s = jnp.einsum('bnqhd,bnkhd->bnhqk', qb, kk).astype(jnp.float32)
    qi = jnp.arange(blk)[:, None] + blk
    ki = jnp.arange(2 * blk)[None, :]
    dist = qi - ki
    band = (dist >= 0) & (dist <= reach)
    inside = (jnp.arange(nblk)[:, None, None] > 0) | (ki[None] >= blk)
    mask = band[None] & inside
    s = jnp.where(mask[None, :, None], s, -jnp.inf)
    m = jnp.max(s, axis=-1, keepdims=True)
    p = jnp.exp(s - m)
    den = jnp.sum(p, axis=-1, keepdims=True)
    o = jnp.einsum('bnhqk,bnkhd->bnqhd', p / den, vv.astype(jnp.float32))
    lse = (m + jnp.log(den))[..., 0]
    o = o.reshape(BD, Lp, H, dh)[:, :L]
    lse = lse.transpose(0, 1, 3, 2).reshape(BD, Lp, H)[:, :L]
    o = o.reshape(B, dilation, L, H, dh).transpose(0, 2, 1, 3, 4).reshape(B, S, H, dh)
    lse = lse.reshape(B, dilation, L, H).transpose(0, 2, 1, 3).reshape(B, S, H)
    return o, lse


def dilated_window_attn_sample(q, k, v, kv_buf, window, dilation):
    DB, T, H, dh = q.shape
    W_len = kv_buf.shape[1]
    k_all = jnp.concatenate([kv_buf[:, :, 0].astype(k.dtype), k], axis=1)
    v_all = jnp.concatenate([kv_buf[:, :, 1].astype(v.dtype), v], axis=1)
    reach = window // dilation
    idx = W_len + jnp.arange(T)[:, None] - dilation * jnp.arange(reach + 1)[None, :]
    valid = idx >= 0
    idx = jnp.maximum(idx, 0)
    kg = k_all[:, idx]
    vg = v_all[:, idx]
    s = jnp.einsum('bthd,btrhd->bthr', q, kg).astype(jnp.float32)
    s = jnp.where(valid[None, :, None, :], s, -jnp.inf)
    m = jnp.max(s, axis=-1, keepdims=True)
    p = jnp.exp(s - m)
    den = jnp.sum(p, axis=-1, keepdims=True)
    o = jnp.einsum('bthr,btrhd->bthd', p / den, vg.astype(jnp.float32))
    lse = (m + jnp.log(den))[..., 0]
    return o, lse


def combine_groups(outs, lses):
    o = jnp.stack(outs, axis=0)
    w = jax.nn.softmax(jnp.stack(lses, axis=0), axis=0)
    comb = jnp.einsum('gbsh,gbshd->bshd', w, o)
    B, S = comb.shape[0], comb.shape[1]
    return comb.reshape(B, S, ATTN_WIDTH)


def gla_recurrence(q, k, v, log_a, s0, chunk):
    B, S, H, dk = q.shape
    dv = v.shape[-1]
    nc = S // chunk

    def to_chunks(t):
        return t.reshape(B, nc, chunk, H, t.shape[-1]).transpose(1, 0, 3, 2, 4)

    qc, kc, vc, gc = to_chunks(q), to_chunks(k), to_chunks(v), to_chunks(log_a)
    bc = jnp.cumsum(gc, axis=3)
    causal = jnp.tril(jnp.ones((chunk, chunk), dtype=bool))

    def step(state, inp):
        q_, k_, v_, b_ = inp
        b_last = b_[:, :, -1:, :]
        qe = q_ * jnp.exp(b_)
        ke = k_ * jnp.exp(-b_)
        att = jnp.where(causal, jnp.einsum('bhtk,bhsk->bhts', qe, ke), 0.0)
        o = jnp.einsum('bhts,bhsv->bhtv', att, v_) + jnp.einsum('bhtk,bhkv->bhtv', qe, state)
        kd = k_ * jnp.exp(b_last - b_)
        new_state = jnp.exp(b_last[:, :, 0, :, None]) * state + jnp.einsum('bhsk,bhsv->bhkv', kd, v_)
        return new_state, o

    s_fin, o = lax.scan(step, s0, (qc, kc, vc, bc))
    o = o.transpose(1, 0, 3, 2, 4).reshape(B, S, H, dv)
    return o, s_fin


def merge_out(x, o_attn, az, o_gla, gz, mg_a, mg_b, gla_norm_w, w_attn_out, w_gla_out, w_out):
    B, S, _ = x.shape
    o_gla = rms_norm(o_gla, gla_norm_w).reshape(B, S, GLA_VAL_WIDTH).astype(x.dtype)
    ya = jnp.einsum('bsc,cd->bsd', o_attn.astype(x.dtype) * jax.nn.silu(az), w_attn_out)
    yb = jnp.einsum('bsc,cd->bsd', o_gla * jax.nn.silu(gz), w_gla_out)
    mixed = jax.nn.sigmoid(mg_a) * ya + jax.nn.sigmoid(mg_b) * yb
    return x + jnp.einsum('bsd,de->bse', mixed, w_out)


def setup_inputs(seed: int = 0) -> dict:
    key = jax.random.key(seed)
    ks = jax.random.split(key, 16)
    f32 = jnp.float32
    w_lens = [min(w, PAST_LEN) for w, _ in ATTN_GROUPS]
    return {
        'x_prompt': jax.random.normal(ks[0], (BATCH, SEQ, D_MODEL), f32),
        'x_sample': jax.random.normal(ks[1], (DEC_BATCH, DEC_SEQ, D_MODEL), f32),
        'cache_kv_w128': jax.random.normal(ks[2], (DEPTH, DEC_BATCH, w_lens[0], 2, HEADS_PER_GROUP, HEAD_DIM), f32),
        'cache_kv_w512': jax.random.normal(ks[3], (DEPTH, DEC_BATCH, w_lens[1], 2, HEADS_PER_GROUP, HEAD_DIM), f32),
        'cache_kv_w2048': jax.random.normal(ks[4], (DEPTH, DEC_BATCH, w_lens[2], 2, HEADS_PER_GROUP, HEAD_DIM), f32),
        'state_gla': jax.random.normal(ks[5], (DEPTH, DEC_BATCH, GLA_HEADS, GLA_DK, GLA_DV), f32),
        'ln_w': 1.0 + 0.01 * jax.random.normal(ks[6], (DEPTH, D_MODEL), f32),
        'w_in': jax.random.normal(ks[7], (DEPTH, D_MODEL, IN_WIDTH), f32) * D_MODEL ** -0.5,
        'w_gla_a2': jax.random.normal(ks[8], (DEPTH, GLA_GATE_RANK, GLA_KEY_WIDTH), f32) * GLA_GATE_RANK ** -0.5,
        'b_gla_a': 0.1 * jax.random.normal(ks[9], (DEPTH, GLA_KEY_WIDTH), f32),
        'gla_norm_w': 1.0 + 0.01 * jax.random.normal(ks[10], (DEPTH, GLA_DV), f32),
        'w_attn_out': jax.random.normal(ks[11], (DEPTH, ATTN_WIDTH, D_MODEL), f32) * ATTN_WIDTH ** -0.5,
        'w_gla_out': jax.random.normal(ks[12], (DEPTH, GLA_VAL_WIDTH, D_MODEL), f32) * GLA_VAL_WIDTH ** -0.5,
        'w_out': jax.random.normal(ks[13], (DEPTH, D_MODEL, D_MODEL), f32) * D_MODEL ** -0.5,
        'final_norm_w': 1.0 + 0.01 * jax.random.normal(ks[14], (D_MODEL,), f32),
    }


def reference(x_prompt, x_sample, cache_kv_w128, cache_kv_w512, cache_kv_w2048, state_gla,
              ln_w, w_in, w_gla_a2, b_gla_a, gla_norm_w, w_attn_out, w_gla_out, w_out, final_norm_w):
    B, S, _ = x_prompt.shape
    DB, T, _ = x_sample.shape
    pos_p = jnp.arange(S, dtype=jnp.int32)
    pos_s = PAST_LEN + jnp.arange(T, dtype=jnp.int32)
    caches = (cache_kv_w128, cache_kv_w512, cache_kv_w2048)
    hp, hs = x_prompt, x_sample
    kv_p = [[] for _ in range(N_GROUPS)]
    kv_s = [[] for _ in range(N_GROUPS)]
    gla_p, gla_s = [], []
    for layer in range(DEPTH):
        aq, ak, av, az, gq, gk, gv, ga, gz, ma, mb = project_in(
            hp, pos_p, ln_w[layer], w_in[layer], w_gla_a2[layer], b_gla_a[layer])
        outs, lses = [], []
        for g, (win, dil) in enumerate(ATTN_GROUPS):
            hs_ = slice(g * HEADS_PER_GROUP, (g + 1) * HEADS_PER_GROUP)
            o, l = dilated_window_attn_prompt(aq[:, :, hs_], ak[:, :, hs_], av[:, :, hs_], win, dil)
            outs.append(o)
            lses.append(l)
            kv_p[g].append(jnp.stack([ak[:, :, hs_], av[:, :, hs_]], axis=2)[:, S - min(win, S):])
        o_attn = combine_groups(outs, lses)
        s0 = jnp.zeros((B, GLA_HEADS, GLA_DK, GLA_DV), jnp.float32)
        o_gla, s_fin = gla_recurrence(gq, gk, gv, ga, s0, min(GLA_CHUNK, S))
        gla_p.append(s_fin)
        hp = merge_out(hp, o_attn, az, o_gla, gz, ma, mb, gla_norm_w[layer],
                       w_attn_out[layer], w_gla_out[layer], w_out[layer])
        aq, ak, av, az, gq, gk, gv, ga, gz, ma, mb = project_in(
            hs, pos_s, ln_w[layer], w_in[layer], w_gla_a2[layer], b_gla_a[layer])
        outs, lses = [], []
        for g, (win, dil) in enumerate(ATTN_GROUPS):
            hs_ = slice(g * HEADS_PER_GROUP, (g + 1) * HEADS_PER_GROUP)
            o, l = dilated_window_attn_sample(aq[:, :, hs_], ak[:, :, hs_], av[:, :, hs_],
                                              caches[g][layer], win, dil)
            outs.append(o)
            lses.append(l)
            kv_s[g].append(jnp.stack([ak[:, :, hs_], av[:, :, hs_]], axis=2))
        o_attn = combine_groups(outs, lses)
        o_gla, s_new = gla_recurrence(gq, gk, gv, ga, state_gla[layer].astype(jnp.float32), T)
        gla_s.append(s_new)
        hs = merge_out(hs, o_attn, az, o_gla, gz, ma, mb, gla_norm_w[layer],
                       w_attn_out[layer], w_gla_out[layer], w_out[layer])
    y_prompt = rms_norm(hp, final_norm_w)
    y_sample = rms_norm(hs, final_norm_w)
    kv128_p, kv512_p, kv2048_p = (jnp.stack(kv_p[0]), jnp.stack(kv_p[1]), jnp.stack(kv_p[2]))
    kv128_s, kv512_s, kv2048_s = (jnp.stack(kv_s[0]), jnp.stack(kv_s[1]), jnp.stack(kv_s[2]))
    gla_prompt = jnp.stack(gla_p)
    gla_sample = jnp.stack(gla_s)
    return (y_prompt, y_sample, kv128_p, kv512_p, kv2048_p, gla_prompt, kv128_s, kv512_s, kv2048_s, gla_sample)
```

```python
import functools

import numpy as np
import jax
import jax.numpy as jnp
from jax import lax
from jax.experimental import pallas as pl
from jax.experimental.pallas import tpu as pltpu

F32 = jnp.float32
BF16 = jnp.bfloat16

D_MODEL = 1024
HEAD_DIM = 64
HEADS = 8
GROUPS = ((128, 1), (512, 4), (2048, 16))
N_GROUPS = len(GROUPS)
GW = HEADS * HEAD_DIM
QKV_W = N_GROUPS * GW
ROPE_DIM = HEAD_DIM // 4
ROPE_THETA = 500000.0
BAND = 128
GLA_HEADS = 4
GLA_DK = 64
GLA_DV = 128
GLA_KW = GLA_HEADS * GLA_DK
GLA_VW = GLA_HEADS * GLA_DV
GATE_RANK = 16
GLA_TAU = 16.0
GLA_CHUNK = 64
NORM_EPS = 1e-6
PAST_LEN = 8192

LANES = 128
NEG = -1e30

OFF_Q = 0
OFF_K = OFF_Q + QKV_W
OFF_V = OFF_K + QKV_W
OFF_AZ = OFF_V + QKV_W
OFF_GQ = OFF_AZ + GW
OFF_GK = OFF_GQ + GLA_KW
OFF_GV = OFF_GK + GLA_KW
OFF_GZ = OFF_GV + GLA_VW
OFF_MA = OFF_GZ + GLA_VW
OFF_MB = OFF_MA + D_MODEL
OFF_LR = OFF_MB + D_MODEL
PROJ_W = OFF_LR + LANES
ORIG_LR = OFF_MA

VMEM_LIMIT = 56 * 1024 * 1024


def _cparams(sem):
    return pltpu.CompilerParams(dimension_semantics=sem, vmem_limit_bytes=VMEM_LIMIT)


def _dot(a, b):
    return jnp.dot(a, b, preferred_element_type=F32)


def _dot_nt(a, b):
    return lax.dot_general(a, b, (((1,), (1,)), ((), ())), preferred_element_type=F32)


def _dot_tn(a, b):
    return lax.dot_general(a, b, (((0,), (0,)), ((), ())), preferred_element_type=F32)


def _rb(x):
    return x.astype(BF16).astype(F32)


def _split3(x, dtype=BF16):
    hi = x.astype(BF16)
    r1 = x - hi.astype(F32)
    mid = r1.astype(BF16)
    lo = (r1 - mid.astype(F32)).astype(BF16)
    return hi.astype(dtype), mid.astype(dtype), lo.astype(dtype)


def _sigmoid(x):
    return 1.0 / (1.0 + jnp.exp(-x))


def _log_sigmoid(x):
    return jnp.minimum(x, 0.0) - jnp.log1p(jnp.exp(-jnp.abs(x)))


def _proj_kernel(x_ref, lnw_ref, w_ref, rc_ref, rs1_ref, rs2_ref, wa2_ref, ba_ref,
                 q_ref, k_ref, v_ref, saz_ref, sgz_ref, sa_ref, sb_ref,
                 gq_ref, gk_ref, gv_ref, la_ref, kv0_ref, kv1_ref, kv2_ref, *, tails):
    x = x_ref[0]
    ms = jnp.mean(x * x, axis=-1, keepdims=True)
    h = (x * lax.rsqrt(ms + NORM_EPS) * lnw_ref[...]).astype(BF16)
    tm = x.shape[0]
    rc, rs1, rs2 = rc_ref[...], rs1_ref[...], rs2_ref[...]

    def mm(off, width):
        return _dot(h, w_ref[:, off:off + width])

    def rope(t):
        outs = []
        for j in range(t.shape[1] // LANES):
            c = t[:, j * LANES:(j + 1) * LANES]
            outs.append(c * rc + pltpu.roll(c, ROPE_DIM // 2, axis=1) * rs1
                        + pltpu.roll(c, LANES - ROPE_DIM // 2, axis=1) * rs2)
        return jnp.concatenate(outs, axis=1)

    kv_refs = (kv0_ref, kv1_ref, kv2_ref)
    for g in range(N_GROUPS):
        sl = slice(g * GW, (g + 1) * GW)
        q_ref[0, :, sl] = (rope(mm(OFF_Q + g * GW, GW)) * (HEAD_DIM ** -0.5)).astype(BF16)
        kr = rope(mm(OFF_K + g * GW, GW))
        vv = mm(OFF_V + g * GW, GW)
        k_ref[0, :, sl] = kr.astype(BF16)
        v_ref[0, :, sl] = vv.astype(BF16)
        tb = min(tails[g], tm)
        kv_refs[g][0, :, 0:GW] = kr[tm - tb:, :]
        kv_refs[g][0, :, GW:2 * GW] = vv[tm - tb:, :]

    az = mm(OFF_AZ, GW)
    saz_ref[0] = (az * _sigmoid(az)).astype(BF16)
    gz = mm(OFF_GZ, GLA_VW)
    sgz_ref[0] = (gz * _sigmoid(gz)).astype(BF16)
    for j in range(D_MODEL // GW):
        sa_ref[0, :, j * GW:(j + 1) * GW] = _sigmoid(mm(OFF_MA + j * GW, GW)).astype(BF16)
        sb_ref[0, :, j * GW:(j + 1) * GW] = _sigmoid(mm(OFF_MB + j * GW, GW)).astype(BF16)
    gq_ref[0] = (mm(OFF_GQ, GLA_KW) * (GLA_DK ** -0.5)).astype(BF16)
    gk_ref[0] = mm(OFF_GK, GLA_KW).astype(BF16)
    gv_ref[0] = mm(OFF_GV, GLA_VW).astype(BF16)
    glr = mm(OFF_LR, LANES)
    gate_pre = _dot(glr.astype(BF16), wa2_ref[...]) + ba_ref[...]
    la_ref[0] = _log_sigmoid(gate_pre) * (1.0 / GLA_TAU)


def _proj(x, lnw, w_perm, rope_tabs, wa2, ba, tails, tm):
    B, S, D = x.shape
    assert S % tm == 0
    nt = S // tm
    tbs = [min(t, tm) for t in tails]
    for t, tb in zip(tails, tbs):
        assert t % tb == 0 and t <= S
    ntt = [t // tb for t, tb in zip(tails, tbs)]

    row = lambda w: pl.BlockSpec((1, tm, w), lambda b, s: (b, s, 0))
    const = lambda shp: pl.BlockSpec(shp, lambda b, s: (0,) * len(shp))
    tab = pl.BlockSpec((tm, LANES), lambda b, s: (s, 0))

    def kv_spec(g):
        first = nt - ntt[g]
        return pl.BlockSpec((1, tbs[g], 2 * GW), lambda b, s: (b, jnp.maximum(s - first, 0), 0))

    bf = lambda w: jax.ShapeDtypeStruct((B, S, w), BF16)
    out_shape = [bf(QKV_W), bf(QKV_W), bf(QKV_W), bf(GW), bf(GLA_VW), bf(D_MODEL), bf(D_MODEL),
                 bf(GLA_KW), bf(GLA_KW), bf(GLA_VW), jax.ShapeDtypeStruct((B, S, GLA_KW), F32)]
    out_shape += [jax.ShapeDtypeStruct((B, tails[g], 2 * GW), F32) for g in range(N_GROUPS)]
    out_specs = [row(QKV_W), row(QKV_W), row(QKV_W), row(GW), row(GLA_VW), row(D_MODEL), row(D_MODEL),
                 row(GLA_KW), row(GLA_KW), row(GLA_VW), row(GLA_KW)]
    out_specs += [kv_spec(g) for g in range(N_GROUPS)]
    in_specs = [row(D), const((1, D)),
                pl.BlockSpec((D, PROJ_W), lambda b, s: (0, 0), pipeline_mode=pl.Buffered(1)),
                tab, tab, tab, const((LANES, GLA_KW)), const((1, GLA_KW))]
    return pl.pallas_call(
        functools.partial(_proj_kernel, tails=tuple(tails)),
        grid=(B, nt), in_specs=in_specs, out_specs=out_specs, out_shape=out_shape,
        compiler_params=_cparams(("arbitrary", "arbitrary")), name="proj",
    )(x, lnw, w_perm, *rope_tabs, wa2, ba)


def _attn_kernel(q_ref, k_ref, kp_ref, v_ref, vp_ref, o_ref, lse_ref, kall, vall, *, qb):
    n = pl.program_id(2)
    nsub = qb // BAND
    kall[0:BAND, :] = kp_ref[0]
    kall[BAND:, :] = k_ref[0]
    vall[0:BAND, :] = vp_ref[0]
    vall[BAND:, :] = v_ref[0]

    qi = lax.broadcasted_iota(jnp.int32, (BAND, BAND), 0)
    ki = lax.broadcasted_iota(jnp.int32, (BAND, BAND), 1)
    mask_prev = ki >= qi
    mask_cur = ki <= qi
    lane = lax.broadcasted_iota(jnp.int32, (BAND, LANES), 1)
    low_half = lane < HEAD_DIM

    def sub_block(j, carry):
        r0 = pl.multiple_of(j * BAND, BAND)
        has_prev = (n * nsub + j) > 0
        mprev = jnp.logical_and(mask_prev, has_prev)
        lse_tile = jnp.zeros((BAND, LANES), F32)
        for hp in range(HEADS // 2):
            cs = slice(hp * LANES, (hp + 1) * LANES)
            qp = q_ref[0, pl.ds(r0, BAND), cs]
            k_prev = kall[pl.ds(r0, BAND), cs]
            k_cur = kall[pl.ds(r0 + BAND, BAND), cs]
            v_prev = vall[pl.ds(r0, BAND), cs]
            v_cur = vall[pl.ds(r0 + BAND, BAND), cs]
            o_pair = None
            for a in range(2):
                sel = low_half if a == 0 else jnp.logical_not(low_half)
                qm = jnp.where(sel, qp, jnp.zeros_like(qp))
                s_p = jnp.where(mprev, _dot_nt(qm, k_prev), NEG)
                s_c = jnp.where(mask_cur, _dot_nt(qm, k_cur), NEG)
                m = jnp.maximum(jnp.max(s_p, axis=1, keepdims=True), jnp.max(s_c, axis=1, keepdims=True))
                p_p = jnp.exp(s_p - m)
                p_c = jnp.exp(s_c - m)
                den = jnp.sum(p_p, axis=1, keepdims=True) + jnp.sum(p_c, axis=1, keepdims=True)
                o = (_dot(p_p.astype(BF16), v_prev) + _dot(p_c.astype(BF16), v_cur)) / den
                o_pair = o if a == 0 else jnp.where(low_half, o_pair, o)
                lse = m + jnp.log(den)
                lse_tile = jnp.where(lane == (2 * hp + a), lse, lse_tile)
            o_ref[0, pl.ds(r0, BAND), cs] = o_pair.astype(BF16)
        lse_ref[0, pl.ds(r0, BAND), :] = lse_tile
        return carry

    lax.fori_loop(0, nsub, sub_block, 0)


def _attn_group(q, k, v, g, qb):
    B, S, _ = q.shape
    win, dil = GROUPS[g]
    assert win // dil == BAND and S % dil == 0
    L = S // dil
    qb = min(qb, L)
    assert L % qb == 0 and qb % BAND == 0
    nsub = qb // BAND
    view = lambda t: t.reshape(B, L, dil * t.shape[-1])
    nblk = QKV_W // GW
    cur = pl.BlockSpec((1, qb, GW), lambda b, r, n: (b, n, r * nblk + g))
    prev = pl.BlockSpec((1, BAND, GW), lambda b, r, n: (b, jnp.maximum(n * nsub - 1, 0), r * nblk + g))
    o, lse = pl.pallas_call(
        functools.partial(_attn_kernel, qb=qb),
        grid=(B, dil, L // qb),
        in_specs=[cur, cur, prev, cur, prev],
        out_specs=[pl.BlockSpec((1, qb, GW), lambda b, r, n: (b, n, r)),
                   pl.BlockSpec((1, qb, LANES), lambda b, r, n: (b, n, r))],
        out_shape=[jax.ShapeDtypeStruct((B, L, dil * GW), BF16),
                   jax.ShapeDtypeStruct((B, L, dil * LANES), F32)],
        scratch_shapes=[pltpu.VMEM((qb + BAND, GW), BF16), pltpu.VMEM((qb + BAND, GW), BF16)],
        compiler_params=_cparams(("arbitrary", "arbitrary", "arbitrary")), name=f"attn_g{g}",
    )(view(q), view(k), view(k), view(v), view(v))
    return o.reshape(B, S, GW), lse.reshape(B, S, LANES)


def _gla_chunk(gq, gk, gv, la, tri3, st_pairs, normw):
    C = gq.shape[0]
    b = _dot(tri3, jnp.concatenate(_split3(la), axis=0))
    b_last = b[C - 1:C, :]
    qe = gq.astype(F32) * jnp.exp(b)
    ke = (gk.astype(F32) * jnp.exp(-b)).astype(BF16)
    kd = (gk.astype(F32) * jnp.exp(b_last - b)).astype(BF16)
    dec = jnp.exp(b_last)
    ti = lax.broadcasted_iota(jnp.int32, (C, C), 0)
    si = lax.broadcasted_iota(jnp.int32, (C, C), 1)
    causal = si <= ti
    lane = lax.broadcasted_iota(jnp.int32, (C, LANES), 1)
    low_half = lane < GLA_DK
    lane_sq = lax.broadcasted_iota(jnp.int32, (GLA_DV, LANES), 1) < GLA_DK
    outs, new_states = [], []
    for p in range(GLA_HEADS // 2):
        cs = slice(p * LANES, (p + 1) * LANES)
        st = st_pairs[p]
        st_b = st.astype(BF16)
        qe_p, ke_p, kd_p = qe[:, cs], ke[:, cs], kd[:, cs]
        upd = None
        for a in range(2):
            hh = 2 * p + a
            sel = low_half if a == 0 else jnp.logical_not(low_half)
            qm = jnp.where(sel, qe_p, 0.0).astype(BF16)
            att = jnp.where(causal, _dot_nt(qm, ke_p), 0.0).astype(BF16)
            v_h = gv[:, hh * GLA_DV:(hh + 1) * GLA_DV]
            o = _dot(att, v_h) + _dot_nt(qm, st_b)
            o = o * lax.rsqrt(jnp.mean(o * o, axis=-1, keepdims=True) + NORM_EPS) * normw
            outs.append(o.astype(BF16))
            u = _dot_tn(v_h, kd_p)
            upd = u if a == 0 else jnp.where(lane_sq, upd, u)
        new_states.append(st * dec[:, cs] + upd)
    return jnp.concatenate(outs, axis=1), new_states


def _gla_kernel(gq_ref, gk_ref, gv_ref, la_ref, tri_ref, nw_ref, o_ref, sfin_ref, st_ref, *, nchunk):
    c = pl.program_id(1)

    @pl.when(c == 0)
    def _():
        st_ref[...] = jnp.zeros_like(st_ref)

    tri3 = tri_ref[...]
    normw = nw_ref[...]

    def body(i, carry):
        r0 = pl.multiple_of(i * GLA_CHUNK, GLA_CHUNK)
        rows = pl.ds(r0, GLA_CHUNK)
        o, new_states = _gla_chunk(gq_ref[0, rows, :], gk_ref[0, rows, :], gv_ref[0, rows, :],
                                   la_ref[0, rows, :], tri3, (st_ref[0], st_ref[1]), normw)
        o_ref[0, rows, :] = o
        st_ref[0] = new_states[0]
        st_ref[1] = new_states[1]
        return carry

    lax.fori_loop(0, nchunk, body, 0)

    @pl.when(c == pl.num_programs(1) - 1)
    def _():
        sfin_ref[0, 0:LANES, :] = st_ref[0].T
        sfin_ref[0, LANES:2 * LANES, :] = st_ref[1].T


def _tri3(C, dtype=BF16):
    tri = np.tril(np.ones((C, C), np.float32))
    return jnp.asarray(np.concatenate([tri, tri, tri], axis=1), dtype)


def _gla_prompt(gq, gk, gv, la, normw, ct):
    B, S, _ = gq.shape
    ct = min(ct, S)
    assert S % ct == 0 and ct % GLA_CHUNK == 0
    row = lambda w: pl.BlockSpec((1, ct, w), lambda b, c: (b, c, 0))
    const = lambda shp: pl.BlockSpec(shp, lambda b, c: (0,) * len(shp))
    o, sfin = pl.pallas_call(
        functools.partial(_gla_kernel, nchunk=ct // GLA_CHUNK),
        grid=(B, S // ct),
        in_specs=[row(GLA_KW), row(GLA_KW), row(GLA_VW), row(GLA_KW),
                  const((GLA_CHUNK, 3 * GLA_CHUNK)), const((1, GLA_DV))],
        out_specs=[row(GLA_VW), pl.BlockSpec((1, GLA_KW, GLA_DV), lambda b, c: (b, 0, 0))],
        out_shape=[jax.ShapeDtypeStruct((B, S, GLA_VW), BF16),
                   jax.ShapeDtypeStruct((B, GLA_KW, GLA_DV), F32)],
        scratch_shapes=[pltpu.VMEM((GLA_HEADS // 2, GLA_DV, LANES), F32)],
        compiler_params=_cparams(("arbitrary", "arbitrary")), name="gla_prompt",
    )(gq, gk, gv, la, _tri3(GLA_CHUNK), normw)
    return o, sfin.reshape(B, GLA_HEADS, GLA_DK, GLA_DV)


def _merge_kernel(*refs, n_groups):
    x_ref = refs[0]
    o_refs = refs[1:1 + n_groups]
    lse_refs = refs[1 + n_groups:1 + 2 * n_groups] if n_groups > 1 else ()
    rest = refs[1 + n_groups + len(lse_refs):]
    (saz_ref, og_ref, sgz_ref, sa_ref, sb_ref, wa_ref, wg_ref, wo_ref, fnw_ref, ex_ref, y_ref) = rest

    if n_groups > 1:
        ls = [r[...] for r in lse_refs]
        mx = functools.reduce(jnp.maximum, ls)
        es = [jnp.exp(l - mx) for l in ls]
        inv = 1.0 / functools.reduce(lambda a, b: a + b, es)
        comb = None
        for e, o_ref in zip(es, o_refs):
            w = e * inv
            w_hi = w.astype(BF16)
            w_lo = (w - w_hi.astype(F32)).astype(BF16)
            wx = _dot(jnp.concatenate([w_hi, w_lo], axis=1), ex_ref[...])
            term = wx * o_ref[...].astype(F32)
            comb = term if comb is None else comb + term
    else:
        comb = o_refs[0][...].astype(F32)

    ua = (comb * saz_ref[...].astype(F32)).astype(BF16)
    ub = (og_ref[...].astype(F32) * sgz_ref[...].astype(F32)).astype(BF16)
    ya = _dot(ua, wa_ref[...])
    yb = _dot(ub, wg_ref[...])
    mixed = (sa_ref[...].astype(F32) * ya + sb_ref[...].astype(F32) * yb).astype(BF16)
    out = x_ref[...] + _dot(mixed, wo_ref[...])
    y_ref[...] = out * lax.rsqrt(jnp.mean(out * out, axis=-1, keepdims=True) + NORM_EPS) * fnw_ref[...]


def _expand_matrix():
    e = np.zeros((LANES, GW), np.float32)
    for hh in range(HEADS):
        e[hh, hh * HEAD_DIM:(hh + 1) * HEAD_DIM] = 1.0
    return jnp.asarray(np.concatenate([e, e], axis=0), BF16)


def _merge(x, o_list, lse_list, saz, og, sgz, sa, sb, wa, wg, wo, fnw, tm):
    R, D = x.shape
    tm = min(tm, R)
    assert R % tm == 0
    n_groups = len(o_list)
    row = lambda w: pl.BlockSpec((tm, w), lambda i: (i, 0))
    const = lambda shp: pl.BlockSpec(shp, lambda i: (0,) * len(shp))
    in_specs = [row(D)] + [row(GW)] * n_groups + [row(LANES)] * len(lse_list)
    in_specs += [row(GW), row(GLA_VW), row(GLA_VW), row(D), row(D),
                 const((GW, D)), const((GLA_VW, D)), const((D, D)), const((1, D)), const((2 * LANES, GW))]
    return pl.pallas_call(
        functools.partial(_merge_kernel, n_groups=n_groups),
        grid=(R // tm,), in_specs=in_specs, out_specs=row(D),
        out_shape=jax.ShapeDtypeStruct((R, D), F32),
        compiler_params=_cparams(("arbitrary",)), name=f"merge_g{n_groups}",
    )(x, *o_list, *lse_list, saz, og, sgz, sa, sb, wa, wg, wo, fnw, _expand_matrix())


def _sample_masks(T, w_lens):
    cols = HEADS * T
    t_of_col = np.arange(cols) % T

    def bias(idx, g):
        win, dil = GROUPS[g]
        dd = w_lens[g] + t_of_col[None, :] - idx[:, None]
        ok = (dd >= 0) & (dd % dil == 0) & (dd // dil <= win // dil)
        return np.where(ok, 0.0, NEG).astype(np.float32)

    cache, new = [], []
    for g, (win, dil) in enumerate(GROUPS):
        nres = _cache_residues(g, T, w_lens[g])
        if nres == dil:
            cache.append(bias(np.arange(w_lens[g]), g))
        else:
            rows = np.arange(w_lens[g] // dil)
            cache.append(np.stack([bias(rows * dil + r, g) for r in range(nres)]))
        new.append(bias(w_lens[g] + np.arange(T), g))
    return [jnp.asarray(c) for c in cache], [jnp.asarray(n) for n in new]


def _cache_residues(g, T, w_len):
    dil = GROUPS[g][1]
    if dil <= T or w_len % dil:
        return dil
    return T if dil % T == 0 else dil


def _sattn_kernel(q_ref, n0_ref, n1_ref, n2_ref, c0_ref, c1_ref, c2_ref,
                  bc0_ref, bc1_ref, bc2_ref, bn0_ref, bn1_ref, bn2_ref, sel_ref, bd_ref, o_ref, *, nres2):
    T = q_ref.shape[1]
    cols = HEADS * T
    new_refs = (n0_ref, n1_ref, n2_ref)
    pieces = []
    rowh = lax.broadcasted_iota(jnp.int32, (GW, cols), 0) // HEAD_DIM
    colh = lax.broadcasted_iota(jnp.int32, (GW, cols), 1) // T
    for g in range(N_GROUPS):
        qg = q_ref[0, :, g * GW:(g + 1) * GW].astype(F32)
        qbd32 = jnp.where(rowh == colh, _dot_tn(qg, sel_ref[...]), 0.0)
        qbd = qbd32.astype(BF16)
        kn = _rb(new_refs[g][0, :, 0:GW])
        vn = _rb(new_refs[g][0, :, GW:2 * GW])
        bn = (bn0_ref, bn1_ref, bn2_ref)[g][...]
        pieces.append((_dot(kn, qbd32) + bn, vn))
        if g < 2 or nres2 == GROUPS[2][1]:
            c_ref = (c0_ref, c1_ref, c2_ref)[g]
            bc = (bc0_ref, bc1_ref, bc2_ref)[g][...]
            pieces.append((_dot(c_ref[0, :, 0:GW].astype(BF16), qbd) + bc,
                           c_ref[0, :, GW:2 * GW].astype(BF16)))
        else:
            for r in range(nres2):
                base = r * 2 * GW
                pieces.append((_dot(c2_ref[0, :, base:base + GW].astype(BF16), qbd) + bc2_ref[r],
                               c2_ref[0, :, base + GW:base + 2 * GW].astype(BF16)))
    m = functools.reduce(jnp.maximum, [jnp.max(s, axis=0, keepdims=True) for s, _ in pieces])
    ps = [jnp.exp(s - m) for s, _ in pieces]
    den = functools.reduce(lambda a, b: a + b, [jnp.sum(p, axis=0, keepdims=True) for p in ps])
    inv = 1.0 / den
    acc = None
    for p, (_, vals) in zip(ps, pieces):
        pn = p * inv
        pn = _rb(pn) if vals.dtype == F32 else pn.astype(BF16)
        r = _dot_tn(pn, vals)
        acc = r if acc is None else acc + r
    acc = acc * bd_ref[...]
    out = acc[0:T, :]
    for hh in range(1, HEADS):
        out = out + acc[hh * T:(hh + 1) * T, :]
    o_ref[0] = out


def _sample_attn(q, kv_new, caches, T):
    DB = q.shape[0]
    w_lens = [c.shape[1] for c in caches]
    cols = HEADS * T
    bias_c, bias_n = _sample_masks(T, w_lens)
    nres2 = _cache_residues(2, T, w_lens[2])
    dil2 = GROUPS[2][1]
    per_db = lambda shp: pl.BlockSpec((1,) + shp, lambda i: (i,) + (0,) * len(shp))
    const = lambda a: pl.BlockSpec(a.shape, lambda i: (0,) * a.ndim)
    c2 = caches[2]
    if nres2 != dil2:
        c2 = c2.reshape(DB, w_lens[2] // dil2, dil2 * 2 * GW)
        c2_spec = pl.BlockSpec((1, w_lens[2] // dil2, nres2 * 2 * GW), lambda i: (i, 0, 0))
    else:
        c2_spec = per_db(c2.shape[1:])
    sel = np.zeros((T, cols), np.float32)
    sel[np.arange(cols) % T, np.arange(cols)] = 1.0
    bd = np.zeros((cols, GW), np.float32)
    for hh in range(HEADS):
        bd[hh * T:(hh + 1) * T, hh * HEAD_DIM:(hh + 1) * HEAD_DIM] = 1.0
    sel, bd = jnp.asarray(sel), jnp.asarray(bd)
    consts = bias_c + bias_n + [sel, bd]
    return pl.pallas_call(
        functools.partial(_sattn_kernel, nres2=nres2),
        grid=(DB,),
        in_specs=[per_db(q.shape[1:])] + [per_db(k.shape[1:]) for k in kv_new]
                 + [per_db(caches[0].shape[1:]), per_db(caches[1].shape[1:]), c2_spec]
                 + [const(a) for a in consts],
        out_specs=per_db((T, GW)),
        out_shape=jax.ShapeDtypeStruct((DB, T, GW), F32),
        compiler_params=_cparams(("arbitrary",)), name="sample_attn",
    )(q, *kv_new, caches[0], caches[1], c2, *consts)


def _sgla_kernel(gq_ref, gk_ref, gv_ref, la_ref, st_ref, tri_ref, nw_ref, o_ref, snew_ref, *, dbt):
    tri3 = tri_ref[...]
    normw = nw_ref[...]
    T = gq_ref.shape[1]
    ti = lax.broadcasted_iota(jnp.int32, (T, T), 0)
    si = lax.broadcasted_iota(jnp.int32, (T, T), 1)
    causal = si <= ti
    low_half = lax.broadcasted_iota(jnp.int32, (T, LANES), 1) < GLA_DK
    ones = jnp.ones((3 * T, LANES), F32)
    row_low = lax.broadcasted_iota(jnp.int32, (LANES, GLA_DV), 0) < GLA_DK

    def body(i, carry):
        la3 = jnp.concatenate(_split3(la_ref[i], F32), axis=0)
        b = _dot(tri3, la3)
        b_last = b[T - 1:T, :]
        gk = gk_ref[i].astype(F32)
        qe = gq_ref[i].astype(F32) * jnp.exp(b)
        ke = _rb(gk * jnp.exp(-b))
        kd = _rb(gk * jnp.exp(b_last - b))
        gv = gv_ref[i].astype(F32)
        outs = []
        for p in range(GLA_HEADS // 2):
            cs = slice(p * LANES, (p + 1) * LANES)
            st = st_ref[i, cs, :]
            st_b = _rb(st)
            dec = jnp.exp(_dot_tn(la3[:, cs], ones))
            upd = None
            for a in range(2):
                hh = 2 * p + a
                sel = low_half if a == 0 else jnp.logical_not(low_half)
                qm = _rb(jnp.where(sel, qe[:, cs], 0.0))
                att = _rb(jnp.where(causal, _dot_nt(qm, ke[:, cs]), 0.0))
                v_h = gv[:, hh * GLA_DV:(hh + 1) * GLA_DV]
                o = _dot(att, v_h) + _dot(qm, st_b)
                outs.append(o * lax.rsqrt(jnp.mean(o * o, axis=-1, keepdims=True) + NORM_EPS) * normw)
                u = _dot_tn(kd[:, cs], v_h)
                upd = u if a == 0 else jnp.where(row_low, upd, u)
            snew_ref[i, cs, :] = st * dec + upd
        o_ref[i] = jnp.concatenate(outs, axis=1)
        return carry

    lax.fori_loop(0, dbt, body, 0)


def _sample_gla(gq, gk, gv, la, state, normw, dbt):
    DB, T, _ = gq.shape
    dbt = min(dbt, DB)
    assert DB % dbt == 0
    st = state.reshape(DB, GLA_KW, GLA_DV)
    blk = lambda shp: pl.BlockSpec((dbt,) + shp, lambda i: (i,) + (0,) * len(shp))
    const = lambda shp: pl.BlockSpec(shp, lambda i: (0,) * len(shp))
    o, snew = pl.pallas_call(
        functools.partial(_sgla_kernel, dbt=dbt),
        grid=(DB // dbt,),
        in_specs=[blk((T, GLA_KW)), blk((T, GLA_KW)), blk((T, GLA_VW)), blk((T, GLA_KW)),
                  blk((GLA_KW, GLA_DV)), const((T, 3 * T)), const((1, GLA_DV))],
        out_specs=[blk((T, GLA_VW)), blk((GLA_KW, GLA_DV))],
        out_shape=[jax.ShapeDtypeStruct((DB, T, GLA_VW), F32),
                   jax.ShapeDtypeStruct((DB, GLA_KW, GLA_DV), F32)],
        compiler_params=_cparams(("arbitrary",)), name="gla_sample",
    )(gq, gk, gv, la, st, _tri3(T, F32), normw)
    return o, snew.reshape(DB, GLA_HEADS, GLA_DK, GLA_DV)


def _rope_tables(pos):
    half = ROPE_DIM // 2
    inv_freq = 1.0 / (ROPE_THETA ** (jnp.arange(half, dtype=F32) * (2.0 / ROPE_DIM)))
    ang = pos.astype(F32)[:, None] * inv_freq[None, :]
    cos, sin = jnp.cos(ang), jnp.sin(ang)
    n = pos.shape[0]
    pad = jnp.zeros((n, HEAD_DIM - ROPE_DIM), F32)
    zero = jnp.zeros((n, half), F32)
    c = jnp.concatenate([cos, cos, pad + 1.0], axis=1)
    s1 = jnp.concatenate([zero, sin, pad], axis=1)
    s2 = jnp.concatenate([-sin, zero, pad], axis=1)
    rep = LANES // HEAD_DIM
    return tuple(jnp.tile(t, (1, rep)) for t in (c, s1, s2))


def _permute_w_in(w):
    lr = w[:, ORIG_LR:ORIG_LR + GATE_RANK]
    pad = jnp.zeros((w.shape[0], LANES - GATE_RANK), w.dtype)
    return jnp.concatenate([w[:, :ORIG_LR], w[:, ORIG_LR + GATE_RANK:], lr, pad], axis=1).astype(BF16)


def kernel(x_prompt, x_sample, cache_kv_w128, cache_kv_w512, cache_kv_w2048, state_gla,
           ln_w, w_in, w_gla_a2, b_gla_a, gla_norm_w, w_attn_out, w_gla_out, w_out, final_norm_w):
    B, S, D = x_prompt.shape
    DB, T, _ = x_sample.shape
    depth = ln_w.shape[0]
    assert depth == 1, "single-layer step"
    caches = (cache_kv_w128, cache_kv_w512, cache_kv_w2048)

    lnw = ln_w[0].reshape(1, D)
    w_perm = _permute_w_in(w_in[0])
    wa2 = jnp.concatenate([w_gla_a2[0], jnp.zeros((LANES - GATE_RANK, GLA_KW), F32)], axis=0).astype(BF16)
    ba = b_gla_a[0].reshape(1, GLA_KW)
    gnw = gla_norm_w[0].reshape(1, GLA_DV)
    wa = w_attn_out[0].astype(BF16)
    wg = w_gla_out[0].astype(BF16)
    wo = w_out[0].astype(BF16)
    fnw = final_norm_w.reshape(1, D)

    tails_p = [min(w, S) for w, _ in GROUPS]
    tm_p = min(512, S)
    tabs_p = _rope_tables(jnp.arange(S, dtype=jnp.int32))
    (q, k, v, saz, sgz, sa, sb, gq, gk, gv, la, kv0, kv1, kv2) = _proj(
        x_prompt, lnw, w_perm, tabs_p, wa2, ba, tails_p, tm_p)
    o_list, lse_list = [], []
    for g in range(N_GROUPS):
        o, lse = _attn_group(q, k, v, g, qb=512)
        o_list.append(o.reshape(B * S, GW))
        lse_list.append(lse.reshape(B * S, LANES))
    og, gla_p = _gla_prompt(gq, gk, gv, la, gnw, ct=512)
    flat = lambda t: t.reshape(B * S, t.shape[-1])
    y_prompt = _merge(flat(x_prompt), o_list, lse_list, flat(saz), flat(og), flat(sgz), flat(sa), flat(sb),
                      wa, wg, wo, fnw, tm=512).reshape(B, S, D)
    kv_p = [t.reshape(1, B, t.shape[1], 2, HEADS, HEAD_DIM) for t in (kv0, kv1, kv2)]

    R = DB * T
    pos_s = PAST_LEN + jnp.arange(T, dtype=jnp.int32)
    tabs_s = tuple(jnp.tile(t, (DB, 1)) for t in _rope_tables(pos_s))
    (q, k, v, saz, sgz, sa, sb, gq, gk, gv, la, kv0, kv1, kv2) = _proj(
        x_sample.reshape(1, R, D), lnw, w_perm, tabs_s, wa2, ba, [R] * N_GROUPS, min(512, R))
    kv_new = [t.reshape(DB, T, 2 * GW) for t in (kv0, kv1, kv2)]
    o_s = _sample_attn(q.reshape(DB, T, QKV_W), kv_new,
                       [c[0].reshape(DB, c.shape[2], 2 * GW) for c in caches], T)
    per_db = lambda t: t.reshape(DB, T, t.shape[-1])
    og_s, gla_s = _sample_gla(per_db(gq), per_db(gk), per_db(gv), per_db(la), state_gla[0], gnw, dbt=8)
    flat_s = lambda t: t.reshape(R, t.shape[-1])
    y_sample = _merge(x_sample.reshape(R, D), [o_s.reshape(R, GW)], [], flat_s(saz), flat_s(og_s), flat_s(sgz),
                      flat_s(sa), flat_s(sb), wa, wg, wo, fnw, tm=512).reshape(DB, T, D)
    kv_s = [t.reshape(1, DB, T, 2, HEADS, HEAD_DIM) for t in kv_new]

    return (y_prompt, y_sample, kv_p[0], kv_p[1], kv_p[2], gla_p[None],
            kv_s[0], kv_s[1], kv_s[2], gla_s[None])
```

```python
import functools

import numpy as np
import jax
import jax.numpy as jnp
from jax import lax
from jax.experimental import pallas as pl
from jax.experimental.pallas import tpu as pltpu

F32 = jnp.float32
BF16 = jnp.bfloat16

D_MODEL = 1024
HEAD_DIM = 64
HEADS = 8
GROUPS = ((128, 1), (512, 4), (2048, 16))
N_GROUPS = len(GROUPS)
GW = HEADS * HEAD_DIM
QKV_W = N_GROUPS * GW
ROPE_DIM = HEAD_DIM // 4
ROPE_THETA = 500000.0
BAND = 128
GLA_HEADS = 4
GLA_DK = 64
GLA_DV = 128
GLA_KW = GLA_HEADS * GLA_DK
GLA_VW = GLA_HEADS * GLA_DV
GATE_RANK = 16
GLA_TAU = 16.0
GLA_CHUNK = 64
NORM_EPS = 1e-6
PAST_LEN = 8192

LANES = 128
NEG = -1e30

OFF_Q = 0
OFF_K = OFF_Q + QKV_W
OFF_V = OFF_K + QKV_W
OFF_AZ = OFF_V + QKV_W
OFF_GQ = OFF_AZ + GW
OFF_GK = OFF_GQ + GLA_KW
OFF_GV = OFF_GK + GLA_KW
OFF_GZ = OFF_GV + GLA_VW
OFF_MA = OFF_GZ + GLA_VW
OFF_MB = OFF_MA + D_MODEL
OFF_LR = OFF_MB + D_MODEL
PROJ_W = OFF_LR + LANES
ORIG_LR = OFF_MA

VMEM_LIMIT = 56 * 1024 * 1024


def _cparams(sem):
    return pltpu.CompilerParams(dimension_semantics=sem, vmem_limit_bytes=VMEM_LIMIT)


def _dot(a, b):
    return jnp.dot(a, b, preferred_element_type=F32)


def _dot_nt(a, b):
    return lax.dot_general(a, b, (((1,), (1,)), ((), ())), preferred_element_type=F32)


def _dot_tn(a, b):
    return lax.dot_general(a, b, (((0,), (0,)), ((), ())), preferred_element_type=F32)


def _rb(x):
    return x.astype(BF16).astype(F32)


def _split3(x, dtype=BF16):
    hi = x.astype(BF16)
    r1 = x - hi.astype(F32)
    mid = r1.astype(BF16)
    lo = (r1 - mid.astype(F32)).astype(BF16)
    return hi.astype(dtype), mid.astype(dtype), lo.astype(dtype)


def _sigmoid(x):
    return 1.0 / (1.0 + jnp.exp(-x))


def _log_sigmoid(x):
    return jnp.minimum(x, 0.0) - jnp.log1p(jnp.exp(-jnp.abs(x)))


def _rms_bf16(x, w):
    ms = jnp.mean(x * x, axis=-1, keepdims=True)
    return (x * lax.rsqrt(ms + NORM_EPS) * w).astype(BF16)


def _proj_kernel(x_ref, lnw_ref, w_ref, rc_ref, rs1_ref, rs2_ref, wa2_ref, ba_ref, *refs, dils, row_major_kv):
    q_refs, k_refs, v_refs = refs[0:3], refs[3:6], refs[6:9]
    saz_ref, sgz_ref, sa_ref, sb_ref, gq_ref, gk_ref, gv_ref, la_ref = refs[9:17]
    kv_refs = refs[17:20] if row_major_kv else ()
    stage_ref = refs[-1]
    h = _rms_bf16(x_ref[0], lnw_ref[...])
    tm = h.shape[0]
    rc, rs1, rs2 = rc_ref[...], rs1_ref[...], rs2_ref[...]

    def mm(off, width):
        return _dot(h, w_ref[:, off:off + width])

    def rope(t):
        outs = []
        for j in range(t.shape[1] // LANES):
            c = t[:, j * LANES:(j + 1) * LANES]
            outs.append(c * rc + pltpu.roll(c, ROPE_DIM // 2, axis=1) * rs1
                        + pltpu.roll(c, LANES - ROPE_DIM // 2, axis=1) * rs2)
        return jnp.concatenate(outs, axis=1)

    def store_by_residue(dst_ref, val, d):
        if d == 1:
            dst_ref[0] = val.astype(BF16)
            return
        for c in range(GW // LANES):
            stage_ref[c] = val[:, c * LANES:(c + 1) * LANES]
        for r in range(d):
            for c in range(GW // LANES):
                col = r * GW + c * LANES
                dst_ref[0, :, col:col + LANES] = stage_ref[c, pl.ds(r, tm // d, stride=d), :].astype(BF16)

    for g in range(N_GROUPS):
        qr = rope(mm(OFF_Q + g * GW, GW)) * (HEAD_DIM ** -0.5)
        store_by_residue(q_refs[g], qr, dils[g])
        kr = rope(mm(OFF_K + g * GW, GW))
        store_by_residue(k_refs[g], kr, dils[g])
        vv = mm(OFF_V + g * GW, GW)
        store_by_residue(v_refs[g], vv, dils[g])
        if row_major_kv:
            kv_refs[g][0, :, 0:GW] = kr
            kv_refs[g][0, :, GW:2 * GW] = vv

    az = mm(OFF_AZ, GW)
    saz_ref[0] = (az * _sigmoid(az)).astype(BF16)
    gz = mm(OFF_GZ, GLA_VW)
    sgz_ref[0] = (gz * _sigmoid(gz)).astype(BF16)
    for j in range(D_MODEL // GW):
        sa_ref[0, :, j * GW:(j + 1) * GW] = _sigmoid(mm(OFF_MA + j * GW, GW)).astype(BF16)
        sb_ref[0, :, j * GW:(j + 1) * GW] = _sigmoid(mm(OFF_MB + j * GW, GW)).astype(BF16)
    gq_ref[0] = (mm(OFF_GQ, GLA_KW) * (GLA_DK ** -0.5)).astype(BF16)
    gk_ref[0] = mm(OFF_GK, GLA_KW).astype(BF16)
    gv_ref[0] = mm(OFF_GV, GLA_VW).astype(BF16)
    glr = mm(OFF_LR, LANES)
    gate_pre = _dot(glr.astype(BF16), wa2_ref[...]) + ba_ref[...]
    la_ref[0] = _log_sigmoid(gate_pre) * (1.0 / GLA_TAU)


def _proj(x, lnw, w_perm, rope_tabs, wa2, ba, tm, dils, row_major_kv):
    B, S, D = x.shape
    assert S % tm == 0 and all(tm % (16 * d) == 0 for d in dils)
    row = lambda w: pl.BlockSpec((1, tm, w), lambda b, s: (b, s, 0))
    const = lambda shp: pl.BlockSpec(shp, lambda b, s: (0,) * len(shp))
    tab = pl.BlockSpec((tm, LANES), lambda b, s: (s, 0))
    bf = lambda w: jax.ShapeDtypeStruct((B, S, w), BF16)

    out_shape, out_specs = [], []
    for _ in range(3):
        for d in dils:
            out_shape.append(jax.ShapeDtypeStruct((B, S // d, d * GW), BF16))
            out_specs.append(pl.BlockSpec((1, tm // d, d * GW), lambda b, s: (b, s, 0)))
    out_shape += [bf(GW), bf(GLA_VW), bf(D_MODEL), bf(D_MODEL), bf(GLA_KW), bf(GLA_KW), bf(GLA_VW),
                  jax.ShapeDtypeStruct((B, S, GLA_KW), F32)]
    out_specs += [row(GW), row(GLA_VW), row(D_MODEL), row(D_MODEL), row(GLA_KW), row(GLA_KW), row(GLA_VW),
                  row(GLA_KW)]
    if row_major_kv:
        out_shape += [jax.ShapeDtypeStruct((B, S, 2 * GW), F32)] * N_GROUPS
        out_specs += [row(2 * GW)] * N_GROUPS
    in_specs = [row(D), const((1, D)),
                pl.BlockSpec((D, PROJ_W), lambda b, s: (0, 0), pipeline_mode=pl.Buffered(1)),
                tab, tab, tab, const((LANES, GLA_KW)), const((1, GLA_KW))]
    return pl.pallas_call(
        functools.partial(_proj_kernel, dils=tuple(dils), row_major_kv=row_major_kv),
        grid=(B, S // tm), in_specs=in_specs, out_specs=out_specs, out_shape=out_shape,
        scratch_shapes=[pltpu.VMEM((GW // LANES, tm, LANES), F32)],
        compiler_params=_cparams(("arbitrary", "arbitrary")), name="proj",
    )(x, lnw, w_perm, *rope_tabs, wa2, ba)


def _kvt_kernel(x_ref, lnw_ref, w_ref, cos_ref, sin_ref, o0_ref, o1_ref, o2_ref, *, plan, nt):
    i = pl.program_id(1)
    h = _rms_bf16(x_ref[...], lnw_ref[...])
    rows = h.shape[0]
    half = ROPE_DIM // 2
    for g, (tiles, cols) in enumerate(plan):
        o_ref = (o0_ref, o1_ref, o2_ref)[g]

        def emit(g=g, cols=cols, o_ref=o_ref):
            y = _dot_nt(w_ref[g], h[rows - cols:, :])
            c = cos_ref[:, rows - cols:]
            s = sin_ref[:, rows - cols:]
            for hd in range(HEADS):
                b0 = hd * HEAD_DIM
                x1 = y[b0:b0 + half, :]
                x2 = y[b0 + half:b0 + ROPE_DIM, :]
                o_ref[b0:b0 + half, :] = x1 * c - x2 * s
                o_ref[b0 + half:b0 + ROPE_DIM, :] = x2 * c + x1 * s
                o_ref[b0 + ROPE_DIM:b0 + HEAD_DIM, :] = y[b0 + ROPE_DIM:b0 + HEAD_DIM, :]
            o_ref[GW:2 * GW, :] = y[GW:2 * GW, :]

        if tiles == nt:
            emit()
        else:
            pl.when(i >= nt - tiles)(emit)


def _kvt_weights(w_t):
    orig_k, orig_v = QKV_W, 2 * QKV_W
    return jnp.stack([jnp.concatenate([w_t[orig_k + g * GW:orig_k + (g + 1) * GW],
                                       w_t[orig_v + g * GW:orig_v + (g + 1) * GW]], axis=0)
                      for g in range(N_GROUPS)]).astype(BF16)


def _rope_cos_sin_t(pos):
    half = ROPE_DIM // 2
    inv_freq = 1.0 / (ROPE_THETA ** (jnp.arange(half, dtype=F32) * (2.0 / ROPE_DIM)))
    ang = pos.astype(F32)[:, None] * inv_freq[None, :]
    return jnp.cos(ang).T, jnp.sin(ang).T


def _kv_tail_prompt(x, lnw, w_kvt, S_tails, tile):
    B, S, D = x.shape
    tile = min(tile, S)
    span = max(S_tails)
    assert span % tile == 0 and S % tile == 0
    nt = span // tile
    first = (S - span) // tile
    plan = []
    for t in S_tails:
        assert t % tile == 0 or t < tile
        plan.append((t // tile, tile) if t >= tile else (1, t))
    cos_t, sin_t = _rope_cos_sin_t(jnp.arange(S, dtype=jnp.int32))
    tabspec = pl.BlockSpec((ROPE_DIM // 2, tile), lambda b, i: (0, first + i))
    out_specs = [pl.BlockSpec((None, 2 * GW, cols), lambda b, i, t=tiles: (b, 0, jnp.maximum(i - (nt - t), 0)))
                 for tiles, cols in plan]
    return pl.pallas_call(
        functools.partial(_kvt_kernel, plan=tuple(plan), nt=nt),
        grid=(B, nt),
        in_specs=[pl.BlockSpec((None, tile, D), lambda b, i: (b, first + i, 0)),
                  pl.BlockSpec((1, D), lambda b, i: (0, 0)),
                  pl.BlockSpec((N_GROUPS, 2 * GW, D), lambda b, i: (0, 0, 0), pipeline_mode=pl.Buffered(1)),
                  tabspec, tabspec],
        out_specs=out_specs,
        out_shape=[jax.ShapeDtypeStruct((B, 2 * GW, t), F32) for t in S_tails],
        compiler_params=_cparams(("arbitrary", "arbitrary")), name="kv_tail_prompt",
    )(x, lnw, w_kvt, cos_t, sin_t)


def _kv_tail_sample(x, lnw, w_kvt, pos):
    DB, T, D = x.shape
    cos_t, sin_t = _rope_cos_sin_t(pos)
    bcast = lambda t: jnp.broadcast_to(t.T[:, :, None], (T, ROPE_DIM // 2, DB))
    tabspec = pl.BlockSpec((None, ROPE_DIM // 2, DB), lambda b, i: (i, 0, 0))
    out_spec = pl.BlockSpec((None, 2 * GW, DB), lambda b, i: (i, 0, 0))
    return pl.pallas_call(
        functools.partial(_kvt_kernel, plan=((T, DB),) * N_GROUPS, nt=T),
        grid=(1, T),
        in_specs=[pl.BlockSpec((None, DB, D), lambda b, i: (i, 0, 0)),
                  pl.BlockSpec((1, D), lambda b, i: (0, 0)),
                  pl.BlockSpec((N_GROUPS, 2 * GW, D), lambda b, i: (0, 0, 0), pipeline_mode=pl.Buffered(1)),
                  tabspec, tabspec],
        out_specs=[out_spec] * N_GROUPS,
        out_shape=[jax.ShapeDtypeStruct((T, 2 * GW, DB), F32)] * N_GROUPS,
        compiler_params=_cparams(("arbitrary", "arbitrary")), name="kv_tail_sample",
    )(x.transpose(1, 0, 2), lnw, w_kvt, bcast(cos_t), bcast(sin_t))


def _band_bias():
    q = np.arange(BAND)[:, None]
    c = np.arange(2 * BAND)[None, :]
    ok = np.where(c < BAND, c >= q, (c - BAND) <= q)
    ok_first = ok & (c >= BAND)
    return jnp.asarray(np.stack([np.where(ok, 0.0, NEG), np.where(ok_first, 0.0, NEG)]).astype(np.float32))


def _attn_kernel(q_ref, k_ref, kp_ref, v_ref, vp_ref, bias_ref, o_ref, lse_ref, kall, vall, *, qb):
    n = pl.program_id(2)
    nsub = qb // BAND
    npair = HEADS // 2
    kall[0:BAND, :] = kp_ref[0]
    kall[BAND:, :] = k_ref[0]
    vall[0:BAND, :] = vp_ref[0]
    vall[BAND:, :] = v_ref[0]
    lane = lax.broadcasted_iota(jnp.int32, (BAND, LANES), 1)
    low_half = lane < HEAD_DIM

    def sub_block(j, carry):
        r0 = pl.multiple_of(j * BAND, BAND)
        first = (n * nsub + j) == 0
        bias = bias_ref[jnp.where(first, 1, 0)]
        bias2 = jnp.concatenate([bias, bias], axis=0)
        scores = []
        for hp in range(npair):
            cs = slice(hp * LANES, (hp + 1) * LANES)
            qp = q_ref[0, pl.ds(r0, BAND), cs]
            zero = jnp.zeros_like(qp)
            qm = jnp.concatenate([jnp.where(low_half, qp, zero), jnp.where(low_half, zero, qp)], axis=0)
            scores.append(_dot_nt(qm, kall[pl.ds(r0, 2 * BAND), cs]) + bias2)
        probs, stats = [], []
        for s in scores:
            m = jnp.max(s, axis=1, keepdims=True)
            p = jnp.exp(s - m)
            stats.append((m, jnp.sum(p, axis=1, keepdims=True)))
            probs.append(p.astype(BF16))
        lse_tile = jnp.zeros((BAND, LANES), F32)
        for hp in range(npair):
            cs = slice(hp * LANES, (hp + 1) * LANES)
            m, den = stats[hp]
            o = _dot(probs[hp], vall[pl.ds(r0, 2 * BAND), cs]) / den
            o_ref[0, pl.ds(r0, BAND), cs] = jnp.where(low_half, o[0:BAND], o[BAND:]).astype(BF16)
            lse = m + jnp.log(den)
            lse_tile = jnp.where(lane == 2 * hp, lse[0:BAND], lse_tile)
            lse_tile = jnp.where(lane == 2 * hp + 1, lse[BAND:], lse_tile)
        lse_ref[0, pl.ds(r0, BAND), :] = lse_tile
        return carry

    lax.fori_loop(0, nsub, sub_block, 0)


def _attn_group(q, k, v, g, qb):
    B, L, _ = q.shape
    win, dil = GROUPS[g]
    assert win // dil == BAND
    qb = min(qb, L)
    assert L % qb == 0 and qb % BAND == 0
    nsub = qb // BAND
    cur = pl.BlockSpec((1, qb, GW), lambda b, r, n: (b, n, r))
    prev = pl.BlockSpec((1, BAND, GW), lambda b, r, n: (b, jnp.maximum(n * nsub - 1, 0), r))
    return pl.pallas_call(
        functools.partial(_attn_kernel, qb=qb),
        grid=(B, dil, L // qb),
        in_specs=[cur, cur, prev, cur, prev, pl.BlockSpec((2, BAND, 2 * BAND), lambda b, r, n: (0, 0, 0))],
        out_specs=[pl.BlockSpec((1, qb, GW), lambda b, r, n: (b, n, r)),
                   pl.BlockSpec((1, qb, LANES), lambda b, r, n: (b, n, r))],
        out_shape=[jax.ShapeDtypeStruct((B, L, dil * GW), BF16),
                   jax.ShapeDtypeStruct((B, L, dil * LANES), F32)],
        scratch_shapes=[pltpu.VMEM((qb + BAND, GW), BF16), pltpu.VMEM((qb + BAND, GW), BF16)],
        compiler_params=_cparams(("arbitrary", "arbitrary", "arbitrary")), name=f"attn_g{g}",
    )(q, k, k, v, v, _band_bias())


def _gla_chunk(gq, gk, gv, la, tri3, st_pairs, normw):
    C = gq.shape[0]
    b = _dot(tri3, jnp.concatenate(_split3(la), axis=0))
    b_last = b[C - 1:C, :]
    qe = gq.astype(F32) * jnp.exp(b)
    ke = (gk.astype(F32) * jnp.exp(-b)).astype(BF16)
    kd = (gk.astype(F32) * jnp.exp(b_last - b)).astype(BF16)
    dec = jnp.exp(b_last)
    ti = lax.broadcasted_iota(jnp.int32, (C, C), 0)
    si = lax.broadcasted_iota(jnp.int32, (C, C), 1)
    causal = si <= ti
    lane = lax.broadcasted_iota(jnp.int32, (C, LANES), 1)
    low_half = lane < GLA_DK
    lane_sq = lax.broadcasted_iota(jnp.int32, (GLA_DV, LANES), 1) < GLA_DK
    outs, new_states = [], []
    for p in range(GLA_HEADS // 2):
        cs = slice(p * LANES, (p + 1) * LANES)
        st = st_pairs[p]
        st_b = st.astype(BF16)
        qe_p, ke_p, kd_p = qe[:, cs], ke[:, cs], kd[:, cs]
        upd = None
        for a in range(2):
            hh = 2 * p + a
            sel = low_half if a == 0 else jnp.logical_not(low_half)
            qm = jnp.where(sel, qe_p, 0.0).astype(BF16)
            att = jnp.where(causal, _dot_nt(qm, ke_p), 0.0).astype(BF16)
            v_h = gv[:, hh * GLA_DV:(hh + 1) * GLA_DV]
            o = _dot(att, v_h) + _dot_nt(qm, st_b)
            o = o * lax.rsqrt(jnp.mean(o * o, axis=-1, keepdims=True) + NORM_EPS) * normw
            outs.append(o.astype(BF16))
            u = _dot_tn(v_h, kd_p)
            upd = u if a == 0 else jnp.where(lane_sq, upd, u)
        new_states.append(st * dec[:, cs] + upd)
    return jnp.concatenate(outs, axis=1), new_states


def _gla_kernel(gq_ref, gk_ref, gv_ref, la_ref, tri_ref, nw_ref, o_ref, sfin_ref, st_ref, *, nchunk):
    c = pl.program_id(1)

    @pl.when(c == 0)
    def _():
        st_ref[...] = jnp.zeros_like(st_ref)

    tri3 = tri_ref[...]
    normw = nw_ref[...]

    def body(i, carry):
        r0 = pl.multiple_of(i * GLA_CHUNK, GLA_CHUNK)
        rows = pl.ds(r0, GLA_CHUNK)
        o, new_states = _gla_chunk(gq_ref[0, rows, :], gk_ref[0, rows, :], gv_ref[0, rows, :],
                                   la_ref[0, rows, :], tri3, (st_ref[0], st_ref[1]), normw)
        o_ref[0, rows, :] = o
        st_ref[0] = new_states[0]
        st_ref[1] = new_states[1]
        return carry

    lax.fori_loop(0, nchunk, body, 0)

    @pl.when(c == pl.num_programs(1) - 1)
    def _():
        sfin_ref[0, 0:LANES, :] = st_ref[0].T
        sfin_ref[0, LANES:2 * LANES, :] = st_ref[1].T


def _tri3(C, dtype=BF16):
    tri = np.tril(np.ones((C, C), np.float32))
    return jnp.asarray(np.concatenate([tri, tri, tri], axis=1), dtype)


def _gla_prompt(gq, gk, gv, la, normw, ct):
    B, S, _ = gq.shape
    ct = min(ct, S)
    assert S % ct == 0 and ct % GLA_CHUNK == 0
    row = lambda w: pl.BlockSpec((1, ct, w), lambda b, c: (b, c, 0))
    const = lambda shp: pl.BlockSpec(shp, lambda b, c: (0,) * len(shp))
    o, sfin = pl.pallas_call(
        functools.partial(_gla_kernel, nchunk=ct // GLA_CHUNK),
        grid=(B, S // ct),
        in_specs=[row(GLA_KW), row(GLA_KW), row(GLA_VW), row(GLA_KW),
                  const((GLA_CHUNK, 3 * GLA_CHUNK)), const((1, GLA_DV))],
        out_specs=[row(GLA_VW), pl.BlockSpec((1, GLA_KW, GLA_DV), lambda b, c: (b, 0, 0))],
        out_shape=[jax.ShapeDtypeStruct((B, S, GLA_VW), BF16),
                   jax.ShapeDtypeStruct((B, GLA_KW, GLA_DV), F32)],
        scratch_shapes=[pltpu.VMEM((GLA_HEADS // 2, GLA_DV, LANES), F32)],
        compiler_params=_cparams(("arbitrary", "arbitrary")), name="gla_prompt",
    )(gq, gk, gv, la, _tri3(GLA_CHUNK), normw)
    return o, sfin.reshape(B, GLA_HEADS, GLA_DK, GLA_DV)


def _merge_kernel(*refs, dils):
    n_groups = len(dils)
    x_ref = refs[0]
    o_refs = refs[1:1 + n_groups]
    lse_refs = refs[1 + n_groups:1 + 2 * n_groups] if n_groups > 1 else ()
    rest = refs[1 + n_groups + len(lse_refs):]
    (saz_ref, og_ref, sgz_ref, sa_ref, sb_ref, wa_ref, wg_ref, wo_ref, fnw_ref, ex_ref, y_ref) = rest[:11]
    scratch = rest[11:]
    tm = x_ref.shape[0]

    def by_position(ref, width, d, stage_ref):
        if d == 1:
            return ref[...].astype(F32)
        planes = width // LANES
        for r in range(d):
            blk = ref[:, r * width:(r + 1) * width].astype(F32)
            for c in range(planes):
                stage_ref[c, pl.ds(r, tm // d, stride=d), :] = blk[:, c * LANES:(c + 1) * LANES]
        return jnp.concatenate([stage_ref[c] for c in range(planes)], axis=1)

    if n_groups > 1:
        ls = [by_position(lse_refs[g], LANES, dils[g], scratch[2 * g + 1]) for g in range(n_groups)]
        mx = functools.reduce(jnp.maximum, ls)
        es = [jnp.exp(l - mx) for l in ls]
        inv = 1.0 / functools.reduce(lambda a, b: a + b, es)
        comb = None
        for g in range(n_groups):
            w = es[g] * inv
            w_hi = w.astype(BF16)
            w_lo = (w - w_hi.astype(F32)).astype(BF16)
            wx = _dot(jnp.concatenate([w_hi, w_lo], axis=1), ex_ref[...])
            term = wx * by_position(o_refs[g], GW, dils[g], scratch[2 * g])
            comb = term if comb is None else comb + term
    else:
        comb = o_refs[0][...].astype(F32)

    ua = (comb * saz_ref[...].astype(F32)).astype(BF16)
    ub = (og_ref[...].astype(F32) * sgz_ref[...].astype(F32)).astype(BF16)
    ya = _dot(ua, wa_ref[...])
    yb = _dot(ub, wg_ref[...])
    mixed = (sa_ref[...].astype(F32) * ya + sb_ref[...].astype(F32) * yb).astype(BF16)
    out = x_ref[...] + _dot(mixed, wo_ref[...])
    y_ref[...] = out * lax.rsqrt(jnp.mean(out * out, axis=-1, keepdims=True) + NORM_EPS) * fnw_ref[...]


def _expand_matrix():
    e = np.zeros((LANES, GW), np.float32)
    for hh in range(HEADS):
        e[hh, hh * HEAD_DIM:(hh + 1) * HEAD_DIM] = 1.0
    return jnp.asarray(np.concatenate([e, e], axis=0), BF16)


def _merge(x, o_list, lse_list, dils, saz, og, sgz, sa, sb, wa, wg, wo, fnw, tm):
    R, D = x.shape
    tm = min(tm, R)
    assert R % tm == 0
    n_groups = len(o_list)
    row = lambda w: pl.BlockSpec((tm, w), lambda i: (i, 0))
    const = lambda shp: pl.BlockSpec(shp, lambda i: (0,) * len(shp))
    in_specs = [row(D)]
    in_specs += [pl.BlockSpec((tm // d, d * GW), lambda i: (i, 0)) for d in dils]
    in_specs += [pl.BlockSpec((tm // d, d * LANES), lambda i: (i, 0)) for d in dils[:len(lse_list)]]
    in_specs += [row(GW), row(GLA_VW), row(GLA_VW), row(D), row(D),
                 const((GW, D)), const((GLA_VW, D)), const((D, D)), const((1, D)), const((2 * LANES, GW))]
    scratch = []
    if n_groups > 1:
        for _ in dils:
            scratch += [pltpu.VMEM((GW // LANES, tm, LANES), F32), pltpu.VMEM((1, tm, LANES), F32)]
    return pl.pallas_call(
        functools.partial(_merge_kernel, dils=tuple(dils)),
        grid=(R // tm,), in_specs=in_specs, out_specs=row(D),
        out_shape=jax.ShapeDtypeStruct((R, D), F32), scratch_shapes=scratch,
        compiler_params=_cparams(("arbitrary",)), name=f"merge_g{n_groups}",
    )(x, *o_list, *lse_list, saz, og, sgz, sa, sb, wa, wg, wo, fnw, _expand_matrix())


def _sample_masks(T, w_lens):
    t_of_row = np.arange(HEADS * T) % T

    def bias(idx, g):
        win, dil = GROUPS[g]
        dd = w_lens[g] + t_of_row[:, None] - idx[None, :]
        ok = (dd >= 0) & (dd % dil == 0) & (dd // dil <= win // dil)
        return jnp.asarray(np.where(ok, 0.0, NEG).astype(np.float32))

    cache = [bias(np.arange(w_lens[g]), g) for g in range(N_GROUPS)]
    new = [bias(w_lens[g] + np.arange(T), g) for g in range(N_GROUPS)]
    return cache, new


def _sattn_kernel(q0_ref, q1_ref, q2_ref, n0_ref, n1_ref, n2_ref, c0_ref, c1_ref, c2_ref,
                  bc0_ref, bc1_ref, bc2_ref, bn0_ref, bn1_ref, bn2_ref, bd_ref, o_ref):
    T = q0_ref.shape[1]
    bd = bd_ref[...]
    q_refs, new_refs, c_refs = (q0_ref, q1_ref, q2_ref), (n0_ref, n1_ref, n2_ref), (c0_ref, c1_ref, c2_ref)
    bc_refs, bn_refs = (bc0_ref, bc1_ref, bc2_ref), (bn0_ref, bn1_ref, bn2_ref)
    s_new, s_old, v_new, v_old = [], [], [], []
    for g in range(N_GROUPS):
        qbd = jnp.tile(q_refs[g][0].astype(F32), (HEADS, 1)) * bd
        s_new.append(_dot_nt(qbd, _rb(new_refs[g][0, :, 0:GW])) + bn_refs[g][...])
        v_new.append(_rb(new_refs[g][0, :, GW:2 * GW]))
        s_old.append(_dot(qbd.astype(BF16), c_refs[g][0, 0:GW, :].astype(BF16)) + bc_refs[g][...])
        v_old.append(c_refs[g][0, GW:2 * GW, :].astype(BF16))
    m = functools.reduce(jnp.maximum, [jnp.max(s, axis=1, keepdims=True) for s in s_new + s_old])
    p_new = [jnp.exp(s - m) for s in s_new]
    p_old = [jnp.exp(s - m) for s in s_old]
    den = functools.reduce(lambda a, b: a + b, [jnp.sum(p, axis=1, keepdims=True) for p in p_new + p_old])
    inv = 1.0 / den
    acc = None
    for g in range(N_GROUPS):
        r = _dot(_rb(p_new[g] * inv), v_new[g]) + _dot_nt((p_old[g] * inv).astype(BF16), v_old[g])
        acc = r if acc is None else acc + r
    acc = acc * bd
    out = acc[0:T, :]
    for hh in range(1, HEADS):
        out = out + acc[hh * T:(hh + 1) * T, :]
    o_ref[0] = out


def _sample_attn(q_list, kv_new, caches_t, T):
    DB = q_list[0].shape[0]
    w_lens = [c.shape[2] for c in caches_t]
    bias_c, bias_n = _sample_masks(T, w_lens)
    bd = np.zeros((HEADS * T, GW), np.float32)
    for hh in range(HEADS):
        bd[hh * T:(hh + 1) * T, hh * HEAD_DIM:(hh + 1) * HEAD_DIM] = 1.0
    consts = bias_c + bias_n + [jnp.asarray(bd)]
    per_db = lambda a: pl.BlockSpec((1,) + a.shape[1:], lambda i: (i,) + (0,) * (a.ndim - 1))
    const = lambda a: pl.BlockSpec(a.shape, lambda i: (0,) * a.ndim)
    args = list(q_list) + list(kv_new) + list(caches_t)
    return pl.pallas_call(
        _sattn_kernel, grid=(DB,),
        in_specs=[per_db(a) for a in args] + [const(a) for a in consts],
        out_specs=pl.BlockSpec((1, T, GW), lambda i: (i, 0, 0)),
        out_shape=jax.ShapeDtypeStruct((DB, T, GW), F32),
        compiler_params=_cparams(("arbitrary",)), name="sample_attn",
    )(*args, *consts)


def _sgla_kernel(gq_ref, gk_ref, gv_ref, la_ref, st_ref, tri_ref, nw_ref, o_ref, snew_ref, *, dbt):
    tri3 = tri_ref[...]
    normw = nw_ref[...]
    T = gq_ref.shape[1]
    ti = lax.broadcasted_iota(jnp.int32, (T, T), 0)
    si = lax.broadcasted_iota(jnp.int32, (T, T), 1)
    causal = si <= ti
    low_half = lax.broadcasted_iota(jnp.int32, (T, LANES), 1) < GLA_DK
    ones = jnp.ones((3 * T, LANES), F32)
    row_low = lax.broadcasted_iota(jnp.int32, (LANES, GLA_DV), 0) < GLA_DK

    def body(i, carry):
        la3 = jnp.concatenate(_split3(la_ref[i], F32), axis=0)
        b = _dot(tri3, la3)
        b_last = b[T - 1:T, :]
        gk = gk_ref[i].astype(F32)
        qe = gq_ref[i].astype(F32) * jnp.exp(b)
        ke = _rb(gk * jnp.exp(-b))
        kd = _rb(gk * jnp.exp(b_last - b))
        gv = gv_ref[i].astype(F32)
        outs = []
        for p in range(GLA_HEADS // 2):
            cs = slice(p * LANES, (p + 1) * LANES)
            st = st_ref[i, cs, :]
            st_b = _rb(st)
            dec = jnp.exp(_dot_tn(la3[:, cs], ones))
            upd = None
            for a in range(2):
                hh = 2 * p + a
                sel = low_half if a == 0 else jnp.logical_not(low_half)
                qm = _rb(jnp.where(sel, qe[:, cs], 0.0))
                att = _rb(jnp.where(causal, _dot_nt(qm, ke[:, cs]), 0.0))
                v_h = gv[:, hh * GLA_DV:(hh + 1) * GLA_DV]
                o = _dot(att, v_h) + _dot(qm, st_b)
                outs.append(o * lax.rsqrt(jnp.mean(o * o, axis=-1, keepdims=True) + NORM_EPS) * normw)
                u = _dot_tn(kd[:, cs], v_h)
                upd = u if a == 0 else jnp.where(row_low, upd, u)
            snew_ref[i, cs, :] = st * dec + upd
        o_ref[i] = jnp.concatenate(outs, axis=1)
        return carry

    lax.fori_loop(0, dbt, body, 0)


def _sample_gla(gq, gk, gv, la, state, normw, dbt):
    DB, T, _ = gq.shape
    dbt = min(dbt, DB)
    assert DB % dbt == 0
    st = state.reshape(DB, GLA_KW, GLA_DV)
    blk = lambda shp: pl.BlockSpec((dbt,) + shp, lambda i: (i,) + (0,) * len(shp))
    const = lambda shp: pl.BlockSpec(shp, lambda i: (0,) * len(shp))
    o, snew = pl.pallas_call(
        functools.partial(_sgla_kernel, dbt=dbt),
        grid=(DB // dbt,),
        in_specs=[blk((T, GLA_KW)), blk((T, GLA_KW)), blk((T, GLA_VW)), blk((T, GLA_KW)),
                  blk((GLA_KW, GLA_DV)), const((T, 3 * T)), const((1, GLA_DV))],
        out_specs=[blk((T, GLA_VW)), blk((GLA_KW, GLA_DV))],
        out_shape=[jax.ShapeDtypeStruct((DB, T, GLA_VW), F32),
                   jax.ShapeDtypeStruct((DB, GLA_KW, GLA_DV), F32)],
        compiler_params=_cparams(("arbitrary",)), name="gla_sample",
    )(gq, gk, gv, la, st, _tri3(T, F32), normw)
    return o, snew.reshape(DB, GLA_HEADS, GLA_DK, GLA_DV)


def _rope_tables(pos):
    half = ROPE_DIM // 2
    inv_freq = 1.0 / (ROPE_THETA ** (jnp.arange(half, dtype=F32) * (2.0 / ROPE_DIM)))
    ang = pos.astype(F32)[:, None] * inv_freq[None, :]
    cos, sin = jnp.cos(ang), jnp.sin(ang)
    n = pos.shape[0]
    pad = jnp.zeros((n, HEAD_DIM - ROPE_DIM), F32)
    zero = jnp.zeros((n, half), F32)
    c = jnp.concatenate([cos, cos, pad + 1.0], axis=1)
    s1 = jnp.concatenate([zero, sin, pad], axis=1)
    s2 = jnp.concatenate([-sin, zero, pad], axis=1)
    rep = LANES // HEAD_DIM
    return tuple(jnp.tile(t, (1, rep)) for t in (c, s1, s2))


def _permute_w_in(w):
    lr = w[:, ORIG_LR:ORIG_LR + GATE_RANK]
    pad = jnp.zeros((w.shape[0], LANES - GATE_RANK), w.dtype)
    return jnp.concatenate([w[:, :ORIG_LR], w[:, ORIG_LR + GATE_RANK:], lr, pad], axis=1).astype(BF16)


def kernel(x_prompt, x_sample, cache_kv_w128, cache_kv_w512, cache_kv_w2048, state_gla,
           ln_w, w_in, w_gla_a2, b_gla_a, gla_norm_w, w_attn_out, w_gla_out, w_out, final_norm_w):
    B, S, D = x_prompt.shape
    DB, T, _ = x_sample.shape
    assert ln_w.shape[0] == 1, "single-layer step"
    caches = (cache_kv_w128, cache_kv_w512, cache_kv_w2048)
    dils = [d for _, d in GROUPS]

    lnw = ln_w[0].reshape(1, D)
    w_perm = _permute_w_in(w_in[0])
    w_kvt = _kvt_weights(w_in[0].T)
    wa2 = jnp.concatenate([w_gla_a2[0], jnp.zeros((LANES - GATE_RANK, GLA_KW), F32)], axis=0).astype(BF16)
    ba = b_gla_a[0].reshape(1, GLA_KW)
    gnw = gla_norm_w[0].reshape(1, GLA_DV)
    wa = w_attn_out[0].astype(BF16)
    wg = w_gla_out[0].astype(BF16)
    wo = w_out[0].astype(BF16)
    fnw = final_norm_w.reshape(1, D)

    tabs_p = _rope_tables(jnp.arange(S, dtype=jnp.int32))
    outs = _proj(x_prompt, lnw, w_perm, tabs_p, wa2, ba, min(512, S), dils, row_major_kv=False)
    saz, sgz, sa, sb, gq, gk, gv, la = outs[9:17]
    o_list, lse_list = [], []
    for g in range(N_GROUPS):
        o, lse = _attn_group(outs[g], outs[3 + g], outs[6 + g], g, qb=512)
        o_list.append(o.reshape(-1, o.shape[-1]))
        lse_list.append(lse.reshape(-1, lse.shape[-1]))
    og, gla_p = _gla_prompt(gq, gk, gv, la, gnw, ct=512)
    flat = lambda t: t.reshape(B * S, t.shape[-1])
    y_prompt = _merge(flat(x_prompt), o_list, lse_list, dils, flat(saz), flat(og), flat(sgz), flat(sa),
                      flat(sb), wa, wg, wo, fnw, tm=512).reshape(B, S, D)
    kvt = _kv_tail_prompt(x_prompt, lnw, w_kvt, [min(w, S) for w, _ in GROUPS], tile=512)
    kv_p = [t.reshape(B, 2, HEADS, HEAD_DIM, t.shape[-1]).transpose(0, 4, 1, 2, 3)[None] for t in kvt]

    R = DB * T
    pos_s = PAST_LEN + jnp.arange(T, dtype=jnp.int32)
    tabs_s = tuple(jnp.tile(t, (DB, 1)) for t in _rope_tables(pos_s))
    outs = _proj(x_sample.reshape(1, R, D), lnw, w_perm, tabs_s, wa2, ba, min(512, R), (1,) * N_GROUPS,
                 row_major_kv=True)
    saz, sgz, sa, sb, gq, gk, gv, la = outs[9:17]
    per_db = lambda t: t.reshape(DB, T, t.shape[-1])
    caches_t = [c[0].transpose(0, 2, 3, 4, 1).reshape(DB, 2 * GW, c.shape[2]) for c in caches]
    o_s = _sample_attn([per_db(q) for q in outs[0:3]], [per_db(t) for t in outs[17:20]], caches_t, T)
    og_s, gla_s = _sample_gla(per_db(gq), per_db(gk), per_db(gv), per_db(la), state_gla[0], gnw, dbt=8)
    flat_s = lambda t: t.reshape(R, t.shape[-1])
    y_sample = _merge(x_sample.reshape(R, D), [o_s.reshape(R, GW)], [], (1,), flat_s(saz), flat_s(og_s),
                      flat_s(sgz), flat_s(sa), flat_s(sb), wa, wg, wo, fnw, tm=512).reshape(DB, T, D)
    kvt_s = _kv_tail_sample(x_sample, lnw, w_kvt, pos_s)
    kv_s = [t.reshape(T, 2, HEADS, HEAD_DIM, DB).transpose(4, 0, 1, 2, 3)[None] for t in kvt_s]

    return (y_prompt, y_sample, kv_p[0], kv_p[1], kv_p[2], gla_p[None],
            kv_s[0], kv_s[1], kv_s[2], gla_s[None])
```

```python
import functools

import numpy as np
import jax
import jax.numpy as jnp
from jax import lax
from jax.experimental import pallas as pl
from jax.experimental.pallas import tpu as pltpu

F32 = jnp.float32
BF16 = jnp.bfloat16

D_MODEL = 1024
HEAD_DIM = 64
HEADS = 8
GROUPS = ((128, 1), (512, 4), (2048, 16))
N_GROUPS = len(GROUPS)
GW = HEADS * HEAD_DIM
QKV_W = N_GROUPS * GW
ROPE_DIM = HEAD_DIM // 4
ROPE_THETA = 500000.0
BAND = 128
GLA_HEADS = 4
GLA_DK = 64
GLA_DV = 128
GLA_KW = GLA_HEADS * GLA_DK
GLA_VW = GLA_HEADS * GLA_DV
GATE_RANK = 16
GLA_TAU = 16.0
GLA_CHUNK = 64
NORM_EPS = 1e-6
PAST_LEN = 8192

LANES = 128
NEG = -1e30

OFF_Q = 0
OFF_K = OFF_Q + QKV_W
OFF_V = OFF_K + QKV_W
OFF_AZ = OFF_V + QKV_W
OFF_GQ = OFF_AZ + GW
OFF_GK = OFF_GQ + GLA_KW
OFF_GV = OFF_GK + GLA_KW
OFF_GZ = OFF_GV + GLA_VW
OFF_LR = OFF_GZ + GLA_VW
OFF_MA = OFF_LR + GATE_RANK
OFF_MB = OFF_MA + D_MODEL
PROJ_W = OFF_MB + D_MODEL
assert all(o % 16 == 0 for o in (OFF_LR, OFF_MA, OFF_MB, PROJ_W))

VMEM_LIMIT = 56 * 1024 * 1024


def _cparams(sem):
    return pltpu.CompilerParams(dimension_semantics=sem, vmem_limit_bytes=VMEM_LIMIT)


def _dot(a, b):
    return jnp.dot(a, b, preferred_element_type=F32)


def _dot_nt(a, b):
    return lax.dot_general(a, b, (((1,), (1,)), ((), ())), preferred_element_type=F32)


def _dot_tn(a, b):
    return lax.dot_general(a, b, (((0,), (0,)), ((), ())), preferred_element_type=F32)


def _rb(x):
    return x.astype(BF16).astype(F32)


def _split3(x, dtype=BF16):
    hi = x.astype(BF16)
    r1 = x - hi.astype(F32)
    mid = r1.astype(BF16)
    lo = (r1 - mid.astype(F32)).astype(BF16)
    return hi.astype(dtype), mid.astype(dtype), lo.astype(dtype)


def _sigmoid(x):
    return 1.0 / (1.0 + jnp.exp(-x))


def _log_sigmoid(x):
    return jnp.minimum(x, 0.0) - jnp.log1p(jnp.exp(-jnp.abs(x)))


def _rms_bf16(x, w):
    ms = jnp.mean(x * x, axis=-1, keepdims=True)
    return (x * lax.rsqrt(ms + NORM_EPS) * w).astype(BF16)


def _proj_kernel(x_ref, lnw_ref, w_ref, rc_ref, rs1_ref, rs2_ref, wa2_ref, ba_ref, *refs, dils, row_major_kv):
    q_refs, k_refs, v_refs = refs[0:3], refs[3:6], refs[6:9]
    saz_ref, sgz_ref, sa_ref, sb_ref, gq_ref, gk_ref, gv_ref, la_ref = refs[9:17]
    kv_refs = refs[17:20] if row_major_kv else ()
    stage_ref = refs[-1]
    h = _rms_bf16(x_ref[0], lnw_ref[...])
    tm = h.shape[0]
    rc, rs1, rs2 = rc_ref[...], rs1_ref[...], rs2_ref[...]

    def mm(off, width):
        return _dot_nt(h, w_ref[off:off + width, :])

    def rope(t):
        outs = []
        for j in range(t.shape[1] // LANES):
            c = t[:, j * LANES:(j + 1) * LANES]
            outs.append(c * rc + pltpu.roll(c, ROPE_DIM // 2, axis=1) * rs1
                        + pltpu.roll(c, LANES - ROPE_DIM // 2, axis=1) * rs2)
        return jnp.concatenate(outs, axis=1)

    def store_by_residue(dst_ref, val, d):
        if d == 1:
            dst_ref[0] = val.astype(BF16)
            return
        for c in range(GW // LANES):
            stage_ref[c] = val[:, c * LANES:(c + 1) * LANES]
        for r in range(d):
            for c in range(GW // LANES):
                col = r * GW + c * LANES
                dst_ref[0, :, col:col + LANES] = stage_ref[c, pl.ds(r, tm // d, stride=d), :].astype(BF16)

    for g in range(N_GROUPS):
        qr = rope(mm(OFF_Q + g * GW, GW)) * (HEAD_DIM ** -0.5)
        store_by_residue(q_refs[g], qr, dils[g])
        kr = rope(mm(OFF_K + g * GW, GW))
        store_by_residue(k_refs[g], kr, dils[g])
        vv = mm(OFF_V + g * GW, GW)
        store_by_residue(v_refs[g], vv, dils[g])
        if row_major_kv:
            kv_refs[g][0, :, 0:GW] = kr
            kv_refs[g][0, :, GW:2 * GW] = vv

    az = mm(OFF_AZ, GW)
    saz_ref[0] = (az * _sigmoid(az)).astype(BF16)
    gz = mm(OFF_GZ, GLA_VW)
    sgz_ref[0] = (gz * _sigmoid(gz)).astype(BF16)
    for j in range(D_MODEL // GW):
        sa_ref[0, :, j * GW:(j + 1) * GW] = _sigmoid(mm(OFF_MA + j * GW, GW)).astype(BF16)
        sb_ref[0, :, j * GW:(j + 1) * GW] = _sigmoid(mm(OFF_MB + j * GW, GW)).astype(BF16)
    gq_ref[0] = (mm(OFF_GQ, GLA_KW) * (GLA_DK ** -0.5)).astype(BF16)
    gk_ref[0] = mm(OFF_GK, GLA_KW).astype(BF16)
    gv_ref[0] = mm(OFF_GV, GLA_VW).astype(BF16)
    glr = mm(OFF_LR, GATE_RANK)
    gate_pre = _dot(glr.astype(BF16), wa2_ref[...]) + ba_ref[...]
    la_ref[0] = _log_sigmoid(gate_pre) * (1.0 / GLA_TAU)


def _proj(x, lnw, w_t, rope_tabs, wa2, ba, tm, dils, row_major_kv):
    B, S, D = x.shape
    assert S % tm == 0 and all(tm % (16 * d) == 0 for d in dils)
    row = lambda w: pl.BlockSpec((1, tm, w), lambda b, s: (b, s, 0))
    const = lambda shp: pl.BlockSpec(shp, lambda b, s: (0,) * len(shp))
    tab = pl.BlockSpec((tm, LANES), lambda b, s: (s, 0))
    bf = lambda w: jax.ShapeDtypeStruct((B, S, w), BF16)

    out_shape, out_specs = [], []
    for _ in range(3):
        for d in dils:
            out_shape.append(jax.ShapeDtypeStruct((B, S // d, d * GW), BF16))
            out_specs.append(pl.BlockSpec((1, tm // d, d * GW), lambda b, s: (b, s, 0)))
    out_shape += [bf(GW), bf(GLA_VW), bf(D_MODEL), bf(D_MODEL), bf(GLA_KW), bf(GLA_KW), bf(GLA_VW),
                  jax.ShapeDtypeStruct((B, S, GLA_KW), F32)]
    out_specs += [row(GW), row(GLA_VW), row(D_MODEL), row(D_MODEL), row(GLA_KW), row(GLA_KW), row(GLA_VW),
                  row(GLA_KW)]
    if row_major_kv:
        out_shape += [jax.ShapeDtypeStruct((B, S, 2 * GW), F32)] * N_GROUPS
        out_specs += [row(2 * GW)] * N_GROUPS
    in_specs = [row(D), const((1, D)),
                pl.BlockSpec((PROJ_W, D), lambda b, s: (0, 0), pipeline_mode=pl.Buffered(1)),
                tab, tab, tab, const((GATE_RANK, GLA_KW)), const((1, GLA_KW))]
    return pl.pallas_call(
        functools.partial(_proj_kernel, dils=tuple(dils), row_major_kv=row_major_kv),
        grid=(B, S // tm), in_specs=in_specs, out_specs=out_specs, out_shape=out_shape,
        scratch_shapes=[pltpu.VMEM((GW // LANES, tm, LANES), F32)],
        compiler_params=_cparams(("arbitrary", "arbitrary")), name="proj",
    )(x, lnw, w_t, *rope_tabs, wa2, ba)


def _kvt_kernel(x_ref, lnw_ref, wk_ref, wv_ref, cos_ref, sin_ref, o0_ref, o1_ref, o2_ref, *, plan, nt):
    i = pl.program_id(1)
    h = _rms_bf16(x_ref[...], lnw_ref[...])
    rows = h.shape[0]
    half = ROPE_DIM // 2
    for g, (tiles, cols) in enumerate(plan):
        o_ref = (o0_ref, o1_ref, o2_ref)[g]

        def emit(g=g, cols=cols, o_ref=o_ref):
            hh = h[rows - cols:, :]
            y = _dot_nt(wk_ref[g * GW:(g + 1) * GW, :], hh)
            c = cos_ref[:, rows - cols:]
            s = sin_ref[:, rows - cols:]
            for hd in range(HEADS):
                b0 = hd * HEAD_DIM
                x1 = y[b0:b0 + half, :]
                x2 = y[b0 + half:b0 + ROPE_DIM, :]
                o_ref[b0:b0 + half, :] = x1 * c - x2 * s
                o_ref[b0 + half:b0 + ROPE_DIM, :] = x2 * c + x1 * s
                o_ref[b0 + ROPE_DIM:b0 + HEAD_DIM, :] = y[b0 + ROPE_DIM:b0 + HEAD_DIM, :]
            o_ref[GW:2 * GW, :] = _dot_nt(wv_ref[g * GW:(g + 1) * GW, :], hh)

        if tiles == nt:
            emit()
        else:
            pl.when(i >= nt - tiles)(emit)


def _rope_cos_sin_t(pos):
    half = ROPE_DIM // 2
    inv_freq = 1.0 / (ROPE_THETA ** (jnp.arange(half, dtype=F32) * (2.0 / ROPE_DIM)))
    ang = pos.astype(F32)[:, None] * inv_freq[None, :]
    return jnp.cos(ang).T, jnp.sin(ang).T


def _kv_tail_prompt(x, lnw, w_t, S_tails, tile):
    B, S, D = x.shape
    tile = min(tile, S)
    span = max(S_tails)
    assert span % tile == 0 and S % tile == 0
    nt = span // tile
    first = (S - span) // tile
    plan = []
    for t in S_tails:
        assert t % tile == 0 or t < tile
        plan.append((t // tile, tile) if t >= tile else (1, t))
    cos_t, sin_t = _rope_cos_sin_t(jnp.arange(S, dtype=jnp.int32))
    tabspec = pl.BlockSpec((ROPE_DIM // 2, tile), lambda b, i: (0, first + i))
    out_specs = [pl.BlockSpec((None, 2 * GW, cols), lambda b, i, t=tiles: (b, 0, jnp.maximum(i - (nt - t), 0)))
                 for tiles, cols in plan]
    return pl.pallas_call(
        functools.partial(_kvt_kernel, plan=tuple(plan), nt=nt),
        grid=(B, nt),
        in_specs=[pl.BlockSpec((None, tile, D), lambda b, i: (b, first + i, 0)),
                  pl.BlockSpec((1, D), lambda b, i: (0, 0)),
                  pl.BlockSpec((QKV_W, D), lambda b, i: (OFF_K // QKV_W, 0), pipeline_mode=pl.Buffered(1)),
                  pl.BlockSpec((QKV_W, D), lambda b, i: (OFF_V // QKV_W, 0), pipeline_mode=pl.Buffered(1)),
                  tabspec, tabspec],
        out_specs=out_specs,
        out_shape=[jax.ShapeDtypeStruct((B, 2 * GW, t), F32) for t in S_tails],
        compiler_params=_cparams(("arbitrary", "arbitrary")), name="kv_tail_prompt",
    )(x, lnw, w_t, w_t, cos_t, sin_t)


def _kv_tail_sample(x, lnw, w_t, pos):
    DB, T, D = x.shape
    cos_t, sin_t = _rope_cos_sin_t(pos)
    bcast = lambda t: jnp.broadcast_to(t.T[:, :, None], (T, ROPE_DIM // 2, DB))
    tabspec = pl.BlockSpec((None, ROPE_DIM // 2, DB), lambda b, i: (i, 0, 0))
    out_spec = pl.BlockSpec((None, 2 * GW, DB), lambda b, i: (i, 0, 0))
    return pl.pallas_call(
        functools.partial(_kvt_kernel, plan=((T, DB),) * N_GROUPS, nt=T),
        grid=(1, T),
        in_specs=[pl.BlockSpec((None, DB, D), lambda b, i: (i, 0, 0)),
                  pl.BlockSpec((1, D), lambda b, i: (0, 0)),
                  pl.BlockSpec((QKV_W, D), lambda b, i: (OFF_K // QKV_W, 0), pipeline_mode=pl.Buffered(1)),
                  pl.BlockSpec((QKV_W, D), lambda b, i: (OFF_V // QKV_W, 0), pipeline_mode=pl.Buffered(1)),
                  tabspec, tabspec],
        out_specs=[out_spec] * N_GROUPS,
        out_shape=[jax.ShapeDtypeStruct((T, 2 * GW, DB), F32)] * N_GROUPS,
        compiler_params=_cparams(("arbitrary", "arbitrary")), name="kv_tail_sample",
    )(x.transpose(1, 0, 2), lnw, w_t, w_t, bcast(cos_t), bcast(sin_t))


def _band_bias():
    q = np.arange(BAND)[:, None]
    c = np.arange(2 * BAND)[None, :]
    ok = np.where(c < BAND, c >= q, (c - BAND) <= q)
    ok_first = ok & (c >= BAND)
    return jnp.asarray(np.stack([np.where(ok, 0.0, NEG), np.where(ok_first, 0.0, NEG)]).astype(np.float32))


def _attn_kernel(q_ref, k_ref, kp_ref, v_ref, vp_ref, bias_ref, o_ref, lse_ref, kall, vall, *, qb):
    n = pl.program_id(2)
    nsub = qb // BAND
    npair = HEADS // 2
    kall[0:BAND, :] = kp_ref[0]
    kall[BAND:, :] = k_ref[0]
    vall[0:BAND, :] = vp_ref[0]
    vall[BAND:, :] = v_ref[0]
    lane = lax.broadcasted_iota(jnp.int32, (BAND, LANES), 1)
    low_half = lane < HEAD_DIM

    def sub_block(j, carry):
        r0 = pl.multiple_of(j * BAND, BAND)
        first = (n * nsub + j) == 0
        bias = bias_ref[jnp.where(first, 1, 0)]
        bias2 = jnp.concatenate([bias, bias], axis=0)
        scores = []
        for hp in range(npair):
            cs = slice(hp * LANES, (hp + 1) * LANES)
            qp = q_ref[0, pl.ds(r0, BAND), cs]
            zero = jnp.zeros_like(qp)
            qm = jnp.concatenate([jnp.where(low_half, qp, zero), jnp.where(low_half, zero, qp)], axis=0)
            scores.append(_dot_nt(qm, kall[pl.ds(r0, 2 * BAND), cs]) + bias2)
        probs, stats = [], []
        for s in scores:
            m = jnp.max(s, axis=1, keepdims=True)
            p = jnp.exp(s - m)
            stats.append((m, jnp.sum(p, axis=1, keepdims=True)))
            probs.append(p.astype(BF16))
        lse_tile = jnp.zeros((BAND, LANES), F32)
        for hp in range(npair):
            cs = slice(hp * LANES, (hp + 1) * LANES)
            m, den = stats[hp]
            o = _dot(probs[hp], vall[pl.ds(r0, 2 * BAND), cs]) / den
            o_ref[0, pl.ds(r0, BAND), cs] = jnp.where(low_half, o[0:BAND], o[BAND:]).astype(BF16)
            lse = m + jnp.log(den)
            lse_tile = jnp.where(lane == 2 * hp, lse[0:BAND], lse_tile)
            lse_tile = jnp.where(lane == 2 * hp + 1, lse[BAND:], lse_tile)
        lse_ref[0, pl.ds(r0, BAND), :] = lse_tile
        return carry

    lax.fori_loop(0, nsub, sub_block, 0)


def _attn_group(q, k, v, g, qb):
    B, L, _ = q.shape
    win, dil = GROUPS[g]
    assert win // dil == BAND
    qb = min(qb, L)
    assert L % qb == 0 and qb % BAND == 0
    nsub = qb // BAND
    cur = pl.BlockSpec((1, qb, GW), lambda b, r, n: (b, n, r))
    prev = pl.BlockSpec((1, BAND, GW), lambda b, r, n: (b, jnp.maximum(n * nsub - 1, 0), r))
    return pl.pallas_call(
        functools.partial(_attn_kernel, qb=qb),
        grid=(B, dil, L // qb),
        in_specs=[cur, cur, prev, cur, prev, pl.BlockSpec((2, BAND, 2 * BAND), lambda b, r, n: (0, 0, 0))],
        out_specs=[pl.BlockSpec((1, qb, GW), lambda b, r, n: (b, n, r)),
                   pl.BlockSpec((1, qb, LANES), lambda b, r, n: (b, n, r))],
        out_shape=[jax.ShapeDtypeStruct((B, L, dil * GW), BF16),
                   jax.ShapeDtypeStruct((B, L, dil * LANES), F32)],
        scratch_shapes=[pltpu.VMEM((qb + BAND, GW), BF16), pltpu.VMEM((qb + BAND, GW), BF16)],
        compiler_params=_cparams(("arbitrary", "arbitrary", "arbitrary")), name=f"attn_g{g}",
    )(q, k, k, v, v, _band_bias())


def _gla_kernel(gq_ref, gk_ref, gv_ref, la_ref, tri_ref, nw_ref, o_ref, sfin_ref, st_ref, *, nchunk):
    c = pl.program_id(1)

    @pl.when(c == 0)
    def _():
        st_ref[...] = jnp.zeros_like(st_ref)

    C = GLA_CHUNK
    npair = GLA_HEADS // 2
    tri3 = tri_ref[...]
    normw = nw_ref[...]
    hi, mid, lo = _split3(la_ref[0])
    b_chunks = []
    for ci in range(nchunk):
        rows = slice(ci * C, (ci + 1) * C)
        b_chunks.append(_dot(tri3, jnp.concatenate([hi[rows], mid[rows], lo[rows]], axis=0)))
    b = jnp.concatenate(b_chunks, axis=0)
    b_last = jnp.concatenate([jnp.broadcast_to(bc[C - 1:C, :], bc.shape) for bc in b_chunks], axis=0)
    gk = gk_ref[0].astype(F32)
    qe = gq_ref[0].astype(F32) * jnp.exp(b)
    ke = (gk * jnp.exp(-b)).astype(BF16)
    kd = (gk * jnp.exp(b_last - b)).astype(BF16)
    gv = gv_ref[0]

    low_half = lax.broadcasted_iota(jnp.int32, (C, LANES), 1) < GLA_DK
    ri = lax.broadcasted_iota(jnp.int32, (2 * C, 2 * C), 0)
    ki = lax.broadcasted_iota(jnp.int32, (2 * C, 2 * C), 1)
    pair_causal = jnp.logical_and((ri < C) == (ki < C),
                                  jnp.bitwise_and(ki, C - 1) <= jnp.bitwise_and(ri, C - 1))

    lhs, vrows, upd, dec = {}, {}, {}, {}
    for ci in range(nchunk):
        rows = slice(ci * C, (ci + 1) * C)
        for p in range(npair):
            cs = slice(p * LANES, (p + 1) * LANES)
            qe_p = qe[rows, cs]
            qm = jnp.concatenate([jnp.where(low_half, qe_p, 0.0), jnp.where(low_half, 0.0, qe_p)],
                                 axis=0).astype(BF16)
            ke_p = ke[rows, cs]
            sc = _dot_nt(qm, jnp.concatenate([ke_p, ke_p], axis=0))
            att = jnp.where(pair_causal, sc, 0.0).astype(BF16)
            lhs[ci, p] = jnp.concatenate([att, qm], axis=1)
            v_ab = gv[rows, 2 * p * GLA_DV:2 * (p + 1) * GLA_DV]
            vrows[ci, p] = jnp.concatenate([v_ab[:, 0:GLA_DV], v_ab[:, GLA_DV:]], axis=0)
            u = _dot_tn(kd[rows, cs], v_ab)
            upd[ci, p] = jnp.concatenate([u[0:GLA_DK, 0:GLA_DV], u[GLA_DK:, GLA_DV:]], axis=0)
            bl = jnp.broadcast_to(b_chunks[ci][C - 1:C, cs], (LANES, LANES))
            dec[ci, p] = jnp.exp(bl.T)

    outs = {}
    states = [st_ref[p] for p in range(npair)]
    for ci in range(nchunk):
        for p in range(npair):
            rhs = jnp.concatenate([vrows[ci, p], states[p].astype(BF16)], axis=0)
            outs[ci, p] = _dot(lhs[ci, p], rhs)
            states[p] = dec[ci, p] * states[p] + upd[ci, p]
    for p in range(npair):
        st_ref[p] = states[p]

    for ci in range(nchunk):
        for p in range(npair):
            o = outs[ci, p]
            o = (o * lax.rsqrt(jnp.mean(o * o, axis=-1, keepdims=True) + NORM_EPS) * normw).astype(BF16)
            for a in range(2):
                hh = 2 * p + a
                o_ref[0, ci * C:(ci + 1) * C, hh * GLA_DV:(hh + 1) * GLA_DV] = o[a * C:(a + 1) * C, :]

    @pl.when(c == pl.num_programs(1) - 1)
    def _():
        for p in range(npair):
            sfin_ref[0, p * LANES:(p + 1) * LANES, :] = states[p]


def _tri3(C, dtype=BF16):
    tri = np.tril(np.ones((C, C), np.float32))
    return jnp.asarray(np.concatenate([tri, tri, tri], axis=1), dtype)


def _gla_prompt(gq, gk, gv, la, normw, ct):
    B, S, _ = gq.shape
    ct = min(ct, S)
    assert S % ct == 0 and ct % GLA_CHUNK == 0
    row = lambda w: pl.BlockSpec((1, ct, w), lambda b, c: (b, c, 0))
    const = lambda shp: pl.BlockSpec(shp, lambda b, c: (0,) * len(shp))
    o, sfin = pl.pallas_call(
        functools.partial(_gla_kernel, nchunk=ct // GLA_CHUNK),
        grid=(B, S // ct),
        in_specs=[row(GLA_KW), row(GLA_KW), row(GLA_VW), row(GLA_KW),
                  const((GLA_CHUNK, 3 * GLA_CHUNK)), const((1, GLA_DV))],
        out_specs=[row(GLA_VW), pl.BlockSpec((1, GLA_KW, GLA_DV), lambda b, c: (b, 0, 0))],
        out_shape=[jax.ShapeDtypeStruct((B, S, GLA_VW), BF16),
                   jax.ShapeDtypeStruct((B, GLA_KW, GLA_DV), F32)],
        scratch_shapes=[pltpu.VMEM((GLA_HEADS // 2, GLA_DV, LANES), F32)],
        compiler_params=_cparams(("arbitrary", "arbitrary")), name="gla_prompt",
    )(gq, gk, gv, la, _tri3(GLA_CHUNK), normw)
    return o, sfin.reshape(B, GLA_HEADS, GLA_DK, GLA_DV)


def _merge_kernel(*refs, dils):
    n_groups = len(dils)
    x_ref = refs[0]
    o_refs = refs[1:1 + n_groups]
    lse_refs = refs[1 + n_groups:1 + 2 * n_groups] if n_groups > 1 else ()
    rest = refs[1 + n_groups + len(lse_refs):]
    (saz_ref, og_ref, sgz_ref, sa_ref, sb_ref, wa_ref, wg_ref, wo_ref, fnw_ref, ex_ref, y_ref) = rest[:11]
    scratch = rest[11:]
    tm = x_ref.shape[0]

    def by_position(ref, width, d, stage_ref):
        if d == 1:
            return ref[...].astype(F32)
        planes = width // LANES
        for r in range(d):
            blk = ref[:, r * width:(r + 1) * width].astype(F32)
            for c in range(planes):
                stage_ref[c, pl.ds(r, tm // d, stride=d), :] = blk[:, c * LANES:(c + 1) * LANES]
        return jnp.concatenate([stage_ref[c] for c in range(planes)], axis=1)

    if n_groups > 1:
        ls = [by_position(lse_refs[g], LANES, dils[g], scratch[2 * g + 1]) for g in range(n_groups)]
        mx = functools.reduce(jnp.maximum, ls)
        es = [jnp.exp(l - mx) for l in ls]
        inv = 1.0 / functools.reduce(lambda a, b: a + b, es)
        comb = None
        for g in range(n_groups):
            w = es[g] * inv
            w_hi = w.astype(BF16)
            w_lo = (w - w_hi.astype(F32)).astype(BF16)
            wx = _dot(jnp.concatenate([w_hi, w_lo], axis=1), ex_ref[...])
            term = wx * by_position(o_refs[g], GW, dils[g], scratch[2 * g])
            comb = term if comb is None else comb + term
    else:
        comb = o_refs[0][...].astype(F32)

    ua = (comb * saz_ref[...].astype(F32)).astype(BF16)
    ub = (og_ref[...].astype(F32) * sgz_ref[...].astype(F32)).astype(BF16)
    ya = _dot(ua, wa_ref[...])
    yb = _dot(ub, wg_ref[...])
    mixed = (sa_ref[...].astype(F32) * ya + sb_ref[...].astype(F32) * yb).astype(BF16)
    out = x_ref[...] + _dot(mixed, wo_ref[...])
    y_ref[...] = out * lax.rsqrt(jnp.mean(out * out, axis=-1, keepdims=True) + NORM_EPS) * fnw_ref[...]


def _expand_matrix():
    e = np.zeros((LANES, GW), np.float32)
    for hh in range(HEADS):
        e[hh, hh * HEAD_DIM:(hh + 1) * HEAD_DIM] = 1.0
    return jnp.asarray(np.concatenate([e, e], axis=0), BF16)


def _merge(x, o_list, lse_list, dils, saz, og, sgz, sa, sb, wa, wg, wo, fnw, tm):
    R, D = x.shape
    tm = min(tm, R)
    assert R % tm == 0
    n_groups = len(o_list)
    row = lambda w: pl.BlockSpec((tm, w), lambda i: (i, 0))
    const = lambda shp: pl.BlockSpec(shp, lambda i: (0,) * len(shp))
    in_specs = [row(D)]
    in_specs += [pl.BlockSpec((tm // d, d * GW), lambda i: (i, 0)) for d in dils]
    in_specs += [pl.BlockSpec((tm // d, d * LANES), lambda i: (i, 0)) for d in dils[:len(lse_list)]]
    in_specs += [row(GW), row(GLA_VW), row(GLA_VW), row(D), row(D),
                 const((GW, D)), const((GLA_VW, D)), const((D, D)), const((1, D)), const((2 * LANES, GW))]
    scratch = []
    if n_groups > 1:
        for _ in dils:
            scratch += [pltpu.VMEM((GW // LANES, tm, LANES), F32), pltpu.VMEM((1, tm, LANES), F32)]
    return pl.pallas_call(
        functools.partial(_merge_kernel, dils=tuple(dils)),
        grid=(R // tm,), in_specs=in_specs, out_specs=row(D),
        out_shape=jax.ShapeDtypeStruct((R, D), F32), scratch_shapes=scratch,
        compiler_params=_cparams(("arbitrary",)), name=f"merge_g{n_groups}",
    )(x, *o_list, *lse_list, saz, og, sgz, sa, sb, wa, wg, wo, fnw, _expand_matrix())


def _sample_masks(T, w_lens):
    t_of_row = np.arange(HEADS * T) % T

    def bias(idx, g):
        win, dil = GROUPS[g]
        dd = w_lens[g] + t_of_row[:, None] - idx[None, :]
        ok = (dd >= 0) & (dd % dil == 0) & (dd // dil <= win // dil)
        return jnp.asarray(np.where(ok, 0.0, NEG).astype(np.float32))

    cache = [bias(np.arange(w_lens[g]), g) for g in range(N_GROUPS)]
    new = [bias(w_lens[g] + np.arange(T), g) for g in range(N_GROUPS)]
    return cache, new


def _sattn_kernel(q0_ref, q1_ref, q2_ref, n0_ref, n1_ref, n2_ref, c0_ref, c1_ref, c2_ref,
                  bc0_ref, bc1_ref, bc2_ref, bn0_ref, bn1_ref, bn2_ref, bd_ref, o_ref):
    T = q0_ref.shape[1]
    bd = bd_ref[...]
    q_refs, new_refs, c_refs = (q0_ref, q1_ref, q2_ref), (n0_ref, n1_ref, n2_ref), (c0_ref, c1_ref, c2_ref)
    bc_refs, bn_refs = (bc0_ref, bc1_ref, bc2_ref), (bn0_ref, bn1_ref, bn2_ref)
    s_new, s_old, v_new, v_old = [], [], [], []
    for g in range(N_GROUPS):
        qbd = jnp.tile(q_refs[g][0].astype(F32), (HEADS, 1)) * bd
        s_new.append(_dot_nt(qbd, _rb(new_refs[g][0, :, 0:GW])) + bn_refs[g][...])
        v_new.append(_rb(new_refs[g][0, :, GW:2 * GW]))
        s_old.append(_dot(qbd.astype(BF16), c_refs[g][0, 0:GW, :].astype(BF16)) + bc_refs[g][...])
        v_old.append(c_refs[g][0, GW:2 * GW, :].astype(BF16))
    m = functools.reduce(jnp.maximum, [jnp.max(s, axis=1, keepdims=True) for s in s_new + s_old])
    p_new = [jnp.exp(s - m) for s in s_new]
    p_old = [jnp.exp(s - m) for s in s_old]
    den = functools.reduce(lambda a, b: a + b, [jnp.sum(p, axis=1, keepdims=True) for p in p_new + p_old])
    inv = 1.0 / den
    acc = None
    for g in range(N_GROUPS):
        r = _dot(_rb(p_new[g] * inv), v_new[g]) + _dot_nt((p_old[g] * inv).astype(BF16), v_old[g])
        acc = r if acc is None else acc + r
    acc = acc * bd
    out = acc[0:T, :]
    for hh in range(1, HEADS):
        out = out + acc[hh * T:(hh + 1) * T, :]
    o_ref[0] = out


def _sample_attn(q_list, kv_new, caches_t, T):
    DB = q_list[0].shape[0]
    w_lens = [c.shape[2] for c in caches_t]
    bias_c, bias_n = _sample_masks(T, w_lens)
    bd = np.zeros((HEADS * T, GW), np.float32)
    for hh in range(HEADS):
        bd[hh * T:(hh + 1) * T, hh * HEAD_DIM:(hh + 1) * HEAD_DIM] = 1.0
    consts = bias_c + bias_n + [jnp.asarray(bd)]
    per_db = lambda a: pl.BlockSpec((1,) + a.shape[1:], lambda i: (i,) + (0,) * (a.ndim - 1))
    const = lambda a: pl.BlockSpec(a.shape, lambda i: (0,) * a.ndim)
    args = list(q_list) + list(kv_new) + list(caches_t)
    return pl.pallas_call(
        _sattn_kernel, grid=(DB,),
        in_specs=[per_db(a) for a in args] + [const(a) for a in consts],
        out_specs=pl.BlockSpec((1, T, GW), lambda i: (i, 0, 0)),
        out_shape=jax.ShapeDtypeStruct((DB, T, GW), F32),
        compiler_params=_cparams(("arbitrary",)), name="sample_attn",
    )(*args, *consts)


def _sgla_kernel(gq_ref, gk_ref, gv_ref, la_ref, st_ref, tri_ref, nw_ref, o_ref, snew_ref, *, dbt):
    tri3 = tri_ref[...]
    normw = nw_ref[...]
    T = gq_ref.shape[1]
    ti = lax.broadcasted_iota(jnp.int32, (T, T), 0)
    si = lax.broadcasted_iota(jnp.int32, (T, T), 1)
    causal = si <= ti
    low_half = lax.broadcasted_iota(jnp.int32, (T, LANES), 1) < GLA_DK
    ones = jnp.ones((3 * T, LANES), F32)
    row_low = lax.broadcasted_iota(jnp.int32, (LANES, GLA_DV), 0) < GLA_DK

    def body(i, carry):
        la3 = jnp.concatenate(_split3(la_ref[i], F32), axis=0)
        b = _dot(tri3, la3)
        b_last = b[T - 1:T, :]
        gk = gk_ref[i].astype(F32)
        qe = gq_ref[i].astype(F32) * jnp.exp(b)
        ke = _rb(gk * jnp.exp(-b))
        kd = _rb(gk * jnp.exp(b_last - b))
        gv = gv_ref[i].astype(F32)
        outs = []
        for p in range(GLA_HEADS // 2):
            cs = slice(p * LANES, (p + 1) * LANES)
            st = st_ref[i, cs, :]
            st_b = _rb(st)
            dec = jnp.exp(_dot_tn(la3[:, cs], ones))
            upd = None
            for a in range(2):
                hh = 2 * p + a
                sel = low_half if a == 0 else jnp.logical_not(low_half)
                qm = _rb(jnp.where(sel, qe[:, cs], 0.0))
                att = _rb(jnp.where(causal, _dot_nt(qm, ke[:, cs]), 0.0))
                v_h = gv[:, hh * GLA_DV:(hh + 1) * GLA_DV]
                o = _dot(att, v_h) + _dot(qm, st_b)
                outs.append(o * lax.rsqrt(jnp.mean(o * o, axis=-1, keepdims=True) + NORM_EPS) * normw)
                u = _dot_tn(kd[:, cs], v_h)
                upd = u if a == 0 else jnp.where(row_low, upd, u)
            snew_ref[i, cs, :] = st * dec + upd
        o_ref[i] = jnp.concatenate(outs, axis=1)
        return carry

    for i in range(dbt):
        body(i, 0)


def _sample_gla(gq, gk, gv, la, state, normw, dbt):
    DB, T, _ = gq.shape
    dbt = min(dbt, DB)
    assert DB % dbt == 0
    st = state.reshape(DB, GLA_KW, GLA_DV)
    blk = lambda shp: pl.BlockSpec((dbt,) + shp, lambda i: (i,) + (0,) * len(shp))
    const = lambda shp: pl.BlockSpec(shp, lambda i: (0,) * len(shp))
    o, snew = pl.pallas_call(
        functools.partial(_sgla_kernel, dbt=dbt),
        grid=(DB // dbt,),
        in_specs=[blk((T, GLA_KW)), blk((T, GLA_KW)), blk((T, GLA_VW)), blk((T, GLA_KW)),
                  blk((GLA_KW, GLA_DV)), const((T, 3 * T)), const((1, GLA_DV))],
        out_specs=[blk((T, GLA_VW)), blk((GLA_KW, GLA_DV))],
        out_shape=[jax.ShapeDtypeStruct((DB, T, GLA_VW), F32),
                   jax.ShapeDtypeStruct((DB, GLA_KW, GLA_DV), F32)],
        compiler_params=_cparams(("arbitrary",)), name="gla_sample",
    )(gq, gk, gv, la, st, _tri3(T, F32), normw)
    return o, snew.reshape(DB, GLA_HEADS, GLA_DK, GLA_DV)


def _rope_tables(pos):
    half = ROPE_DIM // 2
    inv_freq = 1.0 / (ROPE_THETA ** (jnp.arange(half, dtype=F32) * (2.0 / ROPE_DIM)))
    ang = pos.astype(F32)[:, None] * inv_freq[None, :]
    cos, sin = jnp.cos(ang), jnp.sin(ang)
    n = pos.shape[0]
    pad = jnp.zeros((n, HEAD_DIM - ROPE_DIM), F32)
    zero = jnp.zeros((n, half), F32)
    c = jnp.concatenate([cos, cos, pad + 1.0], axis=1)
    s1 = jnp.concatenate([zero, sin, pad], axis=1)
    s2 = jnp.concatenate([-sin, zero, pad], axis=1)
    rep = LANES // HEAD_DIM
    return tuple(jnp.tile(t, (1, rep)) for t in (c, s1, s2))


def kernel(x_prompt, x_sample, cache_kv_w128, cache_kv_w512, cache_kv_w2048, state_gla,
           ln_w, w_in, w_gla_a2, b_gla_a, gla_norm_w, w_attn_out, w_gla_out, w_out, final_norm_w):
    B, S, D = x_prompt.shape
    DB, T, _ = x_sample.shape
    assert ln_w.shape[0] == 1, "single-layer step"
    caches = (cache_kv_w128, cache_kv_w512, cache_kv_w2048)
    dils = [d for _, d in GROUPS]

    lnw = ln_w[0].reshape(1, D)
    w_t = w_in[0].T.astype(BF16)
    wa2 = w_gla_a2[0].astype(BF16)
    ba = b_gla_a[0].reshape(1, GLA_KW)
    gnw = gla_norm_w[0].reshape(1, GLA_DV)
    wa = w_attn_out[0].astype(BF16)
    wg = w_gla_out[0].astype(BF16)
    wo = w_out[0].astype(BF16)
    fnw = final_norm_w.reshape(1, D)

    tabs_p = _rope_tables(jnp.arange(S, dtype=jnp.int32))
    outs = _proj(x_prompt, lnw, w_t, tabs_p, wa2, ba, min(512, S), dils, row_major_kv=False)
    saz, sgz, sa, sb, gq, gk, gv, la = outs[9:17]
    o_list, lse_list = [], []
    for g in range(N_GROUPS):
        o, lse = _attn_group(outs[g], outs[3 + g], outs[6 + g], g, qb=512)
        o_list.append(o.reshape(-1, o.shape[-1]))
        lse_list.append(lse.reshape(-1, lse.shape[-1]))
    og, gla_p = _gla_prompt(gq, gk, gv, la, gnw, ct=512)
    flat = lambda t: t.reshape(B * S, t.shape[-1])
    y_prompt = _merge(flat(x_prompt), o_list, lse_list, dils, flat(saz), flat(og), flat(sgz), flat(sa),
                      flat(sb), wa, wg, wo, fnw, tm=512).reshape(B, S, D)
    kvt = _kv_tail_prompt(x_prompt, lnw, w_t, [min(w, S) for w, _ in GROUPS], tile=512)
    kv_p = [t.reshape(B, 2, HEADS, HEAD_DIM, t.shape[-1]).transpose(0, 4, 1, 2, 3)[None] for t in kvt]

    R = DB * T
    pos_s = PAST_LEN + jnp.arange(T, dtype=jnp.int32)
    tabs_s = tuple(jnp.tile(t, (DB, 1)) for t in _rope_tables(pos_s))
    outs = _proj(x_sample.reshape(1, R, D), lnw, w_t, tabs_s, wa2, ba, min(512, R), (1,) * N_GROUPS,
                 row_major_kv=True)
    saz, sgz, sa, sb, gq, gk, gv, la = outs[9:17]
    per_db = lambda t: t.reshape(DB, T, t.shape[-1])
    caches_t = [c[0].transpose(0, 2, 3, 4, 1).reshape(DB, 2 * GW, c.shape[2]) for c in caches]
    o_s = _sample_attn([per_db(q) for q in outs[0:3]], [per_db(t) for t in outs[17:20]], caches_t, T)
    og_s, gla_s = _sample_gla(per_db(gq), per_db(gk), per_db(gv), per_db(la), state_gla[0], gnw, dbt=8)
    flat_s = lambda t: t.reshape(R, t.shape[-1])
    y_sample = _merge(x_sample.reshape(R, D), [o_s.reshape(R, GW)], [], (1,), flat_s(saz), flat_s(og_s),
                      flat_s(sgz), flat_s(sa), flat_s(sb), wa, wg, wo, fnw, tm=512).reshape(DB, T, D)
    kvt_s = _kv_tail_sample(x_sample, lnw, w_t, pos_s)
    kv_s = [t.reshape(T, 2, HEADS, HEAD_DIM, DB).transpose(4, 0, 1, 2, 3)[None] for t in kvt_s]

    return (y_prompt, y_sample, kv_p[0], kv_p[1], kv_p[2], gla_p[None],
            kv_s[0], kv_s[1], kv_s[2], gla_s[None])
```

```python
import functools

import numpy as np
import jax
import jax.numpy as jnp
from jax import lax
from jax.experimental import pallas as pl
from jax.experimental.pallas import tpu as pltpu

F32 = jnp.float32
BF16 = jnp.bfloat16

D_MODEL = 1024
HEAD_DIM = 64
HEADS = 8
GROUPS = ((128, 1), (512, 4), (2048, 16))
N_GROUPS = len(GROUPS)
GW = HEADS * HEAD_DIM
QKV_W = N_GROUPS * GW
ROPE_DIM = HEAD_DIM // 4
ROPE_THETA = 500000.0
BAND = 128
GLA_HEADS = 4
GLA_DK = 64
GLA_DV = 128
GLA_KW = GLA_HEADS * GLA_DK
GLA_VW = GLA_HEADS * GLA_DV
GATE_RANK = 16
GLA_TAU = 16.0
GLA_CHUNK = 64
NORM_EPS = 1e-6
PAST_LEN = 8192

LANES = 128
NEG = -1e30
LOG2E = float(np.log2(np.e))
LN2 = float(np.log(2.0))

OFF_Q = 0
OFF_K = OFF_Q + QKV_W
OFF_V = OFF_K + QKV_W
OFF_AZ = OFF_V + QKV_W
OFF_GQ = OFF_AZ + GW
OFF_GK = OFF_GQ + GLA_KW
OFF_GV = OFF_GK + GLA_KW
OFF_GZ = OFF_GV + GLA_VW
OFF_LR = OFF_GZ + GLA_VW
OFF_MA = OFF_LR + GATE_RANK
OFF_MB = OFF_MA + D_MODEL
PROJ_W = OFF_MB + D_MODEL
assert all(o % 16 == 0 for o in (OFF_LR, OFF_MA, OFF_MB, PROJ_W))

VMEM_LIMIT = 56 * 1024 * 1024
MERGE_ROWS = 512


def _cparams(sem):
    return pltpu.CompilerParams(dimension_semantics=sem, vmem_limit_bytes=VMEM_LIMIT)


def _dot(a, b):
    return jnp.dot(a, b, preferred_element_type=F32)


def _dot_nt(a, b):
    return lax.dot_general(a, b, (((1,), (1,)), ((), ())), preferred_element_type=F32)


def _dot_tn(a, b):
    return lax.dot_general(a, b, (((0,), (0,)), ((), ())), preferred_element_type=F32)


def _rb(x):
    return x.astype(BF16).astype(F32)


def _split3(x, dtype=BF16):
    hi = x.astype(BF16)
    r1 = x - hi.astype(F32)
    mid = r1.astype(BF16)
    lo = (r1 - mid.astype(F32)).astype(BF16)
    return hi.astype(dtype), mid.astype(dtype), lo.astype(dtype)


def _sigmoid(x):
    return 1.0 / (1.0 + jnp.exp(-x))


def _log_sigmoid(x):
    return jnp.minimum(x, 0.0) - jnp.log1p(jnp.exp(-jnp.abs(x)))


def _rms_bf16(x, w):
    ms = jnp.mean(x * x, axis=-1, keepdims=True)
    return (x * lax.rsqrt(ms + NORM_EPS) * w).astype(BF16)


def _proj_kernel(x_ref, lnw_ref, w_ref, rc_ref, rs1_ref, rs2_ref, wa2_ref, ba_ref, *refs,
                 dils, row_major_kv, q_scale):
    q_refs, k_refs, v_refs = refs[0:3], refs[3:6], refs[6:9]
    saz_ref, sgz_ref, sa_ref, sb_ref, gq_ref, gk_ref, gv_ref, la_ref = refs[9:17]
    kv_refs = refs[17:20] if row_major_kv else ()
    stage_ref = refs[-1]
    h = _rms_bf16(x_ref[0], lnw_ref[...])
    tm = h.shape[0]
    rc, rs1, rs2 = rc_ref[...], rs1_ref[...], rs2_ref[...]

    def mm(off, width):
        return _dot_nt(h, w_ref[off:off + width, :])

    def rope(t):
        outs = []
        for j in range(t.shape[1] // LANES):
            c = t[:, j * LANES:(j + 1) * LANES]
            outs.append(c * rc + pltpu.roll(c, ROPE_DIM // 2, axis=1) * rs1
                        + pltpu.roll(c, LANES - ROPE_DIM // 2, axis=1) * rs2)
        return jnp.concatenate(outs, axis=1)

    def store_by_residue(dst_ref, val, d):
        if d == 1:
            dst_ref[0] = val.astype(BF16)
            return
        for c in range(GW // LANES):
            stage_ref[c] = val[:, c * LANES:(c + 1) * LANES]
        for r in range(d):
            for c in range(GW // LANES):
                col = r * GW + c * LANES
                dst_ref[0, :, col:col + LANES] = stage_ref[c, pl.ds(r, tm // d, stride=d), :].astype(BF16)

    for g in range(N_GROUPS):
        qr = rope(mm(OFF_Q + g * GW, GW)) * q_scale
        store_by_residue(q_refs[g], qr, dils[g])
        kr = rope(mm(OFF_K + g * GW, GW))
        store_by_residue(k_refs[g], kr, dils[g])
        vv = mm(OFF_V + g * GW, GW)
        store_by_residue(v_refs[g], vv, dils[g])
        if row_major_kv:
            kv_refs[g][0, :, 0:GW] = kr
            kv_refs[g][0, :, GW:2 * GW] = vv

    az = mm(OFF_AZ, GW)
    saz_ref[0] = (az * _sigmoid(az)).astype(BF16)
    gz = mm(OFF_GZ, GLA_VW)
    sgz_ref[0] = (gz * _sigmoid(gz)).astype(BF16)
    for j in range(D_MODEL // GW):
        sa_ref[0, :, j * GW:(j + 1) * GW] = _sigmoid(mm(OFF_MA + j * GW, GW)).astype(BF16)
        sb_ref[0, :, j * GW:(j + 1) * GW] = _sigmoid(mm(OFF_MB + j * GW, GW)).astype(BF16)
    gq_ref[0] = (mm(OFF_GQ, GLA_KW) * (GLA_DK ** -0.5)).astype(BF16)
    gk_ref[0] = mm(OFF_GK, GLA_KW).astype(BF16)
    gv_ref[0] = mm(OFF_GV, GLA_VW).astype(BF16)
    glr = mm(OFF_LR, GATE_RANK)
    gate_pre = _dot(glr.astype(BF16), wa2_ref[...]) + ba_ref[...]
    la_ref[0] = _log_sigmoid(gate_pre) * (1.0 / GLA_TAU)


def _proj(x, lnw, w_t, rope_tabs, wa2, ba, tm, dils, row_major_kv, q_scale):
    B, S, D = x.shape
    assert S % tm == 0 and all(tm % (16 * d) == 0 for d in dils)
    row = lambda w: pl.BlockSpec((1, tm, w), lambda b, s: (b, s, 0))
    const = lambda shp: pl.BlockSpec(shp, lambda b, s: (0,) * len(shp))
    tab = pl.BlockSpec((tm, LANES), lambda b, s: (s, 0))
    bf = lambda w: jax.ShapeDtypeStruct((B, S, w), BF16)

    out_shape, out_specs = [], []
    for _ in range(3):
        for d in dils:
            out_shape.append(jax.ShapeDtypeStruct((B, S // d, d * GW), BF16))
            out_specs.append(pl.BlockSpec((1, tm // d, d * GW), lambda b, s: (b, s, 0)))
    out_shape += [bf(GW), bf(GLA_VW), bf(D_MODEL), bf(D_MODEL), bf(GLA_KW), bf(GLA_KW), bf(GLA_VW),
                  jax.ShapeDtypeStruct((B, S, GLA_KW), F32)]
    out_specs += [row(GW), row(GLA_VW), row(D_MODEL), row(D_MODEL), row(GLA_KW), row(GLA_KW), row(GLA_VW),
                  row(GLA_KW)]
    if row_major_kv:
        out_shape += [jax.ShapeDtypeStruct((B, S, 2 * GW), F32)] * N_GROUPS
        out_specs += [row(2 * GW)] * N_GROUPS
    in_specs = [row(D), const((1, D)),
                pl.BlockSpec((PROJ_W, D), lambda b, s: (0, 0), pipeline_mode=pl.Buffered(1)),
                tab, tab, tab, const((GATE_RANK, GLA_KW)), const((1, GLA_KW))]
    return pl.pallas_call(
        functools.partial(_proj_kernel, dils=tuple(dils), row_major_kv=row_major_kv, q_scale=q_scale),
        grid=(B, S // tm), in_specs=in_specs, out_specs=out_specs, out_shape=out_shape,
        scratch_shapes=[pltpu.VMEM((GW // LANES, tm, LANES), F32)],
        compiler_params=_cparams(("arbitrary", "arbitrary")), name="proj",
    )(x, lnw, w_t, *rope_tabs, wa2, ba)


def _kvt_kernel(x_ref, lnw_ref, wk_ref, wv_ref, cos_ref, sin_ref, o0_ref, o1_ref, o2_ref, *, plan, nt):
    i = pl.program_id(1)
    h = _rms_bf16(x_ref[...], lnw_ref[...])
    rows = h.shape[0]
    half = ROPE_DIM // 2
    for g, (tiles, cols) in enumerate(plan):
        o_ref = (o0_ref, o1_ref, o2_ref)[g]

        def emit(g=g, cols=cols, o_ref=o_ref):
            hh = h[rows - cols:, :]
            y = _dot_nt(wk_ref[g * GW:(g + 1) * GW, :], hh)
            c = cos_ref[:, rows - cols:]
            s = sin_ref[:, rows - cols:]
            for hd in range(HEADS):
                b0 = hd * HEAD_DIM
                x1 = y[b0:b0 + half, :]
                x2 = y[b0 + half:b0 + ROPE_DIM, :]
                o_ref[b0:b0 + half, :] = x1 * c - x2 * s
                o_ref[b0 + half:b0 + ROPE_DIM, :] = x2 * c + x1 * s
                o_ref[b0 + ROPE_DIM:b0 + HEAD_DIM, :] = y[b0 + ROPE_DIM:b0 + HEAD_DIM, :]
            o_ref[GW:2 * GW, :] = _dot_nt(wv_ref[g * GW:(g + 1) * GW, :], hh)

        if tiles == nt:
            emit()
        else:
            pl.when(i >= nt - tiles)(emit)


def _rope_cos_sin_t(pos):
    half = ROPE_DIM // 2
    inv_freq = 1.0 / (ROPE_THETA ** (jnp.arange(half, dtype=F32) * (2.0 / ROPE_DIM)))
    ang = pos.astype(F32)[:, None] * inv_freq[None, :]
    return jnp.cos(ang).T, jnp.sin(ang).T


def _kv_tail_prompt(x, lnw, w_t, S_tails, tile):
    B, S, D = x.shape
    tile = min(tile, S)
    span = max(S_tails)
    assert span % tile == 0 and S % tile == 0
    nt = span // tile
    first = (S - span) // tile
    plan = []
    for t in S_tails:
        assert t % tile == 0 or t < tile
        plan.append((t // tile, tile) if t >= tile else (1, t))
    cos_t, sin_t = _rope_cos_sin_t(jnp.arange(S, dtype=jnp.int32))
    tabspec = pl.BlockSpec((ROPE_DIM // 2, tile), lambda b, i: (0, first + i))
    out_specs = [pl.BlockSpec((None, 2 * GW, cols), lambda b, i, t=tiles: (b, 0, jnp.maximum(i - (nt - t), 0)))
                 for tiles, cols in plan]
    return pl.pallas_call(
        functools.partial(_kvt_kernel, plan=tuple(plan), nt=nt),
        grid=(B, nt),
        in_specs=[pl.BlockSpec((None, tile, D), lambda b, i: (b, first + i, 0)),
                  pl.BlockSpec((1, D), lambda b, i: (0, 0)),
                  pl.BlockSpec((QKV_W, D), lambda b, i: (OFF_K // QKV_W, 0), pipeline_mode=pl.Buffered(1)),
                  pl.BlockSpec((QKV_W, D), lambda b, i: (OFF_V // QKV_W, 0), pipeline_mode=pl.Buffered(1)),
                  tabspec, tabspec],
        out_specs=out_specs,
        out_shape=[jax.ShapeDtypeStruct((B, 2 * GW, t), F32) for t in S_tails],
        compiler_params=_cparams(("arbitrary", "arbitrary")), name="kv_tail_prompt",
    )(x, lnw, w_t, w_t, cos_t, sin_t)


def _kv_tail_sample(x, lnw, w_t, pos):
    DB, T, D = x.shape
    cos_t, sin_t = _rope_cos_sin_t(pos)
    bcast = lambda t: jnp.broadcast_to(t.T[:, :, None], (T, ROPE_DIM // 2, DB))
    tabspec = pl.BlockSpec((None, ROPE_DIM // 2, DB), lambda b, i: (i, 0, 0))
    out_spec = pl.BlockSpec((None, 2 * GW, DB), lambda b, i: (i, 0, 0))
    return pl.pallas_call(
        functools.partial(_kvt_kernel, plan=((T, DB),) * N_GROUPS, nt=T),
        grid=(1, T),
        in_specs=[pl.BlockSpec((None, DB, D), lambda b, i: (i, 0, 0)),
                  pl.BlockSpec((1, D), lambda b, i: (0, 0)),
                  pl.BlockSpec((QKV_W, D), lambda b, i: (OFF_K // QKV_W, 0), pipeline_mode=pl.Buffered(1)),
                  pl.BlockSpec((QKV_W, D), lambda b, i: (OFF_V // QKV_W, 0), pipeline_mode=pl.Buffered(1)),
                  tabspec, tabspec],
        out_specs=[out_spec] * N_GROUPS,
        out_shape=[jax.ShapeDtypeStruct((T, 2 * GW, DB), F32)] * N_GROUPS,
        compiler_params=_cparams(("arbitrary", "arbitrary")), name="kv_tail_sample",
    )(x.transpose(1, 0, 2), lnw, w_t, w_t, bcast(cos_t), bcast(sin_t))


def _band_bias():
    q = np.arange(BAND)[:, None]
    c = np.arange(2 * BAND)[None, :]
    ok = np.where(c < BAND, c >= q, (c - BAND) <= q)
    ok_first = ok & (c >= BAND)
    return jnp.asarray(np.stack([np.where(ok, 0.0, NEG), np.where(ok_first, 0.0, NEG)]).astype(np.float32))


def _attn_kernel(q_ref, k_ref, kp_ref, v_ref, vp_ref, bias_ref, o_ref, lse_ref, kall, vall, *, qb):
    n = pl.program_id(2)
    nsub = qb // BAND
    npair = HEADS // 2
    kall[0:BAND, :] = kp_ref[0]
    kall[BAND:, :] = k_ref[0]
    vall[0:BAND, :] = vp_ref[0]
    vall[BAND:, :] = v_ref[0]
    lane = lax.broadcasted_iota(jnp.int32, (BAND, LANES), 1)
    low_half = lane < HEAD_DIM

    def sub_block(j, carry):
        r0 = pl.multiple_of(j * BAND, BAND)
        first = (n * nsub + j) == 0
        bias = bias_ref[jnp.where(first, 1, 0)]
        bias2 = jnp.concatenate([bias, bias], axis=0)
        scores = []
        for hp in range(npair):
            cs = slice(hp * LANES, (hp + 1) * LANES)
            qp = q_ref[0, pl.ds(r0, BAND), cs]
            zero = jnp.zeros_like(qp)
            qm = jnp.concatenate([jnp.where(low_half, qp, zero), jnp.where(low_half, zero, qp)], axis=0)
            scores.append(_dot_nt(qm, kall[pl.ds(r0, 2 * BAND), cs]) + bias2)
        probs, stats = [], []
        for s in scores:
            m = jnp.max(s, axis=1, keepdims=True)
            p = jnp.exp2(s - m)
            stats.append((m, jnp.sum(p, axis=1, keepdims=True)))
            probs.append(p.astype(BF16))
        m_tile = jnp.zeros((BAND, LANES), F32)
        den_tile = jnp.ones((BAND, LANES), F32)
        for hp in range(npair):
            cs = slice(hp * LANES, (hp + 1) * LANES)
            m, den = stats[hp]
            o = _dot(probs[hp], vall[pl.ds(r0, 2 * BAND), cs])
            o = jnp.where(low_half, o[0:BAND], o[BAND:]) / jnp.where(low_half, den[0:BAND], den[BAND:])
            o_ref[0, pl.ds(r0, BAND), cs] = o.astype(BF16)
            for a in range(2):
                rows = slice(a * BAND, (a + 1) * BAND)
                m_tile = jnp.where(lane == 2 * hp + a, m[rows], m_tile)
                den_tile = jnp.where(lane == 2 * hp + a, den[rows], den_tile)
        lse_ref[0, pl.ds(r0, BAND), :] = (m_tile + jnp.log2(den_tile)) * LN2
        return carry

    lax.fori_loop(0, nsub, sub_block, 0, unroll=True)


def _attn_group(q, k, v, g, qb):
    B, L, _ = q.shape
    win, dil = GROUPS[g]
    assert win // dil == BAND
    qb = min(qb, L)
    assert L % qb == 0 and qb % BAND == 0
    nsub = qb // BAND
    cur = pl.BlockSpec((1, qb, GW), lambda b, r, n: (b, n, r))
    prev = pl.BlockSpec((1, BAND, GW), lambda b, r, n: (b, jnp.maximum(n * nsub - 1, 0), r))
    return pl.pallas_call(
        functools.partial(_attn_kernel, qb=qb),
        grid=(B, dil, L // qb),
        in_specs=[cur, cur, prev, cur, prev, pl.BlockSpec((2, BAND, 2 * BAND), lambda b, r, n: (0, 0, 0))],
        out_specs=[pl.BlockSpec((1, qb, GW), lambda b, r, n: (b, n, r)),
                   pl.BlockSpec((1, qb, LANES), lambda b, r, n: (b, n, r))],
        out_shape=[jax.ShapeDtypeStruct((B, L, dil * GW), BF16),
                   jax.ShapeDtypeStruct((B, L, dil * LANES), F32)],
        scratch_shapes=[pltpu.VMEM((qb + BAND, GW), BF16), pltpu.VMEM((qb + BAND, GW), BF16)],
        compiler_params=_cparams(("arbitrary", "arbitrary", "arbitrary")), name=f"attn_g{g}",
    )(q, k, k, v, v, _band_bias())


def _gla_kernel(gq_ref, gk_ref, gv_ref, la_ref, tri_ref, nw_ref, o_ref, sfin_ref, st_ref, *, nchunk):
    c = pl.program_id(1)

    @pl.when(c == 0)
    def _():
        st_ref[...] = jnp.zeros_like(st_ref)

    C = GLA_CHUNK
    npair = GLA_HEADS // 2
    tri3 = tri_ref[...]
    normw = nw_ref[...]
    hi, mid, lo = _split3(la_ref[0])
    b_chunks = []
    for ci in range(nchunk):
        rows = slice(ci * C, (ci + 1) * C)
        b_chunks.append(_dot(tri3, jnp.concatenate([hi[rows], mid[rows], lo[rows]], axis=0)))
    b = jnp.concatenate(b_chunks, axis=0)
    b_last = jnp.concatenate([jnp.broadcast_to(bc[C - 1:C, :], bc.shape) for bc in b_chunks], axis=0)
    gk = gk_ref[0].astype(F32)
    qe = gq_ref[0].astype(F32) * jnp.exp(b)
    ke = (gk * jnp.exp(-b)).astype(BF16)
    kd = (gk * jnp.exp(b_last - b)).astype(BF16)
    gv = gv_ref[0]

    low_half = lax.broadcasted_iota(jnp.int32, (C, LANES), 1) < GLA_DK
    ri = lax.broadcasted_iota(jnp.int32, (2 * C, 2 * C), 0)
    ki = lax.broadcasted_iota(jnp.int32, (2 * C, 2 * C), 1)
    pair_causal = jnp.logical_and((ri < C) == (ki < C),
                                  jnp.bitwise_and(ki, C - 1) <= jnp.bitwise_and(ri, C - 1))

    lhs, vrows, upd, dec = {}, {}, {}, {}
    for ci in range(nchunk):
        rows = slice(ci * C, (ci + 1) * C)
        for p in range(npair):
            cs = slice(p * LANES, (p + 1) * LANES)
            qe_p = qe[rows, cs]
            qm = jnp.concatenate([jnp.where(low_half, qe_p, 0.0), jnp.where(low_half, 0.0, qe_p)],
                                 axis=0).astype(BF16)
            ke_p = ke[rows, cs]
            sc = _dot_nt(qm, jnp.concatenate([ke_p, ke_p], axis=0))
            att = jnp.where(pair_causal, sc, 0.0).astype(BF16)
            lhs[ci, p] = jnp.concatenate([att, qm], axis=1)
            v_ab = gv[rows, 2 * p * GLA_DV:2 * (p + 1) * GLA_DV]
            vrows[ci, p] = jnp.concatenate([v_ab[:, 0:GLA_DV], v_ab[:, GLA_DV:]], axis=0)
            u = _dot_tn(kd[rows, cs], v_ab)
            upd[ci, p] = jnp.concatenate([u[0:GLA_DK, 0:GLA_DV], u[GLA_DK:, GLA_DV:]], axis=0)
            bl = jnp.broadcast_to(b_chunks[ci][C - 1:C, cs], (LANES, LANES))
            dec[ci, p] = jnp.exp(bl.T)

    outs = {}
    states = [st_ref[p] for p in range(npair)]
    for ci in range(nchunk):
        for p in range(npair):
            rhs = jnp.concatenate([vrows[ci, p], states[p].astype(BF16)], axis=0)
            outs[ci, p] = _dot(lhs[ci, p], rhs)
            states[p] = dec[ci, p] * states[p] + upd[ci, p]
    for p in range(npair):
        st_ref[p] = states[p]

    for ci in range(nchunk):
        for p in range(npair):
            o = outs[ci, p]
            o = (o * lax.rsqrt(jnp.mean(o * o, axis=-1, keepdims=True) + NORM_EPS) * normw).astype(BF16)
            for a in range(2):
                hh = 2 * p + a
                o_ref[0, ci * C:(ci + 1) * C, hh * GLA_DV:(hh + 1) * GLA_DV] = o[a * C:(a + 1) * C, :]

    @pl.when(c == pl.num_programs(1) - 1)
    def _():
        for p in range(npair):
            sfin_ref[0, p * LANES:(p + 1) * LANES, :] = states[p]


def _tri3(C, dtype=BF16):
    tri = np.tril(np.ones((C, C), np.float32))
    return jnp.asarray(np.concatenate([tri, tri, tri], axis=1), dtype)


def _gla_prompt(gq, gk, gv, la, normw, ct):
    B, S, _ = gq.shape
    ct = min(ct, S)
    assert S % ct == 0 and ct % GLA_CHUNK == 0
    row = lambda w: pl.BlockSpec((1, ct, w), lambda b, c: (b, c, 0))
    const = lambda shp: pl.BlockSpec(shp, lambda b, c: (0,) * len(shp))
    o, sfin = pl.pallas_call(
        functools.partial(_gla_kernel, nchunk=ct // GLA_CHUNK),
        grid=(B, S // ct),
        in_specs=[row(GLA_KW), row(GLA_KW), row(GLA_VW), row(GLA_KW),
                  const((GLA_CHUNK, 3 * GLA_CHUNK)), const((1, GLA_DV))],
        out_specs=[row(GLA_VW), pl.BlockSpec((1, GLA_KW, GLA_DV), lambda b, c: (b, 0, 0))],
        out_shape=[jax.ShapeDtypeStruct((B, S, GLA_VW), BF16),
                   jax.ShapeDtypeStruct((B, GLA_KW, GLA_DV), F32)],
        scratch_shapes=[pltpu.VMEM((GLA_HEADS // 2, GLA_DV, LANES), F32)],
        compiler_params=_cparams(("arbitrary", "arbitrary")), name="gla_prompt",
    )(gq, gk, gv, la, _tri3(GLA_CHUNK), normw)
    return o, sfin.reshape(B, GLA_HEADS, GLA_DK, GLA_DV)


def _merge_kernel(*refs, dils):
    n_groups = len(dils)
    x_ref = refs[0]
    o_refs = refs[1:1 + n_groups]
    lse_refs = refs[1 + n_groups:1 + 2 * n_groups] if n_groups > 1 else ()
    rest = refs[1 + n_groups + len(lse_refs):]
    (saz_ref, og_ref, sgz_ref, sa_ref, sb_ref, wa_ref, wg_ref, wo_ref, fnw_ref, ex_ref, y_ref) = rest[:11]
    scratch = rest[11:]
    tm = x_ref.shape[0]

    def by_position(ref, width, d, stage_ref):
        if d == 1:
            return lambda rows: ref[rows, :].astype(F32)
        planes = width // LANES
        for r in range(d):
            blk = ref[:, r * width:(r + 1) * width].astype(F32)
            for c in range(planes):
                stage_ref[c, pl.ds(r, tm // d, stride=d), :] = blk[:, c * LANES:(c + 1) * LANES]
        return lambda rows: jnp.concatenate([stage_ref[c, rows, :] for c in range(planes)], axis=1)

    if n_groups > 1:
        lse_rows = [by_position(lse_refs[g], LANES, dils[g], scratch[2 * g + 1]) for g in range(n_groups)]
        o_rows = [by_position(o_refs[g], GW, dils[g], scratch[2 * g]) for g in range(n_groups)]

    rc = min(tm, MERGE_ROWS)
    for r0 in range(0, tm, rc):
        rows = slice(r0, r0 + rc)
        if n_groups > 1:
            ls = [f(rows) for f in lse_rows]
            mx = functools.reduce(jnp.maximum, ls)
            es = [jnp.exp(l - mx) for l in ls]
            inv = 1.0 / functools.reduce(lambda a, b: a + b, es)
            comb = None
            for g in range(n_groups):
                w = es[g] * inv
                w_hi = w.astype(BF16)
                w_lo = (w - w_hi.astype(F32)).astype(BF16)
                wx = _dot(jnp.concatenate([w_hi, w_lo], axis=1), ex_ref[...])
                term = wx * o_rows[g](rows)
                comb = term if comb is None else comb + term
        else:
            comb = o_refs[0][rows, :].astype(F32)

        ua = (comb * saz_ref[rows, :].astype(F32)).astype(BF16)
        ub = (og_ref[rows, :].astype(F32) * sgz_ref[rows, :].astype(F32)).astype(BF16)
        ya = _dot(ua, wa_ref[...])
        yb = _dot(ub, wg_ref[...])
        mixed = (sa_ref[rows, :].astype(F32) * ya + sb_ref[rows, :].astype(F32) * yb).astype(BF16)
        out = x_ref[rows, :] + _dot(mixed, wo_ref[...])
        y_ref[rows, :] = (out * lax.rsqrt(jnp.mean(out * out, axis=-1, keepdims=True) + NORM_EPS)
                          * fnw_ref[...])


def _expand_matrix():
    e = np.zeros((LANES, GW), np.float32)
    for hh in range(HEADS):
        e[hh, hh * HEAD_DIM:(hh + 1) * HEAD_DIM] = 1.0
    return jnp.asarray(np.concatenate([e, e], axis=0), BF16)


def _merge(x, o_list, lse_list, dils, saz, og, sgz, sa, sb, wa, wg, wo, fnw, tm):
    R, D = x.shape
    tm = min(tm, R)
    assert R % tm == 0
    n_groups = len(o_list)
    row = lambda w: pl.BlockSpec((tm, w), lambda i: (i, 0))
    const = lambda shp: pl.BlockSpec(shp, lambda i: (0,) * len(shp))
    in_specs = [row(D)]
    in_specs += [pl.BlockSpec((tm // d, d * GW), lambda i: (i, 0)) for d in dils]
    in_specs += [pl.BlockSpec((tm // d, d * LANES), lambda i: (i, 0)) for d in dils[:len(lse_list)]]
    in_specs += [row(GW), row(GLA_VW), row(GLA_VW), row(D), row(D),
                 const((GW, D)), const((GLA_VW, D)), const((D, D)), const((1, D)), const((2 * LANES, GW))]
    scratch = []
    if n_groups > 1:
        for _ in dils:
            scratch += [pltpu.VMEM((GW // LANES, tm, LANES), F32), pltpu.VMEM((1, tm, LANES), F32)]
    return pl.pallas_call(
        functools.partial(_merge_kernel, dils=tuple(dils)),
        grid=(R // tm,), in_specs=in_specs, out_specs=row(D),
        out_shape=jax.ShapeDtypeStruct((R, D), F32), scratch_shapes=scratch,
        compiler_params=_cparams(("arbitrary",)), name=f"merge_g{n_groups}",
    )(x, *o_list, *lse_list, saz, og, sgz, sa, sb, wa, wg, wo, fnw, _expand_matrix())


def _sample_masks(T, w_lens):
    t_of_row = np.arange(HEADS * T) % T

    def bias(idx, g):
        win, dil = GROUPS[g]
        dd = w_lens[g] + t_of_row[:, None] - idx[None, :]
        ok = (dd >= 0) & (dd % dil == 0) & (dd // dil <= win // dil)
        return jnp.asarray(np.where(ok, 0.0, NEG).astype(np.float32))

    cache = [bias(np.arange(w_lens[g]), g) for g in range(N_GROUPS)]
    new = [bias(w_lens[g] + np.arange(T), g) for g in range(N_GROUPS)]
    return cache, new


def _sattn_kernel(q0_ref, q1_ref, q2_ref, n0_ref, n1_ref, n2_ref, c0_ref, c1_ref, c2_ref,
                  bc0_ref, bc1_ref, bc2_ref, bn0_ref, bn1_ref, bn2_ref, bd_ref, o_ref):
    T = q0_ref.shape[1]
    bd = bd_ref[...]
    q_refs, new_refs, c_refs = (q0_ref, q1_ref, q2_ref), (n0_ref, n1_ref, n2_ref), (c0_ref, c1_ref, c2_ref)
    bc_refs, bn_refs = (bc0_ref, bc1_ref, bc2_ref), (bn0_ref, bn1_ref, bn2_ref)
    s_new, s_old, v_new, v_old = [], [], [], []
    for g in range(N_GROUPS):
        qbd = jnp.tile(q_refs[g][0].astype(F32), (HEADS, 1)) * bd
        s_new.append(_dot_nt(qbd, _rb(new_refs[g][0, :, 0:GW])) + bn_refs[g][...])
        v_new.append(_rb(new_refs[g][0, :, GW:2 * GW]))
        s_old.append(_dot(qbd.astype(BF16), c_refs[g][0, 0:GW, :].astype(BF16)) + bc_refs[g][...])
        v_old.append(c_refs[g][0, GW:2 * GW, :].astype(BF16))
    m = functools.reduce(jnp.maximum, [jnp.max(s, axis=1, keepdims=True) for s in s_new + s_old])
    p_new = [jnp.exp(s - m) for s in s_new]
    p_old = [jnp.exp(s - m) for s in s_old]
    den = functools.reduce(lambda a, b: a + b, [jnp.sum(p, axis=1, keepdims=True) for p in p_new + p_old])
    inv = 1.0 / den
    acc = None
    for g in range(N_GROUPS):
        r = _dot(_rb(p_new[g] * inv), v_new[g]) + _dot_nt((p_old[g] * inv).astype(BF16), v_old[g])
        acc = r if acc is None else acc + r
    acc = acc * bd
    out = acc[0:T, :]
    for hh in range(1, HEADS):
        out = out + acc[hh * T:(hh + 1) * T, :]
    o_ref[0] = out


def _sample_attn(q_list, kv_new, caches_t, T):
    DB = q_list[0].shape[0]
    w_lens = [c.shape[2] for c in caches_t]
    bias_c, bias_n = _sample_masks(T, w_lens)
    bd = np.zeros((HEADS * T, GW), np.float32)
    for hh in range(HEADS):
        bd[hh * T:(hh + 1) * T, hh * HEAD_DIM:(hh + 1) * HEAD_DIM] = 1.0
    consts = bias_c + bias_n + [jnp.asarray(bd)]
    per_db = lambda a: pl.BlockSpec((1,) + a.shape[1:], lambda i: (i,) + (0,) * (a.ndim - 1))
    const = lambda a: pl.BlockSpec(a.shape, lambda i: (0,) * a.ndim)
    args = list(q_list) + list(kv_new) + list(caches_t)
    return pl.pallas_call(
        _sattn_kernel, grid=(DB,),
        in_specs=[per_db(a) for a in args] + [const(a) for a in consts],
        out_specs=pl.BlockSpec((1, T, GW), lambda i: (i, 0, 0)),
        out_shape=jax.ShapeDtypeStruct((DB, T, GW), F32),
        compiler_params=_cparams(("arbitrary",)), name="sample_attn",
    )(*args, *consts)


def _sgla_kernel(gq_ref, gk_ref, gv_ref, la_ref, st_ref, tri_ref, nw_ref, o_ref, snew_ref, *, dbt):
    tri3 = tri_ref[...]
    normw = nw_ref[...]
    T = gq_ref.shape[1]
    ti = lax.broadcasted_iota(jnp.int32, (T, T), 0)
    si = lax.broadcasted_iota(jnp.int32, (T, T), 1)
    causal = si <= ti
    low_half = lax.broadcasted_iota(jnp.int32, (T, LANES), 1) < GLA_DK
    ones = jnp.ones((3 * T, LANES), F32)
    row_low = lax.broadcasted_iota(jnp.int32, (LANES, GLA_DV), 0) < GLA_DK

    def body(i, carry):
        la3 = jnp.concatenate(_split3(la_ref[i], F32), axis=0)
        b = _dot(tri3, la3)
        b_last = b[T - 1:T, :]
        gk = gk_ref[i].astype(F32)
        qe = gq_ref[i].astype(F32) * jnp.exp(b)
        ke = _rb(gk * jnp.exp(-b))
        kd = _rb(gk * jnp.exp(b_last - b))
        gv = gv_ref[i].astype(F32)
        outs = []
        for p in range(GLA_HEADS // 2):
            cs = slice(p * LANES, (p + 1) * LANES)
            st = st_ref[i, cs, :]
            st_b = _rb(st)
            dec = jnp.exp(_dot_tn(la3[:, cs], ones))
            upd = None
            for a in range(2):
                hh = 2 * p + a
                sel = low_half if a == 0 else jnp.logical_not(low_half)
                qm = _rb(jnp.where(sel, qe[:, cs], 0.0))
                att = _rb(jnp.where(causal, _dot_nt(qm, ke[:, cs]), 0.0))
                v_h = gv[:, hh * GLA_DV:(hh + 1) * GLA_DV]
                o = _dot(att, v_h) + _dot(qm, st_b)
                outs.append(o * lax.rsqrt(jnp.mean(o * o, axis=-1, keepdims=True) + NORM_EPS) * normw)
                u = _dot_tn(kd[:, cs], v_h)
                upd = u if a == 0 else jnp.where(row_low, upd, u)
            snew_ref[i, cs, :] = st * dec + upd
        o_ref[i] = jnp.concatenate(outs, axis=1)
        return carry

    for i in range(dbt):
        body(i, 0)


def _sample_gla(gq, gk, gv, la, state, normw, dbt):
    DB, T, _ = gq.shape
    dbt = min(dbt, DB)
    assert DB % dbt == 0
    st = state.reshape(DB, GLA_KW, GLA_DV)
    blk = lambda shp: pl.BlockSpec((dbt,) + shp, lambda i: (i,) + (0,) * len(shp))
    const = lambda shp: pl.BlockSpec(shp, lambda i: (0,) * len(shp))
    o, snew = pl.pallas_call(
        functools.partial(_sgla_kernel, dbt=dbt),
        grid=(DB // dbt,),
        in_specs=[blk((T, GLA_KW)), blk((T, GLA_KW)), blk((T, GLA_VW)), blk((T, GLA_KW)),
                  blk((GLA_KW, GLA_DV)), const((T, 3 * T)), const((1, GLA_DV))],
        out_specs=[blk((T, GLA_VW)), blk((GLA_KW, GLA_DV))],
        out_shape=[jax.ShapeDtypeStruct((DB, T, GLA_VW), F32),
                   jax.ShapeDtypeStruct((DB, GLA_KW, GLA_DV), F32)],
        compiler_params=_cparams(("arbitrary",)), name="gla_sample",
    )(gq, gk, gv, la, st, _tri3(T, F32), normw)
    return o, snew.reshape(DB, GLA_HEADS, GLA_DK, GLA_DV)


def _rope_tables(pos):
    half = ROPE_DIM // 2
    inv_freq = 1.0 / (ROPE_THETA ** (jnp.arange(half, dtype=F32) * (2.0 / ROPE_DIM)))
    ang = pos.astype(F32)[:, None] * inv_freq[None, :]
    cos, sin = jnp.cos(ang), jnp.sin(ang)
    n = pos.shape[0]
    pad = jnp.zeros((n, HEAD_DIM - ROPE_DIM), F32)
    zero = jnp.zeros((n, half), F32)
    c = jnp.concatenate([cos, cos, pad + 1.0], axis=1)
    s1 = jnp.concatenate([zero, sin, pad], axis=1)
    s2 = jnp.concatenate([-sin, zero, pad], axis=1)
    rep = LANES // HEAD_DIM
    return tuple(jnp.tile(t, (1, rep)) for t in (c, s1, s2))


def kernel(x_prompt, x_sample, cache_kv_w128, cache_kv_w512, cache_kv_w2048, state_gla,
           ln_w, w_in, w_gla_a2, b_gla_a, gla_norm_w, w_attn_out, w_gla_out, w_out, final_norm_w):
    B, S, D = x_prompt.shape
    DB, T, _ = x_sample.shape
    assert ln_w.shape[0] == 1, "single-layer step"
    caches = (cache_kv_w128, cache_kv_w512, cache_kv_w2048)
    dils = [d for _, d in GROUPS]

    lnw = ln_w[0].reshape(1, D)
    w_t = w_in[0].T.astype(BF16)
    wa2 = w_gla_a2[0].astype(BF16)
    ba = b_gla_a[0].reshape(1, GLA_KW)
    gnw = gla_norm_w[0].reshape(1, GLA_DV)
    wa = w_attn_out[0].astype(BF16)
    wg = w_gla_out[0].astype(BF16)
    wo = w_out[0].astype(BF16)
    fnw = final_norm_w.reshape(1, D)

    tabs_p = _rope_tables(jnp.arange(S, dtype=jnp.int32))
    outs = _proj(x_prompt, lnw, w_t, tabs_p, wa2, ba, min(512, S), dils, row_major_kv=False,
                 q_scale=HEAD_DIM ** -0.5 * LOG2E)
    saz, sgz, sa, sb, gq, gk, gv, la = outs[9:17]
    o_list, lse_list = [], []
    for g in range(N_GROUPS):
        o, lse = _attn_group(outs[g], outs[3 + g], outs[6 + g], g, qb=512)
        o_list.append(o.reshape(-1, o.shape[-1]))
        lse_list.append(lse.reshape(-1, lse.shape[-1]))
    og, gla_p = _gla_prompt(gq, gk, gv, la, gnw, ct=512)
    flat = lambda t: t.reshape(B * S, t.shape[-1])
    y_prompt = _merge(flat(x_prompt), o_list, lse_list, dils, flat(saz), flat(og), flat(sgz), flat(sa),
                      flat(sb), wa, wg, wo, fnw, tm=512).reshape(B, S, D)
    kvt = _kv_tail_prompt(x_prompt, lnw, w_t, [min(w, S) for w, _ in GROUPS], tile=512)
    kv_p = [t.reshape(B, 2, HEADS, HEAD_DIM, t.shape[-1]).transpose(0, 4, 1, 2, 3)[None] for t in kvt]

    R = DB * T
    pos_s = PAST_LEN + jnp.arange(T, dtype=jnp.int32)
    tabs_s = tuple(jnp.tile(t, (DB, 1)) for t in _rope_tables(pos_s))
    outs = _proj(x_sample.reshape(1, R, D), lnw, w_t, tabs_s, wa2, ba, min(512, R), (1,) * N_GROUPS,
                 row_major_kv=True, q_scale=HEAD_DIM ** -0.5)
    saz, sgz, sa, sb, gq, gk, gv, la = outs[9:17]
    per_db = lambda t: t.reshape(DB, T, t.shape[-1])
    caches_t = [c[0].transpose(0, 2, 3, 4, 1).reshape(DB, 2 * GW, c.shape[2]) for c in caches]
    o_s = _sample_attn([per_db(q) for q in outs[0:3]], [per_db(t) for t in outs[17:20]], caches_t, T)
    og_s, gla_s = _sample_gla(per_db(gq), per_db(gk), per_db(gv), per_db(la), state_gla[0], gnw, dbt=8)
    flat_s = lambda t: t.reshape(R, t.shape[-1])
    y_sample = _merge(x_sample.reshape(R, D), [o_s.reshape(R, GW)], [], (1,), flat_s(saz), flat_s(og_s),
                      flat_s(sgz), flat_s(sa), flat_s(sb), wa, wg, wo, fnw, tm=512).reshape(DB, T, D)
    kvt_s = _kv_tail_sample(x_sample, lnw, w_t, pos_s)
    kv_s = [t.reshape(T, 2, HEADS, HEAD_DIM, DB).transpose(4, 0, 1, 2, 3)[None] for t in kvt_s]

    return (y_prompt, y_sample, kv_p[0], kv_p[1], kv_p[2], gla_p[None],
            kv_s[0], kv_s[1], kv_s[2], gla_s[None])
```

```python
import functools

import numpy as np
import jax
import jax.numpy as jnp
from jax import lax
from jax.experimental import pallas as pl
from jax.experimental.pallas import tpu as pltpu

F32 = jnp.float32
BF16 = jnp.bfloat16

D_MODEL = 1024
HEAD_DIM = 64
HEADS = 8
GROUPS = ((128, 1), (512, 4), (2048, 16))
N_GROUPS = len(GROUPS)
GW = HEADS * HEAD_DIM
QKV_W = N_GROUPS * GW
ROPE_DIM = HEAD_DIM // 4
ROPE_THETA = 500000.0
BAND = 128
GLA_HEADS = 4
GLA_DK = 64
GLA_DV = 128
GLA_KW = GLA_HEADS * GLA_DK
GLA_VW = GLA_HEADS * GLA_DV
GATE_RANK = 16
GLA_TAU = 16.0
GLA_CHUNK = 64
NORM_EPS = 1e-6
PAST_LEN = 8192

LANES = 128
NEG = -1e30
LOG2E = float(np.log2(np.e))
LN2 = float(np.log(2.0))

OFF_Q = 0
OFF_K = OFF_Q + QKV_W
OFF_V = OFF_K + QKV_W
OFF_AZ = OFF_V + QKV_W
OFF_GQ = OFF_AZ + GW
OFF_GK = OFF_GQ + GLA_KW
OFF_GV = OFF_GK + GLA_KW
OFF_GZ = OFF_GV + GLA_VW
OFF_LR = OFF_GZ + GLA_VW
OFF_MA = OFF_LR + GATE_RANK
OFF_MB = OFF_MA + D_MODEL
PROJ_W = OFF_MB + D_MODEL
assert all(o % 16 == 0 for o in (OFF_LR, OFF_MA, OFF_MB, PROJ_W))

VMEM_LIMIT = 56 * 1024 * 1024
MERGE_ROWS = 512


def _cparams(sem):
    return pltpu.CompilerParams(dimension_semantics=sem, vmem_limit_bytes=VMEM_LIMIT)


def _dot(a, b):
    return jnp.dot(a, b, preferred_element_type=F32)


def _dot_nt(a, b):
    return lax.dot_general(a, b, (((1,), (1,)), ((), ())), preferred_element_type=F32)


def _dot_tn(a, b):
    return lax.dot_general(a, b, (((0,), (0,)), ((), ())), preferred_element_type=F32)


def _rb(x):
    return x.astype(BF16).astype(F32)


def _split3(x, dtype=BF16):
    hi = x.astype(BF16)
    r1 = x - hi.astype(F32)
    mid = r1.astype(BF16)
    lo = (r1 - mid.astype(F32)).astype(BF16)
    return hi.astype(dtype), mid.astype(dtype), lo.astype(dtype)


def _sigmoid(x):
    return 1.0 / (1.0 + jnp.exp(-x))


def _log_sigmoid(x):
    return jnp.minimum(x, 0.0) - jnp.log1p(jnp.exp(-jnp.abs(x)))


def _rms_bf16(x, w):
    ms = jnp.mean(x * x, axis=-1, keepdims=True)
    return (x * lax.rsqrt(ms + NORM_EPS) * w).astype(BF16)


def _proj_kernel(x_ref, lnw_ref, w_ref, rc_ref, rs1_ref, rs2_ref, wa2_ref, ba_ref, *refs,
                 dils, row_major_kv, q_scale):
    q_refs, k_refs, v_refs = refs[0:3], refs[3:6], refs[6:9]
    saz_ref, sgz_ref, sa_ref, sb_ref, gq_ref, gk_ref, gv_ref, la_ref = refs[9:17]
    kv_refs = refs[17:20] if row_major_kv else ()
    stage_ref = refs[-1]
    h = _rms_bf16(x_ref[0], lnw_ref[...])
    tm = h.shape[0]
    rc, rs1, rs2 = rc_ref[...], rs1_ref[...], rs2_ref[...]

    def mm(off, width):
        return _dot_nt(h, w_ref[off:off + width, :])

    def rope(t):
        outs = []
        for j in range(t.shape[1] // LANES):
            c = t[:, j * LANES:(j + 1) * LANES]
            outs.append(c * rc + pltpu.roll(c, ROPE_DIM // 2, axis=1) * rs1
                        + pltpu.roll(c, LANES - ROPE_DIM // 2, axis=1) * rs2)
        return jnp.concatenate(outs, axis=1)

    def store_by_residue(dst_ref, val, d):
        if d == 1:
            dst_ref[0] = val.astype(BF16)
            return
        for c in range(GW // LANES):
            stage_ref[c] = val[:, c * LANES:(c + 1) * LANES]
        for r in range(d):
            for c in range(GW // LANES):
                col = r * GW + c * LANES
                dst_ref[0, :, col:col + LANES] = stage_ref[c, pl.ds(r, tm // d, stride=d), :].astype(BF16)

    for g in range(N_GROUPS):
        qr = rope(mm(OFF_Q + g * GW, GW)) * q_scale
        store_by_residue(q_refs[g], qr, dils[g])
        kr = rope(mm(OFF_K + g * GW, GW))
        store_by_residue(k_refs[g], kr, dils[g])
        vv = mm(OFF_V + g * GW, GW)
        store_by_residue(v_refs[g], vv, dils[g])
        if row_major_kv:
            kv_refs[g][0, :, 0:GW] = kr
            kv_refs[g][0, :, GW:2 * GW] = vv

    az = mm(OFF_AZ, GW)
    saz_ref[0] = (az * _sigmoid(az)).astype(BF16)
    gz = mm(OFF_GZ, GLA_VW)
    sgz_ref[0] = (gz * _sigmoid(gz)).astype(BF16)
    for j in range(D_MODEL // GW):
        sa_ref[0, :, j * GW:(j + 1) * GW] = _sigmoid(mm(OFF_MA + j * GW, GW)).astype(BF16)
        sb_ref[0, :, j * GW:(j + 1) * GW] = _sigmoid(mm(OFF_MB + j * GW, GW)).astype(BF16)
    gq_ref[0] = (mm(OFF_GQ, GLA_KW) * (GLA_DK ** -0.5)).astype(BF16)
    gk_ref[0] = mm(OFF_GK, GLA_KW).astype(BF16)
    gv_ref[0] = mm(OFF_GV, GLA_VW).astype(BF16)
    glr = mm(OFF_LR, GATE_RANK)
    gate_pre = _dot(glr.astype(BF16), wa2_ref[...]) + ba_ref[...]
    la_ref[0] = _log_sigmoid(gate_pre) * (1.0 / GLA_TAU)


def _proj(x, lnw, w_t, rope_tabs, wa2, ba, tm, dils, row_major_kv, q_scale):
    B, S, D = x.shape
    assert S % tm == 0 and all(tm % (16 * d) == 0 for d in dils)
    row = lambda w: pl.BlockSpec((1, tm, w), lambda b, s: (b, s, 0))
    const = lambda shp: pl.BlockSpec(shp, lambda b, s: (0,) * len(shp))
    tab = pl.BlockSpec((tm, LANES), lambda b, s: (s, 0))
    bf = lambda w: jax.ShapeDtypeStruct((B, S, w), BF16)

    out_shape, out_specs = [], []
    for _ in range(3):
        for d in dils:
            out_shape.append(jax.ShapeDtypeStruct((B, S // d, d * GW), BF16))
            out_specs.append(pl.BlockSpec((1, tm // d, d * GW), lambda b, s: (b, s, 0)))
    out_shape += [bf(GW), bf(GLA_VW), bf(D_MODEL), bf(D_MODEL), bf(GLA_KW), bf(GLA_KW), bf(GLA_VW),
                  jax.ShapeDtypeStruct((B, S, GLA_KW), F32)]
    out_specs += [row(GW), row(GLA_VW), row(D_MODEL), row(D_MODEL), row(GLA_KW), row(GLA_KW), row(GLA_VW),
                  row(GLA_KW)]
    if row_major_kv:
        out_shape += [jax.ShapeDtypeStruct((B, S, 2 * GW), F32)] * N_GROUPS
        out_specs += [row(2 * GW)] * N_GROUPS
    in_specs = [row(D), const((1, D)),
                pl.BlockSpec((PROJ_W, D), lambda b, s: (0, 0), pipeline_mode=pl.Buffered(1)),
                tab, tab, tab, const((GATE_RANK, GLA_KW)), const((1, GLA_KW))]
    return pl.pallas_call(
        functools.partial(_proj_kernel, dils=tuple(dils), row_major_kv=row_major_kv, q_scale=q_scale),
        grid=(B, S // tm), in_specs=in_specs, out_specs=out_specs, out_shape=out_shape,
        scratch_shapes=[pltpu.VMEM((GW // LANES, tm, LANES), F32)],
        compiler_params=_cparams(("arbitrary", "arbitrary")), name="proj",
    )(x, lnw, w_t, *rope_tabs, wa2, ba)


def _kvt_kernel(x_ref, lnw_ref, wk_ref, wv_ref, cos_ref, sin_ref, o0_ref, o1_ref, o2_ref, *, plan, nt):
    i = pl.program_id(1)
    h = _rms_bf16(x_ref[...], lnw_ref[...])
    rows = h.shape[0]
    half = ROPE_DIM // 2
    for g, (tiles, cols) in enumerate(plan):
        o_ref = (o0_ref, o1_ref, o2_ref)[g]

        def emit(g=g, cols=cols, o_ref=o_ref):
            hh = h[rows - cols:, :]
            y = _dot_nt(wk_ref[g * GW:(g + 1) * GW, :], hh)
            c = cos_ref[:, rows - cols:]
            s = sin_ref[:, rows - cols:]
            for hd in range(HEADS):
                b0 = hd * HEAD_DIM
                x1 = y[b0:b0 + half, :]
                x2 = y[b0 + half:b0 + ROPE_DIM, :]
                o_ref[b0:b0 + half, :] = x1 * c - x2 * s
                o_ref[b0 + half:b0 + ROPE_DIM, :] = x2 * c + x1 * s
                o_ref[b0 + ROPE_DIM:b0 + HEAD_DIM, :] = y[b0 + ROPE_DIM:b0 + HEAD_DIM, :]
            o_ref[GW:2 * GW, :] = _dot_nt(wv_ref[g * GW:(g + 1) * GW, :], hh)

        if tiles == nt:
            emit()
        else:
            pl.when(i >= nt - tiles)(emit)


def _rope_cos_sin_t(pos):
    half = ROPE_DIM // 2
    inv_freq = 1.0 / (ROPE_THETA ** (jnp.arange(half, dtype=F32) * (2.0 / ROPE_DIM)))
    ang = pos.astype(F32)[:, None] * inv_freq[None, :]
    return jnp.cos(ang).T, jnp.sin(ang).T


def _kv_tail_prompt(x, lnw, w_t, S_tails, tile):
    B, S, D = x.shape
    tile = min(tile, S)
    span = max(S_tails)
    assert span % tile == 0 and S % tile == 0
    nt = span // tile
    first = (S - span) // tile
    plan = []
    for t in S_tails:
        assert t % tile == 0 or t < tile
        plan.append((t // tile, tile) if t >= tile else (1, t))
    cos_t, sin_t = _rope_cos_sin_t(jnp.arange(S, dtype=jnp.int32))
    tabspec = pl.BlockSpec((ROPE_DIM // 2, tile), lambda b, i: (0, first + i))
    out_specs = [pl.BlockSpec((None, 2 * GW, cols), lambda b, i, t=tiles: (b, 0, jnp.maximum(i - (nt - t), 0)))
                 for tiles, cols in plan]
    return pl.pallas_call(
        functools.partial(_kvt_kernel, plan=tuple(plan), nt=nt),
        grid=(B, nt),
        in_specs=[pl.BlockSpec((None, tile, D), lambda b, i: (b, first + i, 0)),
                  pl.BlockSpec((1, D), lambda b, i: (0, 0)),
                  pl.BlockSpec((QKV_W, D), lambda b, i: (OFF_K // QKV_W, 0), pipeline_mode=pl.Buffered(1)),
                  pl.BlockSpec((QKV_W, D), lambda b, i: (OFF_V // QKV_W, 0), pipeline_mode=pl.Buffered(1)),
                  tabspec, tabspec],
        out_specs=out_specs,
        out_shape=[jax.ShapeDtypeStruct((B, 2 * GW, t), F32) for t in S_tails],
        compiler_params=_cparams(("arbitrary", "arbitrary")), name="kv_tail_prompt",
    )(x, lnw, w_t, w_t, cos_t, sin_t)


def _kv_tail_sample(x, lnw, w_t, pos):
    DB, T, D = x.shape
    cos_t, sin_t = _rope_cos_sin_t(pos)
    bcast = lambda t: jnp.broadcast_to(t.T[:, :, None], (T, ROPE_DIM // 2, DB))
    tabspec = pl.BlockSpec((None, ROPE_DIM // 2, DB), lambda b, i: (i, 0, 0))
    out_spec = pl.BlockSpec((None, 2 * GW, DB), lambda b, i: (i, 0, 0))
    return pl.pallas_call(
        functools.partial(_kvt_kernel, plan=((T, DB),) * N_GROUPS, nt=T),
        grid=(1, T),
        in_specs=[pl.BlockSpec((None, DB, D), lambda b, i: (i, 0, 0)),
                  pl.BlockSpec((1, D), lambda b, i: (0, 0)),
                  pl.BlockSpec((QKV_W, D), lambda b, i: (OFF_K // QKV_W, 0), pipeline_mode=pl.Buffered(1)),
                  pl.BlockSpec((QKV_W, D), lambda b, i: (OFF_V // QKV_W, 0), pipeline_mode=pl.Buffered(1)),
                  tabspec, tabspec],
        out_specs=[out_spec] * N_GROUPS,
        out_shape=[jax.ShapeDtypeStruct((T, 2 * GW, DB), F32)] * N_GROUPS,
        compiler_params=_cparams(("arbitrary", "arbitrary")), name="kv_tail_sample",
    )(x.transpose(1, 0, 2), lnw, w_t, w_t, bcast(cos_t), bcast(sin_t))


def _band_bias():
    q = np.arange(BAND)[:, None]
    c = np.arange(2 * BAND)[None, :]
    ok = np.where(c < BAND, c >= q, (c - BAND) <= q)
    ok_first = ok & (c >= BAND)
    return jnp.asarray(np.stack([np.where(ok, 0.0, NEG), np.where(ok_first, 0.0, NEG)]).astype(np.float32))


def _decode_scores(step, nsteps, sq_refs, sc_refs, sb_refs, bd_ref, ss_refs, active, dbs):
    def emit():
        bd = bd_ref[...]
        for i in range(dbs):
            for g in range(N_GROUPS):
                qbd = jnp.tile(sq_refs[g][i].astype(F32), (HEADS, 1)) * bd
                ss_refs[g][i] = _dot(qbd, sc_refs[g][i]) + sb_refs[g][...]

    if active == nsteps:
        emit()
    else:
        pl.when(step < active)(emit)


def _attn_kernel(q_ref, k_ref, kp_ref, v_ref, vp_ref, bias_ref, *refs, qb, rider):
    if rider is None:
        o_ref, lse_ref, kall, vall = refs
    else:
        G = N_GROUPS
        sq_refs, sc_refs, sb_refs, bd_ref = refs[0:G], refs[G:2 * G], refs[2 * G:3 * G], refs[3 * G]
        o_ref, lse_ref = refs[3 * G + 1:3 * G + 3]
        ss_refs = refs[3 * G + 3:4 * G + 3]
        kall, vall = refs[4 * G + 3:]
        active, dbs, nsteps = rider
        step = (pl.program_id(0) * pl.num_programs(1) + pl.program_id(1)) * pl.num_programs(2) + pl.program_id(2)
        _decode_scores(step, nsteps, sq_refs, sc_refs, sb_refs, bd_ref, ss_refs, active, dbs)
    n = pl.program_id(2)
    nsub = qb // BAND
    npair = HEADS // 2
    kall[0:BAND, :] = kp_ref[0]
    kall[BAND:, :] = k_ref[0]
    vall[0:BAND, :] = vp_ref[0]
    vall[BAND:, :] = v_ref[0]
    lane = lax.broadcasted_iota(jnp.int32, (BAND, LANES), 1)
    low_half = lane < HEAD_DIM

    def sub_block(j, carry):
        r0 = pl.multiple_of(j * BAND, BAND)
        first = (n * nsub + j) == 0
        bias = bias_ref[jnp.where(first, 1, 0)]
        bias2 = jnp.concatenate([bias, bias], axis=0)
        scores = []
        for hp in range(npair):
            cs = slice(hp * LANES, (hp + 1) * LANES)
            qp = q_ref[0, pl.ds(r0, BAND), cs]
            zero = jnp.zeros_like(qp)
            qm = jnp.concatenate([jnp.where(low_half, qp, zero), jnp.where(low_half, zero, qp)], axis=0)
            scores.append(_dot_nt(qm, kall[pl.ds(r0, 2 * BAND), cs]) + bias2)
        probs, stats = [], []
        for s in scores:
            m = jnp.max(s, axis=1, keepdims=True)
            p = jnp.exp2(s - m)
            stats.append((m, jnp.sum(p, axis=1, keepdims=True)))
            probs.append(p.astype(BF16))
        m_tile = jnp.zeros((BAND, LANES), F32)
        den_tile = jnp.ones((BAND, LANES), F32)
        for hp in range(npair):
            cs = slice(hp * LANES, (hp + 1) * LANES)
            m, den = stats[hp]
            o = _dot(probs[hp], vall[pl.ds(r0, 2 * BAND), cs])
            o = jnp.where(low_half, o[0:BAND], o[BAND:]) / jnp.where(low_half, den[0:BAND], den[BAND:])
            o_ref[0, pl.ds(r0, BAND), cs] = o.astype(BF16)
            for a in range(2):
                rows = slice(a * BAND, (a + 1) * BAND)
                m_tile = jnp.where(lane == 2 * hp + a, m[rows], m_tile)
                den_tile = jnp.where(lane == 2 * hp + a, den[rows], den_tile)
        lse_ref[0, pl.ds(r0, BAND), :] = (m_tile + jnp.log2(den_tile)) * LN2
        return carry

    lax.fori_loop(0, nsub, sub_block, 0, unroll=True)


def _attn_group(q, k, v, g, qb, decode=None):
    B, L, _ = q.shape
    win, dil = GROUPS[g]
    assert win // dil == BAND
    qb = min(qb, L)
    assert L % qb == 0 and qb % BAND == 0
    nsub = qb // BAND
    nl = L // qb
    cur = pl.BlockSpec((1, qb, GW), lambda b, r, n: (b, n, r))
    prev = pl.BlockSpec((1, BAND, GW), lambda b, r, n: (b, jnp.maximum(n * nsub - 1, 0), r))
    in_specs = [cur, cur, prev, cur, prev, pl.BlockSpec((2, BAND, 2 * BAND), lambda b, r, n: (0, 0, 0))]
    out_specs = [pl.BlockSpec((1, qb, GW), lambda b, r, n: (b, n, r)),
                 pl.BlockSpec((1, qb, LANES), lambda b, r, n: (b, n, r))]
    out_shape = [jax.ShapeDtypeStruct((B, L, dil * GW), BF16),
                 jax.ShapeDtypeStruct((B, L, dil * LANES), F32)]
    args = [q, k, k, v, v, _band_bias()]
    rider = None
    if decode is not None:
        q_list, caches_t, db0, ndb = decode
        T = q_list[0].shape[1]
        nsteps = B * dil * nl
        dbs = -(-ndb // nsteps)
        assert ndb % dbs == 0 and db0 % dbs == 0
        active = ndb // dbs
        rider = (active, dbs, nsteps)
        w_lens = [c.shape[2] for c in caches_t]
        bias_c, _ = _sample_masks(T, w_lens)
        blk = lambda b, r, n: jnp.minimum((b * dil + r) * nl + n, active - 1)
        in_specs += [pl.BlockSpec((dbs, T, GW), lambda b, r, n: (db0 // dbs + blk(b, r, n), 0, 0))
                     for _ in q_list]
        in_specs += [pl.BlockSpec((dbs, GW, w), lambda b, r, n: (db0 // dbs + blk(b, r, n), 0, 0))
                     for w in w_lens]
        in_specs += [pl.BlockSpec(a.shape, lambda b, r, n: (0, 0)) for a in bias_c]
        in_specs += [pl.BlockSpec((HEADS * T, GW), lambda b, r, n: (0, 0))]
        out_specs += [pl.BlockSpec((dbs, HEADS * T, w), lambda b, r, n: (blk(b, r, n), 0, 0)) for w in w_lens]
        out_shape += [jax.ShapeDtypeStruct((ndb, HEADS * T, w), F32) for w in w_lens]
        args += list(q_list) + list(caches_t) + bias_c + [_head_diag(T)]
    return pl.pallas_call(
        functools.partial(_attn_kernel, qb=qb, rider=rider),
        grid=(B, dil, nl), in_specs=in_specs, out_specs=out_specs, out_shape=out_shape,
        scratch_shapes=[pltpu.VMEM((qb + BAND, GW), BF16), pltpu.VMEM((qb + BAND, GW), BF16)],
        compiler_params=_cparams(("arbitrary", "arbitrary", "arbitrary")), name=f"attn_g{g}",
    )(*args)


def _gla_kernel(gq_ref, gk_ref, gv_ref, la_ref, tri_ref, nw_ref, o_ref, sfin_ref, st_ref, *, nchunk):
    c = pl.program_id(1)

    @pl.when(c == 0)
    def _():
        st_ref[...] = jnp.zeros_like(st_ref)

    C = GLA_CHUNK
    npair = GLA_HEADS // 2
    tri3 = tri_ref[...]
    normw = nw_ref[...]
    hi, mid, lo = _split3(la_ref[0])
    b_chunks = []
    for ci in range(nchunk):
        rows = slice(ci * C, (ci + 1) * C)
        b_chunks.append(_dot(tri3, jnp.concatenate([hi[rows], mid[rows], lo[rows]], axis=0)))
    b = jnp.concatenate(b_chunks, axis=0)
    b_last = jnp.concatenate([jnp.broadcast_to(bc[C - 1:C, :], bc.shape) for bc in b_chunks], axis=0)
    gk = gk_ref[0].astype(F32)
    qe = gq_ref[0].astype(F32) * jnp.exp(b)
    ke = (gk * jnp.exp(-b)).astype(BF16)
    kd = (gk * jnp.exp(b_last - b)).astype(BF16)
    gv = gv_ref[0]

    low_half = lax.broadcasted_iota(jnp.int32, (C, LANES), 1) < GLA_DK
    ri = lax.broadcasted_iota(jnp.int32, (2 * C, 2 * C), 0)
    ki = lax.broadcasted_iota(jnp.int32, (2 * C, 2 * C), 1)
    pair_causal = jnp.logical_and((ri < C) == (ki < C),
                                  jnp.bitwise_and(ki, C - 1) <= jnp.bitwise_and(ri, C - 1))

    lhs, vrows, upd, dec = {}, {}, {}, {}
    for ci in range(nchunk):
        rows = slice(ci * C, (ci + 1) * C)
        for p in range(npair):
            cs = slice(p * LANES, (p + 1) * LANES)
            qe_p = qe[rows, cs]
            qm = jnp.concatenate([jnp.where(low_half, qe_p, 0.0), jnp.where(low_half, 0.0, qe_p)],
                                 axis=0).astype(BF16)
            ke_p = ke[rows, cs]
            sc = _dot_nt(qm, jnp.concatenate([ke_p, ke_p], axis=0))
            att = jnp.where(pair_causal, sc, 0.0).astype(BF16)
            lhs[ci, p] = jnp.concatenate([att, qm], axis=1)
            v_ab = gv[rows, 2 * p * GLA_DV:2 * (p + 1) * GLA_DV]
            vrows[ci, p] = jnp.concatenate([v_ab[:, 0:GLA_DV], v_ab[:, GLA_DV:]], axis=0)
            u = _dot_tn(kd[rows, cs], v_ab)
            upd[ci, p] = jnp.concatenate([u[0:GLA_DK, 0:GLA_DV], u[GLA_DK:, GLA_DV:]], axis=0)
            bl = jnp.broadcast_to(b_chunks[ci][C - 1:C, cs], (LANES, LANES))
            dec[ci, p] = jnp.exp(bl.T)

    outs = {}
    states = [st_ref[p] for p in range(npair)]
    for ci in range(nchunk):
        for p in range(npair):
            rhs = jnp.concatenate([vrows[ci, p], states[p].astype(BF16)], axis=0)
            outs[ci, p] = _dot(lhs[ci, p], rhs)
            states[p] = dec[ci, p] * states[p] + upd[ci, p]
    for p in range(npair):
        st_ref[p] = states[p]

    for ci in range(nchunk):
        for p in range(npair):
            o = outs[ci, p]
            o = (o * lax.rsqrt(jnp.mean(o * o, axis=-1, keepdims=True) + NORM_EPS) * normw).astype(BF16)
            for a in range(2):
                hh = 2 * p + a
                o_ref[0, ci * C:(ci + 1) * C, hh * GLA_DV:(hh + 1) * GLA_DV] = o[a * C:(a + 1) * C, :]

    @pl.when(c == pl.num_programs(1) - 1)
    def _():
        for p in range(npair):
            sfin_ref[0, p * LANES:(p + 1) * LANES, :] = states[p]


def _tri3(C, dtype=BF16):
    tri = np.tril(np.ones((C, C), np.float32))
    return jnp.asarray(np.concatenate([tri, tri, tri], axis=1), dtype)


def _gla_prompt(gq, gk, gv, la, normw, ct):
    B, S, _ = gq.shape
    ct = min(ct, S)
    assert S % ct == 0 and ct % GLA_CHUNK == 0
    row = lambda w: pl.BlockSpec((1, ct, w), lambda b, c: (b, c, 0))
    const = lambda shp: pl.BlockSpec(shp, lambda b, c: (0,) * len(shp))
    o, sfin = pl.pallas_call(
        functools.partial(_gla_kernel, nchunk=ct // GLA_CHUNK),
        grid=(B, S // ct),
        in_specs=[row(GLA_KW), row(GLA_KW), row(GLA_VW), row(GLA_KW),
                  const((GLA_CHUNK, 3 * GLA_CHUNK)), const((1, GLA_DV))],
        out_specs=[row(GLA_VW), pl.BlockSpec((1, GLA_KW, GLA_DV), lambda b, c: (b, 0, 0))],
        out_shape=[jax.ShapeDtypeStruct((B, S, GLA_VW), BF16),
                   jax.ShapeDtypeStruct((B, GLA_KW, GLA_DV), F32)],
        scratch_shapes=[pltpu.VMEM((GLA_HEADS // 2, GLA_DV, LANES), F32)],
        compiler_params=_cparams(("arbitrary", "arbitrary")), name="gla_prompt",
    )(gq, gk, gv, la, _tri3(GLA_CHUNK), normw)
    return o, sfin.reshape(B, GLA_HEADS, GLA_DK, GLA_DV)


def _merge_kernel(*refs, dils):
    n_groups = len(dils)
    x_ref = refs[0]
    o_refs = refs[1:1 + n_groups]
    lse_refs = refs[1 + n_groups:1 + 2 * n_groups] if n_groups > 1 else ()
    rest = refs[1 + n_groups + len(lse_refs):]
    (saz_ref, og_ref, sgz_ref, sa_ref, sb_ref, wa_ref, wg_ref, wo_ref, fnw_ref, ex_ref, y_ref) = rest[:11]
    scratch = rest[11:]
    tm = x_ref.shape[0]

    def by_position(ref, width, d, stage_ref):
        if d == 1:
            return lambda rows: ref[rows, :].astype(F32)
        planes = width // LANES
        for r in range(d):
            blk = ref[:, r * width:(r + 1) * width].astype(F32)
            for c in range(planes):
                stage_ref[c, pl.ds(r, tm // d, stride=d), :] = blk[:, c * LANES:(c + 1) * LANES]
        return lambda rows: jnp.concatenate([stage_ref[c, rows, :] for c in range(planes)], axis=1)

    if n_groups > 1:
        lse_rows = [by_position(lse_refs[g], LANES, dils[g], scratch[2 * g + 1]) for g in range(n_groups)]
        o_rows = [by_position(o_refs[g], GW, dils[g], scratch[2 * g]) for g in range(n_groups)]

    rc = min(tm, MERGE_ROWS)
    for r0 in range(0, tm, rc):
        rows = slice(r0, r0 + rc)
        if n_groups > 1:
            ls = [f(rows) for f in lse_rows]
            mx = functools.reduce(jnp.maximum, ls)
            es = [jnp.exp(l - mx) for l in ls]
            inv = 1.0 / functools.reduce(lambda a, b: a + b, es)
            comb = None
            for g in range(n_groups):
                w = es[g] * inv
                w_hi = w.astype(BF16)
                w_lo = (w - w_hi.astype(F32)).astype(BF16)
                wx = _dot(jnp.concatenate([w_hi, w_lo], axis=1), ex_ref[...])
                term = wx * o_rows[g](rows)
                comb = term if comb is None else comb + term
        else:
            comb = o_refs[0][rows, :].astype(F32)

        ua = (comb * saz_ref[rows, :].astype(F32)).astype(BF16)
        ub = (og_ref[rows, :].astype(F32) * sgz_ref[rows, :].astype(F32)).astype(BF16)
        ya = _dot(ua, wa_ref[...])
        yb = _dot(ub, wg_ref[...])
        mixed = (sa_ref[rows, :].astype(F32) * ya + sb_ref[rows, :].astype(F32) * yb).astype(BF16)
        out = x_ref[rows, :] + _dot(mixed, wo_ref[...])
        y_ref[rows, :] = (out * lax.rsqrt(jnp.mean(out * out, axis=-1, keepdims=True) + NORM_EPS)
                          * fnw_ref[...])


def _expand_matrix():
    e = np.zeros((LANES, GW), np.float32)
    for hh in range(HEADS):
        e[hh, hh * HEAD_DIM:(hh + 1) * HEAD_DIM] = 1.0
    return jnp.asarray(np.concatenate([e, e], axis=0), BF16)


def _merge(x, o_list, lse_list, dils, saz, og, sgz, sa, sb, wa, wg, wo, fnw, tm):
    R, D = x.shape
    tm = min(tm, R)
    assert R % tm == 0
    n_groups = len(o_list)
    row = lambda w: pl.BlockSpec((tm, w), lambda i: (i, 0))
    const = lambda shp: pl.BlockSpec(shp, lambda i: (0,) * len(shp))
    in_specs = [row(D)]
    in_specs += [pl.BlockSpec((tm // d, d * GW), lambda i: (i, 0)) for d in dils]
    in_specs += [pl.BlockSpec((tm // d, d * LANES), lambda i: (i, 0)) for d in dils[:len(lse_list)]]
    in_specs += [row(GW), row(GLA_VW), row(GLA_VW), row(D), row(D),
                 const((GW, D)), const((GLA_VW, D)), const((D, D)), const((1, D)), const((2 * LANES, GW))]
    scratch = []
    if n_groups > 1:
        for _ in dils:
            scratch += [pltpu.VMEM((GW // LANES, tm, LANES), F32), pltpu.VMEM((1, tm, LANES), F32)]
    return pl.pallas_call(
        functools.partial(_merge_kernel, dils=tuple(dils)),
        grid=(R // tm,), in_specs=in_specs, out_specs=row(D),
        out_shape=jax.ShapeDtypeStruct((R, D), F32), scratch_shapes=scratch,
        compiler_params=_cparams(("arbitrary",)), name=f"merge_g{n_groups}",
    )(x, *o_list, *lse_list, saz, og, sgz, sa, sb, wa, wg, wo, fnw, _expand_matrix())


def _sample_masks(T, w_lens):
    t_of_row = np.arange(HEADS * T) % T

    def bias(idx, g):
        win, dil = GROUPS[g]
        dd = w_lens[g] + t_of_row[:, None] - idx[None, :]
        ok = (dd >= 0) & (dd % dil == 0) & (dd // dil <= win // dil)
        return jnp.asarray(np.where(ok, 0.0, NEG).astype(np.float32))

    cache = [bias(np.arange(w_lens[g]), g) for g in range(N_GROUPS)]
    new = [bias(w_lens[g] + np.arange(T), g) for g in range(N_GROUPS)]
    return cache, new


def _sattn_kernel(q0_ref, q1_ref, q2_ref, n0_ref, n1_ref, n2_ref, s0_ref, s1_ref, s2_ref,
                  c0_ref, c1_ref, c2_ref, bn0_ref, bn1_ref, bn2_ref, bd_ref, o_ref):
    T = q0_ref.shape[1]
    bd = bd_ref[...]
    q_refs, new_refs, c_refs = (q0_ref, q1_ref, q2_ref), (n0_ref, n1_ref, n2_ref), (c0_ref, c1_ref, c2_ref)
    s_refs, bn_refs = (s0_ref, s1_ref, s2_ref), (bn0_ref, bn1_ref, bn2_ref)
    s_new, s_old, v_new, v_old = [], [], [], []
    for g in range(N_GROUPS):
        qbd = jnp.tile(q_refs[g][0].astype(F32), (HEADS, 1)) * bd
        s_new.append(_dot_nt(qbd, _rb(new_refs[g][0, :, 0:GW])) + bn_refs[g][...])
        v_new.append(_rb(new_refs[g][0, :, GW:2 * GW]))
        s_old.append(s_refs[g][0])
        v_old.append(c_refs[g][0].astype(BF16))
    m = functools.reduce(jnp.maximum, [jnp.max(s, axis=1, keepdims=True) for s in s_new + s_old])
    p_new = [jnp.exp(s - m) for s in s_new]
    p_old = [jnp.exp(s - m) for s in s_old]
    den = functools.reduce(lambda a, b: a + b, [jnp.sum(p, axis=1, keepdims=True) for p in p_new + p_old])
    inv = 1.0 / den
    acc = None
    for g in range(N_GROUPS):
        r = _dot(_rb(p_new[g] * inv), v_new[g]) + _dot_nt((p_old[g] * inv).astype(BF16), v_old[g])
        acc = r if acc is None else acc + r
    acc = acc * bd
    out = acc[0:T, :]
    for hh in range(1, HEADS):
        out = out + acc[hh * T:(hh + 1) * T, :]
    o_ref[0] = out


def _head_diag(T):
    bd = np.zeros((HEADS * T, GW), np.float32)
    for hh in range(HEADS):
        bd[hh * T:(hh + 1) * T, hh * HEAD_DIM:(hh + 1) * HEAD_DIM] = 1.0
    return jnp.asarray(bd)


def _sample_attn(q_list, kv_new, scores, caches_t, db0, T):
    ndb = scores[0].shape[0]
    w_lens = [c.shape[2] for c in caches_t]
    _, bias_n = _sample_masks(T, w_lens)
    consts = bias_n + [_head_diag(T)]
    off = lambda a: pl.BlockSpec((1,) + a.shape[1:], lambda i: (db0 + i,) + (0,) * (a.ndim - 1))
    local = lambda a: pl.BlockSpec((1,) + a.shape[1:], lambda i: (i,) + (0,) * (a.ndim - 1))
    vals = lambda a: pl.BlockSpec((1, GW, a.shape[2]), lambda i: (db0 + i, 1, 0))
    const = lambda a: pl.BlockSpec(a.shape, lambda i: (0,) * a.ndim)
    return pl.pallas_call(
        _sattn_kernel, grid=(ndb,),
        in_specs=[off(a) for a in list(q_list) + list(kv_new)] + [local(a) for a in scores]
                 + [vals(a) for a in caches_t] + [const(a) for a in consts],
        out_specs=pl.BlockSpec((1, T, GW), lambda i: (i, 0, 0)),
        out_shape=jax.ShapeDtypeStruct((ndb, T, GW), F32),
        compiler_params=_cparams(("arbitrary",)), name="sample_attn",
    )(*q_list, *kv_new, *scores, *caches_t, *consts)


def _sgla_kernel(gq_ref, gk_ref, gv_ref, la_ref, st_ref, tri_ref, nw_ref, o_ref, snew_ref, *, dbt):
    tri3 = tri_ref[...]
    normw = nw_ref[...]
    T = gq_ref.shape[1]
    ti = lax.broadcasted_iota(jnp.int32, (T, T), 0)
    si = lax.broadcasted_iota(jnp.int32, (T, T), 1)
    causal = si <= ti
    low_half = lax.broadcasted_iota(jnp.int32, (T, LANES), 1) < GLA_DK
    ones = jnp.ones((3 * T, LANES), F32)
    row_low = lax.broadcasted_iota(jnp.int32, (LANES, GLA_DV), 0) < GLA_DK

    def body(i, carry):
        la3 = jnp.concatenate(_split3(la_ref[i], F32), axis=0)
        b = _dot(tri3, la3)
        b_last = b[T - 1:T, :]
        gk = gk_ref[i].astype(F32)
        qe = gq_ref[i].astype(F32) * jnp.exp(b)
        ke = _rb(gk * jnp.exp(-b))
        kd = _rb(gk * jnp.exp(b_last - b))
        gv = gv_ref[i].astype(F32)
        outs = []
        for p in range(GLA_HEADS // 2):
            cs = slice(p * LANES, (p + 1) * LANES)
            st = st_ref[i, cs, :]
            st_b = _rb(st)
            dec = jnp.exp(_dot_tn(la3[:, cs], ones))
            upd = None
            for a in range(2):
                hh = 2 * p + a
                sel = low_half if a == 0 else jnp.logical_not(low_half)
                qm = _rb(jnp.where(sel, qe[:, cs], 0.0))
                att = _rb(jnp.where(causal, _dot_nt(qm, ke[:, cs]), 0.0))
                v_h = gv[:, hh * GLA_DV:(hh + 1) * GLA_DV]
                o = _dot(att, v_h) + _dot(qm, st_b)
                outs.append(o * lax.rsqrt(jnp.mean(o * o, axis=-1, keepdims=True) + NORM_EPS) * normw)
                u = _dot_tn(kd[:, cs], v_h)
                upd = u if a == 0 else jnp.where(row_low, upd, u)
            snew_ref[i, cs, :] = st * dec + upd
        o_ref[i] = jnp.concatenate(outs, axis=1)
        return carry

    for i in range(dbt):
        body(i, 0)


def _sample_gla(gq, gk, gv, la, state, normw, dbt):
    DB, T, _ = gq.shape
    dbt = min(dbt, DB)
    assert DB % dbt == 0
    st = state.reshape(DB, GLA_KW, GLA_DV)
    blk = lambda shp: pl.BlockSpec((dbt,) + shp, lambda i: (i,) + (0,) * len(shp))
    const = lambda shp: pl.BlockSpec(shp, lambda i: (0,) * len(shp))
    o, snew = pl.pallas_call(
        functools.partial(_sgla_kernel, dbt=dbt),
        grid=(DB // dbt,),
        in_specs=[blk((T, GLA_KW)), blk((T, GLA_KW)), blk((T, GLA_VW)), blk((T, GLA_KW)),
                  blk((GLA_KW, GLA_DV)), const((T, 3 * T)), const((1, GLA_DV))],
        out_specs=[blk((T, GLA_VW)), blk((GLA_KW, GLA_DV))],
        out_shape=[jax.ShapeDtypeStruct((DB, T, GLA_VW), F32),
                   jax.ShapeDtypeStruct((DB, GLA_KW, GLA_DV), F32)],
        compiler_params=_cparams(("arbitrary",)), name="gla_sample",
    )(gq, gk, gv, la, st, _tri3(T, F32), normw)
    return o, snew.reshape(DB, GLA_HEADS, GLA_DK, GLA_DV)


def _rope_tables(pos):
    half = ROPE_DIM // 2
    inv_freq = 1.0 / (ROPE_THETA ** (jnp.arange(half, dtype=F32) * (2.0 / ROPE_DIM)))
    ang = pos.astype(F32)[:, None] * inv_freq[None, :]
    cos, sin = jnp.cos(ang), jnp.sin(ang)
    n = pos.shape[0]
    pad = jnp.zeros((n, HEAD_DIM - ROPE_DIM), F32)
    zero = jnp.zeros((n, half), F32)
    c = jnp.concatenate([cos, cos, pad + 1.0], axis=1)
    s1 = jnp.concatenate([zero, sin, pad], axis=1)
    s2 = jnp.concatenate([-sin, zero, pad], axis=1)
    rep = LANES // HEAD_DIM
    return tuple(jnp.tile(t, (1, rep)) for t in (c, s1, s2))


def kernel(x_prompt, x_sample, cache_kv_w128, cache_kv_w512, cache_kv_w2048, state_gla,
           ln_w, w_in, w_gla_a2, b_gla_a, gla_norm_w, w_attn_out, w_gla_out, w_out, final_norm_w):
    B, S, D = x_prompt.shape
    DB, T, _ = x_sample.shape
    assert ln_w.shape[0] == 1, "single-layer step"
    caches = (cache_kv_w128, cache_kv_w512, cache_kv_w2048)
    dils = [d for _, d in GROUPS]

    lnw = ln_w[0].reshape(1, D)
    w_t = w_in[0].T.astype(BF16)
    wa2 = w_gla_a2[0].astype(BF16)
    ba = b_gla_a[0].reshape(1, GLA_KW)
    gnw = gla_norm_w[0].reshape(1, GLA_DV)
    wa = w_attn_out[0].astype(BF16)
    wg = w_gla_out[0].astype(BF16)
    wo = w_out[0].astype(BF16)
    fnw = final_norm_w.reshape(1, D)

    R = DB * T
    pos_s = PAST_LEN + jnp.arange(T, dtype=jnp.int32)
    tabs_s = tuple(jnp.tile(t, (DB, 1)) for t in _rope_tables(pos_s))
    outs_s = _proj(x_sample.reshape(1, R, D), lnw, w_t, tabs_s, wa2, ba, min(512, R), (1,) * N_GROUPS,
                   row_major_kv=True, q_scale=HEAD_DIM ** -0.5)
    per_db = lambda t: t.reshape(DB, T, t.shape[-1])
    q_s = [per_db(q) for q in outs_s[0:3]]
    caches_t = [c[0].transpose(0, 2, 3, 4, 1).reshape(DB, 2 * GW, c.shape[2]) for c in caches]
    assert DB % 2 == 0
    halves = ((0, DB // 2), (DB // 2, DB // 2))

    tabs_p = _rope_tables(jnp.arange(S, dtype=jnp.int32))
    outs = _proj(x_prompt, lnw, w_t, tabs_p, wa2, ba, min(512, S), dils, row_major_kv=False,
                 q_scale=HEAD_DIM ** -0.5 * LOG2E)
    saz, sgz, sa, sb, gq, gk, gv, la = outs[9:17]
    o_list, lse_list, scores = [], [], []
    for g in range(N_GROUPS):
        decode = (q_s, caches_t) + halves[g] if g < len(halves) else None
        res = _attn_group(outs[g], outs[3 + g], outs[6 + g], g, qb=512, decode=decode)
        o_list.append(res[0].reshape(-1, res[0].shape[-1]))
        lse_list.append(res[1].reshape(-1, res[1].shape[-1]))
        if decode is not None:
            scores.append(res[2:])
    og, gla_p = _gla_prompt(gq, gk, gv, la, gnw, ct=512)
    flat = lambda t: t.reshape(B * S, t.shape[-1])
    y_prompt = _merge(flat(x_prompt), o_list, lse_list, dils, flat(saz), flat(og), flat(sgz), flat(sa),
                      flat(sb), wa, wg, wo, fnw, tm=512).reshape(B, S, D)
    kvt = _kv_tail_prompt(x_prompt, lnw, w_t, [min(w, S) for w, _ in GROUPS], tile=512)
    kv_p = [t.reshape(B, 2, HEADS, HEAD_DIM, t.shape[-1]).transpose(0, 4, 1, 2, 3)[None] for t in kvt]

    saz, sgz, sa, sb, gq, gk, gv, la = outs_s[9:17]
    kv_new = [per_db(t) for t in outs_s[17:20]]
    o_s = jnp.concatenate([_sample_attn(q_s, kv_new, sc, caches_t, db0, T)
                           for sc, (db0, _) in zip(scores, halves)], axis=0)
    og_s, gla_s = _sample_gla(per_db(gq), per_db(gk), per_db(gv), per_db(la), state_gla[0], gnw, dbt=8)
    flat_s = lambda t: t.reshape(R, t.shape[-1])
    y_sample = _merge(x_sample.reshape(R, D), [o_s.reshape(R, GW)], [], (1,), flat_s(saz), flat_s(og_s),
                      flat_s(sgz), flat_s(sa), flat_s(sb), wa, wg, wo, fnw, tm=512).reshape(DB, T, D)
    kvt_s = _kv_tail_sample(x_sample, lnw, w_t, pos_s)
    kv_s = [t.reshape(T, 2, HEADS, HEAD_DIM, DB).transpose(4, 0, 1, 2, 3)[None] for t in kvt_s]

    return (y_prompt, y_sample, kv_p[0], kv_p[1], kv_p[2], gla_p[None],
            kv_s[0], kv_s[1], kv_s[2], gla_s[None])
```

```python
import functools

import numpy as np
import jax
import jax.numpy as jnp
from jax import lax
from jax.experimental import pallas as pl
from jax.experimental.pallas import tpu as pltpu

F32 = jnp.float32
BF16 = jnp.bfloat16

D_MODEL = 1024
HEAD_DIM = 64
HEADS = 8
GROUPS = ((128, 1), (512, 4), (2048, 16))
N_GROUPS = len(GROUPS)
GW = HEADS * HEAD_DIM
QKV_W = N_GROUPS * GW
ROPE_DIM = HEAD_DIM // 4
ROPE_THETA = 500000.0
BAND = 128
GLA_HEADS = 4
GLA_DK = 64
GLA_DV = 128
GLA_KW = GLA_HEADS * GLA_DK
GLA_VW = GLA_HEADS * GLA_DV
GATE_RANK = 16
GLA_TAU = 16.0
GLA_CHUNK = 64
NORM_EPS = 1e-6
PAST_LEN = 8192

LANES = 128
NEG = -1e30
LOG2E = float(np.log2(np.e))
LN2 = float(np.log(2.0))

OFF_Q = 0
OFF_K = OFF_Q + QKV_W
OFF_V = OFF_K + QKV_W
OFF_AZ = OFF_V + QKV_W
OFF_GQ = OFF_AZ + GW
OFF_GK = OFF_GQ + GLA_KW
OFF_GV = OFF_GK + GLA_KW
OFF_GZ = OFF_GV + GLA_VW
OFF_LR = OFF_GZ + GLA_VW
OFF_MA = OFF_LR + GATE_RANK
OFF_MB = OFF_MA + D_MODEL
PROJ_W = OFF_MB + D_MODEL
assert all(o % 16 == 0 for o in (OFF_LR, OFF_MA, OFF_MB, PROJ_W))

VMEM_LIMIT = 56 * 1024 * 1024
MERGE_ROWS = 512


def _cparams(sem):
    return pltpu.CompilerParams(dimension_semantics=sem, vmem_limit_bytes=VMEM_LIMIT)


def _dot(a, b):
    return jnp.dot(a, b, preferred_element_type=F32)


def _dot_nt(a, b):
    return lax.dot_general(a, b, (((1,), (1,)), ((), ())), preferred_element_type=F32)


def _dot_tn(a, b):
    return lax.dot_general(a, b, (((0,), (0,)), ((), ())), preferred_element_type=F32)


def _rb(x):
    return x.astype(BF16).astype(F32)


def _split3(x, dtype=BF16):
    hi = x.astype(BF16)
    r1 = x - hi.astype(F32)
    mid = r1.astype(BF16)
    lo = (r1 - mid.astype(F32)).astype(BF16)
    return hi.astype(dtype), mid.astype(dtype), lo.astype(dtype)


def _sigmoid(x):
    return 1.0 / (1.0 + jnp.exp(-x))


def _log_sigmoid(x):
    return jnp.minimum(x, 0.0) - jnp.log1p(jnp.exp(-jnp.abs(x)))


def _rms_bf16(x, w):
    ms = jnp.mean(x * x, axis=-1, keepdims=True)
    return (x * lax.rsqrt(ms + NORM_EPS) * w).astype(BF16)


def _run_rider(rider, in_refs, out_refs):
    body, active, dbs, nsteps = rider
    if active == nsteps:
        body(in_refs, out_refs, dbs)
    else:
        step = pl.program_id(0) * pl.num_programs(1) + pl.program_id(1)
        pl.when(step < active)(lambda: body(in_refs, out_refs, dbs))


def _proj_qkv_kernel(x_ref, lnw_ref, w_ref, rc_ref, rs1_ref, rs2_ref, *refs,
                     dils, row_major_kv, q_scale, rider, n_rider_in):
    rider_in, refs = refs[:n_rider_in], refs[n_rider_in:]
    q_refs, k_refs, v_refs = refs[0:3], refs[3:6], refs[6:9]
    kv_refs = refs[9:12] if row_major_kv else ()
    rider_out = refs[9 + len(kv_refs):-1]
    stage_ref = refs[-1]
    h = _rms_bf16(x_ref[0], lnw_ref[...])
    tm = h.shape[0]
    rc, rs1, rs2 = rc_ref[...], rs1_ref[...], rs2_ref[...]

    def mm(off, width):
        return _dot_nt(h, w_ref[off:off + width, :])

    def rope(t):
        outs = []
        for j in range(t.shape[1] // LANES):
            c = t[:, j * LANES:(j + 1) * LANES]
            outs.append(c * rc + pltpu.roll(c, ROPE_DIM // 2, axis=1) * rs1
                        + pltpu.roll(c, LANES - ROPE_DIM // 2, axis=1) * rs2)
        return jnp.concatenate(outs, axis=1)

    def store_by_residue(dst_ref, val, d):
        if d == 1:
            dst_ref[0] = val.astype(BF16)
            return
        for c in range(GW // LANES):
            stage_ref[c] = val[:, c * LANES:(c + 1) * LANES]
        for r in range(d):
            for c in range(GW // LANES):
                col = r * GW + c * LANES
                dst_ref[0, :, col:col + LANES] = stage_ref[c, pl.ds(r, tm // d, stride=d), :].astype(BF16)

    for g in range(N_GROUPS):
        qr = rope(mm(OFF_Q + g * GW, GW)) * q_scale
        store_by_residue(q_refs[g], qr, dils[g])
        kr = rope(mm(OFF_K + g * GW, GW))
        store_by_residue(k_refs[g], kr, dils[g])
        vv = mm(OFF_V + g * GW, GW)
        store_by_residue(v_refs[g], vv, dils[g])
        if row_major_kv:
            kv_refs[g][0, :, 0:GW] = kr
            kv_refs[g][0, :, GW:2 * GW] = vv
    if rider is not None:
        _run_rider(rider, rider_in, rider_out)


def _proj_rest_kernel(x_ref, lnw_ref, w_ref, wa2_ref, ba_ref, *refs, rider, n_rider_in):
    rider_in, refs = refs[:n_rider_in], refs[n_rider_in:]
    saz_ref, sgz_ref, sa_ref, sb_ref, gq_ref, gk_ref, gv_ref, la_ref = refs[0:8]
    rider_out = refs[8:]
    h = _rms_bf16(x_ref[0], lnw_ref[...])

    def mm(off, width):
        return _dot_nt(h, w_ref[off - OFF_AZ:off - OFF_AZ + width, :])

    az = mm(OFF_AZ, GW)
    saz_ref[0] = (az * _sigmoid(az)).astype(BF16)
    gz = mm(OFF_GZ, GLA_VW)
    sgz_ref[0] = (gz * _sigmoid(gz)).astype(BF16)
    for j in range(D_MODEL // GW):
        sa_ref[0, :, j * GW:(j + 1) * GW] = _sigmoid(mm(OFF_MA + j * GW, GW)).astype(BF16)
        sb_ref[0, :, j * GW:(j + 1) * GW] = _sigmoid(mm(OFF_MB + j * GW, GW)).astype(BF16)
    gq_ref[0] = (mm(OFF_GQ, GLA_KW) * (GLA_DK ** -0.5)).astype(BF16)
    gk_ref[0] = mm(OFF_GK, GLA_KW).astype(BF16)
    gv_ref[0] = mm(OFF_GV, GLA_VW).astype(BF16)
    glr = mm(OFF_LR, GATE_RANK)
    gate_pre = _dot(glr.astype(BF16), wa2_ref[...]) + ba_ref[...]
    la_ref[0] = _log_sigmoid(gate_pre) * (1.0 / GLA_TAU)
    if rider is not None:
        _run_rider(rider, rider_in, rider_out)


def _rider_parts(make_rider, B, nt):
    if make_rider is None:
        return None, [], [], [], []
    r = make_rider(B * nt, lambda b, s: b * nt + s)
    return (r["body"], r["active"], r["dbs"], B * nt), r["arrays"], r["in_specs"], r["out_shape"], r["out_specs"]


def _proj_qkv(x, lnw, w_qkv, rope_tabs, tm, dils, row_major_kv, q_scale, make_rider=None):
    B, S, D = x.shape
    assert S % tm == 0 and all(tm % (16 * d) == 0 for d in dils)
    row = lambda w: pl.BlockSpec((1, tm, w), lambda b, s: (b, s, 0))
    tab = pl.BlockSpec((tm, LANES), lambda b, s: (s, 0))
    rider, r_arrays, r_in_specs, r_out_shape, r_out_specs = _rider_parts(make_rider, B, S // tm)
    out_shape, out_specs = [], []
    for _ in range(3):
        for d in dils:
            out_shape.append(jax.ShapeDtypeStruct((B, S // d, d * GW), BF16))
            out_specs.append(pl.BlockSpec((1, tm // d, d * GW), lambda b, s: (b, s, 0)))
    if row_major_kv:
        out_shape += [jax.ShapeDtypeStruct((B, S, 2 * GW), F32)] * N_GROUPS
        out_specs += [row(2 * GW)] * N_GROUPS
    in_specs = [row(D), pl.BlockSpec((1, D), lambda b, s: (0, 0)),
                pl.BlockSpec(w_qkv.shape, lambda b, s: (0, 0), pipeline_mode=pl.Buffered(1)),
                tab, tab, tab]
    return pl.pallas_call(
        functools.partial(_proj_qkv_kernel, dils=tuple(dils), row_major_kv=row_major_kv, q_scale=q_scale,
                          rider=rider, n_rider_in=len(r_arrays)),
        grid=(B, S // tm), in_specs=in_specs + r_in_specs, out_specs=out_specs + r_out_specs,
        out_shape=out_shape + r_out_shape,
        scratch_shapes=[pltpu.VMEM((GW // LANES, tm, LANES), F32)],
        compiler_params=_cparams(("arbitrary", "arbitrary")), name="proj_qkv",
    )(x, lnw, w_qkv, *rope_tabs, *r_arrays)


def _proj_rest(x, lnw, w_rest, wa2, ba, tm, make_rider=None):
    B, S, D = x.shape
    assert S % tm == 0
    row = lambda w: pl.BlockSpec((1, tm, w), lambda b, s: (b, s, 0))
    const = lambda shp: pl.BlockSpec(shp, lambda b, s: (0,) * len(shp))
    bf = lambda w: jax.ShapeDtypeStruct((B, S, w), BF16)
    rider, r_arrays, r_in_specs, r_out_shape, r_out_specs = _rider_parts(make_rider, B, S // tm)
    out_shape = [bf(GW), bf(GLA_VW), bf(D_MODEL), bf(D_MODEL), bf(GLA_KW), bf(GLA_KW), bf(GLA_VW),
                 jax.ShapeDtypeStruct((B, S, GLA_KW), F32)]
    out_specs = [row(GW), row(GLA_VW), row(D_MODEL), row(D_MODEL), row(GLA_KW), row(GLA_KW), row(GLA_VW),
                 row(GLA_KW)]
    in_specs = [row(D), const((1, D)),
                pl.BlockSpec(w_rest.shape, lambda b, s: (0, 0), pipeline_mode=pl.Buffered(1)),
                const((GATE_RANK, GLA_KW)), const((1, GLA_KW))]
    return pl.pallas_call(
        functools.partial(_proj_rest_kernel, rider=rider, n_rider_in=len(r_arrays)),
        grid=(B, S // tm), in_specs=in_specs + r_in_specs, out_specs=out_specs + r_out_specs,
        out_shape=out_shape + r_out_shape,
        compiler_params=_cparams(("arbitrary", "arbitrary")), name="proj_rest",
    )(x, lnw, w_rest, wa2, ba, *r_arrays)


def _kvt_kernel(x_ref, lnw_ref, wk_ref, wv_ref, cos_ref, sin_ref, o0_ref, o1_ref, o2_ref, *, plan, nt):
    i = pl.program_id(1)
    h = _rms_bf16(x_ref[...], lnw_ref[...])
    rows = h.shape[0]
    half = ROPE_DIM // 2
    for g, (tiles, cols) in enumerate(plan):
        o_ref = (o0_ref, o1_ref, o2_ref)[g]

        def emit(g=g, cols=cols, o_ref=o_ref):
            hh = h[rows - cols:, :]
            y = _dot_nt(wk_ref[g * GW:(g + 1) * GW, :], hh)
            c = cos_ref[:, rows - cols:]
            s = sin_ref[:, rows - cols:]
            for hd in range(HEADS):
                b0 = hd * HEAD_DIM
                x1 = y[b0:b0 + half, :]
                x2 = y[b0 + half:b0 + ROPE_DIM, :]
                o_ref[b0:b0 + half, :] = x1 * c - x2 * s
                o_ref[b0 + half:b0 + ROPE_DIM, :] = x2 * c + x1 * s
                o_ref[b0 + ROPE_DIM:b0 + HEAD_DIM, :] = y[b0 + ROPE_DIM:b0 + HEAD_DIM, :]
            o_ref[GW:2 * GW, :] = _dot_nt(wv_ref[g * GW:(g + 1) * GW, :], hh)

        if tiles == nt:
            emit()
        else:
            pl.when(i >= nt - tiles)(emit)


def _rope_cos_sin_t(pos):
    half = ROPE_DIM // 2
    inv_freq = 1.0 / (ROPE_THETA ** (jnp.arange(half, dtype=F32) * (2.0 / ROPE_DIM)))
    ang = pos.astype(F32)[:, None] * inv_freq[None, :]
    return jnp.cos(ang).T, jnp.sin(ang).T


def _kv_tail_prompt(x, lnw, w_t, S_tails, tile):
    B, S, D = x.shape
    tile = min(tile, S)
    span = max(S_tails)
    assert span % tile == 0 and S % tile == 0
    nt = span // tile
    first = (S - span) // tile
    plan = []
    for t in S_tails:
        assert t % tile == 0 or t < tile
        plan.append((t // tile, tile) if t >= tile else (1, t))
    cos_t, sin_t = _rope_cos_sin_t(jnp.arange(S, dtype=jnp.int32))
    tabspec = pl.BlockSpec((ROPE_DIM // 2, tile), lambda b, i: (0, first + i))
    out_specs = [pl.BlockSpec((None, 2 * GW, cols), lambda b, i, t=tiles: (b, 0, jnp.maximum(i - (nt - t), 0)))
                 for tiles, cols in plan]
    return pl.pallas_call(
        functools.partial(_kvt_kernel, plan=tuple(plan), nt=nt),
        grid=(B, nt),
        in_specs=[pl.BlockSpec((None, tile, D), lambda b, i: (b, first + i, 0)),
                  pl.BlockSpec((1, D), lambda b, i: (0, 0)),
                  pl.BlockSpec((QKV_W, D), lambda b, i: (OFF_K // QKV_W, 0), pipeline_mode=pl.Buffered(1)),
                  pl.BlockSpec((QKV_W, D), lambda b, i: (OFF_V // QKV_W, 0), pipeline_mode=pl.Buffered(1)),
                  tabspec, tabspec],
        out_specs=out_specs,
        out_shape=[jax.ShapeDtypeStruct((B, 2 * GW, t), F32) for t in S_tails],
        compiler_params=_cparams(("arbitrary", "arbitrary")), name="kv_tail_prompt",
    )(x, lnw, w_t, w_t, cos_t, sin_t)


def _kv_tail_sample(x, lnw, w_t, pos):
    DB, T, D = x.shape
    cos_t, sin_t = _rope_cos_sin_t(pos)
    bcast = lambda t: jnp.broadcast_to(t.T[:, :, None], (T, ROPE_DIM // 2, DB))
    tabspec = pl.BlockSpec((None, ROPE_DIM // 2, DB), lambda b, i: (i, 0, 0))
    out_spec = pl.BlockSpec((None, 2 * GW, DB), lambda b, i: (i, 0, 0))
    return pl.pallas_call(
        functools.partial(_kvt_kernel, plan=((T, DB),) * N_GROUPS, nt=T),
        grid=(1, T),
        in_specs=[pl.BlockSpec((None, DB, D), lambda b, i: (i, 0, 0)),
                  pl.BlockSpec((1, D), lambda b, i: (0, 0)),
                  pl.BlockSpec((QKV_W, D), lambda b, i: (OFF_K // QKV_W, 0), pipeline_mode=pl.Buffered(1)),
                  pl.BlockSpec((QKV_W, D), lambda b, i: (OFF_V // QKV_W, 0), pipeline_mode=pl.Buffered(1)),
                  tabspec, tabspec],
        out_specs=[out_spec] * N_GROUPS,
        out_shape=[jax.ShapeDtypeStruct((T, 2 * GW, DB), F32)] * N_GROUPS,
        compiler_params=_cparams(("arbitrary", "arbitrary")), name="kv_tail_sample",
    )(x.transpose(1, 0, 2), lnw, w_t, w_t, bcast(cos_t), bcast(sin_t))


def _band_bias():
    q = np.arange(BAND)[:, None]
    c = np.arange(2 * BAND)[None, :]
    ok = np.where(c < BAND, c >= q, (c - BAND) <= q)
    ok_first = ok & (c >= BAND)
    return jnp.asarray(np.stack([np.where(ok, 0.0, NEG), np.where(ok_first, 0.0, NEG)]).astype(np.float32))


def _attn_kernel(q_ref, k_ref, kp_ref, v_ref, vp_ref, bias_ref, o_ref, lse_ref, kall, vall, *, qb):
    n = pl.program_id(2)
    nsub = qb // BAND
    npair = HEADS // 2
    kall[0:BAND, :] = kp_ref[0]
    kall[BAND:, :] = k_ref[0]
    vall[0:BAND, :] = vp_ref[0]
    vall[BAND:, :] = v_ref[0]
    lane = lax.broadcasted_iota(jnp.int32, (BAND, LANES), 1)
    low_half = lane < HEAD_DIM

    def sub_block(j, carry):
        r0 = pl.multiple_of(j * BAND, BAND)
        first = (n * nsub + j) == 0
        bias = bias_ref[jnp.where(first, 1, 0)]
        bias2 = jnp.concatenate([bias, bias], axis=0)
        scores = []
        for hp in range(npair):
            cs = slice(hp * LANES, (hp + 1) * LANES)
            qp = q_ref[0, pl.ds(r0, BAND), cs]
            zero = jnp.zeros_like(qp)
            qm = jnp.concatenate([jnp.where(low_half, qp, zero), jnp.where(low_half, zero, qp)], axis=0)
            scores.append(_dot_nt(qm, kall[pl.ds(r0, 2 * BAND), cs]) + bias2)
        probs, stats = [], []
        for s in scores:
            m = jnp.max(s, axis=1, keepdims=True)
            p = jnp.exp2(s - m)
            stats.append((m, jnp.sum(p, axis=1, keepdims=True)))
            probs.append(p.astype(BF16))
        m_tile = jnp.zeros((BAND, LANES), F32)
        den_tile = jnp.ones((BAND, LANES), F32)
        for hp in range(npair):
            cs = slice(hp * LANES, (hp + 1) * LANES)
            m, den = stats[hp]
            o = _dot(probs[hp], vall[pl.ds(r0, 2 * BAND), cs])
            o = jnp.where(low_half, o[0:BAND], o[BAND:]) / jnp.where(low_half, den[0:BAND], den[BAND:])
            o_ref[0, pl.ds(r0, BAND), cs] = o.astype(BF16)
            for a in range(2):
                rows = slice(a * BAND, (a + 1) * BAND)
                m_tile = jnp.where(lane == 2 * hp + a, m[rows], m_tile)
                den_tile = jnp.where(lane == 2 * hp + a, den[rows], den_tile)
        lse_ref[0, pl.ds(r0, BAND), :] = (m_tile + jnp.log2(den_tile)) * LN2
        return carry

    lax.fori_loop(0, nsub, sub_block, 0, unroll=True)


def _attn_group(q, k, v, g, qb):
    B, L, _ = q.shape
    win, dil = GROUPS[g]
    assert win // dil == BAND
    qb = min(qb, L)
    assert L % qb == 0 and qb % BAND == 0
    nsub = qb // BAND
    cur = pl.BlockSpec((1, qb, GW), lambda b, r, n: (b, n, r))
    prev = pl.BlockSpec((1, BAND, GW), lambda b, r, n: (b, jnp.maximum(n * nsub - 1, 0), r))
    return pl.pallas_call(
        functools.partial(_attn_kernel, qb=qb),
        grid=(B, dil, L // qb),
        in_specs=[cur, cur, prev, cur, prev, pl.BlockSpec((2, BAND, 2 * BAND), lambda b, r, n: (0, 0, 0))],
        out_specs=[pl.BlockSpec((1, qb, GW), lambda b, r, n: (b, n, r)),
                   pl.BlockSpec((1, qb, LANES), lambda b, r, n: (b, n, r))],
        out_shape=[jax.ShapeDtypeStruct((B, L, dil * GW), BF16),
                   jax.ShapeDtypeStruct((B, L, dil * LANES), F32)],
        scratch_shapes=[pltpu.VMEM((qb + BAND, GW), BF16), pltpu.VMEM((qb + BAND, GW), BF16)],
        compiler_params=_cparams(("arbitrary", "arbitrary", "arbitrary")), name=f"attn_g{g}",
    )(q, k, k, v, v, _band_bias())


def _gla_kernel(gq_ref, gk_ref, gv_ref, la_ref, tri_ref, nw_ref, o_ref, sfin_ref, st_ref, *, nchunk):
    c = pl.program_id(1)

    @pl.when(c == 0)
    def _():
        st_ref[...] = jnp.zeros_like(st_ref)

    C = GLA_CHUNK
    npair = GLA_HEADS // 2
    tri3 = tri_ref[...]
    normw = nw_ref[...]
    hi, mid, lo = _split3(la_ref[0])
    b_chunks = []
    for ci in range(nchunk):
        rows = slice(ci * C, (ci + 1) * C)
        b_chunks.append(_dot(tri3, jnp.concatenate([hi[rows], mid[rows], lo[rows]], axis=0)))
    b = jnp.concatenate(b_chunks, axis=0)
    b_last = jnp.concatenate([jnp.broadcast_to(bc[C - 1:C, :], bc.shape) for bc in b_chunks], axis=0)
    gk = gk_ref[0].astype(F32)
    qe = gq_ref[0].astype(F32) * jnp.exp(b)
    ke = (gk * jnp.exp(-b)).astype(BF16)
    kd = (gk * jnp.exp(b_last - b)).astype(BF16)
    gv = gv_ref[0]

    low_half = lax.broadcasted_iota(jnp.int32, (C, LANES), 1) < GLA_DK
    ri = lax.broadcasted_iota(jnp.int32, (2 * C, 2 * C), 0)
    ki = lax.broadcasted_iota(jnp.int32, (2 * C, 2 * C), 1)
    pair_causal = jnp.logical_and((ri < C) == (ki < C),
                                  jnp.bitwise_and(ki, C - 1) <= jnp.bitwise_and(ri, C - 1))

    lhs, vrows, upd, dec = {}, {}, {}, {}
    for ci in range(nchunk):
        rows = slice(ci * C, (ci + 1) * C)
        for p in range(npair):
            cs = slice(p * LANES, (p + 1) * LANES)
            qe_p = qe[rows, cs]
            qm = jnp.concatenate([jnp.where(low_half, qe_p, 0.0), jnp.where(low_half, 0.0, qe_p)],
                                 axis=0).astype(BF16)
            ke_p = ke[rows, cs]
            sc = _dot_nt(qm, jnp.concatenate([ke_p, ke_p], axis=0))
            att = jnp.where(pair_causal, sc, 0.0).astype(BF16)
            lhs[ci, p] = jnp.concatenate([att, qm], axis=1)
            v_ab = gv[rows, 2 * p * GLA_DV:2 * (p + 1) * GLA_DV]
            vrows[ci, p] = jnp.concatenate([v_ab[:, 0:GLA_DV], v_ab[:, GLA_DV:]], axis=0)
            u = _dot_tn(kd[rows, cs], v_ab)
            upd[ci, p] = jnp.concatenate([u[0:GLA_DK, 0:GLA_DV], u[GLA_DK:, GLA_DV:]], axis=0)
            bl = jnp.broadcast_to(b_chunks[ci][C - 1:C, cs], (LANES, LANES))
            dec[ci, p] = jnp.exp(bl.T)

    outs = {}
    states = [st_ref[p] for p in range(npair)]
    for ci in range(nchunk):
        for p in range(npair):
            rhs = jnp.concatenate([vrows[ci, p], states[p].astype(BF16)], axis=0)
            outs[ci, p] = _dot(lhs[ci, p], rhs)
            states[p] = dec[ci, p] * states[p] + upd[ci, p]
    for p in range(npair):
        st_ref[p] = states[p]

    for ci in range(nchunk):
        for p in range(npair):
            o = outs[ci, p]
            o = (o * lax.rsqrt(jnp.mean(o * o, axis=-1, keepdims=True) + NORM_EPS) * normw).astype(BF16)
            for a in range(2):
                hh = 2 * p + a
                o_ref[0, ci * C:(ci + 1) * C, hh * GLA_DV:(hh + 1) * GLA_DV] = o[a * C:(a + 1) * C, :]

    @pl.when(c == pl.num_programs(1) - 1)
    def _():
        for p in range(npair):
            sfin_ref[0, p * LANES:(p + 1) * LANES, :] = states[p]


def _tri3(C, dtype=BF16):
    tri = np.tril(np.ones((C, C), np.float32))
    return jnp.asarray(np.concatenate([tri, tri, tri], axis=1), dtype)


def _gla_prompt(gq, gk, gv, la, normw, ct):
    B, S, _ = gq.shape
    ct = min(ct, S)
    assert S % ct == 0 and ct % GLA_CHUNK == 0
    row = lambda w: pl.BlockSpec((1, ct, w), lambda b, c: (b, c, 0))
    const = lambda shp: pl.BlockSpec(shp, lambda b, c: (0,) * len(shp))
    o, sfin = pl.pallas_call(
        functools.partial(_gla_kernel, nchunk=ct // GLA_CHUNK),
        grid=(B, S // ct),
        in_specs=[row(GLA_KW), row(GLA_KW), row(GLA_VW), row(GLA_KW),
                  const((GLA_CHUNK, 3 * GLA_CHUNK)), const((1, GLA_DV))],
        out_specs=[row(GLA_VW), pl.BlockSpec((1, GLA_KW, GLA_DV), lambda b, c: (b, 0, 0))],
        out_shape=[jax.ShapeDtypeStruct((B, S, GLA_VW), BF16),
                   jax.ShapeDtypeStruct((B, GLA_KW, GLA_DV), F32)],
        scratch_shapes=[pltpu.VMEM((GLA_HEADS // 2, GLA_DV, LANES), F32)],
        compiler_params=_cparams(("arbitrary", "arbitrary")), name="gla_prompt",
    )(gq, gk, gv, la, _tri3(GLA_CHUNK), normw)
    return o, sfin.reshape(B, GLA_HEADS, GLA_DK, GLA_DV)


def _merge_kernel(*refs, dils):
    n_groups = len(dils)
    x_ref = refs[0]
    o_refs = refs[1:1 + n_groups]
    lse_refs = refs[1 + n_groups:1 + 2 * n_groups] if n_groups > 1 else ()
    rest = refs[1 + n_groups + len(lse_refs):]
    (saz_ref, og_ref, sgz_ref, sa_ref, sb_ref, wa_ref, wg_ref, wo_ref, fnw_ref, ex_ref, y_ref) = rest[:11]
    scratch = rest[11:]
    tm = x_ref.shape[0]

    def by_position(ref, width, d, stage_ref):
        if d == 1:
            return lambda rows: ref[rows, :].astype(F32)
        planes = width // LANES
        for r in range(d):
            blk = ref[:, r * width:(r + 1) * width].astype(F32)
            for c in range(planes):
                stage_ref[c, pl.ds(r, tm // d, stride=d), :] = blk[:, c * LANES:(c + 1) * LANES]
        return lambda rows: jnp.concatenate([stage_ref[c, rows, :] for c in range(planes)], axis=1)

    if n_groups > 1:
        lse_rows = [by_position(lse_refs[g], LANES, dils[g], scratch[2 * g + 1]) for g in range(n_groups)]
        o_rows = [by_position(o_refs[g], GW, dils[g], scratch[2 * g]) for g in range(n_groups)]

    rc = min(tm, MERGE_ROWS)
    for r0 in range(0, tm, rc):
        rows = slice(r0, r0 + rc)
        if n_groups > 1:
            ls = [f(rows) for f in lse_rows]
            mx = functools.reduce(jnp.maximum, ls)
            es = [jnp.exp(l - mx) for l in ls]
            inv = 1.0 / functools.reduce(lambda a, b: a + b, es)
            comb = None
            for g in range(n_groups):
                w = es[g] * inv
                w_hi = w.astype(BF16)
                w_lo = (w - w_hi.astype(F32)).astype(BF16)
                wx = _dot(jnp.concatenate([w_hi, w_lo], axis=1), ex_ref[...])
                term = wx * o_rows[g](rows)
                comb = term if comb is None else comb + term
        else:
            comb = o_refs[0][rows, :].astype(F32)

        ua = (comb * saz_ref[rows, :].astype(F32)).astype(BF16)
        ub = (og_ref[rows, :].astype(F32) * sgz_ref[rows, :].astype(F32)).astype(BF16)
        ya = _dot(ua, wa_ref[...])
        yb = _dot(ub, wg_ref[...])
        mixed = (sa_ref[rows, :].astype(F32) * ya + sb_ref[rows, :].astype(F32) * yb).astype(BF16)
        out = x_ref[rows, :] + _dot(mixed, wo_ref[...])
        y_ref[rows, :] = (out * lax.rsqrt(jnp.mean(out * out, axis=-1, keepdims=True) + NORM_EPS)
                          * fnw_ref[...])


def _expand_matrix():
    e = np.zeros((LANES, GW), np.float32)
    for hh in range(HEADS):
        e[hh, hh * HEAD_DIM:(hh + 1) * HEAD_DIM] = 1.0
    return jnp.asarray(np.concatenate([e, e], axis=0), BF16)


def _merge(x, o_list, lse_list, dils, saz, og, sgz, sa, sb, wa, wg, wo, fnw, tm):
    R, D = x.shape
    tm = min(tm, R)
    assert R % tm == 0
    n_groups = len(o_list)
    row = lambda w: pl.BlockSpec((tm, w), lambda i: (i, 0))
    const = lambda shp: pl.BlockSpec(shp, lambda i: (0,) * len(shp))
    in_specs = [row(D)]
    in_specs += [pl.BlockSpec((tm // d, d * GW), lambda i: (i, 0)) for d in dils]
    in_specs += [pl.BlockSpec((tm // d, d * LANES), lambda i: (i, 0)) for d in dils[:len(lse_list)]]
    in_specs += [row(GW), row(GLA_VW), row(GLA_VW), row(D), row(D),
                 const((GW, D)), const((GLA_VW, D)), const((D, D)), const((1, D)), const((2 * LANES, GW))]
    scratch = []
    if n_groups > 1:
        for _ in dils:
            scratch += [pltpu.VMEM((GW // LANES, tm, LANES), F32), pltpu.VMEM((1, tm, LANES), F32)]
    return pl.pallas_call(
        functools.partial(_merge_kernel, dils=tuple(dils)),
        grid=(R // tm,), in_specs=in_specs, out_specs=row(D),
        out_shape=jax.ShapeDtypeStruct((R, D), F32), scratch_shapes=scratch,
        compiler_params=_cparams(("arbitrary",)), name=f"merge_g{n_groups}",
    )(x, *o_list, *lse_list, saz, og, sgz, sa, sb, wa, wg, wo, fnw, _expand_matrix())


def _sample_masks(T, w_lens):
    t_of_row = np.arange(HEADS * T) % T

    def bias(idx, g):
        win, dil = GROUPS[g]
        dd = w_lens[g] + t_of_row[:, None] - idx[None, :]
        ok = (dd >= 0) & (dd % dil == 0) & (dd // dil <= win // dil)
        return jnp.asarray(np.where(ok, 0.0, NEG).astype(np.float32))

    cache = [bias(np.arange(w_lens[g]), g) for g in range(N_GROUPS)]
    new = [bias(w_lens[g] + np.arange(T), g) for g in range(N_GROUPS)]
    return cache, new


def _decode_scores_body(in_refs, out_refs, dbs):
    G = N_GROUPS
    sq_refs, sc_refs, sb_refs, bd_ref = in_refs[0:G], in_refs[G:2 * G], in_refs[2 * G:3 * G], in_refs[3 * G]
    bd = bd_ref[...]
    for i in range(dbs):
        for g in range(G):
            qbd = jnp.tile(sq_refs[g][i].astype(F32), (HEADS, 1)) * bd
            out_refs[g][i] = _dot(qbd, sc_refs[g][i]) + sb_refs[g][...]


def _decode_mix_body(in_refs, out_refs, dbs):
    G = N_GROUPS
    q_refs, new_refs, s_refs = in_refs[0:G], in_refs[G:2 * G], in_refs[2 * G:3 * G]
    c_refs, bn_refs, bd_ref = in_refs[3 * G:4 * G], in_refs[4 * G:5 * G], in_refs[5 * G]
    o_ref = out_refs[0]
    T = q_refs[0].shape[1]
    bd = bd_ref[...]
    for i in range(dbs):
        s_new, s_old, v_new, v_old = [], [], [], []
        for g in range(G):
            qbd = jnp.tile(q_refs[g][i].astype(F32), (HEADS, 1)) * bd
            s_new.append(_dot_nt(qbd, _rb(new_refs[g][i, :, 0:GW])) + bn_refs[g][...])
            v_new.append(_rb(new_refs[g][i, :, GW:2 * GW]))
            s_old.append(s_refs[g][i])
            v_old.append(c_refs[g][i])
        m = functools.reduce(jnp.maximum, [jnp.max(s, axis=1, keepdims=True) for s in s_new + s_old])
        p_new = [jnp.exp(s - m) for s in s_new]
        p_old = [jnp.exp(s - m) for s in s_old]
        den = functools.reduce(lambda a, b: a + b, [jnp.sum(p, axis=1, keepdims=True) for p in p_new + p_old])
        inv = 1.0 / den
        acc = None
        for g in range(G):
            r = _dot(_rb(p_new[g] * inv), v_new[g]) + _dot_nt(p_old[g] * inv, v_old[g])
            acc = r if acc is None else acc + r
        acc = acc * bd
        out = acc[0:T, :]
        for hh in range(1, HEADS):
            out = out + acc[hh * T:(hh + 1) * T, :]
        o_ref[i] = out


def _head_diag(T):
    bd = np.zeros((HEADS * T, GW), np.float32)
    for hh in range(HEADS):
        bd[hh * T:(hh + 1) * T, hh * HEAD_DIM:(hh + 1) * HEAD_DIM] = 1.0
    return jnp.asarray(bd)


def _rider_common(DB, nsteps, step_of):
    dbs = -(-DB // nsteps)
    assert DB % dbs == 0
    active = DB // dbs
    blk = lambda b, s: jnp.minimum(step_of(b, s), active - 1)
    per = lambda shp, part=0: pl.BlockSpec((dbs,) + shp, lambda b, s: (blk(b, s), part) + (0,) * (len(shp) - 1))
    const = lambda a: pl.BlockSpec(a.shape, lambda b, s: (0,) * a.ndim)
    return dbs, active, per, const


def _scores_rider(q_list, caches_t):
    DB, T, _ = q_list[0].shape
    w_lens = [c.shape[2] for c in caches_t]
    bias_c, _ = _sample_masks(T, w_lens)

    def make(nsteps, step_of):
        dbs, active, per, const = _rider_common(DB, nsteps, step_of)
        return dict(
            body=_decode_scores_body, dbs=dbs, active=active,
            arrays=list(q_list) + list(caches_t) + bias_c + [_head_diag(T)],
            in_specs=[per((T, GW)) for _ in q_list] + [per((GW, w), 0) for w in w_lens]
                     + [const(a) for a in bias_c] + [pl.BlockSpec((HEADS * T, GW), lambda b, s: (0, 0))],
            out_shape=[jax.ShapeDtypeStruct((DB, HEADS * T, w), F32) for w in w_lens],
            out_specs=[per((HEADS * T, w)) for w in w_lens])
    return make


def _mix_rider(q_list, kv_new, scores, caches_t):
    DB, T, _ = q_list[0].shape
    w_lens = [c.shape[2] for c in caches_t]
    _, bias_n = _sample_masks(T, w_lens)

    def make(nsteps, step_of):
        dbs, active, per, const = _rider_common(DB, nsteps, step_of)
        return dict(
            body=_decode_mix_body, dbs=dbs, active=active,
            arrays=list(q_list) + list(kv_new) + list(scores) + list(caches_t) + bias_n + [_head_diag(T)],
            in_specs=[per((T, GW)) for _ in q_list] + [per((T, 2 * GW)) for _ in kv_new]
                     + [per((HEADS * T, w)) for w in w_lens] + [per((GW, w), 1) for w in w_lens]
                     + [const(a) for a in bias_n] + [pl.BlockSpec((HEADS * T, GW), lambda b, s: (0, 0))],
            out_shape=[jax.ShapeDtypeStruct((DB, T, GW), F32)],
            out_specs=[per((T, GW))])
    return make


def _sgla_kernel(gq_ref, gk_ref, gv_ref, la_ref, st_ref, tri_ref, nw_ref, o_ref, snew_ref, *, dbt):
    tri3 = tri_ref[...]
    normw = nw_ref[...]
    T = gq_ref.shape[1]
    ti = lax.broadcasted_iota(jnp.int32, (T, T), 0)
    si = lax.broadcasted_iota(jnp.int32, (T, T), 1)
    causal = si <= ti
    low_half = lax.broadcasted_iota(jnp.int32, (T, LANES), 1) < GLA_DK
    ones = jnp.ones((3 * T, LANES), F32)
    row_low = lax.broadcasted_iota(jnp.int32, (LANES, GLA_DV), 0) < GLA_DK

    def body(i, carry):
        la3 = jnp.concatenate(_split3(la_ref[i], F32), axis=0)
        b = _dot(tri3, la3)
        b_last = b[T - 1:T, :]
        gk = gk_ref[i].astype(F32)
        qe = gq_ref[i].astype(F32) * jnp.exp(b)
        ke = _rb(gk * jnp.exp(-b))
        kd = _rb(gk * jnp.exp(b_last - b))
        gv = gv_ref[i].astype(F32)
        outs = []
        for p in range(GLA_HEADS // 2):
            cs = slice(p * LANES, (p + 1) * LANES)
            st = st_ref[i, cs, :]
            st_b = _rb(st)
            dec = jnp.exp(_dot_tn(la3[:, cs], ones))
            upd = None
            for a in range(2):
                hh = 2 * p + a
                sel = low_half if a == 0 else jnp.logical_not(low_half)
                qm = _rb(jnp.where(sel, qe[:, cs], 0.0))
                att = _rb(jnp.where(causal, _dot_nt(qm, ke[:, cs]), 0.0))
                v_h = gv[:, hh * GLA_DV:(hh + 1) * GLA_DV]
                o = _dot(att, v_h) + _dot(qm, st_b)
                outs.append(o * lax.rsqrt(jnp.mean(o * o, axis=-1, keepdims=True) + NORM_EPS) * normw)
                u = _dot_tn(kd[:, cs], v_h)
                upd = u if a == 0 else jnp.where(row_low, upd, u)
            snew_ref[i, cs, :] = st * dec + upd
        o_ref[i] = jnp.concatenate(outs, axis=1)
        return carry

    for i in range(dbt):
        body(i, 0)


def _sample_gla(gq, gk, gv, la, state, normw, dbt):
    DB, T, _ = gq.shape
    dbt = min(dbt, DB)
    assert DB % dbt == 0
    st = state.reshape(DB, GLA_KW, GLA_DV)
    blk = lambda shp: pl.BlockSpec((dbt,) + shp, lambda i: (i,) + (0,) * len(shp))
    const = lambda shp: pl.BlockSpec(shp, lambda i: (0,) * len(shp))
    o, snew = pl.pallas_call(
        functools.partial(_sgla_kernel, dbt=dbt),
        grid=(DB // dbt,),
        in_specs=[blk((T, GLA_KW)), blk((T, GLA_KW)), blk((T, GLA_VW)), blk((T, GLA_KW)),
                  blk((GLA_KW, GLA_DV)), const((T, 3 * T)), const((1, GLA_DV))],
        out_specs=[blk((T, GLA_VW)), blk((GLA_KW, GLA_DV))],
        out_shape=[jax.ShapeDtypeStruct((DB, T, GLA_VW), F32),
                   jax.ShapeDtypeStruct((DB, GLA_KW, GLA_DV), F32)],
        compiler_params=_cparams(("arbitrary",)), name="gla_sample",
    )(gq, gk, gv, la, st, _tri3(T, F32), normw)
    return o, snew.reshape(DB, GLA_HEADS, GLA_DK, GLA_DV)


def _rope_tables(pos):
    half = ROPE_DIM // 2
    inv_freq = 1.0 / (ROPE_THETA ** (jnp.arange(half, dtype=F32) * (2.0 / ROPE_DIM)))
    ang = pos.astype(F32)[:, None] * inv_freq[None, :]
    cos, sin = jnp.cos(ang), jnp.sin(ang)
    n = pos.shape[0]
    pad = jnp.zeros((n, HEAD_DIM - ROPE_DIM), F32)
    zero = jnp.zeros((n, half), F32)
    c = jnp.concatenate([cos, cos, pad + 1.0], axis=1)
    s1 = jnp.concatenate([zero, sin, pad], axis=1)
    s2 = jnp.concatenate([-sin, zero, pad], axis=1)
    rep = LANES // HEAD_DIM
    return tuple(jnp.tile(t, (1, rep)) for t in (c, s1, s2))


def kernel(x_prompt, x_sample, cache_kv_w128, cache_kv_w512, cache_kv_w2048, state_gla,
           ln_w, w_in, w_gla_a2, b_gla_a, gla_norm_w, w_attn_out, w_gla_out, w_out, final_norm_w):
    B, S, D = x_prompt.shape
    DB, T, _ = x_sample.shape
    assert ln_w.shape[0] == 1, "single-layer step"
    caches = (cache_kv_w128, cache_kv_w512, cache_kv_w2048)
    dils = [d for _, d in GROUPS]

    lnw = ln_w[0].reshape(1, D)
    w_t = w_in[0].T.astype(BF16)
    wa2 = w_gla_a2[0].astype(BF16)
    ba = b_gla_a[0].reshape(1, GLA_KW)
    gnw = gla_norm_w[0].reshape(1, GLA_DV)
    wa = w_attn_out[0].astype(BF16)
    wg = w_gla_out[0].astype(BF16)
    wo = w_out[0].astype(BF16)
    fnw = final_norm_w.reshape(1, D)

    w_qkv, w_rest = w_t[:OFF_AZ], w_t[OFF_AZ:]

    R = DB * T
    xs = x_sample.reshape(1, R, D)
    pos_s = PAST_LEN + jnp.arange(T, dtype=jnp.int32)
    tabs_s = tuple(jnp.tile(t, (DB, 1)) for t in _rope_tables(pos_s))
    qkv_s = _proj_qkv(xs, lnw, w_qkv, tabs_s, min(512, R), (1,) * N_GROUPS, row_major_kv=True,
                      q_scale=HEAD_DIM ** -0.5)
    rest_s = _proj_rest(xs, lnw, w_rest, wa2, ba, min(512, R))
    per_db = lambda t: t.reshape(DB, T, t.shape[-1])
    q_s = [per_db(q) for q in qkv_s[0:3]]
    kv_new = [per_db(t) for t in qkv_s[9:12]]
    caches_t = [c[0].transpose(0, 2, 3, 4, 1).reshape(DB, 2 * GW, c.shape[2]) for c in caches]

    tabs_p = _rope_tables(jnp.arange(S, dtype=jnp.int32))
    qkv = _proj_qkv(x_prompt, lnw, w_qkv, tabs_p, min(512, S), dils, row_major_kv=False,
                    q_scale=HEAD_DIM ** -0.5 * LOG2E,
                    make_rider=_scores_rider(q_s, caches_t))
    scores = qkv[9:12]
    rest = _proj_rest(x_prompt, lnw, w_rest, wa2, ba, min(512, S),
                      make_rider=_mix_rider(q_s, kv_new, scores, caches_t))
    saz, sgz, sa, sb, gq, gk, gv, la, o_s = rest
    o_list, lse_list = [], []
    for g in range(N_GROUPS):
        o, lse = _attn_group(qkv[g], qkv[3 + g], qkv[6 + g], g, qb=512)
        o_list.append(o.reshape(-1, o.shape[-1]))
        lse_list.append(lse.reshape(-1, lse.shape[-1]))
    og, gla_p = _gla_prompt(gq, gk, gv, la, gnw, ct=512)
    flat = lambda t: t.reshape(B * S, t.shape[-1])
    y_prompt = _merge(flat(x_prompt), o_list, lse_list, dils, flat(saz), flat(og), flat(sgz), flat(sa),
                      flat(sb), wa, wg, wo, fnw, tm=512).reshape(B, S, D)
    kvt = _kv_tail_prompt(x_prompt, lnw, w_t, [min(w, S) for w, _ in GROUPS], tile=512)
    kv_p = [t.reshape(B, 2, HEADS, HEAD_DIM, t.shape[-1]).transpose(0, 4, 1, 2, 3)[None] for t in kvt]

    saz, sgz, sa, sb, gq, gk, gv, la = rest_s
    og_s, gla_s = _sample_gla(per_db(gq), per_db(gk), per_db(gv), per_db(la), state_gla[0], gnw, dbt=8)
    flat_s = lambda t: t.reshape(R, t.shape[-1])
    y_sample = _merge(x_sample.reshape(R, D), [o_s.reshape(R, GW)], [], (1,), flat_s(saz), flat_s(og_s),
                      flat_s(sgz), flat_s(sa), flat_s(sb), wa, wg, wo, fnw, tm=512).reshape(DB, T, D)
    kvt_s = _kv_tail_sample(x_sample, lnw, w_t, pos_s)
    kv_s = [t.reshape(T, 2, HEADS, HEAD_DIM, DB).transpose(4, 0, 1, 2, 3)[None] for t in kvt_s]

    return (y_prompt, y_sample, kv_p[0], kv_p[1], kv_p[2], gla_p[None],
            kv_s[0], kv_s[1], kv_s[2], gla_s[None])
```

```python
import functools

import numpy as np
import jax
import jax.numpy as jnp
from jax import lax
from jax.experimental import pallas as pl
from jax.experimental.pallas import tpu as pltpu

F32 = jnp.float32
BF16 = jnp.bfloat16

D_MODEL = 1024
HEAD_DIM = 64
HEADS = 8
GROUPS = ((128, 1), (512, 4), (2048, 16))
N_GROUPS = len(GROUPS)
GW = HEADS * HEAD_DIM
QKV_W = N_GROUPS * GW
ROPE_DIM = HEAD_DIM // 4
ROPE_THETA = 500000.0
BAND = 128
GLA_HEADS = 4
GLA_DK = 64
GLA_DV = 128
GLA_KW = GLA_HEADS * GLA_DK
GLA_VW = GLA_HEADS * GLA_DV
GATE_RANK = 16
GLA_TAU = 16.0
GLA_CHUNK = 64
NORM_EPS = 1e-6
PAST_LEN = 8192

LANES = 128
NEG = -1e30
LOG2E = float(np.log2(np.e))
LN2 = float(np.log(2.0))

OFF_Q = 0
OFF_K = OFF_Q + QKV_W
OFF_V = OFF_K + QKV_W
OFF_AZ = OFF_V + QKV_W
OFF_GQ = OFF_AZ + GW
OFF_GK = OFF_GQ + GLA_KW
OFF_GV = OFF_GK + GLA_KW
OFF_GZ = OFF_GV + GLA_VW
OFF_LR = OFF_GZ + GLA_VW
OFF_MA = OFF_LR + GATE_RANK
OFF_MB = OFF_MA + D_MODEL
PROJ_W = OFF_MB + D_MODEL
assert all(o % 16 == 0 for o in (OFF_LR, OFF_MA, OFF_MB, PROJ_W))

VMEM_LIMIT = 56 * 1024 * 1024
MERGE_ROWS = 512


def _cparams(sem):
    return pltpu.CompilerParams(dimension_semantics=sem, vmem_limit_bytes=VMEM_LIMIT)


def _dot(a, b):
    return jnp.dot(a, b, preferred_element_type=F32)


def _dot_nt(a, b):
    return lax.dot_general(a, b, (((1,), (1,)), ((), ())), preferred_element_type=F32)


def _dot_tn(a, b):
    return lax.dot_general(a, b, (((0,), (0,)), ((), ())), preferred_element_type=F32)


def _rb(x):
    return x.astype(BF16).astype(F32)


def _split3(x, dtype=BF16):
    hi = x.astype(BF16)
    r1 = x - hi.astype(F32)
    mid = r1.astype(BF16)
    lo = (r1 - mid.astype(F32)).astype(BF16)
    return hi.astype(dtype), mid.astype(dtype), lo.astype(dtype)


def _sigmoid(x):
    return 1.0 / (1.0 + jnp.exp(-x))


def _log_sigmoid(x):
    return jnp.minimum(x, 0.0) - jnp.log1p(jnp.exp(-jnp.abs(x)))


def _rms_bf16(x, w):
    ms = jnp.mean(x * x, axis=-1, keepdims=True)
    return (x * lax.rsqrt(ms + NORM_EPS) * w).astype(BF16)


def _run_rider(rider, in_refs, out_refs):
    body, active, dbs, nsteps = rider
    if active == nsteps:
        body(in_refs, out_refs, dbs)
    else:
        step = pl.program_id(0) * pl.num_programs(1) + pl.program_id(1)
        pl.when(step < active)(lambda: body(in_refs, out_refs, dbs))


def _proj_qkv_kernel(x_ref, lnw_ref, w_ref, rc_ref, rs1_ref, rs2_ref, *refs,
                     dils, row_major_kv, q_scale, rider, n_rider_in):
    rider_in, refs = refs[:n_rider_in], refs[n_rider_in:]
    q_refs, k_refs, v_refs = refs[0:3], refs[3:6], refs[6:9]
    kv_refs = refs[9:12] if row_major_kv else ()
    rider_out = refs[9 + len(kv_refs):-1]
    stage_ref = refs[-1]
    h = _rms_bf16(x_ref[0], lnw_ref[...])
    tm = h.shape[0]
    rc, rs1, rs2 = rc_ref[...], rs1_ref[...], rs2_ref[...]

    def mm(off, width):
        return _dot_nt(h, w_ref[off:off + width, :])

    def rope(t):
        outs = []
        for j in range(t.shape[1] // LANES):
            c = t[:, j * LANES:(j + 1) * LANES]
            outs.append(c * rc + pltpu.roll(c, ROPE_DIM // 2, axis=1) * rs1
                        + pltpu.roll(c, LANES - ROPE_DIM // 2, axis=1) * rs2)
        return jnp.concatenate(outs, axis=1)

    def store_by_residue(dst_ref, val, d):
        if d == 1:
            dst_ref[0] = val.astype(BF16)
            return
        for c in range(GW // LANES):
            stage_ref[c] = val[:, c * LANES:(c + 1) * LANES]
        for r in range(d):
            for c in range(GW // LANES):
                col = r * GW + c * LANES
                dst_ref[0, :, col:col + LANES] = stage_ref[c, pl.ds(r, tm // d, stride=d), :].astype(BF16)

    for g in range(N_GROUPS):
        qr = rope(mm(OFF_Q + g * GW, GW)) * q_scale
        store_by_residue(q_refs[g], qr, dils[g])
        kr = rope(mm(OFF_K + g * GW, GW))
        store_by_residue(k_refs[g], kr, dils[g])
        vv = mm(OFF_V + g * GW, GW)
        store_by_residue(v_refs[g], vv, dils[g])
        if row_major_kv:
            kv_refs[g][0, :, 0:GW] = kr
            kv_refs[g][0, :, GW:2 * GW] = vv
    if rider is not None:
        _run_rider(rider, rider_in, rider_out)


def _proj_rest_kernel(x_ref, lnw_ref, w_ref, wa2_ref, ba_ref, *refs, rider, n_rider_in):
    rider_in, refs = refs[:n_rider_in], refs[n_rider_in:]
    saz_ref, sgz_ref, sa_ref, sb_ref, gq_ref, gk_ref, gv_ref, la_ref = refs[0:8]
    rider_out = refs[8:]
    h = _rms_bf16(x_ref[0], lnw_ref[...])

    def mm(off, width):
        return _dot_nt(h, w_ref[off - OFF_AZ:off - OFF_AZ + width, :])

    az = mm(OFF_AZ, GW)
    saz_ref[0] = (az * _sigmoid(az)).astype(BF16)
    gz = mm(OFF_GZ, GLA_VW)
    sgz_ref[0] = (gz * _sigmoid(gz)).astype(BF16)
    for j in range(D_MODEL // GW):
        sa_ref[0, :, j * GW:(j + 1) * GW] = _sigmoid(mm(OFF_MA + j * GW, GW)).astype(BF16)
        sb_ref[0, :, j * GW:(j + 1) * GW] = _sigmoid(mm(OFF_MB + j * GW, GW)).astype(BF16)
    gq_ref[0] = (mm(OFF_GQ, GLA_KW) * (GLA_DK ** -0.5)).astype(BF16)
    gk_ref[0] = mm(OFF_GK, GLA_KW).astype(BF16)
    gv_ref[0] = mm(OFF_GV, GLA_VW).astype(BF16)
    glr = mm(OFF_LR, GATE_RANK)
    gate_pre = _dot(glr.astype(BF16), wa2_ref[...]) + ba_ref[...]
    la_ref[0] = _log_sigmoid(gate_pre) * (1.0 / GLA_TAU)
    if rider is not None:
        _run_rider(rider, rider_in, rider_out)


def _rider_parts(make_rider, B, nt):
    if make_rider is None:
        return None, [], [], [], []
    r = make_rider(B * nt, lambda b, s: b * nt + s)
    return (r["body"], r["active"], r["dbs"], B * nt), r["arrays"], r["in_specs"], r["out_shape"], r["out_specs"]


def _proj_qkv(x, lnw, w_qkv, rope_tabs, tm, dils, row_major_kv, q_scale, make_rider=None):
    B, S, D = x.shape
    assert S % tm == 0 and all(tm % (16 * d) == 0 for d in dils)
    row = lambda w: pl.BlockSpec((1, tm, w), lambda b, s: (b, s, 0))
    tab = pl.BlockSpec((tm, LANES), lambda b, s: (s, 0))
    rider, r_arrays, r_in_specs, r_out_shape, r_out_specs = _rider_parts(make_rider, B, S // tm)
    out_shape, out_specs = [], []
    for _ in range(3):
        for d in dils:
            out_shape.append(jax.ShapeDtypeStruct((B, S // d, d * GW), BF16))
            out_specs.append(pl.BlockSpec((1, tm // d, d * GW), lambda b, s: (b, s, 0)))
    if row_major_kv:
        out_shape += [jax.ShapeDtypeStruct((B, S, 2 * GW), F32)] * N_GROUPS
        out_specs += [row(2 * GW)] * N_GROUPS
    in_specs = [row(D), pl.BlockSpec((1, D), lambda b, s: (0, 0)),
                pl.BlockSpec(w_qkv.shape, lambda b, s: (0, 0), pipeline_mode=pl.Buffered(1)),
                tab, tab, tab]
    return pl.pallas_call(
        functools.partial(_proj_qkv_kernel, dils=tuple(dils), row_major_kv=row_major_kv, q_scale=q_scale,
                          rider=rider, n_rider_in=len(r_arrays)),
        grid=(B, S // tm), in_specs=in_specs + r_in_specs, out_specs=out_specs + r_out_specs,
        out_shape=out_shape + r_out_shape,
        scratch_shapes=[pltpu.VMEM((GW // LANES, tm, LANES), F32)],
        compiler_params=_cparams(("arbitrary", "arbitrary")), name="proj_qkv",
    )(x, lnw, w_qkv, *rope_tabs, *r_arrays)


def _proj_rest(x, lnw, w_rest, wa2, ba, tm, make_rider=None):
    B, S, D = x.shape
    assert S % tm == 0
    row = lambda w: pl.BlockSpec((1, tm, w), lambda b, s: (b, s, 0))
    const = lambda shp: pl.BlockSpec(shp, lambda b, s: (0,) * len(shp))
    bf = lambda w: jax.ShapeDtypeStruct((B, S, w), BF16)
    rider, r_arrays, r_in_specs, r_out_shape, r_out_specs = _rider_parts(make_rider, B, S // tm)
    out_shape = [bf(GW), bf(GLA_VW), bf(D_MODEL), bf(D_MODEL), bf(GLA_KW), bf(GLA_KW), bf(GLA_VW),
                 jax.ShapeDtypeStruct((B, S, GLA_KW), F32)]
    out_specs = [row(GW), row(GLA_VW), row(D_MODEL), row(D_MODEL), row(GLA_KW), row(GLA_KW), row(GLA_VW),
                 row(GLA_KW)]
    in_specs = [row(D), const((1, D)),
                pl.BlockSpec(w_rest.shape, lambda b, s: (0, 0), pipeline_mode=pl.Buffered(1)),
                const((GATE_RANK, GLA_KW)), const((1, GLA_KW))]
    return pl.pallas_call(
        functools.partial(_proj_rest_kernel, rider=rider, n_rider_in=len(r_arrays)),
        grid=(B, S // tm), in_specs=in_specs + r_in_specs, out_specs=out_specs + r_out_specs,
        out_shape=out_shape + r_out_shape,
        compiler_params=_cparams(("arbitrary", "arbitrary")), name="proj_rest",
    )(x, lnw, w_rest, wa2, ba, *r_arrays)


def _kvt_kernel(x_ref, lnw_ref, wk_ref, wv_ref, cos_ref, sin_ref, o0_ref, o1_ref, o2_ref, *, plan, nt):
    i = pl.program_id(1)
    h = _rms_bf16(x_ref[...], lnw_ref[...])
    rows = h.shape[0]
    half = ROPE_DIM // 2
    for g, (tiles, cols) in enumerate(plan):
        o_ref = (o0_ref, o1_ref, o2_ref)[g]

        def emit(g=g, cols=cols, o_ref=o_ref):
            hh = h[rows - cols:, :]
            y = _dot_nt(wk_ref[g * GW:(g + 1) * GW, :], hh)
            c = cos_ref[:, rows - cols:]
            s = sin_ref[:, rows - cols:]
            for hd in range(HEADS):
                b0 = hd * HEAD_DIM
                x1 = y[b0:b0 + half, :]
                x2 = y[b0 + half:b0 + ROPE_DIM, :]
                o_ref[b0:b0 + half, :] = x1 * c - x2 * s
                o_ref[b0 + half:b0 + ROPE_DIM, :] = x2 * c + x1 * s
                o_ref[b0 + ROPE_DIM:b0 + HEAD_DIM, :] = y[b0 + ROPE_DIM:b0 + HEAD_DIM, :]
            o_ref[GW:2 * GW, :] = _dot_nt(wv_ref[g * GW:(g + 1) * GW, :], hh)

        if tiles == nt:
            emit()
        else:
            pl.when(i >= nt - tiles)(emit)


def _rope_cos_sin_t(pos):
    half = ROPE_DIM // 2
    inv_freq = 1.0 / (ROPE_THETA ** (jnp.arange(half, dtype=F32) * (2.0 / ROPE_DIM)))
    ang = pos.astype(F32)[:, None] * inv_freq[None, :]
    return jnp.cos(ang).T, jnp.sin(ang).T


def _kv_tail_prompt(x, lnw, w_t, S_tails, tile):
    B, S, D = x.shape
    tile = min(tile, S)
    span = max(S_tails)
    assert span % tile == 0 and S % tile == 0
    nt = span // tile
    first = (S - span) // tile
    plan = []
    for t in S_tails:
        assert t % tile == 0 or t < tile
        plan.append((t // tile, tile) if t >= tile else (1, t))
    cos_t, sin_t = _rope_cos_sin_t(jnp.arange(S, dtype=jnp.int32))
    tabspec = pl.BlockSpec((ROPE_DIM // 2, tile), lambda b, i: (0, first + i))
    out_specs = [pl.BlockSpec((None, 2 * GW, cols), lambda b, i, t=tiles: (b, 0, jnp.maximum(i - (nt - t), 0)))
                 for tiles, cols in plan]
    return pl.pallas_call(
        functools.partial(_kvt_kernel, plan=tuple(plan), nt=nt),
        grid=(B, nt),
        in_specs=[pl.BlockSpec((None, tile, D), lambda b, i: (b, first + i, 0)),
                  pl.BlockSpec((1, D), lambda b, i: (0, 0)),
                  pl.BlockSpec((QKV_W, D), lambda b, i: (OFF_K // QKV_W, 0), pipeline_mode=pl.Buffered(1)),
                  pl.BlockSpec((QKV_W, D), lambda b, i: (OFF_V // QKV_W, 0), pipeline_mode=pl.Buffered(1)),
                  tabspec, tabspec],
        out_specs=out_specs,
        out_shape=[jax.ShapeDtypeStruct((B, 2 * GW, t), F32) for t in S_tails],
        compiler_params=_cparams(("arbitrary", "arbitrary")), name="kv_tail_prompt",
    )(x, lnw, w_t, w_t, cos_t, sin_t)


def _kv_tail_sample(x, lnw, w_t, pos):
    DB, T, D = x.shape
    cos_t, sin_t = _rope_cos_sin_t(pos)
    bcast = lambda t: jnp.broadcast_to(t.T[:, :, None], (T, ROPE_DIM // 2, DB))
    tabspec = pl.BlockSpec((None, ROPE_DIM // 2, DB), lambda b, i: (i, 0, 0))
    out_spec = pl.BlockSpec((None, 2 * GW, DB), lambda b, i: (i, 0, 0))
    return pl.pallas_call(
        functools.partial(_kvt_kernel, plan=((T, DB),) * N_GROUPS, nt=T),
        grid=(1, T),
        in_specs=[pl.BlockSpec((None, DB, D), lambda b, i: (i, 0, 0)),
                  pl.BlockSpec((1, D), lambda b, i: (0, 0)),
                  pl.BlockSpec((QKV_W, D), lambda b, i: (OFF_K // QKV_W, 0), pipeline_mode=pl.Buffered(1)),
                  pl.BlockSpec((QKV_W, D), lambda b, i: (OFF_V // QKV_W, 0), pipeline_mode=pl.Buffered(1)),
                  tabspec, tabspec],
        out_specs=[out_spec] * N_GROUPS,
        out_shape=[jax.ShapeDtypeStruct((T, 2 * GW, DB), F32)] * N_GROUPS,
        compiler_params=_cparams(("arbitrary", "arbitrary")), name="kv_tail_sample",
    )(x.transpose(1, 0, 2), lnw, w_t, w_t, bcast(cos_t), bcast(sin_t))


def _band_bias():
    q = np.arange(BAND)[:, None]
    c = np.arange(2 * BAND)[None, :]
    ok = np.where(c < BAND, c >= q, (c - BAND) <= q)
    ok_first = ok & (c >= BAND)
    return jnp.asarray(np.stack([np.where(ok, 0.0, NEG), np.where(ok_first, 0.0, NEG)]).astype(np.float32))


def _attn_kernel(q_ref, k_ref, kp_ref, v_ref, vp_ref, bias_ref, o_ref, lse_ref, kall, vall, *, qb):
    n = pl.program_id(2)
    nsub = qb // BAND
    npair = HEADS // 2
    kall[0:BAND, :] = kp_ref[0]
    kall[BAND:, :] = k_ref[0]
    vall[0:BAND, :] = vp_ref[0]
    vall[BAND:, :] = v_ref[0]
    lane = lax.broadcasted_iota(jnp.int32, (BAND, LANES), 1)
    low_half = lane < HEAD_DIM

    def sub_block(j, carry):
        r0 = pl.multiple_of(j * BAND, BAND)
        first = (n * nsub + j) == 0
        bias = bias_ref[jnp.where(first, 1, 0)]
        bias2 = jnp.concatenate([bias, bias], axis=0)
        scores = []
        for hp in range(npair):
            cs = slice(hp * LANES, (hp + 1) * LANES)
            qp = q_ref[0, pl.ds(r0, BAND), cs]
            zero = jnp.zeros_like(qp)
            qm = jnp.concatenate([jnp.where(low_half, qp, zero), jnp.where(low_half, zero, qp)], axis=0)
            scores.append(_dot_nt(qm, kall[pl.ds(r0, 2 * BAND), cs]) + bias2)
        probs, stats = [], []
        for s in scores:
            m = jnp.max(s, axis=1, keepdims=True)
            p = jnp.exp2(s - m)
            stats.append((m, jnp.sum(p, axis=1, keepdims=True)))
            probs.append(p.astype(BF16))
        m_tile = jnp.zeros((BAND, LANES), F32)
        den_tile = jnp.ones((BAND, LANES), F32)
        for hp in range(npair):
            cs = slice(hp * LANES, (hp + 1) * LANES)
            m, den = stats[hp]
            o = _dot(probs[hp], vall[pl.ds(r0, 2 * BAND), cs])
            o = jnp.where(low_half, o[0:BAND], o[BAND:]) / jnp.where(low_half, den[0:BAND], den[BAND:])
            o_ref[0, pl.ds(r0, BAND), cs] = o.astype(BF16)
            for a in range(2):
                rows = slice(a * BAND, (a + 1) * BAND)
                m_tile = jnp.where(lane == 2 * hp + a, m[rows], m_tile)
                den_tile = jnp.where(lane == 2 * hp + a, den[rows], den_tile)
        lse_ref[0, pl.ds(r0, BAND), :] = (m_tile + jnp.log2(den_tile)) * LN2
        return carry

    lax.fori_loop(0, nsub, sub_block, 0, unroll=True)


def _attn_group(q, k, v, g, qb):
    B, L, _ = q.shape
    win, dil = GROUPS[g]
    assert win // dil == BAND
    qb = min(qb, L)
    assert L % qb == 0 and qb % BAND == 0
    nsub = qb // BAND
    cur = pl.BlockSpec((1, qb, GW), lambda b, r, n: (b, n, r))
    prev = pl.BlockSpec((1, BAND, GW), lambda b, r, n: (b, jnp.maximum(n * nsub - 1, 0), r))
    return pl.pallas_call(
        functools.partial(_attn_kernel, qb=qb),
        grid=(B, dil, L // qb),
        in_specs=[cur, cur, prev, cur, prev, pl.BlockSpec((2, BAND, 2 * BAND), lambda b, r, n: (0, 0, 0))],
        out_specs=[pl.BlockSpec((1, qb, GW), lambda b, r, n: (b, n, r)),
                   pl.BlockSpec((1, qb, LANES), lambda b, r, n: (b, n, r))],
        out_shape=[jax.ShapeDtypeStruct((B, L, dil * GW), BF16),
                   jax.ShapeDtypeStruct((B, L, dil * LANES), F32)],
        scratch_shapes=[pltpu.VMEM((qb + BAND, GW), BF16), pltpu.VMEM((qb + BAND, GW), BF16)],
        compiler_params=_cparams(("arbitrary", "arbitrary", "arbitrary")), name=f"attn_g{g}",
    )(q, k, k, v, v, _band_bias())


def _gla_kernel(gq_ref, gk_ref, gv_ref, la_ref, tri_ref, nw_ref, o_ref, sfin_ref, st_ref, *, nchunk):
    c = pl.program_id(1)

    @pl.when(c == 0)
    def _():
        st_ref[...] = jnp.zeros_like(st_ref)

    C = GLA_CHUNK
    npair = GLA_HEADS // 2
    tri3 = tri_ref[...]
    normw = nw_ref[...]
    hi, mid, lo = _split3(la_ref[0])
    b_chunks = []
    for ci in range(nchunk):
        rows = slice(ci * C, (ci + 1) * C)
        b_chunks.append(_dot(tri3, jnp.concatenate([hi[rows], mid[rows], lo[rows]], axis=0)))
    b = jnp.concatenate(b_chunks, axis=0)
    b_last = jnp.concatenate([jnp.broadcast_to(bc[C - 1:C, :], bc.shape) for bc in b_chunks], axis=0)
    gk = gk_ref[0].astype(F32)
    qe = gq_ref[0].astype(F32) * jnp.exp(b)
    ke = (gk * jnp.exp(-b)).astype(BF16)
    kd = (gk * jnp.exp(b_last - b)).astype(BF16)
    gv = gv_ref[0]

    low_half = lax.broadcasted_iota(jnp.int32, (C, LANES), 1) < GLA_DK
    ri = lax.broadcasted_iota(jnp.int32, (2 * C, 2 * C), 0)
    ki = lax.broadcasted_iota(jnp.int32, (2 * C, 2 * C), 1)
    pair_causal = jnp.logical_and((ri < C) == (ki < C),
                                  jnp.bitwise_and(ki, C - 1) <= jnp.bitwise_and(ri, C - 1))

    lhs, vrows, upd, dec = {}, {}, {}, {}
    for ci in range(nchunk):
        rows = slice(ci * C, (ci + 1) * C)
        for p in range(npair):
            cs = slice(p * LANES, (p + 1) * LANES)
            qe_p = qe[rows, cs]
            qm = jnp.concatenate([jnp.where(low_half, qe_p, 0.0), jnp.where(low_half, 0.0, qe_p)],
                                 axis=0).astype(BF16)
            ke_p = ke[rows, cs]
            sc = _dot_nt(qm, jnp.concatenate([ke_p, ke_p], axis=0))
            att = jnp.where(pair_causal, sc, 0.0).astype(BF16)
            lhs[ci, p] = jnp.concatenate([att, qm], axis=1)
            v_ab = gv[rows, 2 * p * GLA_DV:2 * (p + 1) * GLA_DV]
            vrows[ci, p] = jnp.concatenate([v_ab[:, 0:GLA_DV], v_ab[:, GLA_DV:]], axis=0)
            u = _dot_tn(kd[rows, cs], v_ab)
            upd[ci, p] = jnp.concatenate([u[0:GLA_DK, 0:GLA_DV], u[GLA_DK:, GLA_DV:]], axis=0)
            bl = jnp.broadcast_to(b_chunks[ci][C - 1:C, cs], (LANES, LANES))
            dec[ci, p] = jnp.exp(bl.T)

    outs = {}
    states = [st_ref[p] for p in range(npair)]
    for ci in range(nchunk):
        for p in range(npair):
            rhs = jnp.concatenate([vrows[ci, p], states[p].astype(BF16)], axis=0)
            outs[ci, p] = _dot(lhs[ci, p], rhs)
            states[p] = dec[ci, p] * states[p] + upd[ci, p]
    for p in range(npair):
        st_ref[p] = states[p]

    for ci in range(nchunk):
        for p in range(npair):
            o = outs[ci, p]
            o = (o * lax.rsqrt(jnp.mean(o * o, axis=-1, keepdims=True) + NORM_EPS) * normw).astype(BF16)
            for a in range(2):
                hh = 2 * p + a
                o_ref[0, ci * C:(ci + 1) * C, hh * GLA_DV:(hh + 1) * GLA_DV] = o[a * C:(a + 1) * C, :]

    @pl.when(c == pl.num_programs(1) - 1)
    def _():
        for p in range(npair):
            sfin_ref[0, p * LANES:(p + 1) * LANES, :] = states[p]


def _tri3(C, dtype=BF16):
    tri = np.tril(np.ones((C, C), np.float32))
    return jnp.asarray(np.concatenate([tri, tri, tri], axis=1), dtype)


def _gla_prompt(gq, gk, gv, la, normw, ct):
    B, S, _ = gq.shape
    ct = min(ct, S)
    assert S % ct == 0 and ct % GLA_CHUNK == 0
    row = lambda w: pl.BlockSpec((1, ct, w), lambda b, c: (b, c, 0))
    const = lambda shp: pl.BlockSpec(shp, lambda b, c: (0,) * len(shp))
    o, sfin = pl.pallas_call(
        functools.partial(_gla_kernel, nchunk=ct // GLA_CHUNK),
        grid=(B, S // ct),
        in_specs=[row(GLA_KW), row(GLA_KW), row(GLA_VW), row(GLA_KW),
                  const((GLA_CHUNK, 3 * GLA_CHUNK)), const((1, GLA_DV))],
        out_specs=[row(GLA_VW), pl.BlockSpec((1, GLA_KW, GLA_DV), lambda b, c: (b, 0, 0))],
        out_shape=[jax.ShapeDtypeStruct((B, S, GLA_VW), BF16),
                   jax.ShapeDtypeStruct((B, GLA_KW, GLA_DV), F32)],
        scratch_shapes=[pltpu.VMEM((GLA_HEADS // 2, GLA_DV, LANES), F32)],
        compiler_params=_cparams(("arbitrary", "arbitrary")), name="gla_prompt",
    )(gq, gk, gv, la, _tri3(GLA_CHUNK), normw)
    return o, sfin.reshape(B, GLA_HEADS, GLA_DK, GLA_DV)


def _merge_kernel(*refs, dils):
    n_groups = len(dils)
    x_ref = refs[0]
    o_refs = refs[1:1 + n_groups]
    lse_refs = refs[1 + n_groups:1 + 2 * n_groups] if n_groups > 1 else ()
    rest = refs[1 + n_groups + len(lse_refs):]
    (saz_ref, og_ref, sgz_ref, sa_ref, sb_ref, wa_ref, wg_ref, wo_ref, fnw_ref, ex_ref, y_ref) = rest[:11]
    scratch = rest[11:]
    tm = x_ref.shape[0]

    def by_position(ref, width, d, stage_ref):
        if d == 1:
            return lambda rows: ref[rows, :].astype(F32)
        planes = width // LANES
        for r in range(d):
            blk = ref[:, r * width:(r + 1) * width].astype(F32)
            for c in range(planes):
                stage_ref[c, pl.ds(r, tm // d, stride=d), :] = blk[:, c * LANES:(c + 1) * LANES]
        return lambda rows: jnp.concatenate([stage_ref[c, rows, :] for c in range(planes)], axis=1)

    if n_groups > 1:
        lse_rows = [by_position(lse_refs[g], LANES, dils[g], scratch[2 * g + 1]) for g in range(n_groups)]
        o_rows = [by_position(o_refs[g], GW, dils[g], scratch[2 * g]) for g in range(n_groups)]

    rc = min(tm, MERGE_ROWS)
    for r0 in range(0, tm, rc):
        rows = slice(r0, r0 + rc)
        if n_groups > 1:
            ls = [f(rows) for f in lse_rows]
            mx = functools.reduce(jnp.maximum, ls)
            es = [jnp.exp(l - mx) for l in ls]
            inv = 1.0 / functools.reduce(lambda a, b: a + b, es)
            comb = None
            for g in range(n_groups):
                w = es[g] * inv
                w_hi = w.astype(BF16)
                w_lo = (w - w_hi.astype(F32)).astype(BF16)
                wx = _dot(jnp.concatenate([w_hi, w_lo], axis=1), ex_ref[...])
                term = wx * o_rows[g](rows)
                comb = term if comb is None else comb + term
        else:
            comb = o_refs[0][rows, :].astype(F32)

        ua = (comb * saz_ref[rows, :].astype(F32)).astype(BF16)
        ub = (og_ref[rows, :].astype(F32) * sgz_ref[rows, :].astype(F32)).astype(BF16)
        ya = _dot(ua, wa_ref[...])
        yb = _dot(ub, wg_ref[...])
        mixed = (sa_ref[rows, :].astype(F32) * ya + sb_ref[rows, :].astype(F32) * yb).astype(BF16)
        out = x_ref[rows, :] + _dot(mixed, wo_ref[...])
        y_ref[rows, :] = (out * lax.rsqrt(jnp.mean(out * out, axis=-1, keepdims=True) + NORM_EPS)
                          * fnw_ref[...])


def _expand_matrix():
    e = np.zeros((LANES, GW), np.float32)
    for hh in range(HEADS):
        e[hh, hh * HEAD_DIM:(hh + 1) * HEAD_DIM] = 1.0
    return jnp.asarray(np.concatenate([e, e], axis=0), BF16)


def _merge(x, o_list, lse_list, dils, saz, og, sgz, sa, sb, wa, wg, wo, fnw, tm):
    R, D = x.shape
    tm = min(tm, R)
    assert R % tm == 0
    n_groups = len(o_list)
    row = lambda w: pl.BlockSpec((tm, w), lambda i: (i, 0))
    const = lambda shp: pl.BlockSpec(shp, lambda i: (0,) * len(shp))
    in_specs = [row(D)]
    in_specs += [pl.BlockSpec((tm // d, d * GW), lambda i: (i, 0)) for d in dils]
    in_specs += [pl.BlockSpec((tm // d, d * LANES), lambda i: (i, 0)) for d in dils[:len(lse_list)]]
    in_specs += [row(GW), row(GLA_VW), row(GLA_VW), row(D), row(D),
                 const((GW, D)), const((GLA_VW, D)), const((D, D)), const((1, D)), const((2 * LANES, GW))]
    scratch = []
    if n_groups > 1:
        for _ in dils:
            scratch += [pltpu.VMEM((GW // LANES, tm, LANES), F32), pltpu.VMEM((1, tm, LANES), F32)]
    return pl.pallas_call(
        functools.partial(_merge_kernel, dils=tuple(dils)),
        grid=(R // tm,), in_specs=in_specs, out_specs=row(D),
        out_shape=jax.ShapeDtypeStruct((R, D), F32), scratch_shapes=scratch,
        compiler_params=_cparams(("arbitrary",)), name=f"merge_g{n_groups}",
    )(x, *o_list, *lse_list, saz, og, sgz, sa, sb, wa, wg, wo, fnw, _expand_matrix())


def _sample_masks(T, w_lens):
    t_of_row = np.arange(HEADS * T) % T

    def bias(idx, g):
        win, dil = GROUPS[g]
        dd = w_lens[g] + t_of_row[:, None] - idx[None, :]
        ok = (dd >= 0) & (dd % dil == 0) & (dd // dil <= win // dil)
        return jnp.asarray(np.where(ok, 0.0, NEG).astype(np.float32))

    cache = [bias(np.arange(w_lens[g]), g) for g in range(N_GROUPS)]
    new = [bias(w_lens[g] + np.arange(T), g) for g in range(N_GROUPS)]
    return cache, new


def _decode_scores_body(in_refs, out_refs, dbs):
    G = N_GROUPS
    sq_refs, sc_refs, sb_refs, bd_ref = in_refs[0:G], in_refs[G:2 * G], in_refs[2 * G:3 * G], in_refs[3 * G]
    bd = bd_ref[...]
    for i in range(dbs):
        for g in range(G):
            qbd = (jnp.tile(sq_refs[g][i].astype(F32), (HEADS, 1)) * bd).astype(BF16)
            out_refs[g][i] = _dot(qbd, sc_refs[g][i].astype(BF16)) + sb_refs[g][...]


def _decode_mix_body(in_refs, out_refs, dbs):
    G = N_GROUPS
    q_refs, new_refs, s_refs = in_refs[0:G], in_refs[G:2 * G], in_refs[2 * G:3 * G]
    c_refs, bn_refs, bd_ref = in_refs[3 * G:4 * G], in_refs[4 * G:5 * G], in_refs[5 * G]
    o_ref = out_refs[0]
    T = q_refs[0].shape[1]
    bd = bd_ref[...]
    for i in range(dbs):
        s_new, s_old, v_new, v_old = [], [], [], []
        for g in range(G):
            qbd = jnp.tile(q_refs[g][i].astype(F32), (HEADS, 1)) * bd
            s_new.append(_dot_nt(qbd, _rb(new_refs[g][i, :, 0:GW])) + bn_refs[g][...])
            v_new.append(_rb(new_refs[g][i, :, GW:2 * GW]))
            s_old.append(s_refs[g][i])
            v_old.append(c_refs[g][i])
        m = functools.reduce(jnp.maximum, [jnp.max(s, axis=1, keepdims=True) for s in s_new + s_old])
        p_new = [jnp.exp(s - m) for s in s_new]
        p_old = [jnp.exp(s - m) for s in s_old]
        den = functools.reduce(lambda a, b: a + b, [jnp.sum(p, axis=1, keepdims=True) for p in p_new + p_old])
        inv = 1.0 / den
        acc = None
        for g in range(G):
            r = _dot(_rb(p_new[g] * inv), v_new[g]) + _dot_nt(p_old[g] * inv, v_old[g])
            acc = r if acc is None else acc + r
        acc = acc * bd
        out = acc[0:T, :]
        for hh in range(1, HEADS):
            out = out + acc[hh * T:(hh + 1) * T, :]
        o_ref[i] = out


def _head_diag(T):
    bd = np.zeros((HEADS * T, GW), np.float32)
    for hh in range(HEADS):
        bd[hh * T:(hh + 1) * T, hh * HEAD_DIM:(hh + 1) * HEAD_DIM] = 1.0
    return jnp.asarray(bd)


def _rider_common(DB, nsteps, step_of):
    dbs = -(-DB // nsteps)
    assert DB % dbs == 0
    active = DB // dbs
    blk = lambda b, s: jnp.minimum(step_of(b, s), active - 1)
    per = lambda shp, part=0: pl.BlockSpec((dbs,) + shp, lambda b, s: (blk(b, s), part) + (0,) * (len(shp) - 1))
    const = lambda a: pl.BlockSpec(a.shape, lambda b, s: (0,) * a.ndim)
    return dbs, active, per, const


def _scores_rider(q_list, caches_t):
    DB, T, _ = q_list[0].shape
    w_lens = [c.shape[2] for c in caches_t]
    bias_c, _ = _sample_masks(T, w_lens)

    def make(nsteps, step_of):
        dbs, active, per, const = _rider_common(DB, nsteps, step_of)
        return dict(
            body=_decode_scores_body, dbs=dbs, active=active,
            arrays=list(q_list) + list(caches_t) + bias_c + [_head_diag(T)],
            in_specs=[per((T, GW)) for _ in q_list] + [per((GW, w), 0) for w in w_lens]
                     + [const(a) for a in bias_c] + [pl.BlockSpec((HEADS * T, GW), lambda b, s: (0, 0))],
            out_shape=[jax.ShapeDtypeStruct((DB, HEADS * T, w), F32) for w in w_lens],
            out_specs=[per((HEADS * T, w)) for w in w_lens])
    return make


def _mix_rider(q_list, kv_new, scores, caches_t):
    DB, T, _ = q_list[0].shape
    w_lens = [c.shape[2] for c in caches_t]
    _, bias_n = _sample_masks(T, w_lens)

    def make(nsteps, step_of):
        dbs, active, per, const = _rider_common(DB, nsteps, step_of)
        return dict(
            body=_decode_mix_body, dbs=dbs, active=active,
            arrays=list(q_list) + list(kv_new) + list(scores) + list(caches_t) + bias_n + [_head_diag(T)],
            in_specs=[per((T, GW)) for _ in q_list] + [per((T, 2 * GW)) for _ in kv_new]
                     + [per((HEADS * T, w)) for w in w_lens] + [per((GW, w), 1) for w in w_lens]
                     + [const(a) for a in bias_n] + [pl.BlockSpec((HEADS * T, GW), lambda b, s: (0, 0))],
            out_shape=[jax.ShapeDtypeStruct((DB, T, GW), F32)],
            out_specs=[per((T, GW))])
    return make


def _sgla_kernel(gq_ref, gk_ref, gv_ref, la_ref, st_ref, tri_ref, nw_ref, o_ref, snew_ref, *, dbt):
    tri3 = tri_ref[...]
    normw = nw_ref[...]
    T = gq_ref.shape[1]
    ti = lax.broadcasted_iota(jnp.int32, (T, T), 0)
    si = lax.broadcasted_iota(jnp.int32, (T, T), 1)
    causal = si <= ti
    low_half = lax.broadcasted_iota(jnp.int32, (T, LANES), 1) < GLA_DK
    ones = jnp.ones((3 * T, LANES), F32)
    row_low = lax.broadcasted_iota(jnp.int32, (LANES, GLA_DV), 0) < GLA_DK

    def body(i, carry):
        la3 = jnp.concatenate(_split3(la_ref[i], F32), axis=0)
        b = _dot(tri3, la3)
        b_last = b[T - 1:T, :]
        gk = gk_ref[i].astype(F32)
        qe = gq_ref[i].astype(F32) * jnp.exp(b)
        ke = _rb(gk * jnp.exp(-b))
        kd = _rb(gk * jnp.exp(b_last - b))
        gv = gv_ref[i].astype(F32)
        outs = []
        for p in range(GLA_HEADS // 2):
            cs = slice(p * LANES, (p + 1) * LANES)
            st = st_ref[i, cs, :]
            st_b = _rb(st)
            dec = jnp.exp(_dot_tn(la3[:, cs], ones))
            upd = None
            for a in range(2):
                hh = 2 * p + a
                sel = low_half if a == 0 else jnp.logical_not(low_half)
                qm = _rb(jnp.where(sel, qe[:, cs], 0.0))
                att = _rb(jnp.where(causal, _dot_nt(qm, ke[:, cs]), 0.0))
                v_h = gv[:, hh * GLA_DV:(hh + 1) * GLA_DV]
                o = _dot(att, v_h) + _dot(qm, st_b)
                outs.append(o * lax.rsqrt(jnp.mean(o * o, axis=-1, keepdims=True) + NORM_EPS) * normw)
                u = _dot_tn(kd[:, cs], v_h)
                upd = u if a == 0 else jnp.where(row_low, upd, u)
            snew_ref[i, cs, :] = st * dec + upd
        o_ref[i] = jnp.concatenate(outs, axis=1)
        return carry

    for i in range(dbt):
        body(i, 0)


def _sample_gla(gq, gk, gv, la, state, normw, dbt):
    DB, T, _ = gq.shape
    dbt = min(dbt, DB)
    assert DB % dbt == 0
    st = state.reshape(DB, GLA_KW, GLA_DV)
    blk = lambda shp: pl.BlockSpec((dbt,) + shp, lambda i: (i,) + (0,) * len(shp))
    const = lambda shp: pl.BlockSpec(shp, lambda i: (0,) * len(shp))
    o, snew = pl.pallas_call(
        functools.partial(_sgla_kernel, dbt=dbt),
        grid=(DB // dbt,),
        in_specs=[blk((T, GLA_KW)), blk((T, GLA_KW)), blk((T, GLA_VW)), blk((T, GLA_KW)),
                  blk((GLA_KW, GLA_DV)), const((T, 3 * T)), const((1, GLA_DV))],
        out_specs=[blk((T, GLA_VW)), blk((GLA_KW, GLA_DV))],
        out_shape=[jax.ShapeDtypeStruct((DB, T, GLA_VW), F32),
                   jax.ShapeDtypeStruct((DB, GLA_KW, GLA_DV), F32)],
        compiler_params=_cparams(("arbitrary",)), name="gla_sample",
    )(gq, gk, gv, la, st, _tri3(T, F32), normw)
    return o, snew.reshape(DB, GLA_HEADS, GLA_DK, GLA_DV)


def _rope_tables(pos):
    half = ROPE_DIM // 2
    inv_freq = 1.0 / (ROPE_THETA ** (jnp.arange(half, dtype=F32) * (2.0 / ROPE_DIM)))
    ang = pos.astype(F32)[:, None] * inv_freq[None, :]
    cos, sin = jnp.cos(ang), jnp.sin(ang)
    n = pos.shape[0]
    pad = jnp.zeros((n, HEAD_DIM - ROPE_DIM), F32)
    zero = jnp.zeros((n, half), F32)
    c = jnp.concatenate([cos, cos, pad + 1.0], axis=1)
    s1 = jnp.concatenate([zero, sin, pad], axis=1)
    s2 = jnp.concatenate([-sin, zero, pad], axis=1)
    rep = LANES // HEAD_DIM
    return tuple(jnp.tile(t, (1, rep)) for t in (c, s1, s2))


def kernel(x_prompt, x_sample, cache_kv_w128, cache_kv_w512, cache_kv_w2048, state_gla,
           ln_w, w_in, w_gla_a2, b_gla_a, gla_norm_w, w_attn_out, w_gla_out, w_out, final_norm_w):
    B, S, D = x_prompt.shape
    DB, T, _ = x_sample.shape
    assert ln_w.shape[0] == 1, "single-layer step"
    caches = (cache_kv_w128, cache_kv_w512, cache_kv_w2048)
    dils = [d for _, d in GROUPS]

    lnw = ln_w[0].reshape(1, D)
    w_t = w_in[0].T
    w_qkv, w_rest = w_t[:OFF_AZ].astype(BF16), w_t[OFF_AZ:].astype(BF16)
    wa2 = w_gla_a2[0].astype(BF16)
    ba = b_gla_a[0].reshape(1, GLA_KW)
    gnw = gla_norm_w[0].reshape(1, GLA_DV)
    wa = w_attn_out[0].astype(BF16)
    wg = w_gla_out[0].astype(BF16)
    wo = w_out[0].astype(BF16)
    fnw = final_norm_w.reshape(1, D)


    R = DB * T
    xs = x_sample.reshape(1, R, D)
    pos_s = PAST_LEN + jnp.arange(T, dtype=jnp.int32)
    tabs_s = tuple(jnp.tile(t, (DB, 1)) for t in _rope_tables(pos_s))
    qkv_s = _proj_qkv(xs, lnw, w_qkv, tabs_s, min(512, R), (1,) * N_GROUPS, row_major_kv=True,
                      q_scale=HEAD_DIM ** -0.5)
    rest_s = _proj_rest(xs, lnw, w_rest, wa2, ba, min(512, R))
    per_db = lambda t: t.reshape(DB, T, t.shape[-1])
    q_s = [per_db(q) for q in qkv_s[0:3]]
    kv_new = [per_db(t) for t in qkv_s[9:12]]
    caches_t = [c[0].transpose(0, 2, 3, 4, 1).reshape(DB, 2 * GW, c.shape[2]) for c in caches]

    tabs_p = _rope_tables(jnp.arange(S, dtype=jnp.int32))
    qkv = _proj_qkv(x_prompt, lnw, w_qkv, tabs_p, min(512, S), dils, row_major_kv=False,
                    q_scale=HEAD_DIM ** -0.5 * LOG2E,
                    make_rider=_scores_rider(q_s, caches_t))
    scores = qkv[9:12]
    rest = _proj_rest(x_prompt, lnw, w_rest, wa2, ba, min(512, S),
                      make_rider=_mix_rider(q_s, kv_new, scores, caches_t))
    saz, sgz, sa, sb, gq, gk, gv, la, o_s = rest
    o_list, lse_list = [], []
    for g in range(N_GROUPS):
        o, lse = _attn_group(qkv[g], qkv[3 + g], qkv[6 + g], g, qb=1024)
        o_list.append(o.reshape(-1, o.shape[-1]))
        lse_list.append(lse.reshape(-1, lse.shape[-1]))
    og, gla_p = _gla_prompt(gq, gk, gv, la, gnw, ct=512)
    flat = lambda t: t.reshape(B * S, t.shape[-1])
    y_prompt = _merge(flat(x_prompt), o_list, lse_list, dils, flat(saz), flat(og), flat(sgz), flat(sa),
                      flat(sb), wa, wg, wo, fnw, tm=512).reshape(B, S, D)
    kvt = _kv_tail_prompt(x_prompt, lnw, w_qkv, [min(w, S) for w, _ in GROUPS], tile=512)
    kv_p = [t.reshape(B, 2, HEADS, HEAD_DIM, t.shape[-1]).transpose(0, 4, 1, 2, 3)[None] for t in kvt]

    saz, sgz, sa, sb, gq, gk, gv, la = rest_s
    og_s, gla_s = _sample_gla(per_db(gq), per_db(gk), per_db(gv), per_db(la), state_gla[0], gnw, dbt=16)
    flat_s = lambda t: t.reshape(R, t.shape[-1])
    y_sample = _merge(x_sample.reshape(R, D), [o_s.reshape(R, GW)], [], (1,), flat_s(saz), flat_s(og_s),
                      flat_s(sgz), flat_s(sa), flat_s(sb), wa, wg, wo, fnw, tm=512).reshape(DB, T, D)
    kvt_s = _kv_tail_sample(x_sample, lnw, w_qkv, pos_s)
    kv_s = [t.reshape(T, 2, HEADS, HEAD_DIM, DB).transpose(4, 0, 1, 2, 3)[None] for t in kvt_s]

    return (y_prompt, y_sample, kv_p[0], kv_p[1], kv_p[2], gla_p[None],
            kv_s[0], kv_s[1], kv_s[2], gla_s[None])
```

```python
import functools

import numpy as np
import jax
import jax.numpy as jnp
from jax import lax
from jax.experimental import pallas as pl
from jax.experimental.pallas import tpu as pltpu

F32 = jnp.float32
BF16 = jnp.bfloat16

D_MODEL = 1024
HEAD_DIM = 64
HEADS = 8
GROUPS = ((128, 1), (512, 4), (2048, 16))
N_GROUPS = len(GROUPS)
GW = HEADS * HEAD_DIM
QKV_W = N_GROUPS * GW
ROPE_DIM = HEAD_DIM // 4
ROPE_THETA = 500000.0
BAND = 128
GLA_HEADS = 4
GLA_DK = 64
GLA_DV = 128
GLA_KW = GLA_HEADS * GLA_DK
GLA_VW = GLA_HEADS * GLA_DV
GATE_RANK = 16
GLA_TAU = 16.0
GLA_CHUNK = 64
NORM_EPS = 1e-6
PAST_LEN = 8192

LANES = 128
NEG = -1e30
LOG2E = float(np.log2(np.e))
LN2 = float(np.log(2.0))

OFF_Q = 0
OFF_K = OFF_Q + QKV_W
OFF_V = OFF_K + QKV_W
OFF_AZ = OFF_V + QKV_W
OFF_GQ = OFF_AZ + GW
OFF_GK = OFF_GQ + GLA_KW
OFF_GV = OFF_GK + GLA_KW
OFF_GZ = OFF_GV + GLA_VW
OFF_LR = OFF_GZ + GLA_VW
OFF_MA = OFF_LR + GATE_RANK
OFF_MB = OFF_MA + D_MODEL
PROJ_W = OFF_MB + D_MODEL
assert all(o % 16 == 0 for o in (OFF_LR, OFF_MA, OFF_MB, PROJ_W))

VMEM_LIMIT = 56 * 1024 * 1024
MERGE_ROWS = 512

def _cparams(sem):
    return pltpu.CompilerParams(dimension_semantics=sem, vmem_limit_bytes=VMEM_LIMIT)


def _dot(a, b):
    return jnp.dot(a, b, preferred_element_type=F32)


def _dot_nt(a, b):
    return lax.dot_general(a, b, (((1,), (1,)), ((), ())), preferred_element_type=F32)


def _dot_tn(a, b):
    return lax.dot_general(a, b, (((0,), (0,)), ((), ())), preferred_element_type=F32)


def _rb(x):
    return x.astype(BF16).astype(F32)


def _split3(x, dtype=BF16):
    hi = x.astype(BF16)
    r1 = x - hi.astype(F32)
    mid = r1.astype(BF16)
    lo = (r1 - mid.astype(F32)).astype(BF16)
    return hi.astype(dtype), mid.astype(dtype), lo.astype(dtype)


def _sigmoid(x):
    return 1.0 / (1.0 + jnp.exp(-x))


def _log_sigmoid(x):
    return jnp.minimum(x, 0.0) - jnp.log1p(jnp.exp(-jnp.abs(x)))


def _rms_bf16(x, w):
    ms = jnp.mean(x * x, axis=-1, keepdims=True)
    return (x * lax.rsqrt(ms + NORM_EPS) * w).astype(BF16)


def _run_rider(rider, in_refs, out_refs):
    if rider is None:
        return
    pieces, active, dbs, nsteps = rider

    def emit():
        for thunk in pieces(in_refs, out_refs, dbs):
            thunk()

    if active == nsteps:
        emit()
    else:
        step = pl.program_id(0) * pl.num_programs(1) + pl.program_id(1)
        pl.when(step < active)(emit)


def _proj_qkv_kernel(x_ref, lnw_ref, w_ref, rc_ref, rs1_ref, rs2_ref, *refs,
                     dils, row_major_kv, q_scale, rider, n_rider_in):
    rider_in, refs = refs[:n_rider_in], refs[n_rider_in:]
    q_refs, k_refs, v_refs = refs[0:3], refs[3:6], refs[6:9]
    kv_refs = refs[9:12] if row_major_kv else ()
    rider_out = refs[9 + len(kv_refs):-1]
    stage_ref = refs[-1]
    h = _rms_bf16(x_ref[0], lnw_ref[...])
    tm = h.shape[0]
    rc, rs1, rs2 = rc_ref[...], rs1_ref[...], rs2_ref[...]

    def mm(off, width):
        return _dot_nt(h, w_ref[off:off + width, :])

    def rope(t):
        outs = []
        for j in range(t.shape[1] // LANES):
            c = t[:, j * LANES:(j + 1) * LANES]
            outs.append(c * rc + pltpu.roll(c, ROPE_DIM // 2, axis=1) * rs1
                        + pltpu.roll(c, LANES - ROPE_DIM // 2, axis=1) * rs2)
        return jnp.concatenate(outs, axis=1)

    def store_by_residue(dst_ref, val, d):
        if d == 1:
            dst_ref[0] = val.astype(BF16)
            return
        for c in range(GW // LANES):
            stage_ref[c] = val[:, c * LANES:(c + 1) * LANES]
        for r in range(d):
            for c in range(GW // LANES):
                col = r * GW + c * LANES
                dst_ref[0, :, col:col + LANES] = stage_ref[c, pl.ds(r, tm // d, stride=d), :].astype(BF16)

    for g in range(N_GROUPS):
        qr = rope(mm(OFF_Q + g * GW, GW)) * q_scale
        store_by_residue(q_refs[g], qr, dils[g])
        kr = rope(mm(OFF_K + g * GW, GW))
        store_by_residue(k_refs[g], kr, dils[g])
        vv = mm(OFF_V + g * GW, GW)
        store_by_residue(v_refs[g], vv, dils[g])
        if row_major_kv:
            kv_refs[g][0, :, 0:GW] = kr
            kv_refs[g][0, :, GW:2 * GW] = vv
    _run_rider(rider, rider_in, rider_out)


def _proj_rest_kernel(x_ref, lnw_ref, w_ref, wa2_ref, ba_ref, *refs, rider, n_rider_in):
    rider_in, refs = refs[:n_rider_in], refs[n_rider_in:]
    saz_ref, sgz_ref, sa_ref, sb_ref, gq_ref, gk_ref, gv_ref, la_ref = refs[0:8]
    rider_out = refs[8:]
    h = _rms_bf16(x_ref[0], lnw_ref[...])

    def mm(off, width):
        return _dot_nt(h, w_ref[off - OFF_AZ:off - OFF_AZ + width, :])

    az = mm(OFF_AZ, GW)
    saz_ref[0] = (az * _sigmoid(az)).astype(BF16)
    gz = mm(OFF_GZ, GLA_VW)
    sgz_ref[0] = (gz * _sigmoid(gz)).astype(BF16)
    for j in range(D_MODEL // GW):
        sa_ref[0, :, j * GW:(j + 1) * GW] = _sigmoid(mm(OFF_MA + j * GW, GW)).astype(BF16)
        sb_ref[0, :, j * GW:(j + 1) * GW] = _sigmoid(mm(OFF_MB + j * GW, GW)).astype(BF16)
    gq_ref[0] = (mm(OFF_GQ, GLA_KW) * (GLA_DK ** -0.5)).astype(BF16)
    gk_ref[0] = mm(OFF_GK, GLA_KW).astype(BF16)
    gv_ref[0] = mm(OFF_GV, GLA_VW).astype(BF16)
    glr = mm(OFF_LR, GATE_RANK)
    gate_pre = _dot(glr.astype(BF16), wa2_ref[...]) + ba_ref[...]
    la_ref[0] = _log_sigmoid(gate_pre) * (1.0 / GLA_TAU)
    _run_rider(rider, rider_in, rider_out)


def _rider_parts(make_rider, B, nt):
    if make_rider is None:
        return None, [], [], [], []
    r = make_rider(B * nt, lambda b, s: b * nt + s)
    return (r["pieces"], r["active"], r["dbs"], B * nt), r["arrays"], r["in_specs"], r["out_shape"], r["out_specs"]


def _proj_qkv(x, lnw, w_t, rope_tabs, tm, dils, row_major_kv, q_scale, make_rider=None):
    B, S, D = x.shape
    assert S % tm == 0 and all(tm % (16 * d) == 0 for d in dils)
    row = lambda w: pl.BlockSpec((1, tm, w), lambda b, s: (b, s, 0))
    tab = pl.BlockSpec((tm, LANES), lambda b, s: (s, 0))
    rider, r_arrays, r_in_specs, r_out_shape, r_out_specs = _rider_parts(make_rider, B, S // tm)
    out_shape, out_specs = [], []
    for _ in range(3):
        for d in dils:
            out_shape.append(jax.ShapeDtypeStruct((B, S // d, d * GW), BF16))
            out_specs.append(pl.BlockSpec((1, tm // d, d * GW), lambda b, s: (b, s, 0)))
    if row_major_kv:
        out_shape += [jax.ShapeDtypeStruct((B, S, 2 * GW), F32)] * N_GROUPS
        out_specs += [row(2 * GW)] * N_GROUPS
    in_specs = [row(D), pl.BlockSpec((1, D), lambda b, s: (0, 0)),
                pl.BlockSpec((OFF_AZ, D), lambda b, s: (0, 0), pipeline_mode=pl.Buffered(1)),
                tab, tab, tab]
    return pl.pallas_call(
        functools.partial(_proj_qkv_kernel, dils=tuple(dils), row_major_kv=row_major_kv, q_scale=q_scale,
                          rider=rider, n_rider_in=len(r_arrays)),
        grid=(B, S // tm), in_specs=in_specs + r_in_specs, out_specs=out_specs + r_out_specs,
        out_shape=out_shape + r_out_shape,
        scratch_shapes=[pltpu.VMEM((GW // LANES, tm, LANES), F32)],
        compiler_params=_cparams(("arbitrary", "arbitrary")), name="proj_qkv",
    )(x, lnw, w_t, *rope_tabs, *r_arrays)


def _proj_rest(x, lnw, w_t, wa2, ba, tm, make_rider=None):
    B, S, D = x.shape
    assert S % tm == 0
    row = lambda w: pl.BlockSpec((1, tm, w), lambda b, s: (b, s, 0))
    const = lambda shp: pl.BlockSpec(shp, lambda b, s: (0,) * len(shp))
    bf = lambda w: jax.ShapeDtypeStruct((B, S, w), BF16)
    rider, r_arrays, r_in_specs, r_out_shape, r_out_specs = _rider_parts(make_rider, B, S // tm)
    out_shape = [bf(GW), bf(GLA_VW), bf(D_MODEL), bf(D_MODEL), bf(GLA_KW), bf(GLA_KW), bf(GLA_VW),
                 jax.ShapeDtypeStruct((B, S, GLA_KW), F32)]
    out_specs = [row(GW), row(GLA_VW), row(D_MODEL), row(D_MODEL), row(GLA_KW), row(GLA_KW), row(GLA_VW),
                 row(GLA_KW)]
    in_specs = [row(D), const((1, D)),
                pl.BlockSpec((OFF_AZ, D), lambda b, s: (1, 0), pipeline_mode=pl.Buffered(1)),
                const((GATE_RANK, GLA_KW)), const((1, GLA_KW))]
    return pl.pallas_call(
        functools.partial(_proj_rest_kernel, rider=rider, n_rider_in=len(r_arrays)),
        grid=(B, S // tm), in_specs=in_specs + r_in_specs, out_specs=out_specs + r_out_specs,
        out_shape=out_shape + r_out_shape,
        compiler_params=_cparams(("arbitrary", "arbitrary")), name="proj_rest",
    )(x, lnw, w_t, wa2, ba, *r_arrays)


def _kvt_kernel(x_ref, lnw_ref, wk_ref, wv_ref, cos_ref, sin_ref, o0_ref, o1_ref, o2_ref, *, plan, nt):
    i = pl.program_id(1)
    h = _rms_bf16(x_ref[...], lnw_ref[...])
    rows = h.shape[0]
    half = ROPE_DIM // 2
    for g, (tiles, cols) in enumerate(plan):
        o_ref = (o0_ref, o1_ref, o2_ref)[g]

        def emit(g=g, cols=cols, o_ref=o_ref):
            hh = h[rows - cols:, :]
            y = _dot_nt(wk_ref[g * GW:(g + 1) * GW, :], hh)
            c = cos_ref[:, rows - cols:]
            s = sin_ref[:, rows - cols:]
            for hd in range(HEADS):
                b0 = hd * HEAD_DIM
                x1 = y[b0:b0 + half, :]
                x2 = y[b0 + half:b0 + ROPE_DIM, :]
                o_ref[b0:b0 + half, :] = x1 * c - x2 * s
                o_ref[b0 + half:b0 + ROPE_DIM, :] = x2 * c + x1 * s
                o_ref[b0 + ROPE_DIM:b0 + HEAD_DIM, :] = y[b0 + ROPE_DIM:b0 + HEAD_DIM, :]
            o_ref[GW:2 * GW, :] = _dot_nt(wv_ref[g * GW:(g + 1) * GW, :], hh)

        if tiles == nt:
            emit()
        else:
            pl.when(i >= nt - tiles)(emit)


def _rope_cos_sin_t(pos):
    half = ROPE_DIM // 2
    inv_freq = 1.0 / (ROPE_THETA ** (jnp.arange(half, dtype=F32) * (2.0 / ROPE_DIM)))
    ang = pos.astype(F32)[:, None] * inv_freq[None, :]
    return jnp.cos(ang).T, jnp.sin(ang).T


def _kv_tail_prompt(x, lnw, w_t, S_tails, tile):
    B, S, D = x.shape
    tile = min(tile, S)
    span = max(S_tails)
    assert span % tile == 0 and S % tile == 0
    nt = span // tile
    first = (S - span) // tile
    plan = []
    for t in S_tails:
        assert t % tile == 0 or t < tile
        plan.append((t // tile, tile) if t >= tile else (1, t))
    cos_t, sin_t = _rope_cos_sin_t(jnp.arange(S, dtype=jnp.int32))
    tabspec = pl.BlockSpec((ROPE_DIM // 2, tile), lambda b, i: (0, first + i))
    out_specs = [pl.BlockSpec((None, 2 * GW, cols), lambda b, i, t=tiles: (b, 0, jnp.maximum(i - (nt - t), 0)))
                 for tiles, cols in plan]
    return pl.pallas_call(
        functools.partial(_kvt_kernel, plan=tuple(plan), nt=nt),
        grid=(B, nt),
        in_specs=[pl.BlockSpec((None, tile, D), lambda b, i: (b, first + i, 0)),
                  pl.BlockSpec((1, D), lambda b, i: (0, 0)),
                  pl.BlockSpec((QKV_W, D), lambda b, i: (OFF_K // QKV_W, 0), pipeline_mode=pl.Buffered(1)),
                  pl.BlockSpec((QKV_W, D), lambda b, i: (OFF_V // QKV_W, 0), pipeline_mode=pl.Buffered(1)),
                  tabspec, tabspec],
        out_specs=out_specs,
        out_shape=[jax.ShapeDtypeStruct((B, 2 * GW, t), F32) for t in S_tails],
        compiler_params=_cparams(("arbitrary", "arbitrary")), name="kv_tail_prompt",
    )(x, lnw, w_t, w_t, cos_t, sin_t)


def _kv_tail_sample(x, lnw, w_t, pos):
    DB, T, D = x.shape
    cos_t, sin_t = _rope_cos_sin_t(pos)
    bcast = lambda t: jnp.broadcast_to(t.T[:, :, None], (T, ROPE_DIM // 2, DB))
    tabspec = pl.BlockSpec((None, ROPE_DIM // 2, DB), lambda b, i: (i, 0, 0))
    out_spec = pl.BlockSpec((None, 2 * GW, DB), lambda b, i: (i, 0, 0))
    return pl.pallas_call(
        functools.partial(_kvt_kernel, plan=((T, DB),) * N_GROUPS, nt=T),
        grid=(1, T),
        in_specs=[pl.BlockSpec((None, DB, D), lambda b, i: (i, 0, 0)),
                  pl.BlockSpec((1, D), lambda b, i: (0, 0)),
                  pl.BlockSpec((QKV_W, D), lambda b, i: (OFF_K // QKV_W, 0), pipeline_mode=pl.Buffered(1)),
                  pl.BlockSpec((QKV_W, D), lambda b, i: (OFF_V // QKV_W, 0), pipeline_mode=pl.Buffered(1)),
                  tabspec, tabspec],
        out_specs=[out_spec] * N_GROUPS,
        out_shape=[jax.ShapeDtypeStruct((T, 2 * GW, DB), F32)] * N_GROUPS,
        compiler_params=_cparams(("arbitrary", "arbitrary")), name="kv_tail_sample",
    )(x.transpose(1, 0, 2), lnw, w_t, w_t, bcast(cos_t), bcast(sin_t))


def _band_bias():
    q = np.arange(BAND)[:, None]
    c = np.arange(2 * BAND)[None, :]
    ok = np.where(c < BAND, c >= q, (c - BAND) <= q)
    ok_first = ok & (c >= BAND)
    return jnp.asarray(np.stack([np.where(ok, 0.0, NEG), np.where(ok_first, 0.0, NEG)]).astype(np.float32))


def _attn_kernel(q_ref, k_ref, kp_ref, v_ref, vp_ref, bias_ref, o_ref, lse_ref, kall, vall, *, qb):
    n = pl.program_id(2)
    nsub = qb // BAND
    npair = HEADS // 2
    kall[0:BAND, :] = kp_ref[0]
    kall[BAND:, :] = k_ref[0]
    vall[0:BAND, :] = vp_ref[0]
    vall[BAND:, :] = v_ref[0]
    lane = lax.broadcasted_iota(jnp.int32, (BAND, LANES), 1)
    low_half = lane < HEAD_DIM

    def sub_block(j, carry):
        r0 = pl.multiple_of(j * BAND, BAND)
        first = (n * nsub + j) == 0
        bias = bias_ref[jnp.where(first, 1, 0)]
        bias2 = jnp.concatenate([bias, bias], axis=0)
        scores = []
        for hp in range(npair):
            cs = slice(hp * LANES, (hp + 1) * LANES)
            qp = q_ref[0, pl.ds(r0, BAND), cs]
            zero = jnp.zeros_like(qp)
            qm = jnp.concatenate([jnp.where(low_half, qp, zero), jnp.where(low_half, zero, qp)], axis=0)
            scores.append(_dot_nt(qm, kall[pl.ds(r0, 2 * BAND), cs]) + bias2)
        probs, stats = [], []
        for s in scores:
            m = jnp.max(s, axis=1, keepdims=True)
            p = jnp.exp2(s - m)
            stats.append((m, jnp.sum(p, axis=1, keepdims=True)))
            probs.append(p.astype(BF16))
        m_tile = jnp.zeros((BAND, LANES), F32)
        den_tile = jnp.ones((BAND, LANES), F32)
        for hp in range(npair):
            cs = slice(hp * LANES, (hp + 1) * LANES)
            m, den = stats[hp]
            o = _dot(probs[hp], vall[pl.ds(r0, 2 * BAND), cs])
            o = jnp.where(low_half, o[0:BAND], o[BAND:]) / jnp.where(low_half, den[0:BAND], den[BAND:])
            o_ref[0, pl.ds(r0, BAND), cs] = o.astype(BF16)
            for a in range(2):
                rows = slice(a * BAND, (a + 1) * BAND)
                m_tile = jnp.where(lane == 2 * hp + a, m[rows], m_tile)
                den_tile = jnp.where(lane == 2 * hp + a, den[rows], den_tile)
        lse_ref[0, pl.ds(r0, BAND), :] = (m_tile + jnp.log2(den_tile)) * LN2
        return carry

    lax.fori_loop(0, nsub, sub_block, 0, unroll=True)


def _attn_group(q, k, v, g, qb):
    B, L, _ = q.shape
    win, dil = GROUPS[g]
    assert win // dil == BAND
    qb = min(qb, L)
    assert L % qb == 0 and qb % BAND == 0
    nsub = qb // BAND
    cur = pl.BlockSpec((1, qb, GW), lambda b, r, n: (b, n, r))
    prev = pl.BlockSpec((1, BAND, GW), lambda b, r, n: (b, jnp.maximum(n * nsub - 1, 0), r))
    return pl.pallas_call(
        functools.partial(_attn_kernel, qb=qb),
        grid=(B, dil, L // qb),
        in_specs=[cur, cur, prev, cur, prev, pl.BlockSpec((2, BAND, 2 * BAND), lambda b, r, n: (0, 0, 0))],
        out_specs=[pl.BlockSpec((1, qb, GW), lambda b, r, n: (b, n, r)),
                   pl.BlockSpec((1, qb, LANES), lambda b, r, n: (b, n, r))],
        out_shape=[jax.ShapeDtypeStruct((B, L, dil * GW), BF16),
                   jax.ShapeDtypeStruct((B, L, dil * LANES), F32)],
        scratch_shapes=[pltpu.VMEM((qb + BAND, GW), BF16), pltpu.VMEM((qb + BAND, GW), BF16)],
        compiler_params=_cparams(("arbitrary", "arbitrary", "arbitrary")), name=f"attn_g{g}",
    )(q, k, k, v, v, _band_bias())


def _gla_kernel(gq_ref, gk_ref, gv_ref, la_ref, tri_ref, nw_ref, o_ref, sfin_ref, st_ref, *, nchunk):
    c = pl.program_id(1)

    @pl.when(c == 0)
    def _():
        st_ref[...] = jnp.zeros_like(st_ref)

    C = GLA_CHUNK
    npair = GLA_HEADS // 2
    tri3 = tri_ref[...]
    normw = nw_ref[...]
    hi, mid, lo = _split3(la_ref[0])
    b_chunks = []
    for ci in range(nchunk):
        rows = slice(ci * C, (ci + 1) * C)
        b_chunks.append(_dot(tri3, jnp.concatenate([hi[rows], mid[rows], lo[rows]], axis=0)))
    b = jnp.concatenate(b_chunks, axis=0)
    b_last = jnp.concatenate([jnp.broadcast_to(bc[C - 1:C, :], bc.shape) for bc in b_chunks], axis=0)
    gk = gk_ref[0].astype(F32)
    qe = gq_ref[0].astype(F32) * jnp.exp(b)
    ke = (gk * jnp.exp(-b)).astype(BF16)
    kd = (gk * jnp.exp(b_last - b)).astype(BF16)
    gv = gv_ref[0]

    low_half = lax.broadcasted_iota(jnp.int32, (C, LANES), 1) < GLA_DK
    ri = lax.broadcasted_iota(jnp.int32, (2 * C, 2 * C), 0)
    ki = lax.broadcasted_iota(jnp.int32, (2 * C, 2 * C), 1)
    pair_causal = jnp.logical_and((ri < C) == (ki < C),
                                  jnp.bitwise_and(ki, C - 1) <= jnp.bitwise_and(ri, C - 1))

    lhs, vrows, upd, dec = {}, {}, {}, {}
    for ci in range(nchunk):
        rows = slice(ci * C, (ci + 1) * C)
        for p in range(npair):
            cs = slice(p * LANES, (p + 1) * LANES)
            qe_p = qe[rows, cs]
            qm = jnp.concatenate([jnp.where(low_half, qe_p, 0.0), jnp.where(low_half, 0.0, qe_p)],
                                 axis=0).astype(BF16)
            ke_p = ke[rows, cs]
            sc = _dot_nt(qm, jnp.concatenate([ke_p, ke_p], axis=0))
            att = jnp.where(pair_causal, sc, 0.0).astype(BF16)
            lhs[ci, p] = jnp.concatenate([att, qm], axis=1)
            v_ab = gv[rows, 2 * p * GLA_DV:2 * (p + 1) * GLA_DV]
            vrows[ci, p] = jnp.concatenate([v_ab[:, 0:GLA_DV], v_ab[:, GLA_DV:]], axis=0)
            u = _dot_tn(kd[rows, cs], v_ab)
            upd[ci, p] = jnp.concatenate([u[0:GLA_DK, 0:GLA_DV], u[GLA_DK:, GLA_DV:]], axis=0)
            bl = jnp.broadcast_to(b_chunks[ci][C - 1:C, cs], (LANES, LANES))
            dec[ci, p] = jnp.exp(bl.T)

    outs = {}
    states = [st_ref[p] for p in range(npair)]
    for ci in range(nchunk):
        for p in range(npair):
            rhs = jnp.concatenate([vrows[ci, p], states[p].astype(BF16)], axis=0)
            outs[ci, p] = _dot(lhs[ci, p], rhs)
            states[p] = dec[ci, p] * states[p] + upd[ci, p]
    for p in range(npair):
        st_ref[p] = states[p]

    for ci in range(nchunk):
        for p in range(npair):
            o = outs[ci, p]
            o = (o * lax.rsqrt(jnp.mean(o * o, axis=-1, keepdims=True) + NORM_EPS) * normw).astype(BF16)
            for a in range(2):
                hh = 2 * p + a
                o_ref[0, ci * C:(ci + 1) * C, hh * GLA_DV:(hh + 1) * GLA_DV] = o[a * C:(a + 1) * C, :]

    @pl.when(c == pl.num_programs(1) - 1)
    def _():
        for p in range(npair):
            sfin_ref[0, p * LANES:(p + 1) * LANES, :] = states[p]


def _tri3(C, dtype=BF16):
    tri = np.tril(np.ones((C, C), np.float32))
    return jnp.asarray(np.concatenate([tri, tri, tri], axis=1), dtype)


def _gla_prompt(gq, gk, gv, la, normw, ct):
    B, S, _ = gq.shape
    ct = min(ct, S)
    assert S % ct == 0 and ct % GLA_CHUNK == 0
    row = lambda w: pl.BlockSpec((1, ct, w), lambda b, c: (b, c, 0))
    const = lambda shp: pl.BlockSpec(shp, lambda b, c: (0,) * len(shp))
    o, sfin = pl.pallas_call(
        functools.partial(_gla_kernel, nchunk=ct // GLA_CHUNK),
        grid=(B, S // ct),
        in_specs=[row(GLA_KW), row(GLA_KW), row(GLA_VW), row(GLA_KW),
                  const((GLA_CHUNK, 3 * GLA_CHUNK)), const((1, GLA_DV))],
        out_specs=[row(GLA_VW), pl.BlockSpec((1, GLA_KW, GLA_DV), lambda b, c: (b, 0, 0))],
        out_shape=[jax.ShapeDtypeStruct((B, S, GLA_VW), BF16),
                   jax.ShapeDtypeStruct((B, GLA_KW, GLA_DV), F32)],
        scratch_shapes=[pltpu.VMEM((GLA_HEADS // 2, GLA_DV, LANES), F32)],
        compiler_params=_cparams(("arbitrary", "arbitrary")), name="gla_prompt",
    )(gq, gk, gv, la, _tri3(GLA_CHUNK), normw)
    return o, sfin.reshape(B, GLA_HEADS, GLA_DK, GLA_DV)


def _merge_kernel(*refs, dils):
    n_groups = len(dils)
    x_ref = refs[0]
    o_refs = refs[1:1 + n_groups]
    lse_refs = refs[1 + n_groups:1 + 2 * n_groups] if n_groups > 1 else ()
    rest = refs[1 + n_groups + len(lse_refs):]
    (saz_ref, og_ref, sgz_ref, sa_ref, sb_ref, wa_ref, wg_ref, wo_ref, fnw_ref, ex_ref, y_ref) = rest[:11]
    scratch = rest[11:]
    tm = x_ref.shape[0]

    def by_position(ref, width, d, stage_ref):
        if d == 1:
            return lambda rows: ref[rows, :].astype(F32)
        planes = width // LANES
        for r in range(d):
            blk = ref[:, r * width:(r + 1) * width].astype(F32)
            for c in range(planes):
                stage_ref[c, pl.ds(r, tm // d, stride=d), :] = blk[:, c * LANES:(c + 1) * LANES]
        return lambda rows: jnp.concatenate([stage_ref[c, rows, :] for c in range(planes)], axis=1)

    if n_groups > 1:
        lse_rows = [by_position(lse_refs[g], LANES, dils[g], scratch[2 * g + 1]) for g in range(n_groups)]
        o_rows = [by_position(o_refs[g], GW, dils[g], scratch[2 * g]) for g in range(n_groups)]

    rc = min(tm, MERGE_ROWS)
    for r0 in range(0, tm, rc):
        rows = slice(r0, r0 + rc)
        if n_groups > 1:
            ls = [f(rows) for f in lse_rows]
            mx = functools.reduce(jnp.maximum, ls)
            es = [jnp.exp(l - mx) for l in ls]
            inv = 1.0 / functools.reduce(lambda a, b: a + b, es)
            comb = None
            for g in range(n_groups):
                w = es[g] * inv
                w_hi = w.astype(BF16)
                w_lo = (w - w_hi.astype(F32)).astype(BF16)
                wx = _dot(jnp.concatenate([w_hi, w_lo], axis=1), ex_ref[...])
                term = wx * o_rows[g](rows)
                comb = term if comb is None else comb + term
        else:
            comb = o_refs[0][rows, :].astype(F32)

        ua = (comb * saz_ref[rows, :].astype(F32)).astype(BF16)
        ub = (og_ref[rows, :].astype(F32) * sgz_ref[rows, :].astype(F32)).astype(BF16)
        ya = _dot(ua, wa_ref[...])
        yb = _dot(ub, wg_ref[...])
        mixed = (sa_ref[rows, :].astype(F32) * ya + sb_ref[rows, :].astype(F32) * yb).astype(BF16)
        out = x_ref[rows, :] + _dot(mixed, wo_ref[...])
        y_ref[rows, :] = (out * lax.rsqrt(jnp.mean(out * out, axis=-1, keepdims=True) + NORM_EPS)
                          * fnw_ref[...])


def _expand_matrix():
    e = np.zeros((LANES, GW), np.float32)
    for hh in range(HEADS):
        e[hh, hh * HEAD_DIM:(hh + 1) * HEAD_DIM] = 1.0
    return jnp.asarray(np.concatenate([e, e], axis=0), BF16)


def _merge(x, o_list, lse_list, dils, saz, og, sgz, sa, sb, wa, wg, wo, fnw, tm):
    R, D = x.shape
    tm = min(tm, R)
    assert R % tm == 0
    n_groups = len(o_list)
    row = lambda w: pl.BlockSpec((tm, w), lambda i: (i, 0))
    const = lambda shp: pl.BlockSpec(shp, lambda i: (0,) * len(shp))
    in_specs = [row(D)]
    in_specs += [pl.BlockSpec((tm // d, d * GW), lambda i: (i, 0)) for d in dils]
    in_specs += [pl.BlockSpec((tm // d, d * LANES), lambda i: (i, 0)) for d in dils[:len(lse_list)]]
    in_specs += [row(GW), row(GLA_VW), row(GLA_VW), row(D), row(D),
                 const((GW, D)), const((GLA_VW, D)), const((D, D)), const((1, D)), const((2 * LANES, GW))]
    scratch = []
    if n_groups > 1:
        for _ in dils:
            scratch += [pltpu.VMEM((GW // LANES, tm, LANES), F32), pltpu.VMEM((1, tm, LANES), F32)]
    return pl.pallas_call(
        functools.partial(_merge_kernel, dils=tuple(dils)),
        grid=(R // tm,), in_specs=in_specs, out_specs=row(D),
        out_shape=jax.ShapeDtypeStruct((R, D), F32), scratch_shapes=scratch,
        compiler_params=_cparams(("arbitrary",)), name=f"merge_g{n_groups}",
    )(x, *o_list, *lse_list, saz, og, sgz, sa, sb, wa, wg, wo, fnw, _expand_matrix())


def _sample_masks(T, w_lens):
    t_of_row = np.arange(HEADS * T) % T

    def bias(idx, g):
        win, dil = GROUPS[g]
        dd = w_lens[g] + t_of_row[:, None] - idx[None, :]
        ok = (dd >= 0) & (dd % dil == 0) & (dd // dil <= win // dil)
        return jnp.asarray(np.where(ok, 0.0, NEG).astype(np.float32))

    cache = [bias(np.arange(w_lens[g]), g) for g in range(N_GROUPS)]
    new = [bias(w_lens[g] + np.arange(T), g) for g in range(N_GROUPS)]
    return cache, new


def _decode_scores_pieces(in_refs, out_refs, dbs):
    G = N_GROUPS
    sq_refs, sc_refs, sb_refs, bd_ref = in_refs[0:G], in_refs[G:2 * G], in_refs[2 * G:3 * G], in_refs[3 * G]

    def piece(i, g):
        qbd = (jnp.tile(sq_refs[g][i].astype(F32), (HEADS, 1)) * bd_ref[...]).astype(BF16)
        out_refs[g][i] = _dot(qbd, sc_refs[g][i].astype(BF16)) + sb_refs[g][...]

    return [functools.partial(piece, i, g) for i in range(dbs) for g in range(G)]


def _decode_mix_pieces(in_refs, out_refs, dbs):
    G = N_GROUPS
    q_refs, new_refs, s_refs = in_refs[0:G], in_refs[G:2 * G], in_refs[2 * G:3 * G]
    c_refs, bn_refs, bd_ref = in_refs[3 * G:4 * G], in_refs[4 * G:5 * G], in_refs[5 * G]
    o_ref = out_refs[0]
    T = q_refs[0].shape[1]
    p_old, acc = {}, {}

    def prologue(i):
        bd = bd_ref[...]
        s_new, v_new = [], []
        for g in range(G):
            qbd = jnp.tile(q_refs[g][i].astype(F32), (HEADS, 1)) * bd
            s_new.append(_dot_nt(qbd, _rb(new_refs[g][i, :, 0:GW])) + bn_refs[g][...])
            v_new.append(_rb(new_refs[g][i, :, GW:2 * GW]))
        s_old = [s_refs[g][i] for g in range(G)]
        m = functools.reduce(jnp.maximum, [jnp.max(s, axis=1, keepdims=True) for s in s_new + s_old])
        e_new = [jnp.exp(s - m) for s in s_new]
        e_old = [jnp.exp(s - m) for s in s_old]
        den = functools.reduce(lambda a, b: a + b, [jnp.sum(p, axis=1, keepdims=True) for p in e_new + e_old])
        inv = 1.0 / den
        for g in range(G):
            p_old[i, g] = e_old[g] * inv
            r = _dot(_rb(e_new[g] * inv), v_new[g])
            acc[i] = r if g == 0 else acc[i] + r

    def piece(i, g):
        acc[i] = acc[i] + _dot_nt(p_old[i, g], c_refs[g][i])

    def epilogue(i):
        a = acc[i] * bd_ref[...]
        out = a[0:T, :]
        for hh in range(1, HEADS):
            out = out + a[hh * T:(hh + 1) * T, :]
        o_ref[i] = out

    thunks = []
    for i in range(dbs):
        thunks += [functools.partial(prologue, i)] + [functools.partial(piece, i, g) for g in range(G)]
        thunks += [functools.partial(epilogue, i)]
    return thunks


def _head_diag(T):
    bd = np.zeros((HEADS * T, GW), np.float32)
    for hh in range(HEADS):
        bd[hh * T:(hh + 1) * T, hh * HEAD_DIM:(hh + 1) * HEAD_DIM] = 1.0
    return jnp.asarray(bd)


def _rider_common(DB, nsteps, step_of):
    dbs = -(-DB // nsteps)
    assert DB % dbs == 0
    active = DB // dbs
    blk = lambda b, s: jnp.minimum(step_of(b, s), active - 1)
    per = lambda shp, part=0: pl.BlockSpec((dbs,) + shp, lambda b, s: (blk(b, s), part) + (0,) * (len(shp) - 1))
    const = lambda a: pl.BlockSpec(a.shape, lambda b, s: (0,) * a.ndim)
    return dbs, active, per, const


def _scores_rider(q_list, caches_t):
    DB, T, _ = q_list[0].shape
    w_lens = [c.shape[2] for c in caches_t]
    bias_c, _ = _sample_masks(T, w_lens)

    def make(nsteps, step_of):
        dbs, active, per, const = _rider_common(DB, nsteps, step_of)
        return dict(
            pieces=_decode_scores_pieces, dbs=dbs, active=active,
            arrays=list(q_list) + list(caches_t) + bias_c + [_head_diag(T)],
            in_specs=[per((T, GW)) for _ in q_list] + [per((GW, w), 0) for w in w_lens]
                     + [const(a) for a in bias_c] + [pl.BlockSpec((HEADS * T, GW), lambda b, s: (0, 0))],
            out_shape=[jax.ShapeDtypeStruct((DB, HEADS * T, w), F32) for w in w_lens],
            out_specs=[per((HEADS * T, w)) for w in w_lens])
    return make


def _mix_rider(q_list, kv_new, scores, caches_t):
    DB, T, _ = q_list[0].shape
    w_lens = [c.shape[2] for c in caches_t]
    _, bias_n = _sample_masks(T, w_lens)

    def make(nsteps, step_of):
        dbs, active, per, const = _rider_common(DB, nsteps, step_of)
        return dict(
            pieces=_decode_mix_pieces, dbs=dbs, active=active,
            arrays=list(q_list) + list(kv_new) + list(scores) + list(caches_t) + bias_n + [_head_diag(T)],
            in_specs=[per((T, GW)) for _ in q_list] + [per((T, 2 * GW)) for _ in kv_new]
                     + [per((HEADS * T, w)) for w in w_lens] + [per((GW, w), 1) for w in w_lens]
                     + [const(a) for a in bias_n] + [pl.BlockSpec((HEADS * T, GW), lambda b, s: (0, 0))],
            out_shape=[jax.ShapeDtypeStruct((DB, T, GW), F32)],
            out_specs=[per((T, GW))])
    return make


def _sgla_kernel(gq_ref, gk_ref, gv_ref, la_ref, st_ref, tri_ref, nw_ref, o_ref, snew_ref, *, dbt):
    tri3 = tri_ref[...]
    normw = nw_ref[...]
    T = gq_ref.shape[1]
    ti = lax.broadcasted_iota(jnp.int32, (T, T), 0)
    si = lax.broadcasted_iota(jnp.int32, (T, T), 1)
    causal = si <= ti
    low_half = lax.broadcasted_iota(jnp.int32, (T, LANES), 1) < GLA_DK
    ones = jnp.ones((3 * T, LANES), F32)
    row_low = lax.broadcasted_iota(jnp.int32, (LANES, GLA_DV), 0) < GLA_DK

    def body(i, carry):
        la3 = jnp.concatenate(_split3(la_ref[i], F32), axis=0)
        b = _dot(tri3, la3)
        b_last = b[T - 1:T, :]
        gk = gk_ref[i].astype(F32)
        qe = gq_ref[i].astype(F32) * jnp.exp(b)
        ke = _rb(gk * jnp.exp(-b))
        kd = _rb(gk * jnp.exp(b_last - b))
        gv = gv_ref[i].astype(F32)
        outs = []
        for p in range(GLA_HEADS // 2):
            cs = slice(p * LANES, (p + 1) * LANES)
            st = st_ref[i, cs, :]
            st_b = _rb(st)
            dec = jnp.exp(_dot_tn(la3[:, cs], ones))
            upd = None
            for a in range(2):
                hh = 2 * p + a
                sel = low_half if a == 0 else jnp.logical_not(low_half)
                qm = _rb(jnp.where(sel, qe[:, cs], 0.0))
                att = _rb(jnp.where(causal, _dot_nt(qm, ke[:, cs]), 0.0))
                v_h = gv[:, hh * GLA_DV:(hh + 1) * GLA_DV]
                o = _dot(att, v_h) + _dot(qm, st_b)
                outs.append(o * lax.rsqrt(jnp.mean(o * o, axis=-1, keepdims=True) + NORM_EPS) * normw)
                u = _dot_tn(kd[:, cs], v_h)
                upd = u if a == 0 else jnp.where(row_low, upd, u)
            snew_ref[i, cs, :] = st * dec + upd
        o_ref[i] = jnp.concatenate(outs, axis=1)
        return carry

    for i in range(dbt):
        body(i, 0)


def _sample_gla(gq, gk, gv, la, state, normw, dbt):
    DB, T, _ = gq.shape
    dbt = min(dbt, DB)
    assert DB % dbt == 0
    st = state.reshape(DB, GLA_KW, GLA_DV)
    blk = lambda shp: pl.BlockSpec((dbt,) + shp, lambda i: (i,) + (0,) * len(shp))
    const = lambda shp: pl.BlockSpec(shp, lambda i: (0,) * len(shp))
    o, snew = pl.pallas_call(
        functools.partial(_sgla_kernel, dbt=dbt),
        grid=(DB // dbt,),
        in_specs=[blk((T, GLA_KW)), blk((T, GLA_KW)), blk((T, GLA_VW)), blk((T, GLA_KW)),
                  blk((GLA_KW, GLA_DV)), const((T, 3 * T)), const((1, GLA_DV))],
        out_specs=[blk((T, GLA_VW)), blk((GLA_KW, GLA_DV))],
        out_shape=[jax.ShapeDtypeStruct((DB, T, GLA_VW), F32),
                   jax.ShapeDtypeStruct((DB, GLA_KW, GLA_DV), F32)],
        compiler_params=_cparams(("arbitrary",)), name="gla_sample",
    )(gq, gk, gv, la, st, _tri3(T, F32), normw)
    return o, snew.reshape(DB, GLA_HEADS, GLA_DK, GLA_DV)


def _rope_tables(pos):
    half = ROPE_DIM // 2
    inv_freq = 1.0 / (ROPE_THETA ** (jnp.arange(half, dtype=F32) * (2.0 / ROPE_DIM)))
    ang = pos.astype(F32)[:, None] * inv_freq[None, :]
    cos, sin = jnp.cos(ang), jnp.sin(ang)
    n = pos.shape[0]
    pad = jnp.zeros((n, HEAD_DIM - ROPE_DIM), F32)
    zero = jnp.zeros((n, half), F32)
    c = jnp.concatenate([cos, cos, pad + 1.0], axis=1)
    s1 = jnp.concatenate([zero, sin, pad], axis=1)
    s2 = jnp.concatenate([-sin, zero, pad], axis=1)
    rep = LANES // HEAD_DIM
    return tuple(jnp.tile(t, (1, rep)) for t in (c, s1, s2))


def kernel(x_prompt, x_sample, cache_kv_w128, cache_kv_w512, cache_kv_w2048, state_gla,
           ln_w, w_in, w_gla_a2, b_gla_a, gla_norm_w, w_attn_out, w_gla_out, w_out, final_norm_w):
    B, S, D = x_prompt.shape
    DB, T, _ = x_sample.shape
    assert ln_w.shape[0] == 1, "single-layer step"
    caches = (cache_kv_w128, cache_kv_w512, cache_kv_w2048)
    dils = [d for _, d in GROUPS]

    lnw = ln_w[0].reshape(1, D)
    w_t = w_in[0].T.astype(BF16)
    assert PROJ_W <= 2 * OFF_AZ
    wa2 = w_gla_a2[0].astype(BF16)
    ba = b_gla_a[0].reshape(1, GLA_KW)
    gnw = gla_norm_w[0].reshape(1, GLA_DV)
    wa = w_attn_out[0].astype(BF16)
    wg = w_gla_out[0].astype(BF16)
    wo = w_out[0].astype(BF16)
    fnw = final_norm_w.reshape(1, D)


    R = DB * T
    xs = x_sample.reshape(1, R, D)
    pos_s = PAST_LEN + jnp.arange(T, dtype=jnp.int32)
    tabs_s = tuple(jnp.tile(t, (DB, 1)) for t in _rope_tables(pos_s))
    qkv_s = _proj_qkv(xs, lnw, w_t, tabs_s, min(512, R), (1,) * N_GROUPS, row_major_kv=True,
                      q_scale=HEAD_DIM ** -0.5)
    rest_s = _proj_rest(xs, lnw, w_t, wa2, ba, min(512, R))
    per_db = lambda t: t.reshape(DB, T, t.shape[-1])
    q_s = [per_db(q) for q in qkv_s[0:3]]
    kv_new = [per_db(t) for t in qkv_s[9:12]]
    caches_t = [c[0].transpose(0, 2, 3, 4, 1).reshape(DB, 2 * GW, c.shape[2]) for c in caches]

    tabs_p = _rope_tables(jnp.arange(S, dtype=jnp.int32))
    qkv = _proj_qkv(x_prompt, lnw, w_t, tabs_p, min(512, S), dils, row_major_kv=False,
                    q_scale=HEAD_DIM ** -0.5 * LOG2E,
                    make_rider=_scores_rider(q_s, caches_t))
    scores = qkv[9:12]
    rest = _proj_rest(x_prompt, lnw, w_t, wa2, ba, min(512, S),
                      make_rider=_mix_rider(q_s, kv_new, scores, caches_t))
    saz, sgz, sa, sb, gq, gk, gv, la, o_s = rest
    o_list, lse_list = [], []
    for g in range(N_GROUPS):
        o, lse = _attn_group(qkv[g], qkv[3 + g], qkv[6 + g], g, qb=1024)
        o_list.append(o.reshape(-1, o.shape[-1]))
        lse_list.append(lse.reshape(-1, lse.shape[-1]))
    og, gla_p = _gla_prompt(gq, gk, gv, la, gnw, ct=512)
    flat = lambda t: t.reshape(B * S, t.shape[-1])
    y_prompt = _merge(flat(x_prompt), o_list, lse_list, dils, flat(saz), flat(og), flat(sgz), flat(sa),
                      flat(sb), wa, wg, wo, fnw, tm=512).reshape(B, S, D)
    kvt = _kv_tail_prompt(x_prompt, lnw, w_t, [min(w, S) for w, _ in GROUPS], tile=512)
    kv_p = [t.reshape(B, 2, HEADS, HEAD_DIM, t.shape[-1]).transpose(0, 4, 1, 2, 3)[None] for t in kvt]

    saz, sgz, sa, sb, gq, gk, gv, la = rest_s
    og_s, gla_s = _sample_gla(per_db(gq), per_db(gk), per_db(gv), per_db(la), state_gla[0], gnw, dbt=16)
    flat_s = lambda t: t.reshape(R, t.shape[-1])
    y_sample = _merge(x_sample.reshape(R, D), [o_s.reshape(R, GW)], [], (1,), flat_s(saz), flat_s(og_s),
                      flat_s(sgz), flat_s(sa), flat_s(sb), wa, wg, wo, fnw, tm=512).reshape(DB, T, D)
    kvt_s = _kv_tail_sample(x_sample, lnw, w_t, pos_s)
    kv_s = [t.reshape(T, 2, HEADS, HEAD_DIM, DB).transpose(4, 0, 1, 2, 3)[None] for t in kvt_s]

    return (y_prompt, y_sample, kv_p[0], kv_p[1], kv_p[2], gla_p[None],
            kv_s[0], kv_s[1], kv_s[2], gla_s[None])
```

```python
import functools

import numpy as np
import jax
import jax.numpy as jnp
from jax import lax
from jax.experimental import pallas as pl
from jax.experimental.pallas import tpu as pltpu

F32 = jnp.float32
BF16 = jnp.bfloat16

D_MODEL = 1024
HEAD_DIM = 64
HEADS = 8
GROUPS = ((128, 1), (512, 4), (2048, 16))
N_GROUPS = len(GROUPS)
GW = HEADS * HEAD_DIM
QKV_W = N_GROUPS * GW
ROPE_DIM = HEAD_DIM // 4
ROPE_THETA = 500000.0
BAND = 128
GLA_HEADS = 4
GLA_DK = 64
GLA_DV = 128
GLA_KW = GLA_HEADS * GLA_DK
GLA_VW = GLA_HEADS * GLA_DV
GATE_RANK = 16
GLA_TAU = 16.0
GLA_CHUNK = 64
NORM_EPS = 1e-6
PAST_LEN = 8192

LANES = 128
NEG = -1e30
LOG2E = float(np.log2(np.e))
LN2 = float(np.log(2.0))

OFF_Q = 0
OFF_K = OFF_Q + QKV_W
OFF_V = OFF_K + QKV_W
OFF_AZ = OFF_V + QKV_W
OFF_GQ = OFF_AZ + GW
OFF_GK = OFF_GQ + GLA_KW
OFF_GV = OFF_GK + GLA_KW
OFF_GZ = OFF_GV + GLA_VW
OFF_LR = OFF_GZ + GLA_VW
OFF_MA = OFF_LR + GATE_RANK
OFF_MB = OFF_MA + D_MODEL
PROJ_W = OFF_MB + D_MODEL
assert all(o % 16 == 0 for o in (OFF_LR, OFF_MA, OFF_MB, PROJ_W))

VMEM_LIMIT = 56 * 1024 * 1024
MERGE_ROWS = 512

def _cparams(sem):
    return pltpu.CompilerParams(dimension_semantics=sem, vmem_limit_bytes=VMEM_LIMIT)


def _dot(a, b):
    return jnp.dot(a, b, preferred_element_type=F32)


def _dot_nt(a, b):
    return lax.dot_general(a, b, (((1,), (1,)), ((), ())), preferred_element_type=F32)


def _dot_tn(a, b):
    return lax.dot_general(a, b, (((0,), (0,)), ((), ())), preferred_element_type=F32)


def _rb(x):
    return x.astype(BF16).astype(F32)


def _split3(x, dtype=BF16):
    hi = x.astype(BF16)
    r1 = x - hi.astype(F32)
    mid = r1.astype(BF16)
    lo = (r1 - mid.astype(F32)).astype(BF16)
    return hi.astype(dtype), mid.astype(dtype), lo.astype(dtype)


def _sigmoid(x):
    return 1.0 / (1.0 + jnp.exp(-x))


def _log_sigmoid(x):
    return jnp.minimum(x, 0.0) - jnp.log1p(jnp.exp(-jnp.abs(x)))


def _rms_bf16(x, w):
    ms = jnp.mean(x * x, axis=-1, keepdims=True)
    return (x * lax.rsqrt(ms + NORM_EPS) * w).astype(BF16)


def _run_rider(rider, in_refs, out_refs):
    if rider is None:
        return
    pieces, active, dbs, nsteps = rider

    def emit():
        for thunk in pieces(in_refs, out_refs, dbs):
            thunk()

    if active == nsteps:
        emit()
    else:
        step = pl.program_id(0) * pl.num_programs(1) + pl.program_id(1)
        pl.when(step < active)(emit)


def _proj_qkv_kernel(x_ref, lnw_ref, w_ref, rc_ref, rs1_ref, rs2_ref, *refs,
                     dils, row_major_kv, q_scale, rider, n_rider_in):
    rider_in, refs = refs[:n_rider_in], refs[n_rider_in:]
    q_refs, k_refs, v_refs = refs[0:3], refs[3:6], refs[6:9]
    kv_refs = refs[9:12] if row_major_kv else ()
    rider_out = refs[9 + len(kv_refs):-1]
    stage_ref = refs[-1]
    h = _rms_bf16(x_ref[0], lnw_ref[...])
    tm = h.shape[0]
    rc, rs1, rs2 = rc_ref[...], rs1_ref[...], rs2_ref[...]

    def mm(off, width):
        return _dot_nt(h, w_ref[off:off + width, :])

    def rope(t):
        outs = []
        for j in range(t.shape[1] // LANES):
            c = t[:, j * LANES:(j + 1) * LANES]
            outs.append(c * rc + pltpu.roll(c, ROPE_DIM // 2, axis=1) * rs1
                        + pltpu.roll(c, LANES - ROPE_DIM // 2, axis=1) * rs2)
        return jnp.concatenate(outs, axis=1)

    def store_by_residue(dst_ref, val, d):
        if d == 1:
            dst_ref[0] = val.astype(BF16)
            return
        for c in range(GW // LANES):
            stage_ref[c] = val[:, c * LANES:(c + 1) * LANES]
        for r in range(d):
            for c in range(GW // LANES):
                col = r * GW + c * LANES
                dst_ref[0, :, col:col + LANES] = stage_ref[c, pl.ds(r, tm // d, stride=d), :].astype(BF16)

    for g in range(N_GROUPS):
        qr = rope(mm(OFF_Q + g * GW, GW)) * q_scale
        store_by_residue(q_refs[g], qr, dils[g])
        kr = rope(mm(OFF_K + g * GW, GW))
        store_by_residue(k_refs[g], kr, dils[g])
        vv = mm(OFF_V + g * GW, GW)
        store_by_residue(v_refs[g], vv, dils[g])
        if row_major_kv:
            kv_refs[g][0, :, 0:GW] = kr
            kv_refs[g][0, :, GW:2 * GW] = vv
    _run_rider(rider, rider_in, rider_out)


def _proj_rest_kernel(x_ref, lnw_ref, w_ref, wa2_ref, ba_ref, *refs, rider, n_rider_in):
    rider_in, refs = refs[:n_rider_in], refs[n_rider_in:]
    saz_ref, sgz_ref, sa_ref, sb_ref, gq_ref, gk_ref, gv_ref, la_ref = refs[0:8]
    rider_out = refs[8:]
    h = _rms_bf16(x_ref[0], lnw_ref[...])

    def mm(off, width):
        return _dot_nt(h, w_ref[off - OFF_AZ:off - OFF_AZ + width, :])

    az = mm(OFF_AZ, GW)
    saz_ref[0] = (az * _sigmoid(az)).astype(BF16)
    gz = mm(OFF_GZ, GLA_VW)
    sgz_ref[0] = (gz * _sigmoid(gz)).astype(BF16)
    for j in range(D_MODEL // GW):
        sa_ref[0, :, j * GW:(j + 1) * GW] = _sigmoid(mm(OFF_MA + j * GW, GW)).astype(BF16)
        sb_ref[0, :, j * GW:(j + 1) * GW] = _sigmoid(mm(OFF_MB + j * GW, GW)).astype(BF16)
    gq_ref[0] = (mm(OFF_GQ, GLA_KW) * (GLA_DK ** -0.5)).astype(BF16)
    gk_ref[0] = mm(OFF_GK, GLA_KW).astype(BF16)
    gv_ref[0] = mm(OFF_GV, GLA_VW).astype(BF16)
    glr = mm(OFF_LR, GATE_RANK)
    gate_pre = _dot(glr.astype(BF16), wa2_ref[...]) + ba_ref[...]
    la_ref[0] = _log_sigmoid(gate_pre) * (1.0 / GLA_TAU)
    _run_rider(rider, rider_in, rider_out)


def _rider_parts(make_rider, B, nt):
    if make_rider is None:
        return None, [], [], [], []
    r = make_rider(B * nt, lambda b, s: b * nt + s)
    return (r["pieces"], r["active"], r["dbs"], B * nt), r["arrays"], r["in_specs"], r["out_shape"], r["out_specs"]


def _proj_qkv(x, lnw, w_t, rope_tabs, tm, dils, row_major_kv, q_scale, make_rider=None):
    B, S, D = x.shape
    assert S % tm == 0 and all(tm % (16 * d) == 0 for d in dils)
    row = lambda w: pl.BlockSpec((1, tm, w), lambda b, s: (b, s, 0))
    tab = pl.BlockSpec((tm, LANES), lambda b, s: (s, 0))
    rider, r_arrays, r_in_specs, r_out_shape, r_out_specs = _rider_parts(make_rider, B, S // tm)
    out_shape, out_specs = [], []
    for _ in range(3):
        for d in dils:
            out_shape.append(jax.ShapeDtypeStruct((B, S // d, d * GW), BF16))
            out_specs.append(pl.BlockSpec((1, tm // d, d * GW), lambda b, s: (b, s, 0)))
    if row_major_kv:
        out_shape += [jax.ShapeDtypeStruct((B, S, 2 * GW), F32)] * N_GROUPS
        out_specs += [row(2 * GW)] * N_GROUPS
    in_specs = [row(D), pl.BlockSpec((1, D), lambda b, s: (0, 0)),
                pl.BlockSpec((OFF_AZ, D), lambda b, s: (0, 0), pipeline_mode=pl.Buffered(1)),
                tab, tab, tab]
    return pl.pallas_call(
        functools.partial(_proj_qkv_kernel, dils=tuple(dils), row_major_kv=row_major_kv, q_scale=q_scale,
                          rider=rider, n_rider_in=len(r_arrays)),
        grid=(B, S // tm), in_specs=in_specs + r_in_specs, out_specs=out_specs + r_out_specs,
        out_shape=out_shape + r_out_shape,
        scratch_shapes=[pltpu.VMEM((GW // LANES, tm, LANES), F32)],
        compiler_params=_cparams(("arbitrary", "arbitrary")), name="proj_qkv",
    )(x, lnw, w_t, *rope_tabs, *r_arrays)


def _proj_rest(x, lnw, w_t, wa2, ba, tm, make_rider=None):
    B, S, D = x.shape
    assert S % tm == 0
    row = lambda w: pl.BlockSpec((1, tm, w), lambda b, s: (b, s, 0))
    const = lambda shp: pl.BlockSpec(shp, lambda b, s: (0,) * len(shp))
    bf = lambda w: jax.ShapeDtypeStruct((B, S, w), BF16)
    rider, r_arrays, r_in_specs, r_out_shape, r_out_specs = _rider_parts(make_rider, B, S // tm)
    out_shape = [bf(GW), bf(GLA_VW), bf(D_MODEL), bf(D_MODEL), bf(GLA_KW), bf(GLA_KW), bf(GLA_VW),
                 jax.ShapeDtypeStruct((B, S, GLA_KW), F32)]
    out_specs = [row(GW), row(GLA_VW), row(D_MODEL), row(D_MODEL), row(GLA_KW), row(GLA_KW), row(GLA_VW),
                 row(GLA_KW)]
    in_specs = [row(D), const((1, D)),
                pl.BlockSpec((OFF_AZ, D), lambda b, s: (1, 0), pipeline_mode=pl.Buffered(1)),
                const((GATE_RANK, GLA_KW)), const((1, GLA_KW))]
    return pl.pallas_call(
        functools.partial(_proj_rest_kernel, rider=rider, n_rider_in=len(r_arrays)),
        grid=(B, S // tm), in_specs=in_specs + r_in_specs, out_specs=out_specs + r_out_specs,
        out_shape=out_shape + r_out_shape,
        compiler_params=_cparams(("arbitrary", "arbitrary")), name="proj_rest",
    )(x, lnw, w_t, wa2, ba, *r_arrays)


def _kvt_kernel(x_ref, lnw_ref, wk_ref, wv_ref, cos_ref, sin_ref, o0_ref, o1_ref, o2_ref, *, plan, nt):
    i = pl.program_id(1)
    h = _rms_bf16(x_ref[...], lnw_ref[...])
    rows = h.shape[0]
    half = ROPE_DIM // 2
    for g, (tiles, cols) in enumerate(plan):
        o_ref = (o0_ref, o1_ref, o2_ref)[g]

        def emit(g=g, cols=cols, o_ref=o_ref):
            hh = h[rows - cols:, :]
            y = _dot_nt(wk_ref[g * GW:(g + 1) * GW, :], hh)
            c = cos_ref[:, rows - cols:]
            s = sin_ref[:, rows - cols:]
            for hd in range(HEADS):
                b0 = hd * HEAD_DIM
                x1 = y[b0:b0 + half, :]
                x2 = y[b0 + half:b0 + ROPE_DIM, :]
                o_ref[b0:b0 + half, :] = x1 * c - x2 * s
                o_ref[b0 + half:b0 + ROPE_DIM, :] = x2 * c + x1 * s
                o_ref[b0 + ROPE_DIM:b0 + HEAD_DIM, :] = y[b0 + ROPE_DIM:b0 + HEAD_DIM, :]
            o_ref[GW:2 * GW, :] = _dot_nt(wv_ref[g * GW:(g + 1) * GW, :], hh)

        if tiles == nt:
            emit()
        else:
            pl.when(i >= nt - tiles)(emit)


def _rope_cos_sin_t(pos):
    half = ROPE_DIM // 2
    inv_freq = 1.0 / (ROPE_THETA ** (jnp.arange(half, dtype=F32) * (2.0 / ROPE_DIM)))
    ang = pos.astype(F32)[:, None] * inv_freq[None, :]
    return jnp.cos(ang).T, jnp.sin(ang).T


def _kv_tail_prompt(x, lnw, w_t, S_tails, tile):
    B, S, D = x.shape
    tile = min(tile, S)
    span = max(S_tails)
    assert span % tile == 0 and S % tile == 0
    nt = span // tile
    first = (S - span) // tile
    plan = []
    for t in S_tails:
        assert t % tile == 0 or t < tile
        plan.append((t // tile, tile) if t >= tile else (1, t))
    cos_t, sin_t = _rope_cos_sin_t(jnp.arange(S, dtype=jnp.int32))
    tabspec = pl.BlockSpec((ROPE_DIM // 2, tile), lambda b, i: (0, first + i))
    out_specs = [pl.BlockSpec((None, 2 * GW, cols), lambda b, i, t=tiles: (b, 0, jnp.maximum(i - (nt - t), 0)))
                 for tiles, cols in plan]
    return pl.pallas_call(
        functools.partial(_kvt_kernel, plan=tuple(plan), nt=nt),
        grid=(B, nt),
        in_specs=[pl.BlockSpec((None, tile, D), lambda b, i: (b, first + i, 0)),
                  pl.BlockSpec((1, D), lambda b, i: (0, 0)),
                  pl.BlockSpec((QKV_W, D), lambda b, i: (OFF_K // QKV_W, 0), pipeline_mode=pl.Buffered(1)),
                  pl.BlockSpec((QKV_W, D), lambda b, i: (OFF_V // QKV_W, 0), pipeline_mode=pl.Buffered(1)),
                  tabspec, tabspec],
        out_specs=out_specs,
        out_shape=[jax.ShapeDtypeStruct((B, 2 * GW, t), F32) for t in S_tails],
        compiler_params=_cparams(("arbitrary", "arbitrary")), name="kv_tail_prompt",
    )(x, lnw, w_t, w_t, cos_t, sin_t)


def _kv_tail_sample(x, lnw, w_t, pos):
    DB, T, D = x.shape
    cos_t, sin_t = _rope_cos_sin_t(pos)
    bcast = lambda t: jnp.broadcast_to(t.T[:, :, None], (T, ROPE_DIM // 2, DB))
    tabspec = pl.BlockSpec((None, ROPE_DIM // 2, DB), lambda b, i: (i, 0, 0))
    out_spec = pl.BlockSpec((None, 2 * GW, DB), lambda b, i: (i, 0, 0))
    return pl.pallas_call(
        functools.partial(_kvt_kernel, plan=((T, DB),) * N_GROUPS, nt=T),
        grid=(1, T),
        in_specs=[pl.BlockSpec((None, DB, D), lambda b, i: (i, 0, 0)),
                  pl.BlockSpec((1, D), lambda b, i: (0, 0)),
                  pl.BlockSpec((QKV_W, D), lambda b, i: (OFF_K // QKV_W, 0), pipeline_mode=pl.Buffered(1)),
                  pl.BlockSpec((QKV_W, D), lambda b, i: (OFF_V // QKV_W, 0), pipeline_mode=pl.Buffered(1)),
                  tabspec, tabspec],
        out_specs=[out_spec] * N_GROUPS,
        out_shape=[jax.ShapeDtypeStruct((T, 2 * GW, DB), F32)] * N_GROUPS,
        compiler_params=_cparams(("arbitrary", "arbitrary")), name="kv_tail_sample",
    )(x.transpose(1, 0, 2), lnw, w_t, w_t, bcast(cos_t), bcast(sin_t))


def _band_bias():
    q = np.arange(BAND)[:, None]
    c = np.arange(2 * BAND)[None, :]
    ok = np.where(c < BAND, c >= q, (c - BAND) <= q)
    ok_first = ok & (c >= BAND)
    return jnp.asarray(np.stack([np.where(ok, 0.0, NEG), np.where(ok_first, 0.0, NEG)]).astype(np.float32))


def _attn_kernel(q_ref, k_ref, kp_ref, v_ref, vp_ref, bias_ref, o_ref, lse_ref, kall, vall, *, qb):
    n = pl.program_id(2)
    nsub = qb // BAND
    npair = HEADS // 2
    kall[0:BAND, :] = kp_ref[0]
    kall[BAND:, :] = k_ref[0]
    vall[0:BAND, :] = vp_ref[0]
    vall[BAND:, :] = v_ref[0]
    lane = lax.broadcasted_iota(jnp.int32, (BAND, LANES), 1)
    low_half = lane < HEAD_DIM

    def sub_block(j, carry):
        r0 = pl.multiple_of(j * BAND, BAND)
        first = (n * nsub + j) == 0
        bias = bias_ref[jnp.where(first, 1, 0)]
        bias2 = jnp.concatenate([bias, bias], axis=0)
        scores = []
        for hp in range(npair):
            cs = slice(hp * LANES, (hp + 1) * LANES)
            qp = q_ref[0, pl.ds(r0, BAND), cs]
            zero = jnp.zeros_like(qp)
            qm = jnp.concatenate([jnp.where(low_half, qp, zero), jnp.where(low_half, zero, qp)], axis=0)
            scores.append(_dot_nt(qm, kall[pl.ds(r0, 2 * BAND), cs]) + bias2)
        probs, stats = [], []
        for s in scores:
            m = jnp.max(s, axis=1, keepdims=True)
            p = jnp.exp2(s - m)
            stats.append((m, jnp.sum(p, axis=1, keepdims=True)))
            probs.append(p.astype(BF16))
        m_tile = jnp.zeros((BAND, LANES), F32)
        den_tile = jnp.ones((BAND, LANES), F32)
        for hp in range(npair):
            cs = slice(hp * LANES, (hp + 1) * LANES)
            m, den = stats[hp]
            o = _dot(probs[hp], vall[pl.ds(r0, 2 * BAND), cs])
            o = jnp.where(low_half, o[0:BAND], o[BAND:]) / jnp.where(low_half, den[0:BAND], den[BAND:])
            o_ref[0, pl.ds(r0, BAND), cs] = o.astype(BF16)
            for a in range(2):
                rows = slice(a * BAND, (a + 1) * BAND)
                m_tile = jnp.where(lane == 2 * hp + a, m[rows], m_tile)
                den_tile = jnp.where(lane == 2 * hp + a, den[rows], den_tile)
        lse_ref[0, pl.ds(r0, BAND), :] = (m_tile + jnp.log2(den_tile)) * LN2
        return carry

    lax.fori_loop(0, nsub, sub_block, 0, unroll=True)


def _attn_group(q, k, v, g, qb):
    B, L, _ = q.shape
    win, dil = GROUPS[g]
    assert win // dil == BAND
    qb = min(qb, L)
    assert L % qb == 0 and qb % BAND == 0
    nsub = qb // BAND
    cur = pl.BlockSpec((1, qb, GW), lambda b, r, n: (b, n, r))
    prev = pl.BlockSpec((1, BAND, GW), lambda b, r, n: (b, jnp.maximum(n * nsub - 1, 0), r))
    return pl.pallas_call(
        functools.partial(_attn_kernel, qb=qb),
        grid=(B, dil, L // qb),
        in_specs=[cur, cur, prev, cur, prev, pl.BlockSpec((2, BAND, 2 * BAND), lambda b, r, n: (0, 0, 0))],
        out_specs=[pl.BlockSpec((1, qb, GW), lambda b, r, n: (b, n, r)),
                   pl.BlockSpec((1, qb, LANES), lambda b, r, n: (b, n, r))],
        out_shape=[jax.ShapeDtypeStruct((B, L, dil * GW), BF16),
                   jax.ShapeDtypeStruct((B, L, dil * LANES), F32)],
        scratch_shapes=[pltpu.VMEM((qb + BAND, GW), BF16), pltpu.VMEM((qb + BAND, GW), BF16)],
        compiler_params=_cparams(("arbitrary", "arbitrary", "arbitrary")), name=f"attn_g{g}",
    )(q, k, k, v, v, _band_bias())


def _gla_kernel(gq_ref, gk_ref, gv_ref, la_ref, tri_ref, nw_ref, o_ref, sfin_ref, st_ref, *, nchunk):
    c = pl.program_id(1)

    @pl.when(c == 0)
    def _():
        st_ref[...] = jnp.zeros_like(st_ref)

    C = GLA_CHUNK
    npair = GLA_HEADS // 2
    tri3 = tri_ref[...]
    normw = nw_ref[...]
    hi, mid, lo = _split3(la_ref[0])
    b_chunks = []
    for ci in range(nchunk):
        rows = slice(ci * C, (ci + 1) * C)
        b_chunks.append(_dot(tri3, jnp.concatenate([hi[rows], mid[rows], lo[rows]], axis=0)))
    b = jnp.concatenate(b_chunks, axis=0)
    b_last = jnp.concatenate([jnp.broadcast_to(bc[C - 1:C, :], bc.shape) for bc in b_chunks], axis=0)
    gk = gk_ref[0].astype(F32)
    qe = gq_ref[0].astype(F32) * jnp.exp(b)
    ke = (gk * jnp.exp(-b)).astype(BF16)
    kd = (gk * jnp.exp(b_last - b)).astype(BF16)
    gv = gv_ref[0]

    low_half = lax.broadcasted_iota(jnp.int32, (C, LANES), 1) < GLA_DK
    ri = lax.broadcasted_iota(jnp.int32, (2 * C, 2 * C), 0)
    ki = lax.broadcasted_iota(jnp.int32, (2 * C, 2 * C), 1)
    pair_causal = jnp.logical_and((ri < C) == (ki < C),
                                  jnp.bitwise_and(ki, C - 1) <= jnp.bitwise_and(ri, C - 1))

    lhs, vrows, upd, dec = {}, {}, {}, {}
    for ci in range(nchunk):
        rows = slice(ci * C, (ci + 1) * C)
        for p in range(npair):
            cs = slice(p * LANES, (p + 1) * LANES)
            qe_p = qe[rows, cs]
            qm = jnp.concatenate([jnp.where(low_half, qe_p, 0.0), jnp.where(low_half, 0.0, qe_p)],
                                 axis=0).astype(BF16)
            ke_p = ke[rows, cs]
            sc = _dot_nt(qm, jnp.concatenate([ke_p, ke_p], axis=0))
            att = jnp.where(pair_causal, sc, 0.0).astype(BF16)
            lhs[ci, p] = jnp.concatenate([att, qm], axis=1)
            v_ab = gv[rows, 2 * p * GLA_DV:2 * (p + 1) * GLA_DV]
            vrows[ci, p] = jnp.concatenate([v_ab[:, 0:GLA_DV], v_ab[:, GLA_DV:]], axis=0)
            u = _dot_tn(kd[rows, cs], v_ab)
            upd[ci, p] = jnp.concatenate([u[0:GLA_DK, 0:GLA_DV], u[GLA_DK:, GLA_DV:]], axis=0)
            bl = jnp.broadcast_to(b_chunks[ci][C - 1:C, cs], (LANES, LANES))
            dec[ci, p] = jnp.exp(bl.T)

    outs = {}
    states = [st_ref[p] for p in range(npair)]
    for ci in range(nchunk):
        for p in range(npair):
            rhs = jnp.concatenate([vrows[ci, p], states[p].astype(BF16)], axis=0)
            outs[ci, p] = _dot(lhs[ci, p], rhs)
            states[p] = dec[ci, p] * states[p] + upd[ci, p]
    for p in range(npair):
        st_ref[p] = states[p]

    for ci in range(nchunk):
        for p in range(npair):
            o = outs[ci, p]
            o = (o * lax.rsqrt(jnp.mean(o * o, axis=-1, keepdims=True) + NORM_EPS) * normw).astype(BF16)
            for a in range(2):
                hh = 2 * p + a
                o_ref[0, ci * C:(ci + 1) * C, hh * GLA_DV:(hh + 1) * GLA_DV] = o[a * C:(a + 1) * C, :]

    @pl.when(c == pl.num_programs(1) - 1)
    def _():
        for p in range(npair):
            sfin_ref[0, p * LANES:(p + 1) * LANES, :] = states[p]


def _tri3(C, dtype=BF16):
    tri = np.tril(np.ones((C, C), np.float32))
    return jnp.asarray(np.concatenate([tri, tri, tri], axis=1), dtype)


def _gla_prompt(gq, gk, gv, la, normw, ct):
    B, S, _ = gq.shape
    ct = min(ct, S)
    assert S % ct == 0 and ct % GLA_CHUNK == 0
    row = lambda w: pl.BlockSpec((1, ct, w), lambda b, c: (b, c, 0))
    const = lambda shp: pl.BlockSpec(shp, lambda b, c: (0,) * len(shp))
    o, sfin = pl.pallas_call(
        functools.partial(_gla_kernel, nchunk=ct // GLA_CHUNK),
        grid=(B, S // ct),
        in_specs=[row(GLA_KW), row(GLA_KW), row(GLA_VW), row(GLA_KW),
                  const((GLA_CHUNK, 3 * GLA_CHUNK)), const((1, GLA_DV))],
        out_specs=[row(GLA_VW), pl.BlockSpec((1, GLA_KW, GLA_DV), lambda b, c: (b, 0, 0))],
        out_shape=[jax.ShapeDtypeStruct((B, S, GLA_VW), BF16),
                   jax.ShapeDtypeStruct((B, GLA_KW, GLA_DV), F32)],
        scratch_shapes=[pltpu.VMEM((GLA_HEADS // 2, GLA_DV, LANES), F32)],
        compiler_params=_cparams(("arbitrary", "arbitrary")), name="gla_prompt",
    )(gq, gk, gv, la, _tri3(GLA_CHUNK), normw)
    return o, sfin.reshape(B, GLA_HEADS, GLA_DK, GLA_DV)


def _merge_kernel(*refs, dils):
    n_groups = len(dils)
    x_ref = refs[0]
    o_refs = refs[1:1 + n_groups]
    lse_refs = refs[1 + n_groups:1 + 2 * n_groups] if n_groups > 1 else ()
    rest = refs[1 + n_groups + len(lse_refs):]
    (saz_ref, og_ref, sgz_ref, sa_ref, sb_ref, wa_ref, wg_ref, wo_ref, fnw_ref, ex_ref, y_ref) = rest[:11]
    scratch = rest[11:]
    tm = x_ref.shape[0]

    def by_position(ref, width, d, stage_ref):
        if d == 1:
            return lambda rows: ref[rows, :].astype(F32)
        planes = width // LANES
        for r in range(d):
            blk = ref[:, r * width:(r + 1) * width].astype(F32)
            for c in range(planes):
                stage_ref[c, pl.ds(r, tm // d, stride=d), :] = blk[:, c * LANES:(c + 1) * LANES]
        return lambda rows: jnp.concatenate([stage_ref[c, rows, :] for c in range(planes)], axis=1)

    if n_groups > 1:
        lse_rows = [by_position(lse_refs[g], LANES, dils[g], scratch[2 * g + 1]) for g in range(n_groups)]
        o_rows = [by_position(o_refs[g], GW, dils[g], scratch[2 * g]) for g in range(n_groups)]

    rc = min(tm, MERGE_ROWS)
    chunks = [slice(r0, r0 + rc) for r0 in range(0, tm, rc)]
    yb = [_dot((og_ref[rows, :].astype(F32) * sgz_ref[rows, :].astype(F32)).astype(BF16), wg_ref[...])
          for rows in chunks]
    if n_groups > 1:
        comb = []
        for rows in chunks:
            ls = [f(rows) for f in lse_rows]
            mx = functools.reduce(jnp.maximum, ls)
            es = [jnp.exp(l - mx) for l in ls]
            inv = 1.0 / functools.reduce(lambda a, b: a + b, es)
            acc = None
            for g in range(n_groups):
                w = es[g] * inv
                w_hi = w.astype(BF16)
                w_lo = (w - w_hi.astype(F32)).astype(BF16)
                wx = _dot(jnp.concatenate([w_hi, w_lo], axis=1), ex_ref[...])
                term = wx * o_rows[g](rows)
                acc = term if acc is None else acc + term
            comb.append(acc)
    else:
        comb = [o_refs[0][rows, :].astype(F32) for rows in chunks]
    ya = [_dot((c * saz_ref[rows, :].astype(F32)).astype(BF16), wa_ref[...]) for c, rows in zip(comb, chunks)]
    outs = []
    for a, b, rows in zip(ya, yb, chunks):
        mixed = (sa_ref[rows, :].astype(F32) * a + sb_ref[rows, :].astype(F32) * b).astype(BF16)
        outs.append(x_ref[rows, :] + _dot(mixed, wo_ref[...]))
    for out, rows in zip(outs, chunks):
        y_ref[rows, :] = (out * lax.rsqrt(jnp.mean(out * out, axis=-1, keepdims=True) + NORM_EPS)
                          * fnw_ref[...])


def _expand_matrix():
    e = np.zeros((LANES, GW), np.float32)
    for hh in range(HEADS):
        e[hh, hh * HEAD_DIM:(hh + 1) * HEAD_DIM] = 1.0
    return jnp.asarray(np.concatenate([e, e], axis=0), BF16)


def _merge(x, o_list, lse_list, dils, saz, og, sgz, sa, sb, wa, wg, wo, fnw, tm):
    R, D = x.shape
    tm = min(tm, R)
    assert R % tm == 0
    n_groups = len(o_list)
    row = lambda w: pl.BlockSpec((tm, w), lambda i: (i, 0))
    const = lambda shp: pl.BlockSpec(shp, lambda i: (0,) * len(shp))
    in_specs = [row(D)]
    in_specs += [pl.BlockSpec((tm // d, d * GW), lambda i: (i, 0)) for d in dils]
    in_specs += [pl.BlockSpec((tm // d, d * LANES), lambda i: (i, 0)) for d in dils[:len(lse_list)]]
    in_specs += [row(GW), row(GLA_VW), row(GLA_VW), row(D), row(D),
                 const((GW, D)), const((GLA_VW, D)), const((D, D)), const((1, D)), const((2 * LANES, GW))]
    scratch = []
    if n_groups > 1:
        for _ in dils:
            scratch += [pltpu.VMEM((GW // LANES, tm, LANES), F32), pltpu.VMEM((1, tm, LANES), F32)]
    return pl.pallas_call(
        functools.partial(_merge_kernel, dils=tuple(dils)),
        grid=(R // tm,), in_specs=in_specs, out_specs=row(D),
        out_shape=jax.ShapeDtypeStruct((R, D), F32), scratch_shapes=scratch,
        compiler_params=_cparams(("arbitrary",)), name=f"merge_g{n_groups}",
    )(x, *o_list, *lse_list, saz, og, sgz, sa, sb, wa, wg, wo, fnw, _expand_matrix())


def _sample_masks(T, w_lens):
    t_of_row = np.arange(HEADS * T) % T

    def bias(idx, g):
        win, dil = GROUPS[g]
        dd = w_lens[g] + t_of_row[:, None] - idx[None, :]
        ok = (dd >= 0) & (dd % dil == 0) & (dd // dil <= win // dil)
        return jnp.asarray(np.where(ok, 0.0, NEG).astype(np.float32))

    cache = [bias(np.arange(w_lens[g]), g) for g in range(N_GROUPS)]
    new = [bias(w_lens[g] + np.arange(T), g) for g in range(N_GROUPS)]
    return cache, new


def _decode_scores_pieces(in_refs, out_refs, dbs):
    G = N_GROUPS
    sq_refs, sc_refs, sb_refs, bd_ref = in_refs[0:G], in_refs[G:2 * G], in_refs[2 * G:3 * G], in_refs[3 * G]

    def piece(i, g):
        qbd = (jnp.tile(sq_refs[g][i].astype(F32), (HEADS, 1)) * bd_ref[...]).astype(BF16)
        out_refs[g][i] = _dot(qbd, sc_refs[g][i].astype(BF16)) + sb_refs[g][...]

    return [functools.partial(piece, i, g) for i in range(dbs) for g in range(G)]


def _decode_mix_pieces(in_refs, out_refs, dbs):
    G = N_GROUPS
    q_refs, new_refs, s_refs = in_refs[0:G], in_refs[G:2 * G], in_refs[2 * G:3 * G]
    c_refs, bn_refs, bd_ref = in_refs[3 * G:4 * G], in_refs[4 * G:5 * G], in_refs[5 * G]
    o_ref = out_refs[0]
    T = q_refs[0].shape[1]
    p_old, acc = {}, {}

    def prologue(i):
        bd = bd_ref[...]
        s_new, v_new = [], []
        for g in range(G):
            qbd = jnp.tile(q_refs[g][i].astype(F32), (HEADS, 1)) * bd
            s_new.append(_dot_nt(qbd, _rb(new_refs[g][i, :, 0:GW])) + bn_refs[g][...])
            v_new.append(_rb(new_refs[g][i, :, GW:2 * GW]))
        s_old = [s_refs[g][i] for g in range(G)]
        m = functools.reduce(jnp.maximum, [jnp.max(s, axis=1, keepdims=True) for s in s_new + s_old])
        e_new = [jnp.exp(s - m) for s in s_new]
        e_old = [jnp.exp(s - m) for s in s_old]
        den = functools.reduce(lambda a, b: a + b, [jnp.sum(p, axis=1, keepdims=True) for p in e_new + e_old])
        inv = 1.0 / den
        for g in range(G):
            p_old[i, g] = e_old[g] * inv
            r = _dot(_rb(e_new[g] * inv), v_new[g])
            acc[i] = r if g == 0 else acc[i] + r

    def piece(i, g):
        acc[i] = acc[i] + _dot_nt(p_old[i, g], c_refs[g][i])

    def epilogue(i):
        a = acc[i] * bd_ref[...]
        out = a[0:T, :]
        for hh in range(1, HEADS):
            out = out + a[hh * T:(hh + 1) * T, :]
        o_ref[i] = out

    thunks = []
    for i in range(dbs):
        thunks += [functools.partial(prologue, i)] + [functools.partial(piece, i, g) for g in range(G)]
        thunks += [functools.partial(epilogue, i)]
    return thunks


def _head_diag(T):
    bd = np.zeros((HEADS * T, GW), np.float32)
    for hh in range(HEADS):
        bd[hh * T:(hh + 1) * T, hh * HEAD_DIM:(hh + 1) * HEAD_DIM] = 1.0
    return jnp.asarray(bd)


def _rider_common(DB, nsteps, step_of):
    dbs = -(-DB // nsteps)
    assert DB % dbs == 0
    active = DB // dbs
    blk = lambda b, s: jnp.minimum(step_of(b, s), active - 1)
    per = lambda shp, part=0: pl.BlockSpec((dbs,) + shp, lambda b, s: (blk(b, s), part) + (0,) * (len(shp) - 1))
    const = lambda a: pl.BlockSpec(a.shape, lambda b, s: (0,) * a.ndim)
    return dbs, active, per, const


def _scores_rider(q_list, caches_t):
    DB, T, _ = q_list[0].shape
    w_lens = [c.shape[2] for c in caches_t]
    bias_c, _ = _sample_masks(T, w_lens)

    def make(nsteps, step_of):
        dbs, active, per, const = _rider_common(DB, nsteps, step_of)
        return dict(
            pieces=_decode_scores_pieces, dbs=dbs, active=active,
            arrays=list(q_list) + list(caches_t) + bias_c + [_head_diag(T)],
            in_specs=[per((T, GW)) for _ in q_list] + [per((GW, w), 0) for w in w_lens]
                     + [const(a) for a in bias_c] + [pl.BlockSpec((HEADS * T, GW), lambda b, s: (0, 0))],
            out_shape=[jax.ShapeDtypeStruct((DB, HEADS * T, w), F32) for w in w_lens],
            out_specs=[per((HEADS * T, w)) for w in w_lens])
    return make


def _mix_rider(q_list, kv_new, scores, caches_t):
    DB, T, _ = q_list[0].shape
    w_lens = [c.shape[2] for c in caches_t]
    _, bias_n = _sample_masks(T, w_lens)

    def make(nsteps, step_of):
        dbs, active, per, const = _rider_common(DB, nsteps, step_of)
        return dict(
            pieces=_decode_mix_pieces, dbs=dbs, active=active,
            arrays=list(q_list) + list(kv_new) + list(scores) + list(caches_t) + bias_n + [_head_diag(T)],
            in_specs=[per((T, GW)) for _ in q_list] + [per((T, 2 * GW)) for _ in kv_new]
                     + [per((HEADS * T, w)) for w in w_lens] + [per((GW, w), 1) for w in w_lens]
                     + [const(a) for a in bias_n] + [pl.BlockSpec((HEADS * T, GW), lambda b, s: (0, 0))],
            out_shape=[jax.ShapeDtypeStruct((DB, T, GW), F32)],
            out_specs=[per((T, GW))])
    return make


def _sgla_kernel(gq_ref, gk_ref, gv_ref, la_ref, st_ref, tri_ref, nw_ref, o_ref, snew_ref, *, dbt):
    tri3 = tri_ref[...]
    normw = nw_ref[...]
    T = gq_ref.shape[1]
    ti = lax.broadcasted_iota(jnp.int32, (T, T), 0)
    si = lax.broadcasted_iota(jnp.int32, (T, T), 1)
    causal = si <= ti
    low_half = lax.broadcasted_iota(jnp.int32, (T, LANES), 1) < GLA_DK
    ones = jnp.ones((3 * T, LANES), F32)
    row_low = lax.broadcasted_iota(jnp.int32, (LANES, GLA_DV), 0) < GLA_DK

    entries = range(dbt)
    pairs = range(GLA_HEADS // 2)
    cs = [slice(p * LANES, (p + 1) * LANES) for p in pairs]
    la3 = {i: jnp.concatenate(_split3(la_ref[i], F32), axis=0) for i in entries}
    b = {i: _dot(tri3, la3[i]) for i in entries}
    dec = {(i, p): jnp.exp(_dot_tn(la3[i][:, cs[p]], ones)) for i in entries for p in pairs}
    qe, ke, kd, gv = {}, {}, {}, {}
    for i in entries:
        b_last = b[i][T - 1:T, :]
        gk = gk_ref[i].astype(F32)
        qe[i] = gq_ref[i].astype(F32) * jnp.exp(b[i])
        ke[i] = _rb(gk * jnp.exp(-b[i]))
        kd[i] = _rb(gk * jnp.exp(b_last - b[i]))
        gv[i] = gv_ref[i].astype(F32)
    heads = [(i, p, a) for i in entries for p in pairs for a in range(2)]
    qm = {(i, p, a): _rb(jnp.where(low_half if a == 0 else jnp.logical_not(low_half), qe[i][:, cs[p]], 0.0))
          for (i, p, a) in heads}
    att = {(i, p, a): _rb(jnp.where(causal, _dot_nt(qm[i, p, a], ke[i][:, cs[p]]), 0.0)) for (i, p, a) in heads}
    v_h = {(i, p, a): gv[i][:, (2 * p + a) * GLA_DV:(2 * p + a + 1) * GLA_DV] for (i, p, a) in heads}
    o = {(i, p, a): _dot(att[i, p, a], v_h[i, p, a]) + _dot(qm[i, p, a], _rb(st_ref[i, cs[p], :]))
         for (i, p, a) in heads}
    u = {(i, p, a): _dot_tn(kd[i][:, cs[p]], v_h[i, p, a]) for (i, p, a) in heads}
    for i in entries:
        for p in pairs:
            upd = jnp.where(row_low, u[i, p, 0], u[i, p, 1])
            snew_ref[i, cs[p], :] = st_ref[i, cs[p], :] * dec[i, p] + upd
        outs = [o[i, p, a] * lax.rsqrt(jnp.mean(o[i, p, a] * o[i, p, a], axis=-1, keepdims=True) + NORM_EPS)
                * normw for p in pairs for a in range(2)]
        o_ref[i] = jnp.concatenate(outs, axis=1)


def _sample_gla(gq, gk, gv, la, state, normw, dbt):
    DB, T, _ = gq.shape
    dbt = min(dbt, DB)
    assert DB % dbt == 0
    st = state.reshape(DB, GLA_KW, GLA_DV)
    blk = lambda shp: pl.BlockSpec((dbt,) + shp, lambda i: (i,) + (0,) * len(shp))
    const = lambda shp: pl.BlockSpec(shp, lambda i: (0,) * len(shp))
    o, snew = pl.pallas_call(
        functools.partial(_sgla_kernel, dbt=dbt),
        grid=(DB // dbt,),
        in_specs=[blk((T, GLA_KW)), blk((T, GLA_KW)), blk((T, GLA_VW)), blk((T, GLA_KW)),
                  blk((GLA_KW, GLA_DV)), const((T, 3 * T)), const((1, GLA_DV))],
        out_specs=[blk((T, GLA_VW)), blk((GLA_KW, GLA_DV))],
        out_shape=[jax.ShapeDtypeStruct((DB, T, GLA_VW), F32),
                   jax.ShapeDtypeStruct((DB, GLA_KW, GLA_DV), F32)],
        compiler_params=_cparams(("arbitrary",)), name="gla_sample",
    )(gq, gk, gv, la, st, _tri3(T, F32), normw)
    return o, snew.reshape(DB, GLA_HEADS, GLA_DK, GLA_DV)


def _rope_tables(pos):
    half = ROPE_DIM // 2
    inv_freq = 1.0 / (ROPE_THETA ** (jnp.arange(half, dtype=F32) * (2.0 / ROPE_DIM)))
    ang = pos.astype(F32)[:, None] * inv_freq[None, :]
    cos, sin = jnp.cos(ang), jnp.sin(ang)
    n = pos.shape[0]
    pad = jnp.zeros((n, HEAD_DIM - ROPE_DIM), F32)
    zero = jnp.zeros((n, half), F32)
    c = jnp.concatenate([cos, cos, pad + 1.0], axis=1)
    s1 = jnp.concatenate([zero, sin, pad], axis=1)
    s2 = jnp.concatenate([-sin, zero, pad], axis=1)
    rep = LANES // HEAD_DIM
    return tuple(jnp.tile(t, (1, rep)) for t in (c, s1, s2))


def kernel(x_prompt, x_sample, cache_kv_w128, cache_kv_w512, cache_kv_w2048, state_gla,
           ln_w, w_in, w_gla_a2, b_gla_a, gla_norm_w, w_attn_out, w_gla_out, w_out, final_norm_w):
    B, S, D = x_prompt.shape
    DB, T, _ = x_sample.shape
    assert ln_w.shape[0] == 1, "single-layer step"
    caches = (cache_kv_w128, cache_kv_w512, cache_kv_w2048)
    dils = [d for _, d in GROUPS]

    lnw = ln_w[0].reshape(1, D)
    w_t = w_in[0].T.astype(BF16)
    assert PROJ_W <= 2 * OFF_AZ
    wa2 = w_gla_a2[0].astype(BF16)
    ba = b_gla_a[0].reshape(1, GLA_KW)
    gnw = gla_norm_w[0].reshape(1, GLA_DV)
    wa = w_attn_out[0].astype(BF16)
    wg = w_gla_out[0].astype(BF16)
    wo = w_out[0].astype(BF16)
    fnw = final_norm_w.reshape(1, D)


    R = DB * T
    xs = x_sample.reshape(1, R, D)
    pos_s = PAST_LEN + jnp.arange(T, dtype=jnp.int32)
    tabs_s = tuple(jnp.tile(t, (DB, 1)) for t in _rope_tables(pos_s))
    qkv_s = _proj_qkv(xs, lnw, w_t, tabs_s, min(512, R), (1,) * N_GROUPS, row_major_kv=True,
                      q_scale=HEAD_DIM ** -0.5)
    rest_s = _proj_rest(xs, lnw, w_t, wa2, ba, min(512, R))
    per_db = lambda t: t.reshape(DB, T, t.shape[-1])
    q_s = [per_db(q) for q in qkv_s[0:3]]
    kv_new = [per_db(t) for t in qkv_s[9:12]]
    caches_t = [c[0].transpose(0, 2, 3, 4, 1).reshape(DB, 2 * GW, c.shape[2]) for c in caches]

    tabs_p = _rope_tables(jnp.arange(S, dtype=jnp.int32))
    qkv = _proj_qkv(x_prompt, lnw, w_t, tabs_p, min(512, S), dils, row_major_kv=False,
                    q_scale=HEAD_DIM ** -0.5 * LOG2E,
                    make_rider=_scores_rider(q_s, caches_t))
    scores = qkv[9:12]
    rest = _proj_rest(x_prompt, lnw, w_t, wa2, ba, min(512, S),
                      make_rider=_mix_rider(q_s, kv_new, scores, caches_t))
    saz, sgz, sa, sb, gq, gk, gv, la, o_s = rest
    o_list, lse_list = [], []
    for g in range(N_GROUPS):
        o, lse = _attn_group(qkv[g], qkv[3 + g], qkv[6 + g], g, qb=1024)
        o_list.append(o.reshape(-1, o.shape[-1]))
        lse_list.append(lse.reshape(-1, lse.shape[-1]))
    og, gla_p = _gla_prompt(gq, gk, gv, la, gnw, ct=1024)
    flat = lambda t: t.reshape(B * S, t.shape[-1])
    y_prompt = _merge(flat(x_prompt), o_list, lse_list, dils, flat(saz), flat(og), flat(sgz), flat(sa),
                      flat(sb), wa, wg, wo, fnw, tm=512).reshape(B, S, D)
    kvt = _kv_tail_prompt(x_prompt, lnw, w_t, [min(w, S) for w, _ in GROUPS], tile=1024)
    kv_p = [t.reshape(B, 2, HEADS, HEAD_DIM, t.shape[-1]).transpose(0, 4, 1, 2, 3)[None] for t in kvt]

    saz, sgz, sa, sb, gq, gk, gv, la = rest_s
    og_s, gla_s = _sample_gla(per_db(gq), per_db(gk), per_db(gv), per_db(la), state_gla[0], gnw, dbt=32)
    flat_s = lambda t: t.reshape(R, t.shape[-1])
    y_sample = _merge(x_sample.reshape(R, D), [o_s.reshape(R, GW)], [], (1,), flat_s(saz), flat_s(og_s),
                      flat_s(sgz), flat_s(sa), flat_s(sb), wa, wg, wo, fnw, tm=512).reshape(DB, T, D)
    kvt_s = _kv_tail_sample(x_sample, lnw, w_t, pos_s)
    kv_s = [t.reshape(T, 2, HEADS, HEAD_DIM, DB).transpose(4, 0, 1, 2, 3)[None] for t in kvt_s]

    return (y_prompt, y_sample, kv_p[0], kv_p[1], kv_p[2], gla_p[None],
            kv_s[0], kv_s[1], kv_s[2], gla_s[None])
```

```python
import functools

import numpy as np
import jax
import jax.numpy as jnp
from jax import lax
from jax.experimental import pallas as pl
from jax.experimental.pallas import tpu as pltpu

F32 = jnp.float32
BF16 = jnp.bfloat16

D_MODEL = 1024
HEAD_DIM = 64
HEADS = 8
GROUPS = ((128, 1), (512, 4), (2048, 16))
N_GROUPS = len(GROUPS)
GW = HEADS * HEAD_DIM
QKV_W = N_GROUPS * GW
ROPE_DIM = HEAD_DIM // 4
ROPE_THETA = 500000.0
BAND = 128
GLA_HEADS = 4
GLA_DK = 64
GLA_DV = 128
GLA_KW = GLA_HEADS * GLA_DK
GLA_VW = GLA_HEADS * GLA_DV
GATE_RANK = 16
GLA_TAU = 16.0
GLA_CHUNK = 64
NORM_EPS = 1e-6
PAST_LEN = 8192

LANES = 128
NEG = -1e30
LOG2E = float(np.log2(np.e))
LN2 = float(np.log(2.0))

OFF_Q = 0
OFF_K = OFF_Q + QKV_W
OFF_V = OFF_K + QKV_W
OFF_AZ = OFF_V + QKV_W
OFF_GQ = OFF_AZ + GW
OFF_GK = OFF_GQ + GLA_KW
OFF_GV = OFF_GK + GLA_KW
OFF_GZ = OFF_GV + GLA_VW
OFF_LR = OFF_GZ + GLA_VW
OFF_MA = OFF_LR + GATE_RANK
OFF_MB = OFF_MA + D_MODEL
PROJ_W = OFF_MB + D_MODEL
assert all(o % 16 == 0 for o in (OFF_LR, OFF_MA, OFF_MB, PROJ_W))

VMEM_LIMIT = 56 * 1024 * 1024

PROJ_ROWS = 512
ATTN_ROWS = 2048
GLA_TOKENS = 2048
KV_TAIL_COLS = 1024
MERGE_TILE = 512
MERGE_ROWS = 512
SAMPLE_GLA_ENTRIES = 32

def _cparams(sem):
    return pltpu.CompilerParams(dimension_semantics=sem, vmem_limit_bytes=VMEM_LIMIT)


def _dot(a, b):
    return jnp.dot(a, b, preferred_element_type=F32)


def _dot_nt(a, b):
    return lax.dot_general(a, b, (((1,), (1,)), ((), ())), preferred_element_type=F32)


def _dot_tn(a, b):
    return lax.dot_general(a, b, (((0,), (0,)), ((), ())), preferred_element_type=F32)


def _rb(x):
    return x.astype(BF16).astype(F32)


def _split3(x, dtype=BF16):
    hi = x.astype(BF16)
    r1 = x - hi.astype(F32)
    mid = r1.astype(BF16)
    lo = (r1 - mid.astype(F32)).astype(BF16)
    return hi.astype(dtype), mid.astype(dtype), lo.astype(dtype)


def _sigmoid(x):
    return 1.0 / (1.0 + jnp.exp(-x))


def _log_sigmoid(x):
    return jnp.minimum(x, 0.0) - jnp.log1p(jnp.exp(-jnp.abs(x)))


def _rms_bf16(x, w):
    ms = jnp.mean(x * x, axis=-1, keepdims=True)
    return (x * lax.rsqrt(ms + NORM_EPS) * w).astype(BF16)


def _run_rider(rider, in_refs, out_refs):
    if rider is None:
        return
    pieces, active, dbs, nsteps = rider

    def emit():
        for thunk in pieces(in_refs, out_refs, dbs):
            thunk()

    if active == nsteps:
        emit()
    else:
        step = pl.program_id(0) * pl.num_programs(1) + pl.program_id(1)
        pl.when(step < active)(emit)


def _proj_qkv_kernel(x_ref, lnw_ref, w_ref, rc_ref, rs1_ref, rs2_ref, *refs,
                     dils, row_major_kv, q_scale, rider, n_rider_in):
    rider_in, refs = refs[:n_rider_in], refs[n_rider_in:]
    q_refs, k_refs, v_refs = refs[0:3], refs[3:6], refs[6:9]
    kv_refs = refs[9:12] if row_major_kv else ()
    rider_out = refs[9 + len(kv_refs):-1]
    stage_ref = refs[-1]
    h = _rms_bf16(x_ref[0], lnw_ref[...])
    tm = h.shape[0]
    rc, rs1, rs2 = rc_ref[...], rs1_ref[...], rs2_ref[...]

    def mm(off, width):
        return _dot_nt(h, w_ref[off:off + width, :])

    def rope(t):
        outs = []
        for j in range(t.shape[1] // LANES):
            c = t[:, j * LANES:(j + 1) * LANES]
            outs.append(c * rc + pltpu.roll(c, ROPE_DIM // 2, axis=1) * rs1
                        + pltpu.roll(c, LANES - ROPE_DIM // 2, axis=1) * rs2)
        return jnp.concatenate(outs, axis=1)

    def store_by_residue(dst_ref, val, d):
        if d == 1:
            dst_ref[0] = val.astype(BF16)
            return
        for c in range(GW // LANES):
            stage_ref[c] = val[:, c * LANES:(c + 1) * LANES]
        for r in range(d):
            for c in range(GW // LANES):
                col = r * GW + c * LANES
                dst_ref[0, :, col:col + LANES] = stage_ref[c, pl.ds(r, tm // d, stride=d), :].astype(BF16)

    for g in range(N_GROUPS):
        qr = rope(mm(OFF_Q + g * GW, GW)) * q_scale
        store_by_residue(q_refs[g], qr, dils[g])
        kr = rope(mm(OFF_K + g * GW, GW))
        store_by_residue(k_refs[g], kr, dils[g])
        vv = mm(OFF_V + g * GW, GW)
        store_by_residue(v_refs[g], vv, dils[g])
        if row_major_kv:
            kv_refs[g][0, :, 0:GW] = kr
            kv_refs[g][0, :, GW:2 * GW] = vv
    _run_rider(rider, rider_in, rider_out)


def _proj_rest_kernel(x_ref, lnw_ref, w_ref, wa2_ref, ba_ref, *refs, rider, n_rider_in):
    rider_in, refs = refs[:n_rider_in], refs[n_rider_in:]
    saz_ref, sgz_ref, sa_ref, sb_ref, gq_ref, gk_ref, gv_ref, la_ref = refs[0:8]
    rider_out = refs[8:]
    h = _rms_bf16(x_ref[0], lnw_ref[...])

    def mm(off, width):
        return _dot_nt(h, w_ref[off - OFF_AZ:off - OFF_AZ + width, :])

    az = mm(OFF_AZ, GW)
    saz_ref[0] = (az * _sigmoid(az)).astype(BF16)
    gz = mm(OFF_GZ, GLA_VW)
    sgz_ref[0] = (gz * _sigmoid(gz)).astype(BF16)
    for j in range(D_MODEL // GW):
        sa_ref[0, :, j * GW:(j + 1) * GW] = _sigmoid(mm(OFF_MA + j * GW, GW)).astype(BF16)
        sb_ref[0, :, j * GW:(j + 1) * GW] = _sigmoid(mm(OFF_MB + j * GW, GW)).astype(BF16)
    gq_ref[0] = (mm(OFF_GQ, GLA_KW) * (GLA_DK ** -0.5)).astype(BF16)
    gk_ref[0] = mm(OFF_GK, GLA_KW).astype(BF16)
    gv_ref[0] = mm(OFF_GV, GLA_VW).astype(BF16)
    glr = mm(OFF_LR, GATE_RANK)
    gate_pre = _dot(glr.astype(BF16), wa2_ref[...]) + ba_ref[...]
    la_ref[0] = _log_sigmoid(gate_pre) * (1.0 / GLA_TAU)
    _run_rider(rider, rider_in, rider_out)


def _rider_parts(make_rider, B, nt):
    if make_rider is None:
        return None, [], [], [], []
    r = make_rider(B * nt, lambda b, s: b * nt + s)
    return (r["pieces"], r["active"], r["dbs"], B * nt), r["arrays"], r["in_specs"], r["out_shape"], r["out_specs"]


def _proj_qkv(x, lnw, w_t, rope_tabs, tm, dils, row_major_kv, q_scale, make_rider=None):
    B, S, D = x.shape
    assert S % tm == 0 and all(tm % (16 * d) == 0 for d in dils)
    row = lambda w: pl.BlockSpec((1, tm, w), lambda b, s: (b, s, 0))
    tab = pl.BlockSpec((tm, LANES), lambda b, s: (s, 0))
    rider, r_arrays, r_in_specs, r_out_shape, r_out_specs = _rider_parts(make_rider, B, S // tm)
    out_shape, out_specs = [], []
    for _ in range(3):
        for d in dils:
            out_shape.append(jax.ShapeDtypeStruct((B, S // d, d * GW), BF16))
            out_specs.append(pl.BlockSpec((1, tm // d, d * GW), lambda b, s: (b, s, 0)))
    if row_major_kv:
        out_shape += [jax.ShapeDtypeStruct((B, S, 2 * GW), F32)] * N_GROUPS
        out_specs += [row(2 * GW)] * N_GROUPS
    in_specs = [row(D), pl.BlockSpec((1, D), lambda b, s: (0, 0)),
                pl.BlockSpec((OFF_AZ, D), lambda b, s: (0, 0), pipeline_mode=pl.Buffered(1)),
                tab, tab, tab]
    return pl.pallas_call(
        functools.partial(_proj_qkv_kernel, dils=tuple(dils), row_major_kv=row_major_kv, q_scale=q_scale,
                          rider=rider, n_rider_in=len(r_arrays)),
        grid=(B, S // tm), in_specs=in_specs + r_in_specs, out_specs=out_specs + r_out_specs,
        out_shape=out_shape + r_out_shape,
        scratch_shapes=[pltpu.VMEM((GW // LANES, tm, LANES), F32)],
        compiler_params=_cparams(("arbitrary", "arbitrary")), name="proj_qkv",
    )(x, lnw, w_t, *rope_tabs, *r_arrays)


def _proj_rest(x, lnw, w_t, wa2, ba, tm, make_rider=None):
    B, S, D = x.shape
    assert S % tm == 0
    row = lambda w: pl.BlockSpec((1, tm, w), lambda b, s: (b, s, 0))
    const = lambda shp: pl.BlockSpec(shp, lambda b, s: (0,) * len(shp))
    bf = lambda w: jax.ShapeDtypeStruct((B, S, w), BF16)
    rider, r_arrays, r_in_specs, r_out_shape, r_out_specs = _rider_parts(make_rider, B, S // tm)
    out_shape = [bf(GW), bf(GLA_VW), bf(D_MODEL), bf(D_MODEL), bf(GLA_KW), bf(GLA_KW), bf(GLA_VW),
                 jax.ShapeDtypeStruct((B, S, GLA_KW), F32)]
    out_specs = [row(GW), row(GLA_VW), row(D_MODEL), row(D_MODEL), row(GLA_KW), row(GLA_KW), row(GLA_VW),
                 row(GLA_KW)]
    in_specs = [row(D), const((1, D)),
                pl.BlockSpec((OFF_AZ, D), lambda b, s: (1, 0), pipeline_mode=pl.Buffered(1)),
                const((GATE_RANK, GLA_KW)), const((1, GLA_KW))]
    return pl.pallas_call(
        functools.partial(_proj_rest_kernel, rider=rider, n_rider_in=len(r_arrays)),
        grid=(B, S // tm), in_specs=in_specs + r_in_specs, out_specs=out_specs + r_out_specs,
        out_shape=out_shape + r_out_shape,
        compiler_params=_cparams(("arbitrary", "arbitrary")), name="proj_rest",
    )(x, lnw, w_t, wa2, ba, *r_arrays)


def _kvt_kernel(x_ref, lnw_ref, wk_ref, wv_ref, cos_ref, sin_ref, o0_ref, o1_ref, o2_ref, *, plan, nt):
    i = pl.program_id(1)
    h = _rms_bf16(x_ref[...], lnw_ref[...])
    rows = h.shape[0]
    half = ROPE_DIM // 2
    for g, (tiles, cols) in enumerate(plan):
        o_ref = (o0_ref, o1_ref, o2_ref)[g]

        def emit(g=g, cols=cols, o_ref=o_ref):
            hh = h[rows - cols:, :]
            y = _dot_nt(wk_ref[g * GW:(g + 1) * GW, :], hh)
            c = cos_ref[:, rows - cols:]
            s = sin_ref[:, rows - cols:]
            for hd in range(HEADS):
                b0 = hd * HEAD_DIM
                x1 = y[b0:b0 + half, :]
                x2 = y[b0 + half:b0 + ROPE_DIM, :]
                o_ref[b0:b0 + half, :] = x1 * c - x2 * s
                o_ref[b0 + half:b0 + ROPE_DIM, :] = x2 * c + x1 * s
                o_ref[b0 + ROPE_DIM:b0 + HEAD_DIM, :] = y[b0 + ROPE_DIM:b0 + HEAD_DIM, :]
            o_ref[GW:2 * GW, :] = _dot_nt(wv_ref[g * GW:(g + 1) * GW, :], hh)

        if tiles == nt:
            emit()
        else:
            pl.when(i >= nt - tiles)(emit)


def _rope_cos_sin_t(pos):
    half = ROPE_DIM // 2
    inv_freq = 1.0 / (ROPE_THETA ** (jnp.arange(half, dtype=F32) * (2.0 / ROPE_DIM)))
    ang = pos.astype(F32)[:, None] * inv_freq[None, :]
    return jnp.cos(ang).T, jnp.sin(ang).T


def _kv_tail_prompt(x, lnw, w_t, S_tails, tile):
    B, S, D = x.shape
    tile = min(tile, S)
    span = max(S_tails)
    assert span % tile == 0 and S % tile == 0
    nt = span // tile
    first = (S - span) // tile
    plan = []
    for t in S_tails:
        assert t % tile == 0 or t < tile
        plan.append((t // tile, tile) if t >= tile else (1, t))
    cos_t, sin_t = _rope_cos_sin_t(jnp.arange(S, dtype=jnp.int32))
    tabspec = pl.BlockSpec((ROPE_DIM // 2, tile), lambda b, i: (0, first + i))
    out_specs = [pl.BlockSpec((None, 2 * GW, cols), lambda b, i, t=tiles: (b, 0, jnp.maximum(i - (nt - t), 0)))
                 for tiles, cols in plan]
    return pl.pallas_call(
        functools.partial(_kvt_kernel, plan=tuple(plan), nt=nt),
        grid=(B, nt),
        in_specs=[pl.BlockSpec((None, tile, D), lambda b, i: (b, first + i, 0)),
                  pl.BlockSpec((1, D), lambda b, i: (0, 0)),
                  pl.BlockSpec((QKV_W, D), lambda b, i: (OFF_K // QKV_W, 0), pipeline_mode=pl.Buffered(1)),
                  pl.BlockSpec((QKV_W, D), lambda b, i: (OFF_V // QKV_W, 0), pipeline_mode=pl.Buffered(1)),
                  tabspec, tabspec],
        out_specs=out_specs,
        out_shape=[jax.ShapeDtypeStruct((B, 2 * GW, t), F32) for t in S_tails],
        compiler_params=_cparams(("arbitrary", "arbitrary")), name="kv_tail_prompt",
    )(x, lnw, w_t, w_t, cos_t, sin_t)


def _kv_tail_sample(x, lnw, w_t, pos):
    DB, T, D = x.shape
    cos_t, sin_t = _rope_cos_sin_t(pos)
    bcast = lambda t: jnp.broadcast_to(t.T[:, :, None], (T, ROPE_DIM // 2, DB))
    tabspec = pl.BlockSpec((None, ROPE_DIM // 2, DB), lambda b, i: (i, 0, 0))
    out_spec = pl.BlockSpec((None, 2 * GW, DB), lambda b, i: (i, 0, 0))
    return pl.pallas_call(
        functools.partial(_kvt_kernel, plan=((T, DB),) * N_GROUPS, nt=T),
        grid=(1, T),
        in_specs=[pl.BlockSpec((None, DB, D), lambda b, i: (i, 0, 0)),
                  pl.BlockSpec((1, D), lambda b, i: (0, 0)),
                  pl.BlockSpec((QKV_W, D), lambda b, i: (OFF_K // QKV_W, 0), pipeline_mode=pl.Buffered(1)),
                  pl.BlockSpec((QKV_W, D), lambda b, i: (OFF_V // QKV_W, 0), pipeline_mode=pl.Buffered(1)),
                  tabspec, tabspec],
        out_specs=[out_spec] * N_GROUPS,
        out_shape=[jax.ShapeDtypeStruct((T, 2 * GW, DB), F32)] * N_GROUPS,
        compiler_params=_cparams(("arbitrary", "arbitrary")), name="kv_tail_sample",
    )(x.transpose(1, 0, 2), lnw, w_t, w_t, bcast(cos_t), bcast(sin_t))


def _band_bias():
    q = np.arange(BAND)[:, None]
    c = np.arange(2 * BAND)[None, :]
    ok = np.where(c < BAND, c >= q, (c - BAND) <= q)
    ok_first = ok & (c >= BAND)
    return jnp.asarray(np.stack([np.where(ok, 0.0, NEG), np.where(ok_first, 0.0, NEG)]).astype(np.float32))


def _attn_kernel(q_ref, k_ref, kp_ref, v_ref, vp_ref, bias_ref, o_ref, lse_ref, kall, vall, *, qb):
    n = pl.program_id(2)
    nsub = qb // BAND
    npair = HEADS // 2
    kall[0:BAND, :] = kp_ref[0]
    kall[BAND:, :] = k_ref[0]
    vall[0:BAND, :] = vp_ref[0]
    vall[BAND:, :] = v_ref[0]
    lane = lax.broadcasted_iota(jnp.int32, (BAND, LANES), 1)
    low_half = lane < HEAD_DIM

    def sub_block(j, carry):
        r0 = pl.multiple_of(j * BAND, BAND)
        first = (n * nsub + j) == 0
        bias = bias_ref[jnp.where(first, 1, 0)]
        bias2 = jnp.concatenate([bias, bias], axis=0)
        scores = []
        for hp in range(npair):
            cs = slice(hp * LANES, (hp + 1) * LANES)
            qp = q_ref[0, pl.ds(r0, BAND), cs]
            zero = jnp.zeros_like(qp)
            qm = jnp.concatenate([jnp.where(low_half, qp, zero), jnp.where(low_half, zero, qp)], axis=0)
            scores.append(_dot_nt(qm, kall[pl.ds(r0, 2 * BAND), cs]) + bias2)
        probs, stats = [], []
        for s in scores:
            m = jnp.max(s, axis=1, keepdims=True)
            p = jnp.exp2(s - m)
            stats.append((m, jnp.sum(p, axis=1, keepdims=True)))
            probs.append(p.astype(BF16))
        m_tile = jnp.zeros((BAND, LANES), F32)
        den_tile = jnp.ones((BAND, LANES), F32)
        for hp in range(npair):
            cs = slice(hp * LANES, (hp + 1) * LANES)
            m, den = stats[hp]
            o = _dot(probs[hp], vall[pl.ds(r0, 2 * BAND), cs])
            o = jnp.where(low_half, o[0:BAND], o[BAND:]) / jnp.where(low_half, den[0:BAND], den[BAND:])
            o_ref[0, pl.ds(r0, BAND), cs] = o.astype(BF16)
            for a in range(2):
                rows = slice(a * BAND, (a + 1) * BAND)
                m_tile = jnp.where(lane == 2 * hp + a, m[rows], m_tile)
                den_tile = jnp.where(lane == 2 * hp + a, den[rows], den_tile)
        lse_ref[0, pl.ds(r0, BAND), :] = (m_tile + jnp.log2(den_tile)) * LN2
        return carry

    lax.fori_loop(0, nsub, sub_block, 0, unroll=True)


def _attn_group(q, k, v, g, qb):
    B, L, _ = q.shape
    win, dil = GROUPS[g]
    assert win // dil == BAND
    qb = min(qb, L)
    assert L % qb == 0 and qb % BAND == 0
    nsub = qb // BAND
    cur = pl.BlockSpec((1, qb, GW), lambda b, r, n: (b, n, r))
    prev = pl.BlockSpec((1, BAND, GW), lambda b, r, n: (b, jnp.maximum(n * nsub - 1, 0), r))
    return pl.pallas_call(
        functools.partial(_attn_kernel, qb=qb),
        grid=(B, dil, L // qb),
        in_specs=[cur, cur, prev, cur, prev, pl.BlockSpec((2, BAND, 2 * BAND), lambda b, r, n: (0, 0, 0))],
        out_specs=[pl.BlockSpec((1, qb, GW), lambda b, r, n: (b, n, r)),
                   pl.BlockSpec((1, qb, LANES), lambda b, r, n: (b, n, r))],
        out_shape=[jax.ShapeDtypeStruct((B, L, dil * GW), BF16),
                   jax.ShapeDtypeStruct((B, L, dil * LANES), F32)],
        scratch_shapes=[pltpu.VMEM((qb + BAND, GW), BF16), pltpu.VMEM((qb + BAND, GW), BF16)],
        compiler_params=_cparams(("arbitrary", "arbitrary", "arbitrary")), name=f"attn_g{g}",
    )(q, k, k, v, v, _band_bias())


def _gla_kernel(gq_ref, gk_ref, gv_ref, la_ref, tri_ref, nw_ref, o_ref, sfin_ref, st_ref, *, nchunk):
    c = pl.program_id(1)

    @pl.when(c == 0)
    def _():
        st_ref[...] = jnp.zeros_like(st_ref)

    C = GLA_CHUNK
    npair = GLA_HEADS // 2
    tri3 = tri_ref[...]
    normw = nw_ref[...]
    hi, mid, lo = _split3(la_ref[0])
    b_chunks = []
    for ci in range(nchunk):
        rows = slice(ci * C, (ci + 1) * C)
        b_chunks.append(_dot(tri3, jnp.concatenate([hi[rows], mid[rows], lo[rows]], axis=0)))
    b = jnp.concatenate(b_chunks, axis=0)
    b_last = jnp.concatenate([jnp.broadcast_to(bc[C - 1:C, :], bc.shape) for bc in b_chunks], axis=0)
    gk = gk_ref[0].astype(F32)
    qe = gq_ref[0].astype(F32) * jnp.exp(b)
    ke = (gk * jnp.exp(-b)).astype(BF16)
    kd = (gk * jnp.exp(b_last - b)).astype(BF16)
    gv = gv_ref[0]

    low_half = lax.broadcasted_iota(jnp.int32, (C, LANES), 1) < GLA_DK
    ri = lax.broadcasted_iota(jnp.int32, (2 * C, 2 * C), 0)
    ki = lax.broadcasted_iota(jnp.int32, (2 * C, 2 * C), 1)
    pair_causal = jnp.logical_and((ri < C) == (ki < C),
                                  jnp.bitwise_and(ki, C - 1) <= jnp.bitwise_and(ri, C - 1))

    lhs, vrows, upd, dec = {}, {}, {}, {}
    for ci in range(nchunk):
        rows = slice(ci * C, (ci + 1) * C)
        for p in range(npair):
            cs = slice(p * LANES, (p + 1) * LANES)
            qe_p = qe[rows, cs]
            qm = jnp.concatenate([jnp.where(low_half, qe_p, 0.0), jnp.where(low_half, 0.0, qe_p)],
                                 axis=0).astype(BF16)
            ke_p = ke[rows, cs]
            sc = _dot_nt(qm, jnp.concatenate([ke_p, ke_p], axis=0))
            att = jnp.where(pair_causal, sc, 0.0).astype(BF16)
            lhs[ci, p] = jnp.concatenate([att, qm], axis=1)
            v_ab = gv[rows, 2 * p * GLA_DV:2 * (p + 1) * GLA_DV]
            vrows[ci, p] = jnp.concatenate([v_ab[:, 0:GLA_DV], v_ab[:, GLA_DV:]], axis=0)
            u = _dot_tn(kd[rows, cs], v_ab)
            upd[ci, p] = jnp.concatenate([u[0:GLA_DK, 0:GLA_DV], u[GLA_DK:, GLA_DV:]], axis=0)
            bl = jnp.broadcast_to(b_chunks[ci][C - 1:C, cs], (LANES, LANES))
            dec[ci, p] = jnp.exp(bl.T)

    outs = {}
    states = [st_ref[p] for p in range(npair)]
    for ci in range(nchunk):
        for p in range(npair):
            rhs = jnp.concatenate([vrows[ci, p], states[p].astype(BF16)], axis=0)
            outs[ci, p] = _dot(lhs[ci, p], rhs)
            states[p] = dec[ci, p] * states[p] + upd[ci, p]
    for p in range(npair):
        st_ref[p] = states[p]

    for ci in range(nchunk):
        for p in range(npair):
            o = outs[ci, p]
            o = (o * lax.rsqrt(jnp.mean(o * o, axis=-1, keepdims=True) + NORM_EPS) * normw).astype(BF16)
            for a in range(2):
                hh = 2 * p + a
                o_ref[0, ci * C:(ci + 1) * C, hh * GLA_DV:(hh + 1) * GLA_DV] = o[a * C:(a + 1) * C, :]

    @pl.when(c == pl.num_programs(1) - 1)
    def _():
        for p in range(npair):
            sfin_ref[0, p * LANES:(p + 1) * LANES, :] = states[p]


def _tri3(C, dtype=BF16):
    tri = np.tril(np.ones((C, C), np.float32))
    return jnp.asarray(np.concatenate([tri, tri, tri], axis=1), dtype)


def _gla_prompt(gq, gk, gv, la, normw, ct):
    B, S, _ = gq.shape
    ct = min(ct, S)
    assert S % ct == 0 and ct % GLA_CHUNK == 0
    row = lambda w: pl.BlockSpec((1, ct, w), lambda b, c: (b, c, 0))
    const = lambda shp: pl.BlockSpec(shp, lambda b, c: (0,) * len(shp))
    o, sfin = pl.pallas_call(
        functools.partial(_gla_kernel, nchunk=ct // GLA_CHUNK),
        grid=(B, S // ct),
        in_specs=[row(GLA_KW), row(GLA_KW), row(GLA_VW), row(GLA_KW),
                  const((GLA_CHUNK, 3 * GLA_CHUNK)), const((1, GLA_DV))],
        out_specs=[row(GLA_VW), pl.BlockSpec((1, GLA_KW, GLA_DV), lambda b, c: (b, 0, 0))],
        out_shape=[jax.ShapeDtypeStruct((B, S, GLA_VW), BF16),
                   jax.ShapeDtypeStruct((B, GLA_KW, GLA_DV), F32)],
        scratch_shapes=[pltpu.VMEM((GLA_HEADS // 2, GLA_DV, LANES), F32)],
        compiler_params=_cparams(("arbitrary", "arbitrary")), name="gla_prompt",
    )(gq, gk, gv, la, _tri3(GLA_CHUNK), normw)
    return o, sfin.reshape(B, GLA_HEADS, GLA_DK, GLA_DV)


def _merge_kernel(*refs, dils):
    n_groups = len(dils)
    x_ref = refs[0]
    o_refs = refs[1:1 + n_groups]
    lse_refs = refs[1 + n_groups:1 + 2 * n_groups] if n_groups > 1 else ()
    rest = refs[1 + n_groups + len(lse_refs):]
    (saz_ref, og_ref, sgz_ref, sa_ref, sb_ref, wa_ref, wg_ref, wo_ref, fnw_ref, ex_ref, y_ref) = rest[:11]
    scratch = rest[11:]
    tm = x_ref.shape[0]

    def by_position(ref, width, d, stage_ref):
        if d == 1:
            return lambda rows: ref[rows, :].astype(F32)
        planes = width // LANES
        for r in range(d):
            blk = ref[:, r * width:(r + 1) * width].astype(F32)
            for c in range(planes):
                stage_ref[c, pl.ds(r, tm // d, stride=d), :] = blk[:, c * LANES:(c + 1) * LANES]
        return lambda rows: jnp.concatenate([stage_ref[c, rows, :] for c in range(planes)], axis=1)

    if n_groups > 1:
        lse_rows = [by_position(lse_refs[g], LANES, dils[g], scratch[2 * g + 1]) for g in range(n_groups)]
        o_rows = [by_position(o_refs[g], GW, dils[g], scratch[2 * g]) for g in range(n_groups)]

    rc = min(tm, MERGE_ROWS)
    chunks = [slice(r0, r0 + rc) for r0 in range(0, tm, rc)]
    yb = [_dot((og_ref[rows, :].astype(F32) * sgz_ref[rows, :].astype(F32)).astype(BF16), wg_ref[...])
          for rows in chunks]
    if n_groups > 1:
        comb = []
        for rows in chunks:
            ls = [f(rows) for f in lse_rows]
            mx = functools.reduce(jnp.maximum, ls)
            es = [jnp.exp(l - mx) for l in ls]
            inv = 1.0 / functools.reduce(lambda a, b: a + b, es)
            acc = None
            for g in range(n_groups):
                w = es[g] * inv
                w_hi = w.astype(BF16)
                w_lo = (w - w_hi.astype(F32)).astype(BF16)
                wx = _dot(jnp.concatenate([w_hi, w_lo], axis=1), ex_ref[...])
                term = wx * o_rows[g](rows)
                acc = term if acc is None else acc + term
            comb.append(acc)
    else:
        comb = [o_refs[0][rows, :].astype(F32) for rows in chunks]
    ya = [_dot((c * saz_ref[rows, :].astype(F32)).astype(BF16), wa_ref[...]) for c, rows in zip(comb, chunks)]
    outs = []
    for a, b, rows in zip(ya, yb, chunks):
        mixed = (sa_ref[rows, :].astype(F32) * a + sb_ref[rows, :].astype(F32) * b).astype(BF16)
        outs.append(x_ref[rows, :] + _dot(mixed, wo_ref[...]))
    for out, rows in zip(outs, chunks):
        y_ref[rows, :] = (out * lax.rsqrt(jnp.mean(out * out, axis=-1, keepdims=True) + NORM_EPS)
                          * fnw_ref[...])


def _expand_matrix():
    e = np.zeros((LANES, GW), np.float32)
    for hh in range(HEADS):
        e[hh, hh * HEAD_DIM:(hh + 1) * HEAD_DIM] = 1.0
    return jnp.asarray(np.concatenate([e, e], axis=0), BF16)


def _merge(x, o_list, lse_list, dils, saz, og, sgz, sa, sb, wa, wg, wo, fnw, tm):
    R, D = x.shape
    tm = min(tm, R)
    assert R % tm == 0
    n_groups = len(o_list)
    row = lambda w: pl.BlockSpec((tm, w), lambda i: (i, 0))
    const = lambda shp: pl.BlockSpec(shp, lambda i: (0,) * len(shp))
    in_specs = [row(D)]
    in_specs += [pl.BlockSpec((tm // d, d * GW), lambda i: (i, 0)) for d in dils]
    in_specs += [pl.BlockSpec((tm // d, d * LANES), lambda i: (i, 0)) for d in dils[:len(lse_list)]]
    in_specs += [row(GW), row(GLA_VW), row(GLA_VW), row(D), row(D),
                 const((GW, D)), const((GLA_VW, D)), const((D, D)), const((1, D)), const((2 * LANES, GW))]
    scratch = []
    if n_groups > 1:
        for _ in dils:
            scratch += [pltpu.VMEM((GW // LANES, tm, LANES), F32), pltpu.VMEM((1, tm, LANES), F32)]
    return pl.pallas_call(
        functools.partial(_merge_kernel, dils=tuple(dils)),
        grid=(R // tm,), in_specs=in_specs, out_specs=row(D),
        out_shape=jax.ShapeDtypeStruct((R, D), F32), scratch_shapes=scratch,
        compiler_params=_cparams(("arbitrary",)), name=f"merge_g{n_groups}",
    )(x, *o_list, *lse_list, saz, og, sgz, sa, sb, wa, wg, wo, fnw, _expand_matrix())


def _sample_masks(T, w_lens):
    t_of_row = np.arange(HEADS * T) % T

    def bias(idx, g):
        win, dil = GROUPS[g]
        dd = w_lens[g] + t_of_row[:, None] - idx[None, :]
        ok = (dd >= 0) & (dd % dil == 0) & (dd // dil <= win // dil)
        return jnp.asarray(np.where(ok, 0.0, NEG).astype(np.float32))

    cache = [bias(np.arange(w_lens[g]), g) for g in range(N_GROUPS)]
    new = [bias(w_lens[g] + np.arange(T), g) for g in range(N_GROUPS)]
    return cache, new


def _decode_scores_pieces(in_refs, out_refs, dbs):
    G = N_GROUPS
    sq_refs, sc_refs, sb_refs, bd_ref = in_refs[0:G], in_refs[G:2 * G], in_refs[2 * G:3 * G], in_refs[3 * G]

    def piece(i, g):
        qbd = (jnp.tile(sq_refs[g][i].astype(F32), (HEADS, 1)) * bd_ref[...]).astype(BF16)
        out_refs[g][i] = _dot(qbd, sc_refs[g][i].astype(BF16)) + sb_refs[g][...]

    return [functools.partial(piece, i, g) for i in range(dbs) for g in range(G)]


def _decode_mix_pieces(in_refs, out_refs, dbs):
    G = N_GROUPS
    q_refs, new_refs, s_refs = in_refs[0:G], in_refs[G:2 * G], in_refs[2 * G:3 * G]
    c_refs, bn_refs, bd_ref = in_refs[3 * G:4 * G], in_refs[4 * G:5 * G], in_refs[5 * G]
    o_ref = out_refs[0]
    T = q_refs[0].shape[1]
    p_old, acc = {}, {}

    def prologue(i):
        bd = bd_ref[...]
        s_new, v_new = [], []
        for g in range(G):
            qbd = jnp.tile(q_refs[g][i].astype(F32), (HEADS, 1)) * bd
            s_new.append(_dot_nt(qbd, _rb(new_refs[g][i, :, 0:GW])) + bn_refs[g][...])
            v_new.append(_rb(new_refs[g][i, :, GW:2 * GW]))
        s_old = [s_refs[g][i] for g in range(G)]
        m = functools.reduce(jnp.maximum, [jnp.max(s, axis=1, keepdims=True) for s in s_new + s_old])
        e_new = [jnp.exp(s - m) for s in s_new]
        e_old = [jnp.exp(s - m) for s in s_old]
        den = functools.reduce(lambda a, b: a + b, [jnp.sum(p, axis=1, keepdims=True) for p in e_new + e_old])
        inv = 1.0 / den
        for g in range(G):
            p_old[i, g] = e_old[g] * inv
            r = _dot(_rb(e_new[g] * inv), v_new[g])
            acc[i] = r if g == 0 else acc[i] + r

    def piece(i, g):
        acc[i] = acc[i] + _dot_nt(p_old[i, g], c_refs[g][i])

    def epilogue(i):
        a = acc[i] * bd_ref[...]
        out = a[0:T, :]
        for hh in range(1, HEADS):
            out = out + a[hh * T:(hh + 1) * T, :]
        o_ref[i] = out

    thunks = []
    for i in range(dbs):
        thunks += [functools.partial(prologue, i)] + [functools.partial(piece, i, g) for g in range(G)]
        thunks += [functools.partial(epilogue, i)]
    return thunks


def _head_diag(T):
    bd = np.zeros((HEADS * T, GW), np.float32)
    for hh in range(HEADS):
        bd[hh * T:(hh + 1) * T, hh * HEAD_DIM:(hh + 1) * HEAD_DIM] = 1.0
    return jnp.asarray(bd)


def _rider_common(DB, nsteps, step_of):
    dbs = -(-DB // nsteps)
    assert DB % dbs == 0
    active = DB // dbs
    blk = lambda b, s: jnp.minimum(step_of(b, s), active - 1)
    per = lambda shp, part=0: pl.BlockSpec((dbs,) + shp, lambda b, s: (blk(b, s), part) + (0,) * (len(shp) - 1))
    const = lambda a: pl.BlockSpec(a.shape, lambda b, s: (0,) * a.ndim)
    return dbs, active, per, const


def _scores_rider(q_list, caches_t):
    DB, T, _ = q_list[0].shape
    w_lens = [c.shape[2] for c in caches_t]
    bias_c, _ = _sample_masks(T, w_lens)

    def make(nsteps, step_of):
        dbs, active, per, const = _rider_common(DB, nsteps, step_of)
        return dict(
            pieces=_decode_scores_pieces, dbs=dbs, active=active,
            arrays=list(q_list) + list(caches_t) + bias_c + [_head_diag(T)],
            in_specs=[per((T, GW)) for _ in q_list] + [per((GW, w), 0) for w in w_lens]
                     + [const(a) for a in bias_c] + [pl.BlockSpec((HEADS * T, GW), lambda b, s: (0, 0))],
            out_shape=[jax.ShapeDtypeStruct((DB, HEADS * T, w), F32) for w in w_lens],
            out_specs=[per((HEADS * T, w)) for w in w_lens])
    return make


def _mix_rider(q_list, kv_new, scores, caches_t):
    DB, T, _ = q_list[0].shape
    w_lens = [c.shape[2] for c in caches_t]
    _, bias_n = _sample_masks(T, w_lens)

    def make(nsteps, step_of):
        dbs, active, per, const = _rider_common(DB, nsteps, step_of)
        return dict(
            pieces=_decode_mix_pieces, dbs=dbs, active=active,
            arrays=list(q_list) + list(kv_new) + list(scores) + list(caches_t) + bias_n + [_head_diag(T)],
            in_specs=[per((T, GW)) for _ in q_list] + [per((T, 2 * GW)) for _ in kv_new]
                     + [per((HEADS * T, w)) for w in w_lens] + [per((GW, w), 1) for w in w_lens]
                     + [const(a) for a in bias_n] + [pl.BlockSpec((HEADS * T, GW), lambda b, s: (0, 0))],
            out_shape=[jax.ShapeDtypeStruct((DB, T, GW), F32)],
            out_specs=[per((T, GW))])
    return make


def _sgla_kernel(gq_ref, gk_ref, gv_ref, la_ref, st_ref, tri_ref, nw_ref, o_ref, snew_ref, *, dbt):
    tri3 = tri_ref[...]
    normw = nw_ref[...]
    T = gq_ref.shape[1]
    ti = lax.broadcasted_iota(jnp.int32, (T, T), 0)
    si = lax.broadcasted_iota(jnp.int32, (T, T), 1)
    causal = si <= ti
    low_half = lax.broadcasted_iota(jnp.int32, (T, LANES), 1) < GLA_DK
    ones = jnp.ones((3 * T, LANES), F32)
    row_low = lax.broadcasted_iota(jnp.int32, (LANES, GLA_DV), 0) < GLA_DK

    entries = range(dbt)
    pairs = range(GLA_HEADS // 2)
    cs = [slice(p * LANES, (p + 1) * LANES) for p in pairs]
    la3 = {i: jnp.concatenate(_split3(la_ref[i], F32), axis=0) for i in entries}
    b = {i: _dot(tri3, la3[i]) for i in entries}
    dec = {(i, p): jnp.exp(_dot_tn(la3[i][:, cs[p]], ones)) for i in entries for p in pairs}
    qe, ke, kd, gv = {}, {}, {}, {}
    for i in entries:
        b_last = b[i][T - 1:T, :]
        gk = gk_ref[i].astype(F32)
        qe[i] = gq_ref[i].astype(F32) * jnp.exp(b[i])
        ke[i] = _rb(gk * jnp.exp(-b[i]))
        kd[i] = _rb(gk * jnp.exp(b_last - b[i]))
        gv[i] = gv_ref[i].astype(F32)
    heads = [(i, p, a) for i in entries for p in pairs for a in range(2)]
    qm = {(i, p, a): _rb(jnp.where(low_half if a == 0 else jnp.logical_not(low_half), qe[i][:, cs[p]], 0.0))
          for (i, p, a) in heads}
    att = {(i, p, a): _rb(jnp.where(causal, _dot_nt(qm[i, p, a], ke[i][:, cs[p]]), 0.0)) for (i, p, a) in heads}
    v_h = {(i, p, a): gv[i][:, (2 * p + a) * GLA_DV:(2 * p + a + 1) * GLA_DV] for (i, p, a) in heads}
    o = {(i, p, a): _dot(att[i, p, a], v_h[i, p, a]) + _dot(qm[i, p, a], _rb(st_ref[i, cs[p], :]))
         for (i, p, a) in heads}
    u = {(i, p, a): _dot_tn(kd[i][:, cs[p]], v_h[i, p, a]) for (i, p, a) in heads}
    for i in entries:
        for p in pairs:
            upd = jnp.where(row_low, u[i, p, 0], u[i, p, 1])
            snew_ref[i, cs[p], :] = st_ref[i, cs[p], :] * dec[i, p] + upd
        outs = [o[i, p, a] * lax.rsqrt(jnp.mean(o[i, p, a] * o[i, p, a], axis=-1, keepdims=True) + NORM_EPS)
                * normw for p in pairs for a in range(2)]
        o_ref[i] = jnp.concatenate(outs, axis=1)


def _sample_gla(gq, gk, gv, la, state, normw, dbt):
    DB, T, _ = gq.shape
    dbt = min(dbt, DB)
    assert DB % dbt == 0
    st = state.reshape(DB, GLA_KW, GLA_DV)
    blk = lambda shp: pl.BlockSpec((dbt,) + shp, lambda i: (i,) + (0,) * len(shp))
    const = lambda shp: pl.BlockSpec(shp, lambda i: (0,) * len(shp))
    o, snew = pl.pallas_call(
        functools.partial(_sgla_kernel, dbt=dbt),
        grid=(DB // dbt,),
        in_specs=[blk((T, GLA_KW)), blk((T, GLA_KW)), blk((T, GLA_VW)), blk((T, GLA_KW)),
                  blk((GLA_KW, GLA_DV)), const((T, 3 * T)), const((1, GLA_DV))],
        out_specs=[blk((T, GLA_VW)), blk((GLA_KW, GLA_DV))],
        out_shape=[jax.ShapeDtypeStruct((DB, T, GLA_VW), F32),
                   jax.ShapeDtypeStruct((DB, GLA_KW, GLA_DV), F32)],
        compiler_params=_cparams(("arbitrary",)), name="gla_sample",
    )(gq, gk, gv, la, st, _tri3(T, F32), normw)
    return o, snew.reshape(DB, GLA_HEADS, GLA_DK, GLA_DV)


def _rope_tables(pos):
    half = ROPE_DIM // 2
    inv_freq = 1.0 / (ROPE_THETA ** (jnp.arange(half, dtype=F32) * (2.0 / ROPE_DIM)))
    ang = pos.astype(F32)[:, None] * inv_freq[None, :]
    cos, sin = jnp.cos(ang), jnp.sin(ang)
    n = pos.shape[0]
    pad = jnp.zeros((n, HEAD_DIM - ROPE_DIM), F32)
    zero = jnp.zeros((n, half), F32)
    c = jnp.concatenate([cos, cos, pad + 1.0], axis=1)
    s1 = jnp.concatenate([zero, sin, pad], axis=1)
    s2 = jnp.concatenate([-sin, zero, pad], axis=1)
    rep = LANES // HEAD_DIM
    return tuple(jnp.tile(t, (1, rep)) for t in (c, s1, s2))


def kernel(x_prompt, x_sample, cache_kv_w128, cache_kv_w512, cache_kv_w2048, state_gla,
           ln_w, w_in, w_gla_a2, b_gla_a, gla_norm_w, w_attn_out, w_gla_out, w_out, final_norm_w):
    B, S, D = x_prompt.shape
    DB, T, _ = x_sample.shape
    assert ln_w.shape[0] == 1, "single-layer step"
    caches = (cache_kv_w128, cache_kv_w512, cache_kv_w2048)
    dils = [d for _, d in GROUPS]

    lnw = ln_w[0].reshape(1, D)
    w_t = w_in[0].T.astype(BF16)
    assert PROJ_W <= 2 * OFF_AZ
    wa2 = w_gla_a2[0].astype(BF16)
    ba = b_gla_a[0].reshape(1, GLA_KW)
    gnw = gla_norm_w[0].reshape(1, GLA_DV)
    wa = w_attn_out[0].astype(BF16)
    wg = w_gla_out[0].astype(BF16)
    wo = w_out[0].astype(BF16)
    fnw = final_norm_w.reshape(1, D)


    R = DB * T
    xs = x_sample.reshape(1, R, D)
    pos_s = PAST_LEN + jnp.arange(T, dtype=jnp.int32)
    tabs_s = tuple(jnp.tile(t, (DB, 1)) for t in _rope_tables(pos_s))
    qkv_s = _proj_qkv(xs, lnw, w_t, tabs_s, min(PROJ_ROWS, R), (1,) * N_GROUPS, row_major_kv=True,
                      q_scale=HEAD_DIM ** -0.5)
    rest_s = _proj_rest(xs, lnw, w_t, wa2, ba, min(PROJ_ROWS, R))
    per_db = lambda t: t.reshape(DB, T, t.shape[-1])
    q_s = [per_db(q) for q in qkv_s[0:3]]
    kv_new = [per_db(t) for t in qkv_s[9:12]]
    caches_t = [c[0].transpose(0, 2, 3, 4, 1).reshape(DB, 2 * GW, c.shape[2]) for c in caches]

    tabs_p = _rope_tables(jnp.arange(S, dtype=jnp.int32))
    qkv = _proj_qkv(x_prompt, lnw, w_t, tabs_p, min(PROJ_ROWS, S), dils, row_major_kv=False,
                    q_scale=HEAD_DIM ** -0.5 * LOG2E,
                    make_rider=_scores_rider(q_s, caches_t))
    scores = qkv[9:12]
    rest = _proj_rest(x_prompt, lnw, w_t, wa2, ba, min(PROJ_ROWS, S),
                      make_rider=_mix_rider(q_s, kv_new, scores, caches_t))
    saz, sgz, sa, sb, gq, gk, gv, la, o_s = rest
    o_list, lse_list = [], []
    for g in range(N_GROUPS):
        o, lse = _attn_group(qkv[g], qkv[3 + g], qkv[6 + g], g, qb=ATTN_ROWS)
        o_list.append(o.reshape(-1, o.shape[-1]))
        lse_list.append(lse.reshape(-1, lse.shape[-1]))
    og, gla_p = _gla_prompt(gq, gk, gv, la, gnw, ct=GLA_TOKENS)
    flat = lambda t: t.reshape(B * S, t.shape[-1])
    y_prompt = _merge(flat(x_prompt), o_list, lse_list, dils, flat(saz), flat(og), flat(sgz), flat(sa),
                      flat(sb), wa, wg, wo, fnw, tm=MERGE_TILE).reshape(B, S, D)
    kvt = _kv_tail_prompt(x_prompt, lnw, w_t, [min(w, S) for w, _ in GROUPS], tile=KV_TAIL_COLS)
    kv_p = [t.reshape(B, 2, HEADS, HEAD_DIM, t.shape[-1]).transpose(0, 4, 1, 2, 3)[None] for t in kvt]

    saz, sgz, sa, sb, gq, gk, gv, la = rest_s
    og_s, gla_s = _sample_gla(per_db(gq), per_db(gk), per_db(gv), per_db(la), state_gla[0], gnw, dbt=SAMPLE_GLA_ENTRIES)
    flat_s = lambda t: t.reshape(R, t.shape[-1])
    y_sample = _merge(x_sample.reshape(R, D), [o_s.reshape(R, GW)], [], (1,), flat_s(saz), flat_s(og_s),
                      flat_s(sgz), flat_s(sa), flat_s(sb), wa, wg, wo, fnw, tm=MERGE_TILE).reshape(DB, T, D)
    kvt_s = _kv_tail_sample(x_sample, lnw, w_t, pos_s)
    kv_s = [t.reshape(T, 2, HEADS, HEAD_DIM, DB).transpose(4, 0, 1, 2, 3)[None] for t in kvt_s]

    return (y_prompt, y_sample, kv_p[0], kv_p[1], kv_p[2], gla_p[None],
            kv_s[0], kv_s[1], kv_s[2], gla_s[None])
```

```python
import functools

import numpy as np
import jax
import jax.numpy as jnp
from jax import lax
from jax.experimental import pallas as pl
from jax.experimental.pallas import tpu as pltpu

F32 = jnp.float32
BF16 = jnp.bfloat16

D_MODEL = 1024
HEAD_DIM = 64
HEADS = 8
GROUPS = ((128, 1), (512, 4), (2048, 16))
N_GROUPS = len(GROUPS)
GW = HEADS * HEAD_DIM
QKV_W = N_GROUPS * GW
ROPE_DIM = HEAD_DIM // 4
ROPE_THETA = 500000.0
BAND = 128
GLA_HEADS = 4
GLA_DK = 64
GLA_DV = 128
GLA_KW = GLA_HEADS * GLA_DK
GLA_VW = GLA_HEADS * GLA_DV
GATE_RANK = 16
GLA_TAU = 16.0
GLA_CHUNK = 64
NORM_EPS = 1e-6
PAST_LEN = 8192

LANES = 128
NEG = -1e30
LOG2E = float(np.log2(np.e))
LN2 = float(np.log(2.0))

OFF_Q = 0
OFF_K = OFF_Q + QKV_W
OFF_V = OFF_K + QKV_W
OFF_AZ = OFF_V + QKV_W
OFF_GQ = OFF_AZ + GW
OFF_GK = OFF_GQ + GLA_KW
OFF_GV = OFF_GK + GLA_KW
OFF_GZ = OFF_GV + GLA_VW
OFF_LR = OFF_GZ + GLA_VW
OFF_MA = OFF_LR + GATE_RANK
OFF_MB = OFF_MA + D_MODEL
PROJ_W = OFF_MB + D_MODEL
assert all(o % 16 == 0 for o in (OFF_LR, OFF_MA, OFF_MB, PROJ_W))

VMEM_LIMIT = 56 * 1024 * 1024

PROJ_ROWS = 512
ATTN_ROWS = 2048
GLA_TOKENS = 2048
KV_TAIL_COLS = 1024
MERGE_TILE = 1024
MERGE_ROWS = 512
SAMPLE_GLA_ENTRIES = 32

def _cparams(sem):
    return pltpu.CompilerParams(dimension_semantics=sem, vmem_limit_bytes=VMEM_LIMIT)


def _dot(a, b):
    return jnp.dot(a, b, preferred_element_type=F32)


def _dot_nt(a, b):
    return lax.dot_general(a, b, (((1,), (1,)), ((), ())), preferred_element_type=F32)


def _dot_tn(a, b):
    return lax.dot_general(a, b, (((0,), (0,)), ((), ())), preferred_element_type=F32)


def _rb(x):
    return x.astype(BF16).astype(F32)


def _split3(x, dtype=BF16):
    hi = x.astype(BF16)
    r1 = x - hi.astype(F32)
    mid = r1.astype(BF16)
    lo = (r1 - mid.astype(F32)).astype(BF16)
    return hi.astype(dtype), mid.astype(dtype), lo.astype(dtype)


def _sigmoid(x):
    return 1.0 / (1.0 + jnp.exp(-x))


def _log_sigmoid(x):
    return jnp.minimum(x, 0.0) - jnp.log1p(jnp.exp(-jnp.abs(x)))


def _rms_bf16(x, w):
    ms = jnp.mean(x * x, axis=-1, keepdims=True)
    return (x * lax.rsqrt(ms + NORM_EPS) * w).astype(BF16)


def _run_rider(rider, in_refs, out_refs):
    if rider is None:
        return
    pieces, active, dbs, nsteps = rider

    def emit():
        for thunk in pieces(in_refs, out_refs, dbs):
            thunk()

    if active == nsteps:
        emit()
    else:
        step = pl.program_id(0) * pl.num_programs(1) + pl.program_id(1)
        pl.when(step < active)(emit)


def _proj_qkv_kernel(x_ref, lnw_ref, w_ref, rc_ref, rs1_ref, rs2_ref, *refs,
                     dils, row_major_kv, q_scale, rider, n_rider_in):
    rider_in, refs = refs[:n_rider_in], refs[n_rider_in:]
    q_refs, k_refs, v_refs = refs[0:3], refs[3:6], refs[6:9]
    kv_refs = refs[9:12] if row_major_kv else ()
    rider_out = refs[9 + len(kv_refs):-1]
    stage_ref = refs[-1]
    h = _rms_bf16(x_ref[0], lnw_ref[...])
    tm = h.shape[0]
    rc, rs1, rs2 = rc_ref[...], rs1_ref[...], rs2_ref[...]

    def mm(off, width):
        return _dot_nt(h, w_ref[off:off + width, :])

    def rope(t):
        outs = []
        for j in range(t.shape[1] // LANES):
            c = t[:, j * LANES:(j + 1) * LANES]
            outs.append(c * rc + pltpu.roll(c, ROPE_DIM // 2, axis=1) * rs1
                        + pltpu.roll(c, LANES - ROPE_DIM // 2, axis=1) * rs2)
        return jnp.concatenate(outs, axis=1)

    def store_by_residue(dst_ref, val, d):
        if d == 1:
            dst_ref[0] = val.astype(BF16)
            return
        for c in range(GW // LANES):
            stage_ref[c] = val[:, c * LANES:(c + 1) * LANES]
        for r in range(d):
            for c in range(GW // LANES):
                col = r * GW + c * LANES
                dst_ref[0, :, col:col + LANES] = stage_ref[c, pl.ds(r, tm // d, stride=d), :].astype(BF16)

    for g in range(N_GROUPS):
        qr = rope(mm(OFF_Q + g * GW, GW)) * q_scale
        store_by_residue(q_refs[g], qr, dils[g])
        kr = rope(mm(OFF_K + g * GW, GW))
        store_by_residue(k_refs[g], kr, dils[g])
        vv = mm(OFF_V + g * GW, GW)
        store_by_residue(v_refs[g], vv, dils[g])
        if row_major_kv:
            kv_refs[g][0, :, 0:GW] = kr
            kv_refs[g][0, :, GW:2 * GW] = vv
    _run_rider(rider, rider_in, rider_out)


def _proj_rest_kernel(x_ref, lnw_ref, w_ref, wa2_ref, ba_ref, *refs, rider, n_rider_in):
    rider_in, refs = refs[:n_rider_in], refs[n_rider_in:]
    saz_ref, sgz_ref, sa_ref, sb_ref, gq_ref, gk_ref, gv_ref, la_ref = refs[0:8]
    rider_out = refs[8:]
    h = _rms_bf16(x_ref[0], lnw_ref[...])

    def mm(off, width):
        return _dot_nt(h, w_ref[off - OFF_AZ:off - OFF_AZ + width, :])

    az = mm(OFF_AZ, GW)
    saz_ref[0] = (az * _sigmoid(az)).astype(BF16)
    gz = mm(OFF_GZ, GLA_VW)
    sgz_ref[0] = (gz * _sigmoid(gz)).astype(BF16)
    for j in range(D_MODEL // GW):
        sa_ref[0, :, j * GW:(j + 1) * GW] = _sigmoid(mm(OFF_MA + j * GW, GW)).astype(BF16)
        sb_ref[0, :, j * GW:(j + 1) * GW] = _sigmoid(mm(OFF_MB + j * GW, GW)).astype(BF16)
    gq_ref[0] = (mm(OFF_GQ, GLA_KW) * (GLA_DK ** -0.5)).astype(BF16)
    gk_ref[0] = mm(OFF_GK, GLA_KW).astype(BF16)
    gv_ref[0] = mm(OFF_GV, GLA_VW).astype(BF16)
    glr = mm(OFF_LR, GATE_RANK)
    gate_pre = _dot(glr.astype(BF16), wa2_ref[...]) + ba_ref[...]
    la_ref[0] = _log_sigmoid(gate_pre) * (1.0 / GLA_TAU)
    _run_rider(rider, rider_in, rider_out)


def _rider_parts(make_rider, B, nt):
    if make_rider is None:
        return None, [], [], [], []
    r = make_rider(B * nt, lambda b, s: b * nt + s)
    return (r["pieces"], r["active"], r["dbs"], B * nt), r["arrays"], r["in_specs"], r["out_shape"], r["out_specs"]


def _proj_qkv(x, lnw, w_t, rope_tabs, tm, dils, row_major_kv, q_scale, make_rider=None):
    B, S, D = x.shape
    assert S % tm == 0 and all(tm % (16 * d) == 0 for d in dils)
    row = lambda w: pl.BlockSpec((1, tm, w), lambda b, s: (b, s, 0))
    tab = pl.BlockSpec((tm, LANES), lambda b, s: (s, 0))
    rider, r_arrays, r_in_specs, r_out_shape, r_out_specs = _rider_parts(make_rider, B, S // tm)
    out_shape, out_specs = [], []
    for _ in range(3):
        for d in dils:
            out_shape.append(jax.ShapeDtypeStruct((B, S // d, d * GW), BF16))
            out_specs.append(pl.BlockSpec((1, tm // d, d * GW), lambda b, s: (b, s, 0)))
    if row_major_kv:
        out_shape += [jax.ShapeDtypeStruct((B, S, 2 * GW), F32)] * N_GROUPS
        out_specs += [row(2 * GW)] * N_GROUPS
    in_specs = [row(D), pl.BlockSpec((1, D), lambda b, s: (0, 0)),
                pl.BlockSpec((OFF_AZ, D), lambda b, s: (0, 0), pipeline_mode=pl.Buffered(1)),
                tab, tab, tab]
    return pl.pallas_call(
        functools.partial(_proj_qkv_kernel, dils=tuple(dils), row_major_kv=row_major_kv, q_scale=q_scale,
                          rider=rider, n_rider_in=len(r_arrays)),
        grid=(B, S // tm), in_specs=in_specs + r_in_specs, out_specs=out_specs + r_out_specs,
        out_shape=out_shape + r_out_shape,
        scratch_shapes=[pltpu.VMEM((GW // LANES, tm, LANES), F32)],
        compiler_params=_cparams(("arbitrary", "arbitrary")), name="proj_qkv",
    )(x, lnw, w_t, *rope_tabs, *r_arrays)


def _proj_rest(x, lnw, w_t, wa2, ba, tm, make_rider=None):
    B, S, D = x.shape
    assert S % tm == 0
    row = lambda w: pl.BlockSpec((1, tm, w), lambda b, s: (b, s, 0))
    const = lambda shp: pl.BlockSpec(shp, lambda b, s: (0,) * len(shp))
    bf = lambda w: jax.ShapeDtypeStruct((B, S, w), BF16)
    rider, r_arrays, r_in_specs, r_out_shape, r_out_specs = _rider_parts(make_rider, B, S // tm)
    out_shape = [bf(GW), bf(GLA_VW), bf(D_MODEL), bf(D_MODEL), bf(GLA_KW), bf(GLA_KW), bf(GLA_VW),
                 jax.ShapeDtypeStruct((B, S, GLA_KW), F32)]
    out_specs = [row(GW), row(GLA_VW), row(D_MODEL), row(D_MODEL), row(GLA_KW), row(GLA_KW), row(GLA_VW),
                 row(GLA_KW)]
    in_specs = [row(D), const((1, D)),
                pl.BlockSpec((OFF_AZ, D), lambda b, s: (1, 0), pipeline_mode=pl.Buffered(1)),
                const((GATE_RANK, GLA_KW)), const((1, GLA_KW))]
    return pl.pallas_call(
        functools.partial(_proj_rest_kernel, rider=rider, n_rider_in=len(r_arrays)),
        grid=(B, S // tm), in_specs=in_specs + r_in_specs, out_specs=out_specs + r_out_specs,
        out_shape=out_shape + r_out_shape,
        compiler_params=_cparams(("arbitrary", "arbitrary")), name="proj_rest",
    )(x, lnw, w_t, wa2, ba, *r_arrays)


def _kvt_kernel(x_ref, lnw_ref, wk_ref, wv_ref, cos_ref, sin_ref, o0_ref, o1_ref, o2_ref, *, plan, nt):
    i = pl.program_id(1)
    h = _rms_bf16(x_ref[...], lnw_ref[...])
    rows = h.shape[0]
    half = ROPE_DIM // 2
    for g, (tiles, cols) in enumerate(plan):
        o_ref = (o0_ref, o1_ref, o2_ref)[g]

        def emit(g=g, cols=cols, o_ref=o_ref):
            hh = h[rows - cols:, :]
            y = _dot_nt(wk_ref[g * GW:(g + 1) * GW, :], hh)
            c = cos_ref[:, rows - cols:]
            s = sin_ref[:, rows - cols:]
            for hd in range(HEADS):
                b0 = hd * HEAD_DIM
                x1 = y[b0:b0 + half, :]
                x2 = y[b0 + half:b0 + ROPE_DIM, :]
                o_ref[b0:b0 + half, :] = x1 * c - x2 * s
                o_ref[b0 + half:b0 + ROPE_DIM, :] = x2 * c + x1 * s
                o_ref[b0 + ROPE_DIM:b0 + HEAD_DIM, :] = y[b0 + ROPE_DIM:b0 + HEAD_DIM, :]
            o_ref[GW:2 * GW, :] = _dot_nt(wv_ref[g * GW:(g + 1) * GW, :], hh)

        if tiles == nt:
            emit()
        else:
            pl.when(i >= nt - tiles)(emit)


def _rope_cos_sin_t(pos):
    half = ROPE_DIM // 2
    inv_freq = 1.0 / (ROPE_THETA ** (jnp.arange(half, dtype=F32) * (2.0 / ROPE_DIM)))
    ang = pos.astype(F32)[:, None] * inv_freq[None, :]
    return jnp.cos(ang).T, jnp.sin(ang).T


def _kv_tail_prompt(x, lnw, w_t, S_tails, tile):
    B, S, D = x.shape
    tile = min(tile, S)
    span = max(S_tails)
    assert span % tile == 0 and S % tile == 0
    nt = span // tile
    first = (S - span) // tile
    plan = []
    for t in S_tails:
        assert t % tile == 0 or t < tile
        plan.append((t // tile, tile) if t >= tile else (1, t))
    cos_t, sin_t = _rope_cos_sin_t(jnp.arange(S, dtype=jnp.int32))
    tabspec = pl.BlockSpec((ROPE_DIM // 2, tile), lambda b, i: (0, first + i))
    out_specs = [pl.BlockSpec((None, 2 * GW, cols), lambda b, i, t=tiles: (b, 0, jnp.maximum(i - (nt - t), 0)))
                 for tiles, cols in plan]
    return pl.pallas_call(
        functools.partial(_kvt_kernel, plan=tuple(plan), nt=nt),
        grid=(B, nt),
        in_specs=[pl.BlockSpec((None, tile, D), lambda b, i: (b, first + i, 0)),
                  pl.BlockSpec((1, D), lambda b, i: (0, 0)),
                  pl.BlockSpec((QKV_W, D), lambda b, i: (OFF_K // QKV_W, 0), pipeline_mode=pl.Buffered(1)),
                  pl.BlockSpec((QKV_W, D), lambda b, i: (OFF_V // QKV_W, 0), pipeline_mode=pl.Buffered(1)),
                  tabspec, tabspec],
        out_specs=out_specs,
        out_shape=[jax.ShapeDtypeStruct((B, 2 * GW, t), F32) for t in S_tails],
        compiler_params=_cparams(("arbitrary", "arbitrary")), name="kv_tail_prompt",
    )(x, lnw, w_t, w_t, cos_t, sin_t)


def _kv_tail_sample(x, lnw, w_t, pos):
    DB, T, D = x.shape
    cos_t, sin_t = _rope_cos_sin_t(pos)
    bcast = lambda t: jnp.broadcast_to(t.T[:, :, None], (T, ROPE_DIM // 2, DB))
    tabspec = pl.BlockSpec((None, ROPE_DIM // 2, DB), lambda b, i: (i, 0, 0))
    out_spec = pl.BlockSpec((None, 2 * GW, DB), lambda b, i: (i, 0, 0))
    return pl.pallas_call(
        functools.partial(_kvt_kernel, plan=((T, DB),) * N_GROUPS, nt=T),
        grid=(1, T),
        in_specs=[pl.BlockSpec((None, DB, D), lambda b, i: (i, 0, 0)),
                  pl.BlockSpec((1, D), lambda b, i: (0, 0)),
                  pl.BlockSpec((QKV_W, D), lambda b, i: (OFF_K // QKV_W, 0), pipeline_mode=pl.Buffered(1)),
                  pl.BlockSpec((QKV_W, D), lambda b, i: (OFF_V // QKV_W, 0), pipeline_mode=pl.Buffered(1)),
                  tabspec, tabspec],
        out_specs=[out_spec] * N_GROUPS,
        out_shape=[jax.ShapeDtypeStruct((T, 2 * GW, DB), F32)] * N_GROUPS,
        compiler_params=_cparams(("arbitrary", "arbitrary")), name="kv_tail_sample",
    )(x.transpose(1, 0, 2), lnw, w_t, w_t, bcast(cos_t), bcast(sin_t))


def _band_bias():
    q = np.arange(BAND)[:, None]
    c = np.arange(2 * BAND)[None, :]
    ok = np.where(c < BAND, c >= q, (c - BAND) <= q)
    ok_first = ok & (c >= BAND)
    return jnp.asarray(np.stack([np.where(ok, 0.0, NEG), np.where(ok_first, 0.0, NEG)]).astype(np.float32))


def _attn_kernel(q_ref, k_ref, kp_ref, v_ref, vp_ref, bias_ref, o_ref, lse_ref, kall, vall, *, qb):
    n = pl.program_id(2)
    nsub = qb // BAND
    npair = HEADS // 2
    kall[0:BAND, :] = kp_ref[0]
    kall[BAND:, :] = k_ref[0]
    vall[0:BAND, :] = vp_ref[0]
    vall[BAND:, :] = v_ref[0]
    lane = lax.broadcasted_iota(jnp.int32, (BAND, LANES), 1)
    low_half = lane < HEAD_DIM

    def sub_block(j, carry):
        r0 = pl.multiple_of(j * BAND, BAND)
        first = (n * nsub + j) == 0
        bias = bias_ref[jnp.where(first, 1, 0)]
        bias2 = jnp.concatenate([bias, bias], axis=0)
        scores = []
        for hp in range(npair):
            cs = slice(hp * LANES, (hp + 1) * LANES)
            qp = q_ref[0, pl.ds(r0, BAND), cs]
            zero = jnp.zeros_like(qp)
            qm = jnp.concatenate([jnp.where(low_half, qp, zero), jnp.where(low_half, zero, qp)], axis=0)
            scores.append(_dot_nt(qm, kall[pl.ds(r0, 2 * BAND), cs]) + bias2)
        probs, stats = [], []
        for s in scores:
            m = jnp.max(s, axis=1, keepdims=True)
            p = jnp.exp2(s - m)
            stats.append((m, jnp.sum(p, axis=1, keepdims=True)))
            probs.append(p.astype(BF16))
        m_tile = jnp.zeros((BAND, LANES), F32)
        den_tile = jnp.ones((BAND, LANES), F32)
        for hp in range(npair):
            cs = slice(hp * LANES, (hp + 1) * LANES)
            m, den = stats[hp]
            o = _dot(probs[hp], vall[pl.ds(r0, 2 * BAND), cs])
            o = jnp.where(low_half, o[0:BAND], o[BAND:]) / jnp.where(low_half, den[0:BAND], den[BAND:])
            o_ref[0, pl.ds(r0, BAND), cs] = o.astype(BF16)
            for a in range(2):
                rows = slice(a * BAND, (a + 1) * BAND)
                m_tile = jnp.where(lane == 2 * hp + a, m[rows], m_tile)
                den_tile = jnp.where(lane == 2 * hp + a, den[rows], den_tile)
        lse_ref[0, pl.ds(r0, BAND), :] = (m_tile + jnp.log2(den_tile)) * LN2
        return carry

    lax.fori_loop(0, nsub, sub_block, 0, unroll=True)


def _attn_group(q, k, v, g, qb):
    B, L, _ = q.shape
    win, dil = GROUPS[g]
    assert win // dil == BAND
    qb = min(qb, L)
    assert L % qb == 0 and qb % BAND == 0
    nsub = qb // BAND
    cur = pl.BlockSpec((1, qb, GW), lambda b, r, n: (b, n, r))
    prev = pl.BlockSpec((1, BAND, GW), lambda b, r, n: (b, jnp.maximum(n * nsub - 1, 0), r))
    return pl.pallas_call(
        functools.partial(_attn_kernel, qb=qb),
        grid=(B, dil, L // qb),
        in_specs=[cur, cur, prev, cur, prev, pl.BlockSpec((2, BAND, 2 * BAND), lambda b, r, n: (0, 0, 0))],
        out_specs=[pl.BlockSpec((1, qb, GW), lambda b, r, n: (b, n, r)),
                   pl.BlockSpec((1, qb, LANES), lambda b, r, n: (b, n, r))],
        out_shape=[jax.ShapeDtypeStruct((B, L, dil * GW), BF16),
                   jax.ShapeDtypeStruct((B, L, dil * LANES), F32)],
        scratch_shapes=[pltpu.VMEM((qb + BAND, GW), BF16), pltpu.VMEM((qb + BAND, GW), BF16)],
        compiler_params=_cparams(("arbitrary", "arbitrary", "arbitrary")), name=f"attn_g{g}",
    )(q, k, k, v, v, _band_bias())


def _gla_kernel(gq_ref, gk_ref, gv_ref, la_ref, tri_ref, nw_ref, o_ref, sfin_ref, st_ref, *, nchunk):
    c = pl.program_id(1)

    @pl.when(c == 0)
    def _():
        st_ref[...] = jnp.zeros_like(st_ref)

    C = GLA_CHUNK
    npair = GLA_HEADS // 2
    tri3 = tri_ref[...]
    normw = nw_ref[...]
    hi, mid, lo = _split3(la_ref[0])
    b_chunks = []
    for ci in range(nchunk):
        rows = slice(ci * C, (ci + 1) * C)
        b_chunks.append(_dot(tri3, jnp.concatenate([hi[rows], mid[rows], lo[rows]], axis=0)))
    b = jnp.concatenate(b_chunks, axis=0)
    b_last = jnp.concatenate([jnp.broadcast_to(bc[C - 1:C, :], bc.shape) for bc in b_chunks], axis=0)
    gk = gk_ref[0].astype(F32)
    qe = gq_ref[0].astype(F32) * jnp.exp(b)
    ke = (gk * jnp.exp(-b)).astype(BF16)
    kd = (gk * jnp.exp(b_last - b)).astype(BF16)
    gv = gv_ref[0]

    low_half = lax.broadcasted_iota(jnp.int32, (C, LANES), 1) < GLA_DK
    ri = lax.broadcasted_iota(jnp.int32, (2 * C, 2 * C), 0)
    ki = lax.broadcasted_iota(jnp.int32, (2 * C, 2 * C), 1)
    pair_causal = jnp.logical_and((ri < C) == (ki < C),
                                  jnp.bitwise_and(ki, C - 1) <= jnp.bitwise_and(ri, C - 1))

    lhs, vrows, upd, dec = {}, {}, {}, {}
    for ci in range(nchunk):
        rows = slice(ci * C, (ci + 1) * C)
        for p in range(npair):
            cs = slice(p * LANES, (p + 1) * LANES)
            qe_p = qe[rows, cs]
            qm = jnp.concatenate([jnp.where(low_half, qe_p, 0.0), jnp.where(low_half, 0.0, qe_p)],
                                 axis=0).astype(BF16)
            ke_p = ke[rows, cs]
            sc = _dot_nt(qm, jnp.concatenate([ke_p, ke_p], axis=0))
            att = jnp.where(pair_causal, sc, 0.0).astype(BF16)
            lhs[ci, p] = jnp.concatenate([att, qm], axis=1)
            v_ab = gv[rows, 2 * p * GLA_DV:2 * (p + 1) * GLA_DV]
            vrows[ci, p] = jnp.concatenate([v_ab[:, 0:GLA_DV], v_ab[:, GLA_DV:]], axis=0)
            u = _dot_tn(kd[rows, cs], v_ab)
            upd[ci, p] = jnp.concatenate([u[0:GLA_DK, 0:GLA_DV], u[GLA_DK:, GLA_DV:]], axis=0)
            bl = jnp.broadcast_to(b_chunks[ci][C - 1:C, cs], (LANES, LANES))
            dec[ci, p] = jnp.exp(bl.T)

    outs = {}
    states = [st_ref[p] for p in range(npair)]
    for ci in range(nchunk):
        for p in range(npair):
            rhs = jnp.concatenate([vrows[ci, p], states[p].astype(BF16)], axis=0)
            outs[ci, p] = _dot(lhs[ci, p], rhs)
            states[p] = dec[ci, p] * states[p] + upd[ci, p]
    for p in range(npair):
        st_ref[p] = states[p]

    for ci in range(nchunk):
        for p in range(npair):
            o = outs[ci, p]
            o = (o * lax.rsqrt(jnp.mean(o * o, axis=-1, keepdims=True) + NORM_EPS) * normw).astype(BF16)
            for a in range(2):
                hh = 2 * p + a
                o_ref[0, ci * C:(ci + 1) * C, hh * GLA_DV:(hh + 1) * GLA_DV] = o[a * C:(a + 1) * C, :]

    @pl.when(c == pl.num_programs(1) - 1)
    def _():
        for p in range(npair):
            sfin_ref[0, p * LANES:(p + 1) * LANES, :] = states[p]


def _tri3(C, dtype=BF16):
    tri = np.tril(np.ones((C, C), np.float32))
    return jnp.asarray(np.concatenate([tri, tri, tri], axis=1), dtype)


def _gla_prompt(gq, gk, gv, la, normw, ct):
    B, S, _ = gq.shape
    ct = min(ct, S)
    assert S % ct == 0 and ct % GLA_CHUNK == 0
    row = lambda w: pl.BlockSpec((1, ct, w), lambda b, c: (b, c, 0))
    const = lambda shp: pl.BlockSpec(shp, lambda b, c: (0,) * len(shp))
    o, sfin = pl.pallas_call(
        functools.partial(_gla_kernel, nchunk=ct // GLA_CHUNK),
        grid=(B, S // ct),
        in_specs=[row(GLA_KW), row(GLA_KW), row(GLA_VW), row(GLA_KW),
                  const((GLA_CHUNK, 3 * GLA_CHUNK)), const((1, GLA_DV))],
        out_specs=[row(GLA_VW), pl.BlockSpec((1, GLA_KW, GLA_DV), lambda b, c: (b, 0, 0))],
        out_shape=[jax.ShapeDtypeStruct((B, S, GLA_VW), BF16),
                   jax.ShapeDtypeStruct((B, GLA_KW, GLA_DV), F32)],
        scratch_shapes=[pltpu.VMEM((GLA_HEADS // 2, GLA_DV, LANES), F32)],
        compiler_params=_cparams(("arbitrary", "arbitrary")), name="gla_prompt",
    )(gq, gk, gv, la, _tri3(GLA_CHUNK), normw)
    return o, sfin.reshape(B, GLA_HEADS, GLA_DK, GLA_DV)


def _merge_kernel(*refs, dils):
    n_groups = len(dils)
    x_ref = refs[0]
    o_refs = refs[1:1 + n_groups]
    lse_refs = refs[1 + n_groups:1 + 2 * n_groups] if n_groups > 1 else ()
    rest = refs[1 + n_groups + len(lse_refs):]
    (saz_ref, og_ref, sgz_ref, sa_ref, sb_ref, wa_ref, wg_ref, wo_ref, fnw_ref, ex_ref, y_ref) = rest[:11]
    scratch = rest[11:]
    tm = x_ref.shape[0]

    def by_position(ref, width, d, stage_ref):
        if d == 1:
            return lambda rows: ref[rows, :].astype(F32)
        planes = width // LANES
        for r in range(d):
            blk = ref[:, r * width:(r + 1) * width].astype(F32)
            for c in range(planes):
                stage_ref[c, pl.ds(r, tm // d, stride=d), :] = blk[:, c * LANES:(c + 1) * LANES]
        return lambda rows: jnp.concatenate([stage_ref[c, rows, :] for c in range(planes)], axis=1)

    if n_groups > 1:
        lse_rows = [by_position(lse_refs[g], LANES, dils[g], scratch[2 * g + 1]) for g in range(n_groups)]
        o_rows = [by_position(o_refs[g], GW, dils[g], scratch[2 * g]) for g in range(n_groups)]

    rc = min(tm, MERGE_ROWS)
    chunks = [slice(r0, r0 + rc) for r0 in range(0, tm, rc)]
    yb = [_dot((og_ref[rows, :].astype(F32) * sgz_ref[rows, :].astype(F32)).astype(BF16), wg_ref[...])
          for rows in chunks]
    if n_groups > 1:
        comb = []
        for rows in chunks:
            ls = [f(rows) for f in lse_rows]
            mx = functools.reduce(jnp.maximum, ls)
            es = [jnp.exp(l - mx) for l in ls]
            inv = 1.0 / functools.reduce(lambda a, b: a + b, es)
            acc = None
            for g in range(n_groups):
                w = es[g] * inv
                w_hi = w.astype(BF16)
                w_lo = (w - w_hi.astype(F32)).astype(BF16)
                wx = _dot(jnp.concatenate([w_hi, w_lo], axis=1), ex_ref[...])
                term = wx * o_rows[g](rows)
                acc = term if acc is None else acc + term
            comb.append(acc)
    else:
        comb = [o_refs[0][rows, :].astype(F32) for rows in chunks]
    ya = [_dot((c * saz_ref[rows, :].astype(F32)).astype(BF16), wa_ref[...]) for c, rows in zip(comb, chunks)]
    outs = []
    for a, b, rows in zip(ya, yb, chunks):
        mixed = (sa_ref[rows, :].astype(F32) * a + sb_ref[rows, :].astype(F32) * b).astype(BF16)
        outs.append(x_ref[rows, :] + _dot(mixed, wo_ref[...]))
    for out, rows in zip(outs, chunks):
        y_ref[rows, :] = (out * lax.rsqrt(jnp.mean(out * out, axis=-1, keepdims=True) + NORM_EPS)
                          * fnw_ref[...])


def _expand_matrix():
    e = np.zeros((LANES, GW), np.float32)
    for hh in range(HEADS):
        e[hh, hh * HEAD_DIM:(hh + 1) * HEAD_DIM] = 1.0
    return jnp.asarray(np.concatenate([e, e], axis=0), BF16)


def _merge(x, o_list, lse_list, dils, saz, og, sgz, sa, sb, wa, wg, wo, fnw, tm):
    R, D = x.shape
    tm = min(tm, R)
    assert R % tm == 0
    n_groups = len(o_list)
    row = lambda w: pl.BlockSpec((tm, w), lambda i: (i, 0))
    const = lambda shp: pl.BlockSpec(shp, lambda i: (0,) * len(shp))
    in_specs = [row(D)]
    in_specs += [pl.BlockSpec((tm // d, d * GW), lambda i: (i, 0)) for d in dils]
    in_specs += [pl.BlockSpec((tm // d, d * LANES), lambda i: (i, 0)) for d in dils[:len(lse_list)]]
    in_specs += [row(GW), row(GLA_VW), row(GLA_VW), row(D), row(D),
                 const((GW, D)), const((GLA_VW, D)), const((D, D)), const((1, D)), const((2 * LANES, GW))]
    scratch = []
    if n_groups > 1:
        for _ in dils:
            scratch += [pltpu.VMEM((GW // LANES, tm, LANES), F32), pltpu.VMEM((1, tm, LANES), F32)]
    return pl.pallas_call(
        functools.partial(_merge_kernel, dils=tuple(dils)),
        grid=(R // tm,), in_specs=in_specs, out_specs=row(D),
        out_shape=jax.ShapeDtypeStruct((R, D), F32), scratch_shapes=scratch,
        compiler_params=_cparams(("arbitrary",)), name=f"merge_g{n_groups}",
    )(x, *o_list, *lse_list, saz, og, sgz, sa, sb, wa, wg, wo, fnw, _expand_matrix())


def _sample_masks(T, w_lens):
    t_of_row = np.arange(HEADS * T) % T

    def bias(idx, g):
        win, dil = GROUPS[g]
        dd = w_lens[g] + t_of_row[:, None] - idx[None, :]
        ok = (dd >= 0) & (dd % dil == 0) & (dd // dil <= win // dil)
        return jnp.asarray(np.where(ok, 0.0, NEG).astype(np.float32))

    cache = [bias(np.arange(w_lens[g]), g) for g in range(N_GROUPS)]
    new = [bias(w_lens[g] + np.arange(T), g) for g in range(N_GROUPS)]
    return cache, new


def _decode_scores_pieces(in_refs, out_refs, dbs):
    G = N_GROUPS
    sq_refs, sc_refs, sb_refs, bd_ref = in_refs[0:G], in_refs[G:2 * G], in_refs[2 * G:3 * G], in_refs[3 * G]

    def piece(i, g):
        qbd = (jnp.tile(sq_refs[g][i].astype(F32), (HEADS, 1)) * bd_ref[...]).astype(BF16)
        out_refs[g][i] = _dot(qbd, sc_refs[g][i].astype(BF16)) + sb_refs[g][...]

    return [functools.partial(piece, i, g) for i in range(dbs) for g in range(G)]


def _decode_mix_pieces(in_refs, out_refs, dbs):
    G = N_GROUPS
    q_refs, new_refs, s_refs = in_refs[0:G], in_refs[G:2 * G], in_refs[2 * G:3 * G]
    c_refs, bn_refs, bd_ref = in_refs[3 * G:4 * G], in_refs[4 * G:5 * G], in_refs[5 * G]
    o_ref = out_refs[0]
    T = q_refs[0].shape[1]
    p_old, acc = {}, {}

    def prologue(i):
        bd = bd_ref[...]
        s_new, v_new = [], []
        for g in range(G):
            qbd = jnp.tile(q_refs[g][i].astype(F32), (HEADS, 1)) * bd
            s_new.append(_dot_nt(qbd, _rb(new_refs[g][i, :, 0:GW])) + bn_refs[g][...])
            v_new.append(_rb(new_refs[g][i, :, GW:2 * GW]))
        s_old = [s_refs[g][i] for g in range(G)]
        m = functools.reduce(jnp.maximum, [jnp.max(s, axis=1, keepdims=True) for s in s_new + s_old])
        e_new = [jnp.exp(s - m) for s in s_new]
        e_old = [jnp.exp(s - m) for s in s_old]
        den = functools.reduce(lambda a, b: a + b, [jnp.sum(p, axis=1, keepdims=True) for p in e_new + e_old])
        inv = 1.0 / den
        for g in range(G):
            p_old[i, g] = e_old[g] * inv
            r = _dot(_rb(e_new[g] * inv), v_new[g])
            acc[i] = r if g == 0 else acc[i] + r

    def piece(i, g):
        acc[i] = acc[i] + _dot_nt(p_old[i, g], c_refs[g][i])

    def epilogue(i):
        a = acc[i] * bd_ref[...]
        out = a[0:T, :]
        for hh in range(1, HEADS):
            out = out + a[hh * T:(hh + 1) * T, :]
        o_ref[i] = out

    thunks = []
    for i in range(dbs):
        thunks += [functools.partial(prologue, i)] + [functools.partial(piece, i, g) for g in range(G)]
        thunks += [functools.partial(epilogue, i)]
    return thunks


def _head_diag(T):
    bd = np.zeros((HEADS * T, GW), np.float32)
    for hh in range(HEADS):
        bd[hh * T:(hh + 1) * T, hh * HEAD_DIM:(hh + 1) * HEAD_DIM] = 1.0
    return jnp.asarray(bd)


def _rider_common(DB, nsteps, step_of):
    dbs = -(-DB // nsteps)
    assert DB % dbs == 0
    active = DB // dbs
    blk = lambda b, s: jnp.minimum(step_of(b, s), active - 1)
    per = lambda shp, part=0: pl.BlockSpec((dbs,) + shp, lambda b, s: (blk(b, s), part) + (0,) * (len(shp) - 1))
    const = lambda a: pl.BlockSpec(a.shape, lambda b, s: (0,) * a.ndim)
    return dbs, active, per, const


def _scores_rider(q_list, caches_t):
    DB, T, _ = q_list[0].shape
    w_lens = [c.shape[2] for c in caches_t]
    bias_c, _ = _sample_masks(T, w_lens)

    def make(nsteps, step_of):
        dbs, active, per, const = _rider_common(DB, nsteps, step_of)
        return dict(
            pieces=_decode_scores_pieces, dbs=dbs, active=active,
            arrays=list(q_list) + list(caches_t) + bias_c + [_head_diag(T)],
            in_specs=[per((T, GW)) for _ in q_list] + [per((GW, w), 0) for w in w_lens]
                     + [const(a) for a in bias_c] + [pl.BlockSpec((HEADS * T, GW), lambda b, s: (0, 0))],
            out_shape=[jax.ShapeDtypeStruct((DB, HEADS * T, w), F32) for w in w_lens],
            out_specs=[per((HEADS * T, w)) for w in w_lens])
    return make


def _mix_rider(q_list, kv_new, scores, caches_t):
    DB, T, _ = q_list[0].shape
    w_lens = [c.shape[2] for c in caches_t]
    _, bias_n = _sample_masks(T, w_lens)

    def make(nsteps, step_of):
        dbs, active, per, const = _rider_common(DB, nsteps, step_of)
        return dict(
            pieces=_decode_mix_pieces, dbs=dbs, active=active,
            arrays=list(q_list) + list(kv_new) + list(scores) + list(caches_t) + bias_n + [_head_diag(T)],
            in_specs=[per((T, GW)) for _ in q_list] + [per((T, 2 * GW)) for _ in kv_new]
                     + [per((HEADS * T, w)) for w in w_lens] + [per((GW, w), 1) for w in w_lens]
                     + [const(a) for a in bias_n] + [pl.BlockSpec((HEADS * T, GW), lambda b, s: (0, 0))],
            out_shape=[jax.ShapeDtypeStruct((DB, T, GW), F32)],
            out_specs=[per((T, GW))])
    return make


def _sgla_kernel(gq_ref, gk_ref, gv_ref, la_ref, st_ref, tri_ref, nw_ref, o_ref, snew_ref, *, dbt):
    tri3 = tri_ref[...]
    normw = nw_ref[...]
    T = gq_ref.shape[1]
    ti = lax.broadcasted_iota(jnp.int32, (T, T), 0)
    si = lax.broadcasted_iota(jnp.int32, (T, T), 1)
    causal = si <= ti
    low_half = lax.broadcasted_iota(jnp.int32, (T, LANES), 1) < GLA_DK
    ones = jnp.ones((3 * T, LANES), F32)
    row_low = lax.broadcasted_iota(jnp.int32, (LANES, GLA_DV), 0) < GLA_DK

    entries = range(dbt)
    pairs = range(GLA_HEADS // 2)
    cs = [slice(p * LANES, (p + 1) * LANES) for p in pairs]
    la3 = {i: jnp.concatenate(_split3(la_ref[i], F32), axis=0) for i in entries}
    b = {i: _dot(tri3, la3[i]) for i in entries}
    dec = {(i, p): jnp.exp(_dot_tn(la3[i][:, cs[p]], ones)) for i in entries for p in pairs}
    qe, ke, kd, gv = {}, {}, {}, {}
    for i in entries:
        b_last = b[i][T - 1:T, :]
        gk = gk_ref[i].astype(F32)
        qe[i] = gq_ref[i].astype(F32) * jnp.exp(b[i])
        ke[i] = _rb(gk * jnp.exp(-b[i]))
        kd[i] = _rb(gk * jnp.exp(b_last - b[i]))
        gv[i] = gv_ref[i].astype(F32)
    heads = [(i, p, a) for i in entries for p in pairs for a in range(2)]
    qm = {(i, p, a): _rb(jnp.where(low_half if a == 0 else jnp.logical_not(low_half), qe[i][:, cs[p]], 0.0))
          for (i, p, a) in heads}
    att = {(i, p, a): _rb(jnp.where(causal, _dot_nt(qm[i, p, a], ke[i][:, cs[p]]), 0.0)) for (i, p, a) in heads}
    v_h = {(i, p, a): gv[i][:, (2 * p + a) * GLA_DV:(2 * p + a + 1) * GLA_DV] for (i, p, a) in heads}
    o = {(i, p, a): _dot(att[i, p, a], v_h[i, p, a]) + _dot(qm[i, p, a], _rb(st_ref[i, cs[p], :]))
         for (i, p, a) in heads}
    u = {(i, p, a): _dot_tn(kd[i][:, cs[p]], v_h[i, p, a]) for (i, p, a) in heads}
    for i in entries:
        for p in pairs:
            upd = jnp.where(row_low, u[i, p, 0], u[i, p, 1])
            snew_ref[i, cs[p], :] = st_ref[i, cs[p], :] * dec[i, p] + upd
        outs = [o[i, p, a] * lax.rsqrt(jnp.mean(o[i, p, a] * o[i, p, a], axis=-1, keepdims=True) + NORM_EPS)
                * normw for p in pairs for a in range(2)]
        o_ref[i] = jnp.concatenate(outs, axis=1)


def _sample_gla(gq, gk, gv, la, state, normw, dbt):
    DB, T, _ = gq.shape
    dbt = min(dbt, DB)
    assert DB % dbt == 0
    st = state.reshape(DB, GLA_KW, GLA_DV)
    blk = lambda shp: pl.BlockSpec((dbt,) + shp, lambda i: (i,) + (0,) * len(shp))
    const = lambda shp: pl.BlockSpec(shp, lambda i: (0,) * len(shp))
    o, snew = pl.pallas_call(
        functools.partial(_sgla_kernel, dbt=dbt),
        grid=(DB // dbt,),
        in_specs=[blk((T, GLA_KW)), blk((T, GLA_KW)), blk((T, GLA_VW)), blk((T, GLA_KW)),
                  blk((GLA_KW, GLA_DV)), const((T, 3 * T)), const((1, GLA_DV))],
        out_specs=[blk((T, GLA_VW)), blk((GLA_KW, GLA_DV))],
        out_shape=[jax.ShapeDtypeStruct((DB, T, GLA_VW), F32),
                   jax.ShapeDtypeStruct((DB, GLA_KW, GLA_DV), F32)],
        compiler_params=_cparams(("arbitrary",)), name="gla_sample",
    )(gq, gk, gv, la, st, _tri3(T, F32), normw)
    return o, snew.reshape(DB, GLA_HEADS, GLA_DK, GLA_DV)


def _rope_tables(pos):
    half = ROPE_DIM // 2
    inv_freq = 1.0 / (ROPE_THETA ** (jnp.arange(half, dtype=F32) * (2.0 / ROPE_DIM)))
    ang = pos.astype(F32)[:, None] * inv_freq[None, :]
    cos, sin = jnp.cos(ang), jnp.sin(ang)
    n = pos.shape[0]
    pad = jnp.zeros((n, HEAD_DIM - ROPE_DIM), F32)
    zero = jnp.zeros((n, half), F32)
    c = jnp.concatenate([cos, cos, pad + 1.0], axis=1)
    s1 = jnp.concatenate([zero, sin, pad], axis=1)
    s2 = jnp.concatenate([-sin, zero, pad], axis=1)
    rep = LANES // HEAD_DIM
    return tuple(jnp.tile(t, (1, rep)) for t in (c, s1, s2))


def kernel(x_prompt, x_sample, cache_kv_w128, cache_kv_w512, cache_kv_w2048, state_gla,
           ln_w, w_in, w_gla_a2, b_gla_a, gla_norm_w, w_attn_out, w_gla_out, w_out, final_norm_w):
    B, S, D = x_prompt.shape
    DB, T, _ = x_sample.shape
    assert ln_w.shape[0] == 1, "single-layer step"
    caches = (cache_kv_w128, cache_kv_w512, cache_kv_w2048)
    dils = [d for _, d in GROUPS]

    lnw = ln_w[0].reshape(1, D)
    w_t = w_in[0].T.astype(BF16)
    assert PROJ_W <= 2 * OFF_AZ
    wa2 = w_gla_a2[0].astype(BF16)
    ba = b_gla_a[0].reshape(1, GLA_KW)
    gnw = gla_norm_w[0].reshape(1, GLA_DV)
    wa = w_attn_out[0].astype(BF16)
    wg = w_gla_out[0].astype(BF16)
    wo = w_out[0].astype(BF16)
    fnw = final_norm_w.reshape(1, D)


    R = DB * T
    xs = x_sample.reshape(1, R, D)
    pos_s = PAST_LEN + jnp.arange(T, dtype=jnp.int32)
    tabs_s = tuple(jnp.tile(t, (DB, 1)) for t in _rope_tables(pos_s))
    qkv_s = _proj_qkv(xs, lnw, w_t, tabs_s, min(PROJ_ROWS, R), (1,) * N_GROUPS, row_major_kv=True,
                      q_scale=HEAD_DIM ** -0.5)
    rest_s = _proj_rest(xs, lnw, w_t, wa2, ba, min(PROJ_ROWS, R))
    per_db = lambda t: t.reshape(DB, T, t.shape[-1])
    q_s = [per_db(q) for q in qkv_s[0:3]]
    kv_new = [per_db(t) for t in qkv_s[9:12]]
    caches_t = [c[0].transpose(0, 2, 3, 4, 1).reshape(DB, 2 * GW, c.shape[2]) for c in caches]

    tabs_p = _rope_tables(jnp.arange(S, dtype=jnp.int32))
    qkv = _proj_qkv(x_prompt, lnw, w_t, tabs_p, min(PROJ_ROWS, S), dils, row_major_kv=False,
                    q_scale=HEAD_DIM ** -0.5 * LOG2E,
                    make_rider=_scores_rider(q_s, caches_t))
    scores = qkv[9:12]
    rest = _proj_rest(x_prompt, lnw, w_t, wa2, ba, min(PROJ_ROWS, S),
                      make_rider=_mix_rider(q_s, kv_new, scores, caches_t))
    saz, sgz, sa, sb, gq, gk, gv, la, o_s = rest
    o_list, lse_list = [], []
    for g in range(N_GROUPS):
        o, lse = _attn_group(qkv[g], qkv[3 + g], qkv[6 + g], g, qb=ATTN_ROWS)
        o_list.append(o.reshape(-1, o.shape[-1]))
        lse_list.append(lse.reshape(-1, lse.shape[-1]))
    og, gla_p = _gla_prompt(gq, gk, gv, la, gnw, ct=GLA_TOKENS)
    flat = lambda t: t.reshape(B * S, t.shape[-1])
    y_prompt = _merge(flat(x_prompt), o_list, lse_list, dils, flat(saz), flat(og), flat(sgz), flat(sa),
                      flat(sb), wa, wg, wo, fnw, tm=MERGE_TILE).reshape(B, S, D)
    kvt = _kv_tail_prompt(x_prompt, lnw, w_t, [min(w, S) for w, _ in GROUPS], tile=KV_TAIL_COLS)
    kv_p = [t.reshape(B, 2, HEADS, HEAD_DIM, t.shape[-1]).transpose(0, 4, 1, 2, 3)[None] for t in kvt]

    saz, sgz, sa, sb, gq, gk, gv, la = rest_s
    og_s, gla_s = _sample_gla(per_db(gq), per_db(gk), per_db(gv), per_db(la), state_gla[0], gnw, dbt=SAMPLE_GLA_ENTRIES)
    flat_s = lambda t: t.reshape(R, t.shape[-1])
    y_sample = _merge(x_sample.reshape(R, D), [o_s.reshape(R, GW)], [], (1,), flat_s(saz), flat_s(og_s),
                      flat_s(sgz), flat_s(sa), flat_s(sb), wa, wg, wo, fnw, tm=MERGE_TILE).reshape(DB, T, D)
    kvt_s = _kv_tail_sample(x_sample, lnw, w_t, pos_s)
    kv_s = [t.reshape(T, 2, HEADS, HEAD_DIM, DB).transpose(4, 0, 1, 2, 3)[None] for t in kvt_s]

    return (y_prompt, y_sample, kv_p[0], kv_p[1], kv_p[2], gla_p[None],
            kv_s[0], kv_s[1], kv_s[2], gla_s[None])
```

```python
import functools

import numpy as np
import jax
import jax.numpy as jnp
from jax import lax
from jax.experimental import pallas as pl
from jax.experimental.pallas import tpu as pltpu

F32 = jnp.float32
BF16 = jnp.bfloat16

D_MODEL = 1024
HEAD_DIM = 64
HEADS = 8
GROUPS = ((128, 1), (512, 4), (2048, 16))
N_GROUPS = len(GROUPS)
GW = HEADS * HEAD_DIM
QKV_W = N_GROUPS * GW
ROPE_DIM = HEAD_DIM // 4
ROPE_THETA = 500000.0
BAND = 128
GLA_HEADS = 4
GLA_DK = 64
GLA_DV = 128
GLA_KW = GLA_HEADS * GLA_DK
GLA_VW = GLA_HEADS * GLA_DV
GATE_RANK = 16
GLA_TAU = 16.0
GLA_CHUNK = 64
NORM_EPS = 1e-6
PAST_LEN = 8192

LANES = 128
NEG = -1e30
LOG2E = float(np.log2(np.e))
LN2 = float(np.log(2.0))

OFF_Q = 0
OFF_K = OFF_Q + QKV_W
OFF_V = OFF_K + QKV_W
OFF_AZ = OFF_V + QKV_W
OFF_GQ = OFF_AZ + GW
OFF_GK = OFF_GQ + GLA_KW
OFF_GV = OFF_GK + GLA_KW
OFF_GZ = OFF_GV + GLA_VW
OFF_LR = OFF_GZ + GLA_VW
OFF_MA = OFF_LR + GATE_RANK
OFF_MB = OFF_MA + D_MODEL
PROJ_W = OFF_MB + D_MODEL
assert all(o % 16 == 0 for o in (OFF_LR, OFF_MA, OFF_MB, PROJ_W))

VMEM_LIMIT = 56 * 1024 * 1024

PROJ_ROWS = 512
ATTN_ROWS = 2048
GLA_TOKENS = 2048
KV_TAIL_COLS = 2048
MERGE_TILE = 1024
MERGE_ROWS = 512
SAMPLE_GLA_ENTRIES = 32

def _cparams(sem):
    return pltpu.CompilerParams(dimension_semantics=sem, vmem_limit_bytes=VMEM_LIMIT)


def _dot(a, b):
    return jnp.dot(a, b, preferred_element_type=F32)


def _dot_nt(a, b):
    return lax.dot_general(a, b, (((1,), (1,)), ((), ())), preferred_element_type=F32)


def _dot_tn(a, b):
    return lax.dot_general(a, b, (((0,), (0,)), ((), ())), preferred_element_type=F32)


def _rb(x):
    return x.astype(BF16).astype(F32)


def _split3(x, dtype=BF16):
    hi = x.astype(BF16)
    r1 = x - hi.astype(F32)
    mid = r1.astype(BF16)
    lo = (r1 - mid.astype(F32)).astype(BF16)
    return hi.astype(dtype), mid.astype(dtype), lo.astype(dtype)


def _sigmoid(x):
    return 1.0 / (1.0 + jnp.exp(-x))


def _log_sigmoid(x):
    return jnp.minimum(x, 0.0) - jnp.log1p(jnp.exp(-jnp.abs(x)))


def _rms_bf16(x, w):
    ms = jnp.mean(x * x, axis=-1, keepdims=True)
    return (x * lax.rsqrt(ms + NORM_EPS) * w).astype(BF16)


def _run_rider(rider, in_refs, out_refs):
    if rider is None:
        return
    pieces, active, dbs, nsteps = rider

    def emit():
        for thunk in pieces(in_refs, out_refs, dbs):
            thunk()

    if active == nsteps:
        emit()
    else:
        step = pl.program_id(0) * pl.num_programs(1) + pl.program_id(1)
        pl.when(step < active)(emit)


def _proj_qkv_kernel(x_ref, lnw_ref, w_ref, rc_ref, rs1_ref, rs2_ref, *refs,
                     dils, row_major_kv, q_scale, rider, n_rider_in):
    rider_in, refs = refs[:n_rider_in], refs[n_rider_in:]
    q_refs, k_refs, v_refs = refs[0:3], refs[3:6], refs[6:9]
    kv_refs = refs[9:12] if row_major_kv else ()
    rider_out = refs[9 + len(kv_refs):-1]
    stage_ref = refs[-1]
    h = _rms_bf16(x_ref[0], lnw_ref[...])
    tm = h.shape[0]
    rc, rs1, rs2 = rc_ref[...], rs1_ref[...], rs2_ref[...]

    def mm(off, width):
        return _dot_nt(h, w_ref[off:off + width, :])

    def rope(t):
        outs = []
        for j in range(t.shape[1] // LANES):
            c = t[:, j * LANES:(j + 1) * LANES]
            outs.append(c * rc + pltpu.roll(c, ROPE_DIM // 2, axis=1) * rs1
                        + pltpu.roll(c, LANES - ROPE_DIM // 2, axis=1) * rs2)
        return jnp.concatenate(outs, axis=1)

    def store_by_residue(dst_ref, val, d):
        if d == 1:
            dst_ref[0] = val.astype(BF16)
            return
        for c in range(GW // LANES):
            stage_ref[c] = val[:, c * LANES:(c + 1) * LANES]
        for r in range(d):
            for c in range(GW // LANES):
                col = r * GW + c * LANES
                dst_ref[0, :, col:col + LANES] = stage_ref[c, pl.ds(r, tm // d, stride=d), :].astype(BF16)

    for g in range(N_GROUPS):
        qr = rope(mm(OFF_Q + g * GW, GW)) * q_scale
        store_by_residue(q_refs[g], qr, dils[g])
        kr = rope(mm(OFF_K + g * GW, GW))
        store_by_residue(k_refs[g], kr, dils[g])
        vv = mm(OFF_V + g * GW, GW)
        store_by_residue(v_refs[g], vv, dils[g])
        if row_major_kv:
            kv_refs[g][0, :, 0:GW] = kr
            kv_refs[g][0, :, GW:2 * GW] = vv
    _run_rider(rider, rider_in, rider_out)


def _proj_rest_kernel(x_ref, lnw_ref, w_ref, wa2_ref, ba_ref, *refs, rider, n_rider_in):
    rider_in, refs = refs[:n_rider_in], refs[n_rider_in:]
    saz_ref, sgz_ref, sa_ref, sb_ref, gq_ref, gk_ref, gv_ref, la_ref = refs[0:8]
    rider_out = refs[8:]
    h = _rms_bf16(x_ref[0], lnw_ref[...])

    def mm(off, width):
        return _dot_nt(h, w_ref[off - OFF_AZ:off - OFF_AZ + width, :])

    az = mm(OFF_AZ, GW)
    saz_ref[0] = (az * _sigmoid(az)).astype(BF16)
    gz = mm(OFF_GZ, GLA_VW)
    sgz_ref[0] = (gz * _sigmoid(gz)).astype(BF16)
    for j in range(D_MODEL // GW):
        sa_ref[0, :, j * GW:(j + 1) * GW] = _sigmoid(mm(OFF_MA + j * GW, GW)).astype(BF16)
        sb_ref[0, :, j * GW:(j + 1) * GW] = _sigmoid(mm(OFF_MB + j * GW, GW)).astype(BF16)
    gq_ref[0] = (mm(OFF_GQ, GLA_KW) * (GLA_DK ** -0.5)).astype(BF16)
    gk_ref[0] = mm(OFF_GK, GLA_KW).astype(BF16)
    gv_ref[0] = mm(OFF_GV, GLA_VW).astype(BF16)
    glr = mm(OFF_LR, GATE_RANK)
    gate_pre = _dot(glr.astype(BF16), wa2_ref[...]) + ba_ref[...]
    la_ref[0] = _log_sigmoid(gate_pre) * (1.0 / GLA_TAU)
    _run_rider(rider, rider_in, rider_out)


def _rider_parts(make_rider, B, nt):
    if make_rider is None:
        return None, [], [], [], []
    r = make_rider(B * nt, lambda b, s: b * nt + s)
    return (r["pieces"], r["active"], r["dbs"], B * nt), r["arrays"], r["in_specs"], r["out_shape"], r["out_specs"]


def _proj_qkv(x, lnw, w_t, rope_tabs, tm, dils, row_major_kv, q_scale, make_rider=None):
    B, S, D = x.shape
    assert S % tm == 0 and all(tm % (16 * d) == 0 for d in dils)
    row = lambda w: pl.BlockSpec((1, tm, w), lambda b, s: (b, s, 0))
    tab = pl.BlockSpec((tm, LANES), lambda b, s: (s, 0))
    rider, r_arrays, r_in_specs, r_out_shape, r_out_specs = _rider_parts(make_rider, B, S // tm)
    out_shape, out_specs = [], []
    for _ in range(3):
        for d in dils:
            out_shape.append(jax.ShapeDtypeStruct((B, S // d, d * GW), BF16))
            out_specs.append(pl.BlockSpec((1, tm // d, d * GW), lambda b, s: (b, s, 0)))
    if row_major_kv:
        out_shape += [jax.ShapeDtypeStruct((B, S, 2 * GW), F32)] * N_GROUPS
        out_specs += [row(2 * GW)] * N_GROUPS
    in_specs = [row(D), pl.BlockSpec((1, D), lambda b, s: (0, 0)),
                pl.BlockSpec((OFF_AZ, D), lambda b, s: (0, 0), pipeline_mode=pl.Buffered(1)),
                tab, tab, tab]
    return pl.pallas_call(
        functools.partial(_proj_qkv_kernel, dils=tuple(dils), row_major_kv=row_major_kv, q_scale=q_scale,
                          rider=rider, n_rider_in=len(r_arrays)),
        grid=(B, S // tm), in_specs=in_specs + r_in_specs, out_specs=out_specs + r_out_specs,
        out_shape=out_shape + r_out_shape,
        scratch_shapes=[pltpu.VMEM((GW // LANES, tm, LANES), F32)],
        compiler_params=_cparams(("arbitrary", "arbitrary")), name="proj_qkv",
    )(x, lnw, w_t, *rope_tabs, *r_arrays)


def _proj_rest(x, lnw, w_t, wa2, ba, tm, make_rider=None):
    B, S, D = x.shape
    assert S % tm == 0
    row = lambda w: pl.BlockSpec((1, tm, w), lambda b, s: (b, s, 0))
    const = lambda shp: pl.BlockSpec(shp, lambda b, s: (0,) * len(shp))
    bf = lambda w: jax.ShapeDtypeStruct((B, S, w), BF16)
    rider, r_arrays, r_in_specs, r_out_shape, r_out_specs = _rider_parts(make_rider, B, S // tm)
    out_shape = [bf(GW), bf(GLA_VW), bf(D_MODEL), bf(D_MODEL), bf(GLA_KW), bf(GLA_KW), bf(GLA_VW),
                 jax.ShapeDtypeStruct((B, S, GLA_KW), F32)]
    out_specs = [row(GW), row(GLA_VW), row(D_MODEL), row(D_MODEL), row(GLA_KW), row(GLA_KW), row(GLA_VW),
                 row(GLA_KW)]
    in_specs = [row(D), const((1, D)),
                pl.BlockSpec((OFF_AZ, D), lambda b, s: (1, 0), pipeline_mode=pl.Buffered(1)),
                const((GATE_RANK, GLA_KW)), const((1, GLA_KW))]
    return pl.pallas_call(
        functools.partial(_proj_rest_kernel, rider=rider, n_rider_in=len(r_arrays)),
        grid=(B, S // tm), in_specs=in_specs + r_in_specs, out_specs=out_specs + r_out_specs,
        out_shape=out_shape + r_out_shape,
        compiler_params=_cparams(("arbitrary", "arbitrary")), name="proj_rest",
    )(x, lnw, w_t, wa2, ba, *r_arrays)


def _kvt_kernel(x_ref, lnw_ref, wk_ref, wv_ref, cos_ref, sin_ref, o0_ref, o1_ref, o2_ref, *, plan, nt):
    i = pl.program_id(1)
    h = _rms_bf16(x_ref[...], lnw_ref[...])
    rows = h.shape[0]
    half = ROPE_DIM // 2
    for g, (tiles, cols) in enumerate(plan):
        o_ref = (o0_ref, o1_ref, o2_ref)[g]

        def emit(g=g, cols=cols, o_ref=o_ref):
            hh = h[rows - cols:, :]
            y = _dot_nt(wk_ref[g * GW:(g + 1) * GW, :], hh)
            c = cos_ref[:, rows - cols:]
            s = sin_ref[:, rows - cols:]
            for hd in range(HEADS):
                b0 = hd * HEAD_DIM
                x1 = y[b0:b0 + half, :]
                x2 = y[b0 + half:b0 + ROPE_DIM, :]
                o_ref[b0:b0 + half, :] = x1 * c - x2 * s
                o_ref[b0 + half:b0 + ROPE_DIM, :] = x2 * c + x1 * s
                o_ref[b0 + ROPE_DIM:b0 + HEAD_DIM, :] = y[b0 + ROPE_DIM:b0 + HEAD_DIM, :]
            o_ref[GW:2 * GW, :] = _dot_nt(wv_ref[g * GW:(g + 1) * GW, :], hh)

        if tiles == nt:
            emit()
        else:
            pl.when(i >= nt - tiles)(emit)


def _rope_cos_sin_t(pos):
    half = ROPE_DIM // 2
    inv_freq = 1.0 / (ROPE_THETA ** (jnp.arange(half, dtype=F32) * (2.0 / ROPE_DIM)))
    ang = pos.astype(F32)[:, None] * inv_freq[None, :]
    return jnp.cos(ang).T, jnp.sin(ang).T


def _kv_tail_prompt(x, lnw, w_t, S_tails, tile):
    B, S, D = x.shape
    tile = min(tile, S)
    span = max(S_tails)
    assert span % tile == 0 and S % tile == 0
    nt = span // tile
    first = (S - span) // tile
    plan = []
    for t in S_tails:
        assert t % tile == 0 or t < tile
        plan.append((t // tile, tile) if t >= tile else (1, t))
    cos_t, sin_t = _rope_cos_sin_t(jnp.arange(S, dtype=jnp.int32))
    tabspec = pl.BlockSpec((ROPE_DIM // 2, tile), lambda b, i: (0, first + i))
    out_specs = [pl.BlockSpec((None, 2 * GW, cols), lambda b, i, t=tiles: (b, 0, jnp.maximum(i - (nt - t), 0)))
                 for tiles, cols in plan]
    return pl.pallas_call(
        functools.partial(_kvt_kernel, plan=tuple(plan), nt=nt),
        grid=(B, nt),
        in_specs=[pl.BlockSpec((None, tile, D), lambda b, i: (b, first + i, 0)),
                  pl.BlockSpec((1, D), lambda b, i: (0, 0)),
                  pl.BlockSpec((QKV_W, D), lambda b, i: (OFF_K // QKV_W, 0), pipeline_mode=pl.Buffered(1)),
                  pl.BlockSpec((QKV_W, D), lambda b, i: (OFF_V // QKV_W, 0), pipeline_mode=pl.Buffered(1)),
                  tabspec, tabspec],
        out_specs=out_specs,
        out_shape=[jax.ShapeDtypeStruct((B, 2 * GW, t), F32) for t in S_tails],
        compiler_params=_cparams(("arbitrary", "arbitrary")), name="kv_tail_prompt",
    )(x, lnw, w_t, w_t, cos_t, sin_t)


def _kv_tail_sample(x, lnw, w_t, pos):
    DB, T, D = x.shape
    cos_t, sin_t = _rope_cos_sin_t(pos)
    bcast = lambda t: jnp.broadcast_to(t.T[:, :, None], (T, ROPE_DIM // 2, DB))
    tabspec = pl.BlockSpec((None, ROPE_DIM // 2, DB), lambda b, i: (i, 0, 0))
    out_spec = pl.BlockSpec((None, 2 * GW, DB), lambda b, i: (i, 0, 0))
    return pl.pallas_call(
        functools.partial(_kvt_kernel, plan=((T, DB),) * N_GROUPS, nt=T),
        grid=(1, T),
        in_specs=[pl.BlockSpec((None, DB, D), lambda b, i: (i, 0, 0)),
                  pl.BlockSpec((1, D), lambda b, i: (0, 0)),
                  pl.BlockSpec((QKV_W, D), lambda b, i: (OFF_K // QKV_W, 0), pipeline_mode=pl.Buffered(1)),
                  pl.BlockSpec((QKV_W, D), lambda b, i: (OFF_V // QKV_W, 0), pipeline_mode=pl.Buffered(1)),
                  tabspec, tabspec],
        out_specs=[out_spec] * N_GROUPS,
        out_shape=[jax.ShapeDtypeStruct((T, 2 * GW, DB), F32)] * N_GROUPS,
        compiler_params=_cparams(("arbitrary", "arbitrary")), name="kv_tail_sample",
    )(x.transpose(1, 0, 2), lnw, w_t, w_t, bcast(cos_t), bcast(sin_t))


def _band_bias():
    q = np.arange(BAND)[:, None]
    c = np.arange(2 * BAND)[None, :]
    ok = np.where(c < BAND, c >= q, (c - BAND) <= q)
    ok_first = ok & (c >= BAND)
    return jnp.asarray(np.stack([np.where(ok, 0.0, NEG), np.where(ok_first, 0.0, NEG)]).astype(np.float32))


def _attn_kernel(q_ref, k_ref, kp_ref, v_ref, vp_ref, bias_ref, o_ref, lse_ref, kall, vall, *, qb):
    n = pl.program_id(2)
    nsub = qb // BAND
    npair = HEADS // 2
    kall[0:BAND, :] = kp_ref[0]
    kall[BAND:, :] = k_ref[0]
    vall[0:BAND, :] = vp_ref[0]
    vall[BAND:, :] = v_ref[0]
    lane = lax.broadcasted_iota(jnp.int32, (BAND, LANES), 1)
    low_half = lane < HEAD_DIM

    def sub_block(j, carry):
        r0 = pl.multiple_of(j * BAND, BAND)
        first = (n * nsub + j) == 0
        bias = bias_ref[jnp.where(first, 1, 0)]
        bias2 = jnp.concatenate([bias, bias], axis=0)
        scores = []
        for hp in range(npair):
            cs = slice(hp * LANES, (hp + 1) * LANES)
            qp = q_ref[0, pl.ds(r0, BAND), cs]
            zero = jnp.zeros_like(qp)
            qm = jnp.concatenate([jnp.where(low_half, qp, zero), jnp.where(low_half, zero, qp)], axis=0)
            scores.append(_dot_nt(qm, kall[pl.ds(r0, 2 * BAND), cs]) + bias2)
        probs, stats = [], []
        for s in scores:
            m = jnp.max(s, axis=1, keepdims=True)
            p = jnp.exp2(s - m)
            stats.append((m, jnp.sum(p, axis=1, keepdims=True)))
            probs.append(p.astype(BF16))
        m_tile = jnp.zeros((BAND, LANES), F32)
        den_tile = jnp.ones((BAND, LANES), F32)
        for hp in range(npair):
            cs = slice(hp * LANES, (hp + 1) * LANES)
            m, den = stats[hp]
            o = _dot(probs[hp], vall[pl.ds(r0, 2 * BAND), cs])
            o = jnp.where(low_half, o[0:BAND], o[BAND:]) / jnp.where(low_half, den[0:BAND], den[BAND:])
            o_ref[0, pl.ds(r0, BAND), cs] = o.astype(BF16)
            for a in range(2):
                rows = slice(a * BAND, (a + 1) * BAND)
                m_tile = jnp.where(lane == 2 * hp + a, m[rows], m_tile)
                den_tile = jnp.where(lane == 2 * hp + a, den[rows], den_tile)
        lse_ref[0, pl.ds(r0, BAND), :] = (m_tile + jnp.log2(den_tile)) * LN2
        return carry

    lax.fori_loop(0, nsub, sub_block, 0, unroll=True)


def _attn_group(q, k, v, g, qb):
    B, L, _ = q.shape
    win, dil = GROUPS[g]
    assert win // dil == BAND
    qb = min(qb, L)
    assert L % qb == 0 and qb % BAND == 0
    nsub = qb // BAND
    cur = pl.BlockSpec((1, qb, GW), lambda b, r, n: (b, n, r))
    prev = pl.BlockSpec((1, BAND, GW), lambda b, r, n: (b, jnp.maximum(n * nsub - 1, 0), r))
    return pl.pallas_call(
        functools.partial(_attn_kernel, qb=qb),
        grid=(B, dil, L // qb),
        in_specs=[cur, cur, prev, cur, prev, pl.BlockSpec((2, BAND, 2 * BAND), lambda b, r, n: (0, 0, 0))],
        out_specs=[pl.BlockSpec((1, qb, GW), lambda b, r, n: (b, n, r)),
                   pl.BlockSpec((1, qb, LANES), lambda b, r, n: (b, n, r))],
        out_shape=[jax.ShapeDtypeStruct((B, L, dil * GW), BF16),
                   jax.ShapeDtypeStruct((B, L, dil * LANES), F32)],
        scratch_shapes=[pltpu.VMEM((qb + BAND, GW), BF16), pltpu.VMEM((qb + BAND, GW), BF16)],
        compiler_params=_cparams(("arbitrary", "arbitrary", "arbitrary")), name=f"attn_g{g}",
    )(q, k, k, v, v, _band_bias())


def _gla_kernel(gq_ref, gk_ref, gv_ref, la_ref, tri_ref, nw_ref, o_ref, sfin_ref, st_ref, *, nchunk):
    c = pl.program_id(1)

    @pl.when(c == 0)
    def _():
        st_ref[...] = jnp.zeros_like(st_ref)

    C = GLA_CHUNK
    npair = GLA_HEADS // 2
    tri3 = tri_ref[...]
    normw = nw_ref[...]
    hi, mid, lo = _split3(la_ref[0])
    b_chunks = []
    for ci in range(nchunk):
        rows = slice(ci * C, (ci + 1) * C)
        b_chunks.append(_dot(tri3, jnp.concatenate([hi[rows], mid[rows], lo[rows]], axis=0)))
    b = jnp.concatenate(b_chunks, axis=0)
    b_last = jnp.concatenate([jnp.broadcast_to(bc[C - 1:C, :], bc.shape) for bc in b_chunks], axis=0)
    gk = gk_ref[0].astype(F32)
    qe = gq_ref[0].astype(F32) * jnp.exp(b)
    ke = (gk * jnp.exp(-b)).astype(BF16)
    kd = (gk * jnp.exp(b_last - b)).astype(BF16)
    gv = gv_ref[0]

    low_half = lax.broadcasted_iota(jnp.int32, (C, LANES), 1) < GLA_DK
    ri = lax.broadcasted_iota(jnp.int32, (2 * C, 2 * C), 0)
    ki = lax.broadcasted_iota(jnp.int32, (2 * C, 2 * C), 1)
    pair_causal = jnp.logical_and((ri < C) == (ki < C),
                                  jnp.bitwise_and(ki, C - 1) <= jnp.bitwise_and(ri, C - 1))

    lhs, vrows, upd, dec = {}, {}, {}, {}
    for ci in range(nchunk):
        rows = slice(ci * C, (ci + 1) * C)
        for p in range(npair):
            cs = slice(p * LANES, (p + 1) * LANES)
            qe_p = qe[rows, cs]
            qm = jnp.concatenate([jnp.where(low_half, qe_p, 0.0), jnp.where(low_half, 0.0, qe_p)],
                                 axis=0).astype(BF16)
            ke_p = ke[rows, cs]
            sc = _dot_nt(qm, jnp.concatenate([ke_p, ke_p], axis=0))
            att = jnp.where(pair_causal, sc, 0.0).astype(BF16)
            lhs[ci, p] = jnp.concatenate([att, qm], axis=1)
            v_ab = gv[rows, 2 * p * GLA_DV:2 * (p + 1) * GLA_DV]
            vrows[ci, p] = jnp.concatenate([v_ab[:, 0:GLA_DV], v_ab[:, GLA_DV:]], axis=0)
            u = _dot_tn(kd[rows, cs], v_ab)
            upd[ci, p] = jnp.concatenate([u[0:GLA_DK, 0:GLA_DV], u[GLA_DK:, GLA_DV:]], axis=0)
            bl = jnp.broadcast_to(b_chunks[ci][C - 1:C, cs], (LANES, LANES))
            dec[ci, p] = jnp.exp(bl.T)

    outs = {}
    states = [st_ref[p] for p in range(npair)]
    for ci in range(nchunk):
        for p in range(npair):
            rhs = jnp.concatenate([vrows[ci, p], states[p].astype(BF16)], axis=0)
            outs[ci, p] = _dot(lhs[ci, p], rhs)
            states[p] = dec[ci, p] * states[p] + upd[ci, p]
    for p in range(npair):
        st_ref[p] = states[p]

    for ci in range(nchunk):
        for p in range(npair):
            o = outs[ci, p]
            o = (o * lax.rsqrt(jnp.mean(o * o, axis=-1, keepdims=True) + NORM_EPS) * normw).astype(BF16)
            for a in range(2):
                hh = 2 * p + a
                o_ref[0, ci * C:(ci + 1) * C, hh * GLA_DV:(hh + 1) * GLA_DV] = o[a * C:(a + 1) * C, :]

    @pl.when(c == pl.num_programs(1) - 1)
    def _():
        for p in range(npair):
            sfin_ref[0, p * LANES:(p + 1) * LANES, :] = states[p]


def _tri3(C, dtype=BF16):
    tri = np.tril(np.ones((C, C), np.float32))
    return jnp.asarray(np.concatenate([tri, tri, tri], axis=1), dtype)


def _gla_prompt(gq, gk, gv, la, normw, ct):
    B, S, _ = gq.shape
    ct = min(ct, S)
    assert S % ct == 0 and ct % GLA_CHUNK == 0
    row = lambda w: pl.BlockSpec((1, ct, w), lambda b, c: (b, c, 0))
    const = lambda shp: pl.BlockSpec(shp, lambda b, c: (0,) * len(shp))
    o, sfin = pl.pallas_call(
        functools.partial(_gla_kernel, nchunk=ct // GLA_CHUNK),
        grid=(B, S // ct),
        in_specs=[row(GLA_KW), row(GLA_KW), row(GLA_VW), row(GLA_KW),
                  const((GLA_CHUNK, 3 * GLA_CHUNK)), const((1, GLA_DV))],
        out_specs=[row(GLA_VW), pl.BlockSpec((1, GLA_KW, GLA_DV), lambda b, c: (b, 0, 0))],
        out_shape=[jax.ShapeDtypeStruct((B, S, GLA_VW), BF16),
                   jax.ShapeDtypeStruct((B, GLA_KW, GLA_DV), F32)],
        scratch_shapes=[pltpu.VMEM((GLA_HEADS // 2, GLA_DV, LANES), F32)],
        compiler_params=_cparams(("arbitrary", "arbitrary")), name="gla_prompt",
    )(gq, gk, gv, la, _tri3(GLA_CHUNK), normw)
    return o, sfin.reshape(B, GLA_HEADS, GLA_DK, GLA_DV)


def _merge_kernel(*refs, dils):
    n_groups = len(dils)
    x_ref = refs[0]
    o_refs = refs[1:1 + n_groups]
    lse_refs = refs[1 + n_groups:1 + 2 * n_groups] if n_groups > 1 else ()
    rest = refs[1 + n_groups + len(lse_refs):]
    (saz_ref, og_ref, sgz_ref, sa_ref, sb_ref, wa_ref, wg_ref, wo_ref, fnw_ref, ex_ref, y_ref) = rest[:11]
    scratch = rest[11:]
    tm = x_ref.shape[0]

    def by_position(ref, width, d, stage_ref):
        if d == 1:
            return lambda rows: ref[rows, :].astype(F32)
        planes = width // LANES
        for r in range(d):
            blk = ref[:, r * width:(r + 1) * width].astype(F32)
            for c in range(planes):
                stage_ref[c, pl.ds(r, tm // d, stride=d), :] = blk[:, c * LANES:(c + 1) * LANES]
        return lambda rows: jnp.concatenate([stage_ref[c, rows, :] for c in range(planes)], axis=1)

    if n_groups > 1:
        lse_rows = [by_position(lse_refs[g], LANES, dils[g], scratch[2 * g + 1]) for g in range(n_groups)]
        o_rows = [by_position(o_refs[g], GW, dils[g], scratch[2 * g]) for g in range(n_groups)]

    rc = min(tm, MERGE_ROWS)
    chunks = [slice(r0, r0 + rc) for r0 in range(0, tm, rc)]
    yb = [_dot((og_ref[rows, :].astype(F32) * sgz_ref[rows, :].astype(F32)).astype(BF16), wg_ref[...])
          for rows in chunks]
    if n_groups > 1:
        comb = []
        for rows in chunks:
            ls = [f(rows) for f in lse_rows]
            mx = functools.reduce(jnp.maximum, ls)
            es = [jnp.exp(l - mx) for l in ls]
            inv = 1.0 / functools.reduce(lambda a, b: a + b, es)
            acc = None
            for g in range(n_groups):
                w = es[g] * inv
                w_hi = w.astype(BF16)
                w_lo = (w - w_hi.astype(F32)).astype(BF16)
                wx = _dot(jnp.concatenate([w_hi, w_lo], axis=1), ex_ref[...])
                term = wx * o_rows[g](rows)
                acc = term if acc is None else acc + term
            comb.append(acc)
    else:
        comb = [o_refs[0][rows, :].astype(F32) for rows in chunks]
    ya = [_dot((c * saz_ref[rows, :].astype(F32)).astype(BF16), wa_ref[...]) for c, rows in zip(comb, chunks)]
    outs = []
    for a, b, rows in zip(ya, yb, chunks):
        mixed = (sa_ref[rows, :].astype(F32) * a + sb_ref[rows, :].astype(F32) * b).astype(BF16)
        outs.append(x_ref[rows, :] + _dot(mixed, wo_ref[...]))
    for out, rows in zip(outs, chunks):
        y_ref[rows, :] = (out * lax.rsqrt(jnp.mean(out * out, axis=-1, keepdims=True) + NORM_EPS)
                          * fnw_ref[...])


def _expand_matrix():
    e = np.zeros((LANES, GW), np.float32)
    for hh in range(HEADS):
        e[hh, hh * HEAD_DIM:(hh + 1) * HEAD_DIM] = 1.0
    return jnp.asarray(np.concatenate([e, e], axis=0), BF16)


def _merge(x, o_list, lse_list, dils, saz, og, sgz, sa, sb, wa, wg, wo, fnw, tm):
    R, D = x.shape
    tm = min(tm, R)
    assert R % tm == 0
    n_groups = len(o_list)
    row = lambda w: pl.BlockSpec((tm, w), lambda i: (i, 0))
    const = lambda shp: pl.BlockSpec(shp, lambda i: (0,) * len(shp))
    in_specs = [row(D)]
    in_specs += [pl.BlockSpec((tm // d, d * GW), lambda i: (i, 0)) for d in dils]
    in_specs += [pl.BlockSpec((tm // d, d * LANES), lambda i: (i, 0)) for d in dils[:len(lse_list)]]
    in_specs += [row(GW), row(GLA_VW), row(GLA_VW), row(D), row(D),
                 const((GW, D)), const((GLA_VW, D)), const((D, D)), const((1, D)), const((2 * LANES, GW))]
    scratch = []
    if n_groups > 1:
        for _ in dils:
            scratch += [pltpu.VMEM((GW // LANES, tm, LANES), F32), pltpu.VMEM((1, tm, LANES), F32)]
    return pl.pallas_call(
        functools.partial(_merge_kernel, dils=tuple(dils)),
        grid=(R // tm,), in_specs=in_specs, out_specs=row(D),
        out_shape=jax.ShapeDtypeStruct((R, D), F32), scratch_shapes=scratch,
        compiler_params=_cparams(("arbitrary",)), name=f"merge_g{n_groups}",
    )(x, *o_list, *lse_list, saz, og, sgz, sa, sb, wa, wg, wo, fnw, _expand_matrix())


def _sample_masks(T, w_lens):
    t_of_row = np.arange(HEADS * T) % T

    def bias(idx, g):
        win, dil = GROUPS[g]
        dd = w_lens[g] + t_of_row[:, None] - idx[None, :]
        ok = (dd >= 0) & (dd % dil == 0) & (dd // dil <= win // dil)
        return jnp.asarray(np.where(ok, 0.0, NEG).astype(np.float32))

    cache = [bias(np.arange(w_lens[g]), g) for g in range(N_GROUPS)]
    new = [bias(w_lens[g] + np.arange(T), g) for g in range(N_GROUPS)]
    return cache, new


def _decode_scores_pieces(in_refs, out_refs, dbs):
    G = N_GROUPS
    sq_refs, sc_refs, sb_refs, bd_ref = in_refs[0:G], in_refs[G:2 * G], in_refs[2 * G:3 * G], in_refs[3 * G]

    def piece(i, g):
        qbd = (jnp.tile(sq_refs[g][i].astype(F32), (HEADS, 1)) * bd_ref[...]).astype(BF16)
        out_refs[g][i] = _dot(qbd, sc_refs[g][i].astype(BF16)) + sb_refs[g][...]

    return [functools.partial(piece, i, g) for i in range(dbs) for g in range(G)]


def _decode_mix_pieces(in_refs, out_refs, dbs):
    G = N_GROUPS
    q_refs, new_refs, s_refs = in_refs[0:G], in_refs[G:2 * G], in_refs[2 * G:3 * G]
    c_refs, bn_refs, bd_ref = in_refs[3 * G:4 * G], in_refs[4 * G:5 * G], in_refs[5 * G]
    o_ref = out_refs[0]
    T = q_refs[0].shape[1]
    p_old, acc = {}, {}

    def prologue(i):
        bd = bd_ref[...]
        s_new, v_new = [], []
        for g in range(G):
            qbd = jnp.tile(q_refs[g][i].astype(F32), (HEADS, 1)) * bd
            s_new.append(_dot_nt(qbd, _rb(new_refs[g][i, :, 0:GW])) + bn_refs[g][...])
            v_new.append(_rb(new_refs[g][i, :, GW:2 * GW]))
        s_old = [s_refs[g][i] for g in range(G)]
        m = functools.reduce(jnp.maximum, [jnp.max(s, axis=1, keepdims=True) for s in s_new + s_old])
        e_new = [jnp.exp(s - m) for s in s_new]
        e_old = [jnp.exp(s - m) for s in s_old]
        den = functools.reduce(lambda a, b: a + b, [jnp.sum(p, axis=1, keepdims=True) for p in e_new + e_old])
        inv = 1.0 / den
        for g in range(G):
            p_old[i, g] = e_old[g] * inv
            r = _dot(_rb(e_new[g] * inv), v_new[g])
            acc[i] = r if g == 0 else acc[i] + r

    def piece(i, g):
        acc[i] = acc[i] + _dot_nt(p_old[i, g], c_refs[g][i])

    def epilogue(i):
        a = acc[i] * bd_ref[...]
        out = a[0:T, :]
        for hh in range(1, HEADS):
            out = out + a[hh * T:(hh + 1) * T, :]
        o_ref[i] = out

    thunks = []
    for i in range(dbs):
        thunks += [functools.partial(prologue, i)] + [functools.partial(piece, i, g) for g in range(G)]
        thunks += [functools.partial(epilogue, i)]
    return thunks


def _head_diag(T):
    bd = np.zeros((HEADS * T, GW), np.float32)
    for hh in range(HEADS):
        bd[hh * T:(hh + 1) * T, hh * HEAD_DIM:(hh + 1) * HEAD_DIM] = 1.0
    return jnp.asarray(bd)


def _rider_common(DB, nsteps, step_of):
    dbs = -(-DB // nsteps)
    assert DB % dbs == 0
    active = DB // dbs
    blk = lambda b, s: jnp.minimum(step_of(b, s), active - 1)
    per = lambda shp, part=0: pl.BlockSpec((dbs,) + shp, lambda b, s: (blk(b, s), part) + (0,) * (len(shp) - 1))
    const = lambda a: pl.BlockSpec(a.shape, lambda b, s: (0,) * a.ndim)
    return dbs, active, per, const


def _scores_rider(q_list, caches_t):
    DB, T, _ = q_list[0].shape
    w_lens = [c.shape[2] for c in caches_t]
    bias_c, _ = _sample_masks(T, w_lens)

    def make(nsteps, step_of):
        dbs, active, per, const = _rider_common(DB, nsteps, step_of)
        return dict(
            pieces=_decode_scores_pieces, dbs=dbs, active=active,
            arrays=list(q_list) + list(caches_t) + bias_c + [_head_diag(T)],
            in_specs=[per((T, GW)) for _ in q_list] + [per((GW, w), 0) for w in w_lens]
                     + [const(a) for a in bias_c] + [pl.BlockSpec((HEADS * T, GW), lambda b, s: (0, 0))],
            out_shape=[jax.ShapeDtypeStruct((DB, HEADS * T, w), F32) for w in w_lens],
            out_specs=[per((HEADS * T, w)) for w in w_lens])
    return make


def _mix_rider(q_list, kv_new, scores, caches_t):
    DB, T, _ = q_list[0].shape
    w_lens = [c.shape[2] for c in caches_t]
    _, bias_n = _sample_masks(T, w_lens)

    def make(nsteps, step_of):
        dbs, active, per, const = _rider_common(DB, nsteps, step_of)
        return dict(
            pieces=_decode_mix_pieces, dbs=dbs, active=active,
            arrays=list(q_list) + list(kv_new) + list(scores) + list(caches_t) + bias_n + [_head_diag(T)],
            in_specs=[per((T, GW)) for _ in q_list] + [per((T, 2 * GW)) for _ in kv_new]
                     + [per((HEADS * T, w)) for w in w_lens] + [per((GW, w), 1) for w in w_lens]
                     + [const(a) for a in bias_n] + [pl.BlockSpec((HEADS * T, GW), lambda b, s: (0, 0))],
            out_shape=[jax.ShapeDtypeStruct((DB, T, GW), F32)],
            out_specs=[per((T, GW))])
    return make


def _sgla_kernel(gq_ref, gk_ref, gv_ref, la_ref, st_ref, tri_ref, nw_ref, o_ref, snew_ref, *, dbt):
    tri3 = tri_ref[...]
    normw = nw_ref[...]
    T = gq_ref.shape[1]
    ti = lax.broadcasted_iota(jnp.int32, (T, T), 0)
    si = lax.broadcasted_iota(jnp.int32, (T, T), 1)
    causal = si <= ti
    low_half = lax.broadcasted_iota(jnp.int32, (T, LANES), 1) < GLA_DK
    ones = jnp.ones((3 * T, LANES), F32)
    row_low = lax.broadcasted_iota(jnp.int32, (LANES, GLA_DV), 0) < GLA_DK

    entries = range(dbt)
    pairs = range(GLA_HEADS // 2)
    cs = [slice(p * LANES, (p + 1) * LANES) for p in pairs]
    la3 = {i: jnp.concatenate(_split3(la_ref[i], F32), axis=0) for i in entries}
    b = {i: _dot(tri3, la3[i]) for i in entries}
    dec = {(i, p): jnp.exp(_dot_tn(la3[i][:, cs[p]], ones)) for i in entries for p in pairs}
    qe, ke, kd, gv = {}, {}, {}, {}
    for i in entries:
        b_last = b[i][T - 1:T, :]
        gk = gk_ref[i].astype(F32)
        qe[i] = gq_ref[i].astype(F32) * jnp.exp(b[i])
        ke[i] = _rb(gk * jnp.exp(-b[i]))
        kd[i] = _rb(gk * jnp.exp(b_last - b[i]))
        gv[i] = gv_ref[i].astype(F32)
    heads = [(i, p, a) for i in entries for p in pairs for a in range(2)]
    qm = {(i, p, a): _rb(jnp.where(low_half if a == 0 else jnp.logical_not(low_half), qe[i][:, cs[p]], 0.0))
          for (i, p, a) in heads}
    att = {(i, p, a): _rb(jnp.where(causal, _dot_nt(qm[i, p, a], ke[i][:, cs[p]]), 0.0)) for (i, p, a) in heads}
    v_h = {(i, p, a): gv[i][:, (2 * p + a) * GLA_DV:(2 * p + a + 1) * GLA_DV] for (i, p, a) in heads}
    o = {(i, p, a): _dot(att[i, p, a], v_h[i, p, a]) + _dot(qm[i, p, a], _rb(st_ref[i, cs[p], :]))
         for (i, p, a) in heads}
    u = {(i, p, a): _dot_tn(kd[i][:, cs[p]], v_h[i, p, a]) for (i, p, a) in heads}
    for i in entries:
        for p in pairs:
            upd = jnp.where(row_low, u[i, p, 0], u[i, p, 1])
            snew_ref[i, cs[p], :] = st_ref[i, cs[p], :] * dec[i, p] + upd
        outs = [o[i, p, a] * lax.rsqrt(jnp.mean(o[i, p, a] * o[i, p, a], axis=-1, keepdims=True) + NORM_EPS)
                * normw for p in pairs for a in range(2)]
        o_ref[i] = jnp.concatenate(outs, axis=1)


def _sample_gla(gq, gk, gv, la, state, normw, dbt):
    DB, T, _ = gq.shape
    dbt = min(dbt, DB)
    assert DB % dbt == 0
    st = state.reshape(DB, GLA_KW, GLA_DV)
    blk = lambda shp: pl.BlockSpec((dbt,) + shp, lambda i: (i,) + (0,) * len(shp))
    const = lambda shp: pl.BlockSpec(shp, lambda i: (0,) * len(shp))
    o, snew = pl.pallas_call(
        functools.partial(_sgla_kernel, dbt=dbt),
        grid=(DB // dbt,),
        in_specs=[blk((T, GLA_KW)), blk((T, GLA_KW)), blk((T, GLA_VW)), blk((T, GLA_KW)),
                  blk((GLA_KW, GLA_DV)), const((T, 3 * T)), const((1, GLA_DV))],
        out_specs=[blk((T, GLA_VW)), blk((GLA_KW, GLA_DV))],
        out_shape=[jax.ShapeDtypeStruct((DB, T, GLA_VW), F32),
                   jax.ShapeDtypeStruct((DB, GLA_KW, GLA_DV), F32)],
        compiler_params=_cparams(("arbitrary",)), name="gla_sample",
    )(gq, gk, gv, la, st, _tri3(T, F32), normw)
    return o, snew.reshape(DB, GLA_HEADS, GLA_DK, GLA_DV)


def _rope_tables(pos):
    half = ROPE_DIM // 2
    inv_freq = 1.0 / (ROPE_THETA ** (jnp.arange(half, dtype=F32) * (2.0 / ROPE_DIM)))
    ang = pos.astype(F32)[:, None] * inv_freq[None, :]
    cos, sin = jnp.cos(ang), jnp.sin(ang)
    n = pos.shape[0]
    pad = jnp.zeros((n, HEAD_DIM - ROPE_DIM), F32)
    zero = jnp.zeros((n, half), F32)
    c = jnp.concatenate([cos, cos, pad + 1.0], axis=1)
    s1 = jnp.concatenate([zero, sin, pad], axis=1)
    s2 = jnp.concatenate([-sin, zero, pad], axis=1)
    rep = LANES // HEAD_DIM
    return tuple(jnp.tile(t, (1, rep)) for t in (c, s1, s2))


def kernel(x_prompt, x_sample, cache_kv_w128, cache_kv_w512, cache_kv_w2048, state_gla,
           ln_w, w_in, w_gla_a2, b_gla_a, gla_norm_w, w_attn_out, w_gla_out, w_out, final_norm_w):
    B, S, D = x_prompt.shape
    DB, T, _ = x_sample.shape
    assert ln_w.shape[0] == 1, "single-layer step"
    caches = (cache_kv_w128, cache_kv_w512, cache_kv_w2048)
    dils = [d for _, d in GROUPS]

    lnw = ln_w[0].reshape(1, D)
    w_t = w_in[0].T.astype(BF16)
    assert PROJ_W <= 2 * OFF_AZ
    wa2 = w_gla_a2[0].astype(BF16)
    ba = b_gla_a[0].reshape(1, GLA_KW)
    gnw = gla_norm_w[0].reshape(1, GLA_DV)
    wa = w_attn_out[0].astype(BF16)
    wg = w_gla_out[0].astype(BF16)
    wo = w_out[0].astype(BF16)
    fnw = final_norm_w.reshape(1, D)


    R = DB * T
    xs = x_sample.reshape(1, R, D)
    pos_s = PAST_LEN + jnp.arange(T, dtype=jnp.int32)
    tabs_s = tuple(jnp.tile(t, (DB, 1)) for t in _rope_tables(pos_s))
    qkv_s = _proj_qkv(xs, lnw, w_t, tabs_s, min(PROJ_ROWS, R), (1,) * N_GROUPS, row_major_kv=True,
                      q_scale=HEAD_DIM ** -0.5)
    rest_s = _proj_rest(xs, lnw, w_t, wa2, ba, min(PROJ_ROWS, R))
    per_db = lambda t: t.reshape(DB, T, t.shape[-1])
    q_s = [per_db(q) for q in qkv_s[0:3]]
    kv_new = [per_db(t) for t in qkv_s[9:12]]
    caches_t = [c[0].transpose(0, 2, 3, 4, 1).reshape(DB, 2 * GW, c.shape[2]) for c in caches]

    tabs_p = _rope_tables(jnp.arange(S, dtype=jnp.int32))
    qkv = _proj_qkv(x_prompt, lnw, w_t, tabs_p, min(PROJ_ROWS, S), dils, row_major_kv=False,
                    q_scale=HEAD_DIM ** -0.5 * LOG2E,
                    make_rider=_scores_rider(q_s, caches_t))
    scores = qkv[9:12]
    rest = _proj_rest(x_prompt, lnw, w_t, wa2, ba, min(PROJ_ROWS, S),
                      make_rider=_mix_rider(q_s, kv_new, scores, caches_t))
    saz, sgz, sa, sb, gq, gk, gv, la, o_s = rest
    o_list, lse_list = [], []
    for g in range(N_GROUPS):
        o, lse = _attn_group(qkv[g], qkv[3 + g], qkv[6 + g], g, qb=ATTN_ROWS)
        o_list.append(o.reshape(-1, o.shape[-1]))
        lse_list.append(lse.reshape(-1, lse.shape[-1]))
    og, gla_p = _gla_prompt(gq, gk, gv, la, gnw, ct=GLA_TOKENS)
    flat = lambda t: t.reshape(B * S, t.shape[-1])
    y_prompt = _merge(flat(x_prompt), o_list, lse_list, dils, flat(saz), flat(og), flat(sgz), flat(sa),
                      flat(sb), wa, wg, wo, fnw, tm=MERGE_TILE).reshape(B, S, D)
    kvt = _kv_tail_prompt(x_prompt, lnw, w_t, [min(w, S) for w, _ in GROUPS], tile=KV_TAIL_COLS)
    kv_p = [t.reshape(B, 2, HEADS, HEAD_DIM, t.shape[-1]).transpose(0, 4, 1, 2, 3)[None] for t in kvt]

    saz, sgz, sa, sb, gq, gk, gv, la = rest_s
    og_s, gla_s = _sample_gla(per_db(gq), per_db(gk), per_db(gv), per_db(la), state_gla[0], gnw, dbt=SAMPLE_GLA_ENTRIES)
    flat_s = lambda t: t.reshape(R, t.shape[-1])
    y_sample = _merge(x_sample.reshape(R, D), [o_s.reshape(R, GW)], [], (1,), flat_s(saz), flat_s(og_s),
                      flat_s(sgz), flat_s(sa), flat_s(sb), wa, wg, wo, fnw, tm=MERGE_TILE).reshape(DB, T, D)
    kvt_s = _kv_tail_sample(x_sample, lnw, w_t, pos_s)
    kv_s = [t.reshape(T, 2, HEADS, HEAD_DIM, DB).transpose(4, 0, 1, 2, 3)[None] for t in kvt_s]

    return (y_prompt, y_sample, kv_p[0], kv_p[1], kv_p[2], gla_p[None],
            kv_s[0], kv_s[1], kv_s[2], gla_s[None])
```

```python
import functools

import numpy as np
import jax
import jax.numpy as jnp
from jax import lax
from jax.experimental import pallas as pl
from jax.experimental.pallas import tpu as pltpu

F32 = jnp.float32
BF16 = jnp.bfloat16

D_MODEL = 1024
HEAD_DIM = 64
HEADS = 8
GROUPS = ((128, 1), (512, 4), (2048, 16))
N_GROUPS = len(GROUPS)
GW = HEADS * HEAD_DIM
QKV_W = N_GROUPS * GW
ROPE_DIM = HEAD_DIM // 4
ROPE_THETA = 500000.0
BAND = 128
GLA_HEADS = 4
GLA_DK = 64
GLA_DV = 128
GLA_KW = GLA_HEADS * GLA_DK
GLA_VW = GLA_HEADS * GLA_DV
GATE_RANK = 16
GLA_TAU = 16.0
GLA_CHUNK = 64
NORM_EPS = 1e-6
PAST_LEN = 8192

LANES = 128
NEG = -1e30
LOG2E = float(np.log2(np.e))
LN2 = float(np.log(2.0))

OFF_Q = 0
OFF_K = OFF_Q + QKV_W
OFF_V = OFF_K + QKV_W
OFF_AZ = OFF_V + QKV_W
OFF_GQ = OFF_AZ + GW
OFF_GK = OFF_GQ + GLA_KW
OFF_GV = OFF_GK + GLA_KW
OFF_GZ = OFF_GV + GLA_VW
OFF_LR = OFF_GZ + GLA_VW
OFF_MA = OFF_LR + GATE_RANK
OFF_MB = OFF_MA + D_MODEL
PROJ_W = OFF_MB + D_MODEL
assert all(o % 16 == 0 for o in (OFF_LR, OFF_MA, OFF_MB, PROJ_W))

VMEM_LIMIT = 56 * 1024 * 1024

PROJ_ROWS = 512
PROJ_SEG = 256
ATTN_ROWS = 2048
GLA_TOKENS = 2048
KV_TAIL_COLS = 2048
MERGE_TILE = 1024
MERGE_ROWS = 512
SAMPLE_GLA_ENTRIES = 32

def _cparams(sem):
    return pltpu.CompilerParams(dimension_semantics=sem, vmem_limit_bytes=VMEM_LIMIT)


def _dot(a, b):
    return jnp.dot(a, b, preferred_element_type=F32)


def _dot_nt(a, b):
    return lax.dot_general(a, b, (((1,), (1,)), ((), ())), preferred_element_type=F32)


def _dot_tn(a, b):
    return lax.dot_general(a, b, (((0,), (0,)), ((), ())), preferred_element_type=F32)


def _rb(x):
    return x.astype(BF16).astype(F32)


def _split3(x, dtype=BF16):
    hi = x.astype(BF16)
    r1 = x - hi.astype(F32)
    mid = r1.astype(BF16)
    lo = (r1 - mid.astype(F32)).astype(BF16)
    return hi.astype(dtype), mid.astype(dtype), lo.astype(dtype)


def _sigmoid(x):
    return 1.0 / (1.0 + jnp.exp(-x))


def _log_sigmoid(x):
    return jnp.minimum(x, 0.0) - jnp.log1p(jnp.exp(-jnp.abs(x)))


def _rms_bf16(x, w):
    ms = jnp.mean(x * x, axis=-1, keepdims=True)
    return (x * lax.rsqrt(ms + NORM_EPS) * w).astype(BF16)


def _run_rider(rider, in_refs, out_refs):
    if rider is None:
        return
    pieces, active, dbs, nsteps = rider

    def emit():
        for thunk in pieces(in_refs, out_refs, dbs):
            thunk()

    if active == nsteps:
        emit()
    else:
        step = pl.program_id(0) * pl.num_programs(1) + pl.program_id(1)
        pl.when(step < active)(emit)


def _proj_qkv_kernel(x_ref, lnw_ref, w_ref, rc_ref, rs1_ref, rs2_ref, *refs,
                     dils, row_major_kv, q_scale, rider, n_rider_in):
    rider_in, refs = refs[:n_rider_in], refs[n_rider_in:]
    q_refs, k_refs, v_refs = refs[0:3], refs[3:6], refs[6:9]
    kv_refs = refs[9:12] if row_major_kv else ()
    rider_out = refs[9 + len(kv_refs):-1]
    stage_ref = refs[-1]
    h = _rms_bf16(x_ref[0], lnw_ref[...])
    tm = h.shape[0]
    rc, rs1, rs2 = rc_ref[...], rs1_ref[...], rs2_ref[...]

    def mm(off, width):
        return _dot_nt(h, w_ref[off:off + width, :])

    def rope(t):
        outs = []
        for j in range(t.shape[1] // LANES):
            c = t[:, j * LANES:(j + 1) * LANES]
            outs.append(c * rc + pltpu.roll(c, ROPE_DIM // 2, axis=1) * rs1
                        + pltpu.roll(c, LANES - ROPE_DIM // 2, axis=1) * rs2)
        return jnp.concatenate(outs, axis=1)

    def store_by_residue(dst_ref, val, d, c0):
        w = val.shape[1]
        if d == 1:
            dst_ref[0, :, c0:c0 + w] = val.astype(BF16)
            return
        for c in range(w // LANES):
            stage_ref[c] = val[:, c * LANES:(c + 1) * LANES]
        for r in range(d):
            for c in range(w // LANES):
                col = r * GW + c0 + c * LANES
                dst_ref[0, :, col:col + LANES] = stage_ref[c, pl.ds(r, tm // d, stride=d), :].astype(BF16)

    for g in range(N_GROUPS):
        for c0 in range(0, GW, PROJ_SEG):
            qr = rope(mm(OFF_Q + g * GW + c0, PROJ_SEG)) * q_scale
            store_by_residue(q_refs[g], qr, dils[g], c0)
            kr = rope(mm(OFF_K + g * GW + c0, PROJ_SEG))
            store_by_residue(k_refs[g], kr, dils[g], c0)
            vv = mm(OFF_V + g * GW + c0, PROJ_SEG)
            store_by_residue(v_refs[g], vv, dils[g], c0)
            if row_major_kv:
                kv_refs[g][0, :, c0:c0 + PROJ_SEG] = kr
                kv_refs[g][0, :, GW + c0:GW + c0 + PROJ_SEG] = vv
    _run_rider(rider, rider_in, rider_out)


def _proj_rest_kernel(x_ref, lnw_ref, w_ref, wa2_ref, ba_ref, *refs, rider, n_rider_in):
    rider_in, refs = refs[:n_rider_in], refs[n_rider_in:]
    saz_ref, sgz_ref, sa_ref, sb_ref, gq_ref, gk_ref, gv_ref, la_ref = refs[0:8]
    rider_out = refs[8:]
    h = _rms_bf16(x_ref[0], lnw_ref[...])

    def mm(off, width):
        return _dot_nt(h, w_ref[off - OFF_AZ:off - OFF_AZ + width, :])

    az = mm(OFF_AZ, GW)
    saz_ref[0] = (az * _sigmoid(az)).astype(BF16)
    gz = mm(OFF_GZ, GLA_VW)
    sgz_ref[0] = (gz * _sigmoid(gz)).astype(BF16)
    for j in range(D_MODEL // GW):
        sa_ref[0, :, j * GW:(j + 1) * GW] = _sigmoid(mm(OFF_MA + j * GW, GW)).astype(BF16)
        sb_ref[0, :, j * GW:(j + 1) * GW] = _sigmoid(mm(OFF_MB + j * GW, GW)).astype(BF16)
    gq_ref[0] = (mm(OFF_GQ, GLA_KW) * (GLA_DK ** -0.5)).astype(BF16)
    gk_ref[0] = mm(OFF_GK, GLA_KW).astype(BF16)
    gv_ref[0] = mm(OFF_GV, GLA_VW).astype(BF16)
    glr = mm(OFF_LR, GATE_RANK)
    gate_pre = _dot(glr.astype(BF16), wa2_ref[...]) + ba_ref[...]
    la_ref[0] = _log_sigmoid(gate_pre) * (1.0 / GLA_TAU)
    _run_rider(rider, rider_in, rider_out)


def _rider_parts(make_rider, B, nt):
    if make_rider is None:
        return None, [], [], [], []
    r = make_rider(B * nt, lambda b, s: b * nt + s)
    return (r["pieces"], r["active"], r["dbs"], B * nt), r["arrays"], r["in_specs"], r["out_shape"], r["out_specs"]


def _proj_qkv(x, lnw, w_t, rope_tabs, tm, dils, row_major_kv, q_scale, make_rider=None):
    B, S, D = x.shape
    assert S % tm == 0 and all(tm % (16 * d) == 0 for d in dils)
    row = lambda w: pl.BlockSpec((1, tm, w), lambda b, s: (b, s, 0))
    tab = pl.BlockSpec((tm, LANES), lambda b, s: (s, 0))
    rider, r_arrays, r_in_specs, r_out_shape, r_out_specs = _rider_parts(make_rider, B, S // tm)
    out_shape, out_specs = [], []
    for _ in range(3):
        for d in dils:
            out_shape.append(jax.ShapeDtypeStruct((B, S // d, d * GW), BF16))
            out_specs.append(pl.BlockSpec((1, tm // d, d * GW), lambda b, s: (b, s, 0)))
    if row_major_kv:
        out_shape += [jax.ShapeDtypeStruct((B, S, 2 * GW), F32)] * N_GROUPS
        out_specs += [row(2 * GW)] * N_GROUPS
    in_specs = [row(D), pl.BlockSpec((1, D), lambda b, s: (0, 0)),
                pl.BlockSpec((OFF_AZ, D), lambda b, s: (0, 0), pipeline_mode=pl.Buffered(1)),
                tab, tab, tab]
    return pl.pallas_call(
        functools.partial(_proj_qkv_kernel, dils=tuple(dils), row_major_kv=row_major_kv, q_scale=q_scale,
                          rider=rider, n_rider_in=len(r_arrays)),
        grid=(B, S // tm), in_specs=in_specs + r_in_specs, out_specs=out_specs + r_out_specs,
        out_shape=out_shape + r_out_shape,
        scratch_shapes=[pltpu.VMEM((GW // LANES, tm, LANES), F32)],
        compiler_params=_cparams(("arbitrary", "arbitrary")), name="proj_qkv",
    )(x, lnw, w_t, *rope_tabs, *r_arrays)


def _proj_rest(x, lnw, w_t, wa2, ba, tm, make_rider=None):
    B, S, D = x.shape
    assert S % tm == 0
    row = lambda w: pl.BlockSpec((1, tm, w), lambda b, s: (b, s, 0))
    const = lambda shp: pl.BlockSpec(shp, lambda b, s: (0,) * len(shp))
    bf = lambda w: jax.ShapeDtypeStruct((B, S, w), BF16)
    rider, r_arrays, r_in_specs, r_out_shape, r_out_specs = _rider_parts(make_rider, B, S // tm)
    out_shape = [bf(GW), bf(GLA_VW), bf(D_MODEL), bf(D_MODEL), bf(GLA_KW), bf(GLA_KW), bf(GLA_VW),
                 jax.ShapeDtypeStruct((B, S, GLA_KW), F32)]
    out_specs = [row(GW), row(GLA_VW), row(D_MODEL), row(D_MODEL), row(GLA_KW), row(GLA_KW), row(GLA_VW),
                 row(GLA_KW)]
    in_specs = [row(D), const((1, D)),
                pl.BlockSpec((OFF_AZ, D), lambda b, s: (1, 0), pipeline_mode=pl.Buffered(1)),
                const((GATE_RANK, GLA_KW)), const((1, GLA_KW))]
    return pl.pallas_call(
        functools.partial(_proj_rest_kernel, rider=rider, n_rider_in=len(r_arrays)),
        grid=(B, S // tm), in_specs=in_specs + r_in_specs, out_specs=out_specs + r_out_specs,
        out_shape=out_shape + r_out_shape,
        compiler_params=_cparams(("arbitrary", "arbitrary")), name="proj_rest",
    )(x, lnw, w_t, wa2, ba, *r_arrays)


def _kvt_kernel(x_ref, lnw_ref, wk_ref, wv_ref, cos_ref, sin_ref, o0_ref, o1_ref, o2_ref, *, plan, nt):
    i = pl.program_id(1)
    h = _rms_bf16(x_ref[...], lnw_ref[...])
    rows = h.shape[0]
    half = ROPE_DIM // 2
    for g, (tiles, cols) in enumerate(plan):
        o_ref = (o0_ref, o1_ref, o2_ref)[g]

        def emit(g=g, cols=cols, o_ref=o_ref):
            hh = h[rows - cols:, :]
            y = _dot_nt(wk_ref[g * GW:(g + 1) * GW, :], hh)
            c = cos_ref[:, rows - cols:]
            s = sin_ref[:, rows - cols:]
            for hd in range(HEADS):
                b0 = hd * HEAD_DIM
                x1 = y[b0:b0 + half, :]
                x2 = y[b0 + half:b0 + ROPE_DIM, :]
                o_ref[b0:b0 + half, :] = x1 * c - x2 * s
                o_ref[b0 + half:b0 + ROPE_DIM, :] = x2 * c + x1 * s
                o_ref[b0 + ROPE_DIM:b0 + HEAD_DIM, :] = y[b0 + ROPE_DIM:b0 + HEAD_DIM, :]
            o_ref[GW:2 * GW, :] = _dot_nt(wv_ref[g * GW:(g + 1) * GW, :], hh)

        if tiles == nt:
            emit()
        else:
            pl.when(i >= nt - tiles)(emit)


def _rope_cos_sin_t(pos):
    half = ROPE_DIM // 2
    inv_freq = 1.0 / (ROPE_THETA ** (jnp.arange(half, dtype=F32) * (2.0 / ROPE_DIM)))
    ang = pos.astype(F32)[:, None] * inv_freq[None, :]
    return jnp.cos(ang).T, jnp.sin(ang).T


def _kv_tail_prompt(x, lnw, w_t, S_tails, tile):
    B, S, D = x.shape
    tile = min(tile, S)
    span = max(S_tails)
    assert span % tile == 0 and S % tile == 0
    nt = span // tile
    first = (S - span) // tile
    plan = []
    for t in S_tails:
        assert t % tile == 0 or t < tile
        plan.append((t // tile, tile) if t >= tile else (1, t))
    cos_t, sin_t = _rope_cos_sin_t(jnp.arange(S, dtype=jnp.int32))
    tabspec = pl.BlockSpec((ROPE_DIM // 2, tile), lambda b, i: (0, first + i))
    out_specs = [pl.BlockSpec((None, 2 * GW, cols), lambda b, i, t=tiles: (b, 0, jnp.maximum(i - (nt - t), 0)))
                 for tiles, cols in plan]
    return pl.pallas_call(
        functools.partial(_kvt_kernel, plan=tuple(plan), nt=nt),
        grid=(B, nt),
        in_specs=[pl.BlockSpec((None, tile, D), lambda b, i: (b, first + i, 0)),
                  pl.BlockSpec((1, D), lambda b, i: (0, 0)),
                  pl.BlockSpec((QKV_W, D), lambda b, i: (OFF_K // QKV_W, 0), pipeline_mode=pl.Buffered(1)),
                  pl.BlockSpec((QKV_W, D), lambda b, i: (OFF_V // QKV_W, 0), pipeline_mode=pl.Buffered(1)),
                  tabspec, tabspec],
        out_specs=out_specs,
        out_shape=[jax.ShapeDtypeStruct((B, 2 * GW, t), F32) for t in S_tails],
        compiler_params=_cparams(("arbitrary", "arbitrary")), name="kv_tail_prompt",
    )(x, lnw, w_t, w_t, cos_t, sin_t)


def _kv_tail_sample(x, lnw, w_t, pos):
    DB, T, D = x.shape
    cos_t, sin_t = _rope_cos_sin_t(pos)
    bcast = lambda t: jnp.broadcast_to(t.T[:, :, None], (T, ROPE_DIM // 2, DB))
    tabspec = pl.BlockSpec((None, ROPE_DIM // 2, DB), lambda b, i: (i, 0, 0))
    out_spec = pl.BlockSpec((None, 2 * GW, DB), lambda b, i: (i, 0, 0))
    return pl.pallas_call(
        functools.partial(_kvt_kernel, plan=((T, DB),) * N_GROUPS, nt=T),
        grid=(1, T),
        in_specs=[pl.BlockSpec((None, DB, D), lambda b, i: (i, 0, 0)),
                  pl.BlockSpec((1, D), lambda b, i: (0, 0)),
                  pl.BlockSpec((QKV_W, D), lambda b, i: (OFF_K // QKV_W, 0), pipeline_mode=pl.Buffered(1)),
                  pl.BlockSpec((QKV_W, D), lambda b, i: (OFF_V // QKV_W, 0), pipeline_mode=pl.Buffered(1)),
                  tabspec, tabspec],
        out_specs=[out_spec] * N_GROUPS,
        out_shape=[jax.ShapeDtypeStruct((T, 2 * GW, DB), F32)] * N_GROUPS,
        compiler_params=_cparams(("arbitrary", "arbitrary")), name="kv_tail_sample",
    )(x.transpose(1, 0, 2), lnw, w_t, w_t, bcast(cos_t), bcast(sin_t))


def _band_bias():
    q = np.arange(BAND)[:, None]
    c = np.arange(2 * BAND)[None, :]
    ok = np.where(c < BAND, c >= q, (c - BAND) <= q)
    ok_first = ok & (c >= BAND)
    return jnp.asarray(np.stack([np.where(ok, 0.0, NEG), np.where(ok_first, 0.0, NEG)]).astype(np.float32))


def _attn_kernel(q_ref, k_ref, kp_ref, v_ref, vp_ref, bias_ref, o_ref, lse_ref, kall, vall, *, qb):
    n = pl.program_id(2)
    nsub = qb // BAND
    npair = HEADS // 2
    kall[0:BAND, :] = kp_ref[0]
    kall[BAND:, :] = k_ref[0]
    vall[0:BAND, :] = vp_ref[0]
    vall[BAND:, :] = v_ref[0]
    lane = lax.broadcasted_iota(jnp.int32, (BAND, LANES), 1)
    low_half = lane < HEAD_DIM

    def sub_block(j, carry):
        r0 = pl.multiple_of(j * BAND, BAND)
        first = (n * nsub + j) == 0
        bias = bias_ref[jnp.where(first, 1, 0)]
        bias2 = jnp.concatenate([bias, bias], axis=0)
        scores = []
        for hp in range(npair):
            cs = slice(hp * LANES, (hp + 1) * LANES)
            qp = q_ref[0, pl.ds(r0, BAND), cs]
            zero = jnp.zeros_like(qp)
            qm = jnp.concatenate([jnp.where(low_half, qp, zero), jnp.where(low_half, zero, qp)], axis=0)
            scores.append(_dot_nt(qm, kall[pl.ds(r0, 2 * BAND), cs]) + bias2)
        probs, stats = [], []
        for s in scores:
            m = jnp.max(s, axis=1, keepdims=True)
            p = jnp.exp2(s - m)
            stats.append((m, jnp.sum(p, axis=1, keepdims=True)))
            probs.append(p.astype(BF16))
        m_tile = jnp.zeros((BAND, LANES), F32)
        den_tile = jnp.ones((BAND, LANES), F32)
        for hp in range(npair):
            cs = slice(hp * LANES, (hp + 1) * LANES)
            m, den = stats[hp]
            o = _dot(probs[hp], vall[pl.ds(r0, 2 * BAND), cs])
            o = jnp.where(low_half, o[0:BAND], o[BAND:]) / jnp.where(low_half, den[0:BAND], den[BAND:])
            o_ref[0, pl.ds(r0, BAND), cs] = o.astype(BF16)
            for a in range(2):
                rows = slice(a * BAND, (a + 1) * BAND)
                m_tile = jnp.where(lane == 2 * hp + a, m[rows], m_tile)
                den_tile = jnp.where(lane == 2 * hp + a, den[rows], den_tile)
        lse_ref[0, pl.ds(r0, BAND), :] = (m_tile + jnp.log2(den_tile)) * LN2
        return carry

    lax.fori_loop(0, nsub, sub_block, 0, unroll=True)


def _attn_group(q, k, v, g, qb):
    B, L, _ = q.shape
    win, dil = GROUPS[g]
    assert win // dil == BAND
    qb = min(qb, L)
    assert L % qb == 0 and qb % BAND == 0
    nsub = qb // BAND
    cur = pl.BlockSpec((1, qb, GW), lambda b, r, n: (b, n, r))
    prev = pl.BlockSpec((1, BAND, GW), lambda b, r, n: (b, jnp.maximum(n * nsub - 1, 0), r))
    return pl.pallas_call(
        functools.partial(_attn_kernel, qb=qb),
        grid=(B, dil, L // qb),
        in_specs=[cur, cur, prev, cur, prev, pl.BlockSpec((2, BAND, 2 * BAND), lambda b, r, n: (0, 0, 0))],
        out_specs=[pl.BlockSpec((1, qb, GW), lambda b, r, n: (b, n, r)),
                   pl.BlockSpec((1, qb, LANES), lambda b, r, n: (b, n, r))],
        out_shape=[jax.ShapeDtypeStruct((B, L, dil * GW), BF16),
                   jax.ShapeDtypeStruct((B, L, dil * LANES), F32)],
        scratch_shapes=[pltpu.VMEM((qb + BAND, GW), BF16), pltpu.VMEM((qb + BAND, GW), BF16)],
        compiler_params=_cparams(("arbitrary", "arbitrary", "arbitrary")), name=f"attn_g{g}",
    )(q, k, k, v, v, _band_bias())


def _gla_kernel(gq_ref, gk_ref, gv_ref, la_ref, tri_ref, nw_ref, o_ref, sfin_ref, st_ref, *, nchunk):
    c = pl.program_id(1)

    @pl.when(c == 0)
    def _():
        st_ref[...] = jnp.zeros_like(st_ref)

    C = GLA_CHUNK
    npair = GLA_HEADS // 2
    tri3 = tri_ref[...]
    normw = nw_ref[...]
    hi, mid, lo = _split3(la_ref[0])
    b_chunks = []
    for ci in range(nchunk):
        rows = slice(ci * C, (ci + 1) * C)
        b_chunks.append(_dot(tri3, jnp.concatenate([hi[rows], mid[rows], lo[rows]], axis=0)))
    b = jnp.concatenate(b_chunks, axis=0)
    b_last = jnp.concatenate([jnp.broadcast_to(bc[C - 1:C, :], bc.shape) for bc in b_chunks], axis=0)
    gk = gk_ref[0].astype(F32)
    qe = gq_ref[0].astype(F32) * jnp.exp(b)
    ke = (gk * jnp.exp(-b)).astype(BF16)
    kd = (gk * jnp.exp(b_last - b)).astype(BF16)
    gv = gv_ref[0]

    low_half = lax.broadcasted_iota(jnp.int32, (C, LANES), 1) < GLA_DK
    ri = lax.broadcasted_iota(jnp.int32, (2 * C, 2 * C), 0)
    ki = lax.broadcasted_iota(jnp.int32, (2 * C, 2 * C), 1)
    pair_causal = jnp.logical_and((ri < C) == (ki < C),
                                  jnp.bitwise_and(ki, C - 1) <= jnp.bitwise_and(ri, C - 1))

    lhs, vrows, upd, dec = {}, {}, {}, {}
    for ci in range(nchunk):
        rows = slice(ci * C, (ci + 1) * C)
        for p in range(npair):
            cs = slice(p * LANES, (p + 1) * LANES)
            qe_p = qe[rows, cs]
            qm = jnp.concatenate([jnp.where(low_half, qe_p, 0.0), jnp.where(low_half, 0.0, qe_p)],
                                 axis=0).astype(BF16)
            ke_p = ke[rows, cs]
            sc = _dot_nt(qm, jnp.concatenate([ke_p, ke_p], axis=0))
            att = jnp.where(pair_causal, sc, 0.0).astype(BF16)
            lhs[ci, p] = jnp.concatenate([att, qm], axis=1)
            v_ab = gv[rows, 2 * p * GLA_DV:2 * (p + 1) * GLA_DV]
            vrows[ci, p] = jnp.concatenate([v_ab[:, 0:GLA_DV], v_ab[:, GLA_DV:]], axis=0)
            u = _dot_tn(kd[rows, cs], v_ab)
            upd[ci, p] = jnp.concatenate([u[0:GLA_DK, 0:GLA_DV], u[GLA_DK:, GLA_DV:]], axis=0)
            bl = jnp.broadcast_to(b_chunks[ci][C - 1:C, cs], (LANES, LANES))
            dec[ci, p] = jnp.exp(bl.T)

    outs = {}
    states = [st_ref[p] for p in range(npair)]
    for ci in range(nchunk):
        for p in range(npair):
            rhs = jnp.concatenate([vrows[ci, p], states[p].astype(BF16)], axis=0)
            outs[ci, p] = _dot(lhs[ci, p], rhs)
            states[p] = dec[ci, p] * states[p] + upd[ci, p]
    for p in range(npair):
        st_ref[p] = states[p]

    for ci in range(nchunk):
        for p in range(npair):
            o = outs[ci, p]
            o = (o * lax.rsqrt(jnp.mean(o * o, axis=-1, keepdims=True) + NORM_EPS) * normw).astype(BF16)
            for a in range(2):
                hh = 2 * p + a
                o_ref[0, ci * C:(ci + 1) * C, hh * GLA_DV:(hh + 1) * GLA_DV] = o[a * C:(a + 1) * C, :]

    @pl.when(c == pl.num_programs(1) - 1)
    def _():
        for p in range(npair):
            sfin_ref[0, p * LANES:(p + 1) * LANES, :] = states[p]


def _tri3(C, dtype=BF16):
    tri = np.tril(np.ones((C, C), np.float32))
    return jnp.asarray(np.concatenate([tri, tri, tri], axis=1), dtype)


def _gla_prompt(gq, gk, gv, la, normw, ct):
    B, S, _ = gq.shape
    ct = min(ct, S)
    assert S % ct == 0 and ct % GLA_CHUNK == 0
    row = lambda w: pl.BlockSpec((1, ct, w), lambda b, c: (b, c, 0))
    const = lambda shp: pl.BlockSpec(shp, lambda b, c: (0,) * len(shp))
    o, sfin = pl.pallas_call(
        functools.partial(_gla_kernel, nchunk=ct // GLA_CHUNK),
        grid=(B, S // ct),
        in_specs=[row(GLA_KW), row(GLA_KW), row(GLA_VW), row(GLA_KW),
                  const((GLA_CHUNK, 3 * GLA_CHUNK)), const((1, GLA_DV))],
        out_specs=[row(GLA_VW), pl.BlockSpec((1, GLA_KW, GLA_DV), lambda b, c: (b, 0, 0))],
        out_shape=[jax.ShapeDtypeStruct((B, S, GLA_VW), BF16),
                   jax.ShapeDtypeStruct((B, GLA_KW, GLA_DV), F32)],
        scratch_shapes=[pltpu.VMEM((GLA_HEADS // 2, GLA_DV, LANES), F32)],
        compiler_params=_cparams(("arbitrary", "arbitrary")), name="gla_prompt",
    )(gq, gk, gv, la, _tri3(GLA_CHUNK), normw)
    return o, sfin.reshape(B, GLA_HEADS, GLA_DK, GLA_DV)


def _merge_kernel(*refs, dils):
    n_groups = len(dils)
    x_ref = refs[0]
    o_refs = refs[1:1 + n_groups]
    lse_refs = refs[1 + n_groups:1 + 2 * n_groups] if n_groups > 1 else ()
    rest = refs[1 + n_groups + len(lse_refs):]
    (saz_ref, og_ref, sgz_ref, sa_ref, sb_ref, wa_ref, wg_ref, wo_ref, fnw_ref, ex_ref, y_ref) = rest[:11]
    scratch = rest[11:]
    tm = x_ref.shape[0]

    def by_position(ref, width, d, stage_ref):
        if d == 1:
            return lambda rows: ref[rows, :].astype(F32)
        planes = width // LANES
        for r in range(d):
            blk = ref[:, r * width:(r + 1) * width].astype(F32)
            for c in range(planes):
                stage_ref[c, pl.ds(r, tm // d, stride=d), :] = blk[:, c * LANES:(c + 1) * LANES]
        return lambda rows: jnp.concatenate([stage_ref[c, rows, :] for c in range(planes)], axis=1)

    if n_groups > 1:
        lse_rows = [by_position(lse_refs[g], LANES, dils[g], scratch[2 * g + 1]) for g in range(n_groups)]
        o_rows = [by_position(o_refs[g], GW, dils[g], scratch[2 * g]) for g in range(n_groups)]

    rc = min(tm, MERGE_ROWS)
    chunks = [slice(r0, r0 + rc) for r0 in range(0, tm, rc)]
    yb = [_dot((og_ref[rows, :].astype(F32) * sgz_ref[rows, :].astype(F32)).astype(BF16), wg_ref[...])
          for rows in chunks]
    if n_groups > 1:
        comb = []
        for rows in chunks:
            ls = [f(rows) for f in lse_rows]
            mx = functools.reduce(jnp.maximum, ls)
            es = [jnp.exp(l - mx) for l in ls]
            inv = 1.0 / functools.reduce(lambda a, b: a + b, es)
            acc = None
            for g in range(n_groups):
                w = es[g] * inv
                w_hi = w.astype(BF16)
                w_lo = (w - w_hi.astype(F32)).astype(BF16)
                wx = _dot(jnp.concatenate([w_hi, w_lo], axis=1), ex_ref[...])
                term = wx * o_rows[g](rows)
                acc = term if acc is None else acc + term
            comb.append(acc)
    else:
        comb = [o_refs[0][rows, :].astype(F32) for rows in chunks]
    ya = [_dot((c * saz_ref[rows, :].astype(F32)).astype(BF16), wa_ref[...]) for c, rows in zip(comb, chunks)]
    outs = []
    for a, b, rows in zip(ya, yb, chunks):
        mixed = (sa_ref[rows, :].astype(F32) * a + sb_ref[rows, :].astype(F32) * b).astype(BF16)
        outs.append(x_ref[rows, :] + _dot(mixed, wo_ref[...]))
    for out, rows in zip(outs, chunks):
        y_ref[rows, :] = (out * lax.rsqrt(jnp.mean(out * out, axis=-1, keepdims=True) + NORM_EPS)
                          * fnw_ref[...])


def _expand_matrix():
    e = np.zeros((LANES, GW), np.float32)
    for hh in range(HEADS):
        e[hh, hh * HEAD_DIM:(hh + 1) * HEAD_DIM] = 1.0
    return jnp.asarray(np.concatenate([e, e], axis=0), BF16)


def _merge(x, o_list, lse_list, dils, saz, og, sgz, sa, sb, wa, wg, wo, fnw, tm):
    R, D = x.shape
    tm = min(tm, R)
    assert R % tm == 0
    n_groups = len(o_list)
    row = lambda w: pl.BlockSpec((tm, w), lambda i: (i, 0))
    const = lambda shp: pl.BlockSpec(shp, lambda i: (0,) * len(shp))
    in_specs = [row(D)]
    in_specs += [pl.BlockSpec((tm // d, d * GW), lambda i: (i, 0)) for d in dils]
    in_specs += [pl.BlockSpec((tm // d, d * LANES), lambda i: (i, 0)) for d in dils[:len(lse_list)]]
    in_specs += [row(GW), row(GLA_VW), row(GLA_VW), row(D), row(D),
                 const((GW, D)), const((GLA_VW, D)), const((D, D)), const((1, D)), const((2 * LANES, GW))]
    scratch = []
    if n_groups > 1:
        for _ in dils:
            scratch += [pltpu.VMEM((GW // LANES, tm, LANES), F32), pltpu.VMEM((1, tm, LANES), F32)]
    return pl.pallas_call(
        functools.partial(_merge_kernel, dils=tuple(dils)),
        grid=(R // tm,), in_specs=in_specs, out_specs=row(D),
        out_shape=jax.ShapeDtypeStruct((R, D), F32), scratch_shapes=scratch,
        compiler_params=_cparams(("arbitrary",)), name=f"merge_g{n_groups}",
    )(x, *o_list, *lse_list, saz, og, sgz, sa, sb, wa, wg, wo, fnw, _expand_matrix())


def _sample_masks(T, w_lens):
    t_of_row = np.arange(HEADS * T) % T

    def bias(idx, g):
        win, dil = GROUPS[g]
        dd = w_lens[g] + t_of_row[:, None] - idx[None, :]
        ok = (dd >= 0) & (dd % dil == 0) & (dd // dil <= win // dil)
        return jnp.asarray(np.where(ok, 0.0, NEG).astype(np.float32))

    cache = [bias(np.arange(w_lens[g]), g) for g in range(N_GROUPS)]
    new = [bias(w_lens[g] + np.arange(T), g) for g in range(N_GROUPS)]
    return cache, new


def _decode_scores_pieces(in_refs, out_refs, dbs):
    G = N_GROUPS
    sq_refs, sc_refs, sb_refs, bd_ref = in_refs[0:G], in_refs[G:2 * G], in_refs[2 * G:3 * G], in_refs[3 * G]

    def piece(i, g):
        qbd = (jnp.tile(sq_refs[g][i].astype(F32), (HEADS, 1)) * bd_ref[...]).astype(BF16)
        out_refs[g][i] = _dot(qbd, sc_refs[g][i].astype(BF16)) + sb_refs[g][...]

    return [functools.partial(piece, i, g) for i in range(dbs) for g in range(G)]


def _decode_mix_pieces(in_refs, out_refs, dbs):
    G = N_GROUPS
    q_refs, new_refs, s_refs = in_refs[0:G], in_refs[G:2 * G], in_refs[2 * G:3 * G]
    c_refs, bn_refs, bd_ref = in_refs[3 * G:4 * G], in_refs[4 * G:5 * G], in_refs[5 * G]
    o_ref = out_refs[0]
    T = q_refs[0].shape[1]
    p_old, acc = {}, {}

    def prologue(i):
        bd = bd_ref[...]
        s_new, v_new = [], []
        for g in range(G):
            qbd = jnp.tile(q_refs[g][i].astype(F32), (HEADS, 1)) * bd
            s_new.append(_dot_nt(qbd, _rb(new_refs[g][i, :, 0:GW])) + bn_refs[g][...])
            v_new.append(_rb(new_refs[g][i, :, GW:2 * GW]))
        s_old = [s_refs[g][i] for g in range(G)]
        m = functools.reduce(jnp.maximum, [jnp.max(s, axis=1, keepdims=True) for s in s_new + s_old])
        e_new = [jnp.exp(s - m) for s in s_new]
        e_old = [jnp.exp(s - m) for s in s_old]
        den = functools.reduce(lambda a, b: a + b, [jnp.sum(p, axis=1, keepdims=True) for p in e_new + e_old])
        inv = 1.0 / den
        for g in range(G):
            p_old[i, g] = e_old[g] * inv
            r = _dot(_rb(e_new[g] * inv), v_new[g])
            acc[i] = r if g == 0 else acc[i] + r

    def piece(i, g):
        acc[i] = acc[i] + _dot_nt(p_old[i, g], c_refs[g][i])

    def epilogue(i):
        a = acc[i] * bd_ref[...]
        out = a[0:T, :]
        for hh in range(1, HEADS):
            out = out + a[hh * T:(hh + 1) * T, :]
        o_ref[i] = out

    thunks = []
    for i in range(dbs):
        thunks += [functools.partial(prologue, i)] + [functools.partial(piece, i, g) for g in range(G)]
        thunks += [functools.partial(epilogue, i)]
    return thunks


def _head_diag(T):
    bd = np.zeros((HEADS * T, GW), np.float32)
    for hh in range(HEADS):
        bd[hh * T:(hh + 1) * T, hh * HEAD_DIM:(hh + 1) * HEAD_DIM] = 1.0
    return jnp.asarray(bd)


def _rider_common(DB, nsteps, step_of):
    dbs = -(-DB // nsteps)
    assert DB % dbs == 0
    active = DB // dbs
    blk = lambda b, s: jnp.minimum(step_of(b, s), active - 1)
    per = lambda shp, part=0: pl.BlockSpec((dbs,) + shp, lambda b, s: (blk(b, s), part) + (0,) * (len(shp) - 1))
    const = lambda a: pl.BlockSpec(a.shape, lambda b, s: (0,) * a.ndim)
    return dbs, active, per, const


def _scores_rider(q_list, caches_t):
    DB, T, _ = q_list[0].shape
    w_lens = [c.shape[2] for c in caches_t]
    bias_c, _ = _sample_masks(T, w_lens)

    def make(nsteps, step_of):
        dbs, active, per, const = _rider_common(DB, nsteps, step_of)
        return dict(
            pieces=_decode_scores_pieces, dbs=dbs, active=active,
            arrays=list(q_list) + list(caches_t) + bias_c + [_head_diag(T)],
            in_specs=[per((T, GW)) for _ in q_list] + [per((GW, w), 0) for w in w_lens]
                     + [const(a) for a in bias_c] + [pl.BlockSpec((HEADS * T, GW), lambda b, s: (0, 0))],
            out_shape=[jax.ShapeDtypeStruct((DB, HEADS * T, w), F32) for w in w_lens],
            out_specs=[per((HEADS * T, w)) for w in w_lens])
    return make


def _mix_rider(q_list, kv_new, scores, caches_t):
    DB, T, _ = q_list[0].shape
    w_lens = [c.shape[2] for c in caches_t]
    _, bias_n = _sample_masks(T, w_lens)

    def make(nsteps, step_of):
        dbs, active, per, const = _rider_common(DB, nsteps, step_of)
        return dict(
            pieces=_decode_mix_pieces, dbs=dbs, active=active,
            arrays=list(q_list) + list(kv_new) + list(scores) + list(caches_t) + bias_n + [_head_diag(T)],
            in_specs=[per((T, GW)) for _ in q_list] + [per((T, 2 * GW)) for _ in kv_new]
                     + [per((HEADS * T, w)) for w in w_lens] + [per((GW, w), 1) for w in w_lens]
                     + [const(a) for a in bias_n] + [pl.BlockSpec((HEADS * T, GW), lambda b, s: (0, 0))],
            out_shape=[jax.ShapeDtypeStruct((DB, T, GW), F32)],
            out_specs=[per((T, GW))])
    return make


def _sgla_kernel(gq_ref, gk_ref, gv_ref, la_ref, st_ref, tri_ref, nw_ref, o_ref, snew_ref, *, dbt):
    tri3 = tri_ref[...]
    normw = nw_ref[...]
    T = gq_ref.shape[1]
    ti = lax.broadcasted_iota(jnp.int32, (T, T), 0)
    si = lax.broadcasted_iota(jnp.int32, (T, T), 1)
    causal = si <= ti
    low_half = lax.broadcasted_iota(jnp.int32, (T, LANES), 1) < GLA_DK
    ones = jnp.ones((3 * T, LANES), F32)
    row_low = lax.broadcasted_iota(jnp.int32, (LANES, GLA_DV), 0) < GLA_DK

    entries = range(dbt)
    pairs = range(GLA_HEADS // 2)
    cs = [slice(p * LANES, (p + 1) * LANES) for p in pairs]
    la3 = {i: jnp.concatenate(_split3(la_ref[i], F32), axis=0) for i in entries}
    b = {i: _dot(tri3, la3[i]) for i in entries}
    dec = {(i, p): jnp.exp(_dot_tn(la3[i][:, cs[p]], ones)) for i in entries for p in pairs}
    qe, ke, kd, gv = {}, {}, {}, {}
    for i in entries:
        b_last = b[i][T - 1:T, :]
        gk = gk_ref[i].astype(F32)
        qe[i] = gq_ref[i].astype(F32) * jnp.exp(b[i])
        ke[i] = _rb(gk * jnp.exp(-b[i]))
        kd[i] = _rb(gk * jnp.exp(b_last - b[i]))
        gv[i] = gv_ref[i].astype(F32)
    heads = [(i, p, a) for i in entries for p in pairs for a in range(2)]
    qm = {(i, p, a): _rb(jnp.where(low_half if a == 0 else jnp.logical_not(low_half), qe[i][:, cs[p]], 0.0))
          for (i, p, a) in heads}
    att = {(i, p, a): _rb(jnp.where(causal, _dot_nt(qm[i, p, a], ke[i][:, cs[p]]), 0.0)) for (i, p, a) in heads}
    v_h = {(i, p, a): gv[i][:, (2 * p + a) * GLA_DV:(2 * p + a + 1) * GLA_DV] for (i, p, a) in heads}
    o = {(i, p, a): _dot(att[i, p, a], v_h[i, p, a]) + _dot(qm[i, p, a], _rb(st_ref[i, cs[p], :]))
         for (i, p, a) in heads}
    u = {(i, p, a): _dot_tn(kd[i][:, cs[p]], v_h[i, p, a]) for (i, p, a) in heads}
    for i in entries:
        for p in pairs:
            upd = jnp.where(row_low, u[i, p, 0], u[i, p, 1])
            snew_ref[i, cs[p], :] = st_ref[i, cs[p], :] * dec[i, p] + upd
        outs = [o[i, p, a] * lax.rsqrt(jnp.mean(o[i, p, a] * o[i, p, a], axis=-1, keepdims=True) + NORM_EPS)
                * normw for p in pairs for a in range(2)]
        o_ref[i] = jnp.concatenate(outs, axis=1)


def _sample_gla(gq, gk, gv, la, state, normw, dbt):
    DB, T, _ = gq.shape
    dbt = min(dbt, DB)
    assert DB % dbt == 0
    st = state.reshape(DB, GLA_KW, GLA_DV)
    blk = lambda shp: pl.BlockSpec((dbt,) + shp, lambda i: (i,) + (0,) * len(shp))
    const = lambda shp: pl.BlockSpec(shp, lambda i: (0,) * len(shp))
    o, snew = pl.pallas_call(
        functools.partial(_sgla_kernel, dbt=dbt),
        grid=(DB // dbt,),
        in_specs=[blk((T, GLA_KW)), blk((T, GLA_KW)), blk((T, GLA_VW)), blk((T, GLA_KW)),
                  blk((GLA_KW, GLA_DV)), const((T, 3 * T)), const((1, GLA_DV))],
        out_specs=[blk((T, GLA_VW)), blk((GLA_KW, GLA_DV))],
        out_shape=[jax.ShapeDtypeStruct((DB, T, GLA_VW), F32),
                   jax.ShapeDtypeStruct((DB, GLA_KW, GLA_DV), F32)],
        compiler_params=_cparams(("arbitrary",)), name="gla_sample",
    )(gq, gk, gv, la, st, _tri3(T, F32), normw)
    return o, snew.reshape(DB, GLA_HEADS, GLA_DK, GLA_DV)


def _rope_tables(pos):
    half = ROPE_DIM // 2
    inv_freq = 1.0 / (ROPE_THETA ** (jnp.arange(half, dtype=F32) * (2.0 / ROPE_DIM)))
    ang = pos.astype(F32)[:, None] * inv_freq[None, :]
    cos, sin = jnp.cos(ang), jnp.sin(ang)
    n = pos.shape[0]
    pad = jnp.zeros((n, HEAD_DIM - ROPE_DIM), F32)
    zero = jnp.zeros((n, half), F32)
    c = jnp.concatenate([cos, cos, pad + 1.0], axis=1)
    s1 = jnp.concatenate([zero, sin, pad], axis=1)
    s2 = jnp.concatenate([-sin, zero, pad], axis=1)
    rep = LANES // HEAD_DIM
    return tuple(jnp.tile(t, (1, rep)) for t in (c, s1, s2))


def kernel(x_prompt, x_sample, cache_kv_w128, cache_kv_w512, cache_kv_w2048, state_gla,
           ln_w, w_in, w_gla_a2, b_gla_a, gla_norm_w, w_attn_out, w_gla_out, w_out, final_norm_w):
    B, S, D = x_prompt.shape
    DB, T, _ = x_sample.shape
    assert ln_w.shape[0] == 1, "single-layer step"
    caches = (cache_kv_w128, cache_kv_w512, cache_kv_w2048)
    dils = [d for _, d in GROUPS]

    lnw = ln_w[0].reshape(1, D)
    w_t = w_in[0].T.astype(BF16)
    assert PROJ_W <= 2 * OFF_AZ
    wa2 = w_gla_a2[0].astype(BF16)
    ba = b_gla_a[0].reshape(1, GLA_KW)
    gnw = gla_norm_w[0].reshape(1, GLA_DV)
    wa = w_attn_out[0].astype(BF16)
    wg = w_gla_out[0].astype(BF16)
    wo = w_out[0].astype(BF16)
    fnw = final_norm_w.reshape(1, D)


    R = DB * T
    xs = x_sample.reshape(1, R, D)
    pos_s = PAST_LEN + jnp.arange(T, dtype=jnp.int32)
    tabs_s = tuple(jnp.tile(t, (DB, 1)) for t in _rope_tables(pos_s))
    qkv_s = _proj_qkv(xs, lnw, w_t, tabs_s, min(PROJ_ROWS, R), (1,) * N_GROUPS, row_major_kv=True,
                      q_scale=HEAD_DIM ** -0.5)
    rest_s = _proj_rest(xs, lnw, w_t, wa2, ba, min(PROJ_ROWS, R))
    per_db = lambda t: t.reshape(DB, T, t.shape[-1])
    q_s = [per_db(q) for q in qkv_s[0:3]]
    kv_new = [per_db(t) for t in qkv_s[9:12]]
    caches_t = [c[0].transpose(0, 2, 3, 4, 1).reshape(DB, 2 * GW, c.shape[2]) for c in caches]

    tabs_p = _rope_tables(jnp.arange(S, dtype=jnp.int32))
    qkv = _proj_qkv(x_prompt, lnw, w_t, tabs_p, min(PROJ_ROWS, S), dils, row_major_kv=False,
                    q_scale=HEAD_DIM ** -0.5 * LOG2E,
                    make_rider=_scores_rider(q_s, caches_t))
    scores = qkv[9:12]
    rest = _proj_rest(x_prompt, lnw, w_t, wa2, ba, min(PROJ_ROWS, S),
                      make_rider=_mix_rider(q_s, kv_new, scores, caches_t))
    saz, sgz, sa, sb, gq, gk, gv, la, o_s = rest
    o_list, lse_list = [], []
    for g in range(N_GROUPS):
        o, lse = _attn_group(qkv[g], qkv[3 + g], qkv[6 + g], g, qb=ATTN_ROWS)
        o_list.append(o.reshape(-1, o.shape[-1]))
        lse_list.append(lse.reshape(-1, lse.shape[-1]))
    og, gla_p = _gla_prompt(gq, gk, gv, la, gnw, ct=GLA_TOKENS)
    flat = lambda t: t.reshape(B * S, t.shape[-1])
    y_prompt = _merge(flat(x_prompt), o_list, lse_list, dils, flat(saz), flat(og), flat(sgz), flat(sa),
                      flat(sb), wa, wg, wo, fnw, tm=MERGE_TILE).reshape(B, S, D)
    kvt = _kv_tail_prompt(x_prompt, lnw, w_t, [min(w, S) for w, _ in GROUPS], tile=KV_TAIL_COLS)
    kv_p = [t.reshape(B, 2, HEADS, HEAD_DIM, t.shape[-1]).transpose(0, 4, 1, 2, 3)[None] for t in kvt]

    saz, sgz, sa, sb, gq, gk, gv, la = rest_s
    og_s, gla_s = _sample_gla(per_db(gq), per_db(gk), per_db(gv), per_db(la), state_gla[0], gnw, dbt=SAMPLE_GLA_ENTRIES)
    flat_s = lambda t: t.reshape(R, t.shape[-1])
    y_sample = _merge(x_sample.reshape(R, D), [o_s.reshape(R, GW)], [], (1,), flat_s(saz), flat_s(og_s),
                      flat_s(sgz), flat_s(sa), flat_s(sb), wa, wg, wo, fnw, tm=MERGE_TILE).reshape(DB, T, D)
    kvt_s = _kv_tail_sample(x_sample, lnw, w_t, pos_s)
    kv_s = [t.reshape(T, 2, HEADS, HEAD_DIM, DB).transpose(4, 0, 1, 2, 3)[None] for t in kvt_s]

    return (y_prompt, y_sample, kv_p[0], kv_p[1], kv_p[2], gla_p[None],
            kv_s[0], kv_s[1], kv_s[2], gla_s[None])
```
